```python
import math
import jax
import jax.numpy as jnp
from jax import lax
import numpy as np


D_MODEL = 1024
BATCH = 1
SEQ = 16384
DEPTH = 1

HEAD_DIM = 64
NSA_HEADS = 8
NSA_GROUPS = 2
NSA_REP = NSA_HEADS // NSA_GROUPS
CMP_BLOCK = 32
CMP_STRIDE = 16
CMP_HIDDEN = 256
SEL_BLOCK = 64
SEL_TOPK = 16
NSA_WINDOW = 512
NSA_QBLOCK = 128
SWA_HEADS = 8
SWA_KV_HEADS = 2
SWA_REP = SWA_HEADS // SWA_KV_HEADS
SWA_WINDOW = 128
SWA_QBLOCK = 128
D_FF = 4 * D_MODEL
NUM_BUCKETS = 32
MAX_DISTANCE = 1024
N_BIAS_HEADS = NSA_HEADS + SWA_HEADS
RMS_EPS = 1e-6

NSA_Q_DIM = NSA_HEADS * HEAD_DIM
NSA_KV_DIM = NSA_GROUPS * HEAD_DIM
SWA_Q_DIM = SWA_HEADS * HEAD_DIM
SWA_KV_DIM = SWA_KV_HEADS * HEAD_DIM
IN_SIZES = (NSA_Q_DIM,) + (NSA_KV_DIM,) * 6 + (3 * NSA_HEADS, SWA_Q_DIM, SWA_KV_DIM, SWA_KV_DIM, D_MODEL, D_MODEL)
IN_DIM = int(sum(IN_SIZES))
IN_OFFSETS = tuple(int(o) for o in np.cumsum(IN_SIZES)[:-1])

kernel_name = 'hybrid_nsa_swa_sink_gated_block'


def rms_norm(x, g):
    xf = x.astype(jnp.float32)
    y = xf * lax.rsqrt(jnp.mean(xf * xf, axis=-1, keepdims=True) + RMS_EPS)
    return (y * g.astype(jnp.float32)).astype(x.dtype)


def t5_bucket(dist):
    max_exact = NUM_BUCKETS // 2
    d = jnp.maximum(dist, 0)
    df = jnp.maximum(d, 1).astype(jnp.float32)
    large = max_exact + (jnp.log(df / max_exact) / math.log(MAX_DISTANCE / max_exact)
                         * (NUM_BUCKETS - max_exact)).astype(jnp.int32)
    large = jnp.minimum(large, NUM_BUCKETS - 1)
    return jnp.where(d < max_exact, d, large)


def masked_softmax(s, mask):
    s = jnp.where(mask, s.astype(jnp.float32), -jnp.inf)
    m = jnp.max(s, axis=-1, keepdims=True)
    m = jnp.where(jnp.isfinite(m), m, 0.0)
    e = jnp.exp(s - m)
    return e / jnp.maximum(jnp.sum(e, axis=-1, keepdims=True), 1e-30)


def compress_kv(kv, pos, w1, w2):
    b, s, g, d = kv.shape
    n_cmp = (s - CMP_BLOCK) // CMP_STRIDE + 1
    kb = kv.reshape(b, s // CMP_STRIDE, CMP_STRIDE, g, d)
    blocks = jnp.concatenate([kb[:, o:o + n_cmp] for o in range(CMP_BLOCK // CMP_STRIDE)], axis=2)
    blocks = blocks + pos[None, None, :, None, :]
    flat = blocks.transpose(0, 1, 3, 2, 4).reshape(b, n_cmp, g, CMP_BLOCK * d)
    return jax.nn.gelu(flat @ w1) @ w2


def nsa_attention(q, kc, vc, k_slc, v_slc, k_win, v_win, gates, bias_tab):
    b, s, g, r, d = q.shape
    scale = d ** -0.5
    n_cmp = kc.shape[1]
    n_sel = s // SEL_BLOCK
    topk = min(SEL_TOPK, n_sel)
    n_chunks = s // NSA_QBLOCK
    ratio = SEL_BLOCK // CMP_STRIDE
    pad_front = CMP_BLOCK // CMP_STRIDE - 1
    pad_back = n_sel * ratio - n_cmp
    ks_blk = k_slc.reshape(b, n_sel, SEL_BLOCK, g, d).transpose(0, 3, 1, 2, 4)
    vs_blk = v_slc.reshape(b, n_sel, SEL_BLOCK, g, d).transpose(0, 3, 1, 2, 4)
    kw = jnp.pad(k_win, ((0, 0), (NSA_WINDOW, 0), (0, 0), (0, 0)))
    vw = jnp.pad(v_win, ((0, 0), (NSA_WINDOW, 0), (0, 0), (0, 0)))
    cmp_end = jnp.arange(n_cmp) * CMP_STRIDE + CMP_BLOCK - 1
    b_idx = jnp.arange(b)[:, None, None, None]
    g_idx = jnp.arange(g)[None, :, None, None]
    win_off = jnp.arange(NSA_QBLOCK + NSA_WINDOW) - NSA_WINDOW
    sel_off = jnp.arange(SEL_BLOCK)
    blk_ids = jnp.arange(n_sel)

    def chunk(args):
        qc, gc, ci = args
        t = ci * NSA_QBLOCK + jnp.arange(NSA_QBLOCK)
        sc = jnp.einsum('bqgrd,bngd->bgrqn', qc, kc).astype(jnp.float32) * scale
        pc = masked_softmax(sc, cmp_end[None, :] <= t[:, None])
        oc = jnp.einsum('bgrqn,bngd->bqgrd', pc.astype(vc.dtype), vc)
        pg = jnp.pad(pc.sum(axis=2), ((0, 0), (0, 0), (0, 0), (pad_front, pad_back)))
        ps = sum(pg[..., o:o + n_sel * ratio:ratio] for o in range(ratio + pad_front))
        cur = t // SEL_BLOCK
        valid = blk_ids[None, :] <= cur[:, None]
        forced = (blk_ids[None, :] == 0) | (blk_ids[None, :] == cur[:, None]) | (blk_ids[None, :] == cur[:, None] - 1)
        score = jnp.where(valid, jnp.where(forced, jnp.inf, ps), -jnp.inf)
        _, idx = lax.top_k(score, topk)
        kg = ks_blk[b_idx, g_idx, idx]
        vg = vs_blk[b_idx, g_idx, idx]
        pos = idx[..., None] * SEL_BLOCK + sel_off
        tq = t[None, None, :, None, None]
        bias_s = bias_tab[t5_bucket(tq - pos), g_idx[..., None]]
        ss = jnp.einsum('bqgrd,bgqjkd->bgrqjk', qc, kg).astype(jnp.float32) * scale + jnp.moveaxis(bias_s, -1, 2)
        n_keys = topk * SEL_BLOCK
        mask_s = (pos <= tq).reshape(b, g, 1, NSA_QBLOCK, n_keys)
        p_s = masked_softmax(ss.reshape(b, g, r, NSA_QBLOCK, n_keys), mask_s)
        o_s = jnp.einsum('bgrqn,bgqnd->bqgrd', p_s.astype(vg.dtype), vg.reshape(b, g, NSA_QBLOCK, n_keys, d))
        kwc = lax.dynamic_slice_in_dim(kw, ci * NSA_QBLOCK, NSA_QBLOCK + NSA_WINDOW, axis=1)
        vwc = lax.dynamic_slice_in_dim(vw, ci * NSA_QBLOCK, NSA_QBLOCK + NSA_WINDOW, axis=1)
        spos = ci * NSA_QBLOCK + win_off
        dist = t[:, None] - spos[None, :]
        mask_w = (dist >= 0) & (dist < NSA_WINDOW) & (spos[None, :] >= 0)
        bias_w = jnp.transpose(bias_tab[t5_bucket(dist)], (2, 3, 0, 1))
        sw = jnp.einsum('bqgrd,bkgd->bgrqk', qc, kwc).astype(jnp.float32) * scale + bias_w
        p_w = masked_softmax(sw, mask_w)
        o_w = jnp.einsum('bgrqk,bkgd->bqgrd', p_w.astype(vwc.dtype), vwc)
        gs = jax.nn.sigmoid(gc.astype(jnp.float32)).astype(qc.dtype)
        return gs[..., 0:1] * oc + gs[..., 1:2] * o_s + gs[..., 2:3] * o_w

    q_ch = q.reshape(b, n_chunks, NSA_QBLOCK, g, r, d).swapaxes(0, 1)
    g_ch = gates.reshape(b, n_chunks, NSA_QBLOCK, g, r, 3).swapaxes(0, 1)
    out = lax.map(chunk, (q_ch, g_ch, jnp.arange(n_chunks)))
    return out.swapaxes(0, 1).reshape(b, s, g * r * d)


def swa_sink_attention(q, k, v, sinks, bias_tab):
    b, s, g, r, d = q.shape
    scale = d ** -0.5
    nb = s // SWA_QBLOCK
    qb = q.reshape(b, nb, SWA_QBLOCK, g, r, d)
    kb = k.reshape(b, nb, SWA_QBLOCK, g, d)
    vb = v.reshape(b, nb, SWA_QBLOCK, g, d)
    pad = ((0, 0), (1, 0), (0, 0), (0, 0), (0, 0))
    kk = jnp.concatenate([jnp.pad(kb, pad)[:, :-1], kb], axis=2)
    vv = jnp.concatenate([jnp.pad(vb, pad)[:, :-1], vb], axis=2)
    qi = jnp.arange(SWA_QBLOCK)[:, None]
    ki = jnp.arange(2 * SWA_QBLOCK)[None, :]
    dist = qi - ki + SWA_QBLOCK
    band = (dist >= 0) & (dist < SWA_WINDOW)
    first = (jnp.arange(nb) > 0)[:, None, None] | (ki >= SWA_QBLOCK)[None]
    mask = (band[None] & first)[None, :, None, None]
    bias = jnp.transpose(bias_tab[t5_bucket(dist)], (2, 3, 0, 1))
    sc = jnp.einsum('bnqgrd,bnkgd->bngrqk', qb, kk).astype(jnp.float32) * scale + bias
    sc = jnp.where(mask, sc, -jnp.inf)
    sink = sinks.astype(jnp.float32)[None, None, :, :, None, None]
    m = jnp.maximum(jnp.max(sc, axis=-1, keepdims=True), sink)
    e = jnp.exp(sc - m)
    p = e / (jnp.sum(e, axis=-1, keepdims=True) + jnp.exp(sink - m))
    o = jnp.einsum('bngrqk,bnkgd->bnqgrd', p.astype(vv.dtype), vv)
    return o.reshape(b, s, g * r * d)


def setup_inputs(seed: int = 0) -> dict:
    key = jax.random.key(seed)
    ks = jax.random.split(key, 20)
    nrm = lambda k, shape, scale: jax.random.normal(k, shape, jnp.float32) * scale
    gain = lambda k: 1.0 + nrm(k, (DEPTH, D_MODEL), 0.05)
    cin = CMP_BLOCK * HEAD_DIM
    return {
        'x': nrm(ks[0], (BATCH, SEQ, D_MODEL), 1.0),
        'norm_mix_pre': gain(ks[1]),
        'norm_mix_post': gain(ks[2]),
        'norm_mlp_pre': gain(ks[3]),
        'norm_mlp_post': gain(ks[4]),
        'w_in': nrm(ks[5], (DEPTH, D_MODEL, IN_DIM), D_MODEL ** -0.5),
        'cmp_pos_k': nrm(ks[6], (DEPTH, CMP_BLOCK, HEAD_DIM), 0.5),
        'cmp_w1_k': nrm(ks[7], (DEPTH, cin, CMP_HIDDEN), cin ** -0.5),
        'cmp_w2_k': nrm(ks[8], (DEPTH, CMP_HIDDEN, HEAD_DIM), CMP_HIDDEN ** -0.5),
        'cmp_pos_v': nrm(ks[9], (DEPTH, CMP_BLOCK, HEAD_DIM), 0.5),
        'cmp_w1_v': nrm(ks[10], (DEPTH, cin, CMP_HIDDEN), cin ** -0.5),
        'cmp_w2_v': nrm(ks[11], (DEPTH, CMP_HIDDEN, HEAD_DIM), CMP_HIDDEN ** -0.5),
        'attn_sinks': nrm(ks[12], (DEPTH, SWA_HEADS), 0.5),
        'rel_bias': nrm(ks[13], (NUM_BUCKETS, N_BIAS_HEADS), 0.5),
        'w_up_nsa': nrm(ks[14], (DEPTH, NSA_Q_DIM, D_MODEL), NSA_Q_DIM ** -0.5),
        'w_up_swa': nrm(ks[15], (DEPTH, SWA_Q_DIM, D_MODEL), SWA_Q_DIM ** -0.5),
        'w_out': nrm(ks[16], (DEPTH, D_MODEL, D_MODEL), D_MODEL ** -0.5),
        'w_mlp_in': nrm(ks[17], (DEPTH, D_MODEL, D_FF), D_MODEL ** -0.5),
        'w_mlp_out': nrm(ks[18], (DEPTH, D_FF, D_MODEL), D_FF ** -0.5),
    }


def reference(x, norm_mix_pre, norm_mix_post, norm_mlp_pre, norm_mlp_post, w_in,
              cmp_pos_k, cmp_w1_k, cmp_w2_k, cmp_pos_v, cmp_w1_v, cmp_w2_v,
              attn_sinks, rel_bias, w_up_nsa, w_up_swa, w_out, w_mlp_in, w_mlp_out):
    b, s, _ = x.shape
    bias_nsa = rel_bias[:, :NSA_HEADS].reshape(NUM_BUCKETS, NSA_GROUPS, NSA_REP)
    bias_swa = rel_bias[:, NSA_HEADS:].reshape(NUM_BUCKETS, SWA_KV_HEADS, SWA_REP)
    h = x
    for l in range(DEPTH):
        u = rms_norm(h, norm_mix_pre[l])
        (q_n, kc_n, vc_n, ks_n, vs_n, kw_n, vw_n, g_n,
         q_s, k_s, v_s, gate_a, gate_b) = jnp.split(u @ w_in[l], IN_OFFSETS, axis=-1)
        kvr = lambda t_: t_.reshape(b, s, NSA_GROUPS, HEAD_DIM)
        kc = compress_kv(kvr(kc_n), cmp_pos_k[l], cmp_w1_k[l], cmp_w2_k[l])
        vc = compress_kv(kvr(vc_n), cmp_pos_v[l], cmp_w1_v[l], cmp_w2_v[l])
        o_a = nsa_attention(q_n.reshape(b, s, NSA_GROUPS, NSA_REP, HEAD_DIM), kc, vc,
                            kvr(ks_n), kvr(vs_n), kvr(kw_n), kvr(vw_n),
                            g_n.reshape(b, s, NSA_GROUPS, NSA_REP, 3), bias_nsa)
        o_b = swa_sink_attention(q_s.reshape(b, s, SWA_KV_HEADS, SWA_REP, HEAD_DIM),
                                 k_s.reshape(b, s, SWA_KV_HEADS, HEAD_DIM),
                                 v_s.reshape(b, s, SWA_KV_HEADS, HEAD_DIM),
                                 attn_sinks[l].reshape(SWA_KV_HEADS, SWA_REP), bias_swa)
        y = jax.nn.sigmoid(gate_a) * (o_a @ w_up_nsa[l]) + jax.nn.sigmoid(gate_b) * (o_b @ w_up_swa[l])
        h = h + rms_norm(y @ w_out[l], norm_mix_post[l])
        u = rms_norm(h, norm_mlp_pre[l])
        f = jnp.square(jax.nn.relu(u @ w_mlp_in[l])) @ w_mlp_out[l]
        h = h + rms_norm(f, norm_mlp_post[l])
    return h
```

```python
import functools
import math

import jax
import jax.numpy as jnp
from jax import lax
from jax.experimental import pallas as pl
from jax.experimental.pallas import tpu as pltpu

F32 = jnp.float32
BF16 = jnp.bfloat16

HEAD_DIM = 64
GROUPS = 2
REP = 4
HEADS = GROUPS * REP
CMP_BLOCK = 32
CMP_STRIDE = 16
SEL_BLOCK = 64
SEL_TOPK = 16
NSA_WINDOW = 512
SWA_WINDOW = 128
QBLOCK = 128
NUM_BUCKETS = 32
MAX_DISTANCE = 1024
RMS_EPS = 1e-6
MASK_BIG = 2.0 ** 99
SEL_TILE = 1024
HQ = REP * QBLOCK
VMEM_LIMIT = 56 * 1024 * 1024

_NT = (((1,), (1,)), ((), ()))
_TN = (((0,), (0,)), ((), ()))


def _params(*sem):
    return pltpu.CompilerParams(dimension_semantics=sem, vmem_limit_bytes=VMEM_LIMIT)


def _t5_bucket(dist):
    max_exact = NUM_BUCKETS // 2
    d = jnp.maximum(dist, 0)
    df = jnp.maximum(d, 1).astype(jnp.float32)
    large = max_exact + (jnp.log(df / max_exact) / math.log(MAX_DISTANCE / max_exact)
                         * (NUM_BUCKETS - max_exact)).astype(jnp.int32)
    large = jnp.minimum(large, NUM_BUCKETS - 1)
    return jnp.where(d < max_exact, d, large)


def _rms(x, gain):
    return x * lax.rsqrt(jnp.mean(x * x, axis=-1, keepdims=True) + RMS_EPS) * gain


def _proj_kernel(x_ref, g_ref, wn_ref, wg_ref, wt_ref, kv_ref, gab_ref, qv_ref, gn_ref, *, nqv):
    u = _rms(x_ref[...], g_ref[...]).astype(BF16)
    kv_ref[...] = jnp.dot(u, wn_ref[...], preferred_element_type=F32).astype(BF16)
    gab_ref[...] = jnp.dot(u, wg_ref[...], preferred_element_type=F32)
    t = lax.dot_general(wt_ref[...], u, _NT, preferred_element_type=F32)
    qv_ref[...] = t[:nqv].astype(BF16)
    gn_ref[...] = t[nqv:]


def _project(x, gain, wn, wg, wt, nqv, ts):
    s, d = x.shape
    nn, ng, ntr = wn.shape[1], wg.shape[1], wt.shape[0]
    full = lambda shp: pl.BlockSpec(shp, lambda i: (0, 0))
    return pl.pallas_call(
        functools.partial(_proj_kernel, nqv=nqv),
        grid=(s // ts,),
        in_specs=[pl.BlockSpec((ts, d), lambda i: (i, 0)), full((1, d)),
                  full((d, nn)), full((d, ng)), full((ntr, d))],
        out_specs=[pl.BlockSpec((ts, nn), lambda i: (i, 0)),
                   pl.BlockSpec((ts, ng), lambda i: (i, 0)),
                   pl.BlockSpec((nqv, ts), lambda i: (0, i)),
                   pl.BlockSpec((ntr - nqv, ts), lambda i: (0, i))],
        out_shape=[jax.ShapeDtypeStruct((s, nn), BF16), jax.ShapeDtypeStruct((s, ng), F32),
                   jax.ShapeDtypeStruct((nqv, s), BF16), jax.ShapeDtypeStruct((ntr - nqv, s), F32)],
        compiler_params=_params("arbitrary"),
        name="in_proj",
    )(x, gain, wn, wg, wt)


def _gelu_tanh(x):
    return x * (0.5 * (1.0 + jnp.tanh(math.sqrt(2.0 / math.pi) * (x + 0.044715 * (x * x * x)))))


def _compress_kernel(kb_ref, pos_ref, w1_ref, w2_ref, w2t_ref, o_ref, ot_ref):
    kb = kb_ref[0, 0]
    half = kb.shape[1]
    n16 = kb.shape[0]
    first = jnp.dot(kb, w1_ref[0, :half, :], preferred_element_type=F32)
    second = jnp.dot(kb, w1_ref[0, half:, :], preferred_element_type=F32)
    posb = jnp.dot(pos_ref[0], w1_ref[0], preferred_element_type=F32)[0:1]
    pre = first + pltpu.roll(second, n16 - 1, 0) + posb
    h = _gelu_tanh(pre).astype(BF16)
    o_ref[0, 0] = jnp.dot(h, w2_ref[0], preferred_element_type=F32).astype(BF16)
    ot_ref[0, 0] = lax.dot_general(w2t_ref[0], h, _NT, preferred_element_type=F32).astype(BF16)


def _compress(kb, pos8, w1, w2, w2t):
    _, g, n16, cin2 = kb.shape
    hid = w1.shape[2]
    return pl.pallas_call(
        _compress_kernel,
        grid=(2, g),
        in_specs=[pl.BlockSpec((1, 1, n16, cin2), lambda a, b: (a, b, 0, 0)),
                  pl.BlockSpec((1, 8, 2 * cin2), lambda a, b: (a, 0, 0)),
                  pl.BlockSpec((1, 2 * cin2, hid), lambda a, b: (a, 0, 0)),
                  pl.BlockSpec((1, hid, HEAD_DIM), lambda a, b: (a, 0, 0)),
                  pl.BlockSpec((1, HEAD_DIM, hid), lambda a, b: (a, 0, 0))],
        out_specs=[pl.BlockSpec((1, 1, n16, HEAD_DIM), lambda a, b: (a, b, 0, 0)),
                   pl.BlockSpec((1, 1, HEAD_DIM, n16), lambda a, b: (a, b, 0, 0))],
        out_shape=[jax.ShapeDtypeStruct((2, g, n16, HEAD_DIM), BF16),
                   jax.ShapeDtypeStruct((2, g, HEAD_DIM, n16), BF16)],
        compiler_params=_params("arbitrary", "arbitrary"),
        name="kv_compress",
    )(kb, pos8, w1, w2, w2t)


def _heads_to_lanes(q_ref):
    return jnp.concatenate([q_ref[0, r * HEAD_DIM:(r + 1) * HEAD_DIM, :] for r in range(REP)], axis=1)


def _store_heads(o_ref, o_t):
    for r in range(REP):
        o_ref[0, r * HEAD_DIM:(r + 1) * HEAD_DIM, :] = o_t[:, r * QBLOCK:(r + 1) * QBLOCK]


def _cmpsel_kernel(q_ref, kc_ref, vct_ref, oc_ref, sel_ref, pg_ref, *, n16, nsel, topk):
    t0 = pl.program_id(1) * QBLOCK
    q4 = _heads_to_lanes(q_ref)
    s = jnp.dot(kc_ref[0, 0], q4, preferred_element_type=F32)
    n_io = lax.broadcasted_iota(jnp.int32, (n16, HQ), 0)
    t_io = t0 + (lax.broadcasted_iota(jnp.int32, (n16, HQ), 1) & (QBLOCK - 1))
    s = jnp.where(n_io * CMP_STRIDE + (CMP_BLOCK - 1) <= t_io, s, -jnp.inf)
    m = jnp.max(s, axis=0, keepdims=True)
    m = jnp.where(m == -jnp.inf, 0.0, m)
    e = jnp.exp(s - m)
    den = jnp.maximum(jnp.sum(e, axis=0, keepdims=True), 1e-30)
    p = e * (1.0 / den)
    _store_heads(oc_ref, jnp.dot(vct_ref[0, 0], p.astype(BF16), preferred_element_type=F32))

    pg = ((p[:, 0:QBLOCK] + p[:, QBLOCK:2 * QBLOCK]) + p[:, 2 * QBLOCK:3 * QBLOCK]) + p[:, 3 * QBLOCK:]
    pg_ref[0:8, :] = jnp.zeros((8, QBLOCK), F32)
    pg_ref[8:8 + n16, :] = pg
    ratio = SEL_BLOCK // CMP_STRIDE
    ps = pg_ref[pl.ds(7, nsel, stride=ratio), :]
    for o in range(1, ratio + 1):
        ps = ps + pg_ref[pl.ds(7 + o, nsel, stride=ratio), :]

    blk = lax.broadcasted_iota(jnp.int32, (nsel, QBLOCK), 0)
    cur = (t0 + lax.broadcasted_iota(jnp.int32, (nsel, QBLOCK), 1)) // SEL_BLOCK
    valid = blk <= cur
    forced = (blk == 0) | (blk == cur) | (blk == cur - 1)
    score = jnp.where(valid, jnp.where(forced, jnp.inf, ps), -jnp.inf)

    def pick_one(_, carry):
        rem, chosen = carry
        top = jnp.max(rem, axis=0, keepdims=True)
        first = jnp.min(jnp.where(rem == top, blk, nsel), axis=0, keepdims=True)
        hit = blk == first
        return jnp.where(hit, -jnp.inf, rem), jnp.where(hit, 1.0, chosen)

    _, chosen = lax.fori_loop(0, topk, pick_one, (score, jnp.zeros((nsel, QBLOCK), F32)))
    sel_ref[0] = jnp.where((chosen > 0.0) & valid, 0.0, -MASK_BIG).astype(BF16)


def _cmp_select(q, kc_all, vct_all, s):
    n16 = s // CMP_STRIDE
    nsel = s // SEL_BLOCK
    topk = min(SEL_TOPK, nsel)
    return pl.pallas_call(
        functools.partial(_cmpsel_kernel, n16=n16, nsel=nsel, topk=topk),
        grid=(GROUPS, s // QBLOCK),
        in_specs=[pl.BlockSpec((1, REP * HEAD_DIM, QBLOCK), lambda g, c: (g, 0, c)),
                  pl.BlockSpec((1, 1, n16, HEAD_DIM), lambda g, c: (0, g, 0, 0)),
                  pl.BlockSpec((1, 1, HEAD_DIM, n16), lambda g, c: (1, g, 0, 0))],
        out_specs=[pl.BlockSpec((1, REP * HEAD_DIM, QBLOCK), lambda g, c: (g, 0, c)),
                   pl.BlockSpec((1, nsel, QBLOCK), lambda g, c: (g, 0, c))],
        out_shape=[jax.ShapeDtypeStruct((GROUPS, REP * HEAD_DIM, s), F32),
                   jax.ShapeDtypeStruct((GROUPS, nsel, s), BF16)],
        scratch_shapes=[pltpu.VMEM((n16 + 8, QBLOCK), F32)],
        compiler_params=_params("arbitrary", "arbitrary"),
        name="cmp_select",
    )(q, kc_all, vct_all)


def _sel_kernel(q_ref, sel_ref, kaug_ref, vt_ref, strip_ref, cst_ref, o_ref,
                qa_ref, s_ref, m_ref, l_ref, acc_ref, *, tk):
    g = pl.program_id(0)
    ci = pl.program_id(1)
    nb = tk // SEL_BLOCK
    cpt = tk // QBLOCK
    kd = GROUPS * HEAD_DIM
    qa_ref[...] = jnp.zeros(qa_ref.shape, BF16)
    row0 = pl.multiple_of(g * HEAD_DIM, HEAD_DIM)
    for r in range(REP):
        qa_ref[pl.ds(row0, HEAD_DIM), r * QBLOCK:(r + 1) * QBLOCK] = q_ref[0, r * HEAD_DIM:(r + 1) * HEAD_DIM, :]
    qa_ref[kd + nb:kd + nb + 16, :] = cst_ref[0]
    m_ref[...] = jnp.full(m_ref.shape, -jnp.inf, F32)
    l_ref[...] = jnp.zeros(l_ref.shape, F32)
    acc_ref[...] = jnp.zeros(acc_ref.shape, F32)
    nt = ci // cpt + 1

    def tile(j, carry):
        sb = sel_ref[0, pl.ds(pl.multiple_of(j * nb, nb), nb), :]
        for r in range(REP):
            qa_ref[kd:kd + nb, r * QBLOCK:(r + 1) * QBLOCK] = sb
        k0 = pl.multiple_of(j * tk, tk)
        s_ref[...] = jnp.dot(kaug_ref[pl.ds(k0, tk), :], qa_ref[...], preferred_element_type=F32)

        @pl.when(j >= nt - 2)
        def _():
            off = pl.multiple_of((2 * tk - QBLOCK) - (ci * QBLOCK - j * tk), QBLOCK)
            for r in range(REP):
                s_ref[:, r * QBLOCK:(r + 1) * QBLOCK] += strip_ref[0, r, pl.ds(off, tk), :]

        s = s_ref[...]
        m_prev = m_ref[...]
        m_new = jnp.maximum(m_prev, jnp.max(s, axis=0, keepdims=True))
        alpha = jnp.exp(m_prev - m_new)
        p = jnp.exp(s - m_new)
        l_ref[...] = alpha * l_ref[...] + jnp.sum(p, axis=0, keepdims=True)
        acc_ref[...] = alpha * acc_ref[...] + jnp.dot(vt_ref[0, :, pl.ds(k0, tk)], p.astype(BF16),
                                                      preferred_element_type=F32)
        m_ref[...] = m_new
        return carry

    lax.fori_loop(0, nt, tile, 0)
    _store_heads(o_ref, acc_ref[...] * (1.0 / jnp.maximum(l_ref[...], 1e-30)))


def _selected(q, sel, kaug, vt, strip, cst, s, tk):
    nsel = s // SEL_BLOCK
    nb = tk // SEL_BLOCK
    sl = strip.shape[2]
    return pl.pallas_call(
        functools.partial(_sel_kernel, tk=tk),
        grid=(GROUPS, s // QBLOCK),
        in_specs=[pl.BlockSpec((1, REP * HEAD_DIM, QBLOCK), lambda g, c: (g, 0, c)),
                  pl.BlockSpec((1, nsel, QBLOCK), lambda g, c: (g, 0, c)),
                  pl.BlockSpec(kaug.shape, lambda g, c: (0, 0)),
                  pl.BlockSpec((1, HEAD_DIM, s), lambda g, c: (g, 0, 0)),
                  pl.BlockSpec((1, REP, sl, QBLOCK), lambda g, c: (g, 0, 0, 0)),
                  pl.BlockSpec((1, 16, HQ), lambda g, c: (g, 0, 0))],
        out_specs=pl.BlockSpec((1, REP * HEAD_DIM, QBLOCK), lambda g, c: (g, 0, c)),
        out_shape=jax.ShapeDtypeStruct((GROUPS, REP * HEAD_DIM, s), F32),
        scratch_shapes=[pltpu.VMEM((kaug.shape[1], HQ), BF16), pltpu.VMEM((tk, HQ), F32),
                        pltpu.VMEM((1, HQ), F32), pltpu.VMEM((1, HQ), F32), pltpu.VMEM((HEAD_DIM, HQ), F32)],
        compiler_params=_params("arbitrary", "arbitrary"),
        name="selected_attn",
    )(q, sel, kaug, vt, strip, cst)


def _band_kernel(q_ref, k_ref, vt_ref, strip_ref, sink_ref, o_ref, qa_ref, *, window, use_sink):
    g = pl.program_id(0)
    t0 = pl.multiple_of(pl.program_id(1) * QBLOCK, QBLOCK)
    nk = window + QBLOCK
    qa_ref[...] = jnp.zeros(qa_ref.shape, BF16)
    row0 = pl.multiple_of(g * HEAD_DIM, HEAD_DIM)
    for r in range(REP):
        qa_ref[pl.ds(row0, HEAD_DIM), r * QBLOCK:(r + 1) * QBLOCK] = q_ref[0, r * HEAD_DIM:(r + 1) * HEAD_DIM, :]
    s = jnp.dot(k_ref[pl.ds(t0, nk), :], qa_ref[...], preferred_element_type=F32) + strip_ref[0]
    row = lax.broadcasted_iota(jnp.int32, (nk, HQ), 0)
    s = jnp.where(row + (t0 - window) >= 0, s, -MASK_BIG)
    m = jnp.max(s, axis=0, keepdims=True)
    if use_sink:
        m = jnp.maximum(m, sink_ref[0])
    e = jnp.exp(s - m)
    den = jnp.sum(e, axis=0, keepdims=True)
    if use_sink:
        den = den + jnp.exp(sink_ref[0] - m)
    else:
        den = jnp.maximum(den, 1e-30)
    o_t = jnp.dot(vt_ref[0, :, pl.ds(t0, nk)], e.astype(BF16), preferred_element_type=F32)
    _store_heads(o_ref, o_t * (1.0 / den))


def _banded(q, kpad, vtpad, strip, sink, s, window, use_sink, name):
    nk = window + QBLOCK
    return pl.pallas_call(
        functools.partial(_band_kernel, window=window, use_sink=use_sink),
        grid=(GROUPS, s // QBLOCK),
        in_specs=[pl.BlockSpec((1, REP * HEAD_DIM, QBLOCK), lambda g, c: (g, 0, c)),
                  pl.BlockSpec(kpad.shape, lambda g, c: (0, 0)),
                  pl.BlockSpec((1, HEAD_DIM, s + window), lambda g, c: (g, 0, 0)),
                  pl.BlockSpec((1, nk, HQ), lambda g, c: (g, 0, 0)),
                  pl.BlockSpec((1, 1, HQ), lambda g, c: (g, 0, 0))],
        out_specs=pl.BlockSpec((1, REP * HEAD_DIM, QBLOCK), lambda g, c: (g, 0, c)),
        out_shape=jax.ShapeDtypeStruct((GROUPS, REP * HEAD_DIM, s), F32),
        scratch_shapes=[pltpu.VMEM((GROUPS * HEAD_DIM, HQ), BF16)],
        compiler_params=_params("arbitrary", "arbitrary"),
        name=name,
    )(q, kpad, vtpad, strip, sink)


def _merge_kernel(oc_ref, os_ref, ow_ref, ob_ref, gn_ref, gab_ref, x_ref, wa_ref, wb_ref, wo_ref, gp_ref, o_ref):
    d = x_ref.shape[1]
    gn = jax.nn.sigmoid(gn_ref[...])
    parts = []
    for h in range(HEADS):
        rows = slice(h * HEAD_DIM, (h + 1) * HEAD_DIM)
        parts.append(gn[h:h + 1] * oc_ref[rows, :] + gn[HEADS + h:HEADS + h + 1] * os_ref[rows, :]
                     + gn[2 * HEADS + h:2 * HEADS + h + 1] * ow_ref[rows, :])
    oa_t = jnp.concatenate(parts, axis=0).astype(BF16)
    ya = lax.dot_general(oa_t, wa_ref[...], _TN, preferred_element_type=F32)
    yb = lax.dot_general(ob_ref[...].astype(BF16), wb_ref[...], _TN, preferred_element_type=F32)
    y = jax.nn.sigmoid(gab_ref[:, :d]) * ya + jax.nn.sigmoid(gab_ref[:, d:]) * yb
    z = jnp.dot(y.astype(BF16), wo_ref[...], preferred_element_type=F32)
    o_ref[...] = x_ref[...] + _rms(z, gp_ref[...])


def _merge(oc, os_, ow, ob, gn, gab, x, wa, wb, wo, gpost, ts):
    s, d = x.shape
    hd = oc.shape[0]
    tok = lambda rows: pl.BlockSpec((rows, ts), lambda i: (0, i))
    full = lambda shp: pl.BlockSpec(shp, lambda i: (0, 0))
    return pl.pallas_call(
        _merge_kernel,
        grid=(s // ts,),
        in_specs=[tok(hd), tok(hd), tok(hd), tok(hd), tok(gn.shape[0]),
                  pl.BlockSpec((ts, 2 * d), lambda i: (i, 0)), pl.BlockSpec((ts, d), lambda i: (i, 0)),
                  full((hd, d)), full((hd, d)), full((d, d)), full((1, d))],
        out_specs=pl.BlockSpec((ts, d), lambda i: (i, 0)),
        out_shape=jax.ShapeDtypeStruct((s, d), F32),
        compiler_params=_params("arbitrary"),
        name="merge_out_proj",
    )(oc, os_, ow, ob, gn, gab, x, wa, wb, wo, gpost)


def _mlp_kernel(h_ref, gpre_ref, w1_ref, w2_ref, gpost_ref, o_ref, u_ref, acc_ref):
    f = pl.program_id(1)

    @pl.when(f == 0)
    def _():
        u_ref[...] = _rms(h_ref[...], gpre_ref[...]).astype(BF16)
        acc_ref[...] = jnp.zeros(acc_ref.shape, F32)

    a = jnp.maximum(jnp.dot(u_ref[...], w1_ref[...], preferred_element_type=F32), 0.0)
    acc_ref[...] += jnp.dot((a * a).astype(BF16), w2_ref[...], preferred_element_type=F32)

    @pl.when(f == pl.num_programs(1) - 1)
    def _():
        o_ref[...] = h_ref[...] + _rms(acc_ref[...], gpost_ref[...])


def _mlp(h, gpre, w1, w2, gpost, ts, tf):
    s, d = h.shape
    dff = w1.shape[1]
    return pl.pallas_call(
        _mlp_kernel,
        grid=(s // ts, dff // tf),
        in_specs=[pl.BlockSpec((ts, d), lambda i, f: (i, 0)), pl.BlockSpec((1, d), lambda i, f: (0, 0)),
                  pl.BlockSpec((d, tf), lambda i, f: (0, f)), pl.BlockSpec((tf, d), lambda i, f: (f, 0)),
                  pl.BlockSpec((1, d), lambda i, f: (0, 0))],
        out_specs=pl.BlockSpec((ts, d), lambda i, f: (i, 0)),
        out_shape=jax.ShapeDtypeStruct((s, d), F32),
        scratch_shapes=[pltpu.VMEM((ts, d), BF16), pltpu.VMEM((ts, d), F32)],
        compiler_params=_params("arbitrary", "arbitrary"),
        name="mlp_relu2",
    )(h, gpre, w1, w2, gpost)


def _band_strip(bias_heads, window):
    nk = window + QBLOCK
    dist = jnp.arange(QBLOCK)[None, :] - jnp.arange(nk)[:, None] + window
    ok = (dist >= 0) & (dist < window)
    tab = bias_heads[_t5_bucket(dist)]
    tab = jnp.where(ok[:, :, None, None], tab, -MASK_BIG)
    return tab.transpose(2, 0, 3, 1).reshape(GROUPS, nk, HQ)


def _sel_strip(bias_heads, far, tk):
    rows = 3 * tk - QBLOCK
    dist = jnp.arange(QBLOCK)[None, :] - jnp.arange(rows)[:, None] + (2 * tk - QBLOCK)
    tab = bias_heads[_t5_bucket(dist)] - far[None, None]
    tab = jnp.where((dist >= 0)[:, :, None, None], tab, -MASK_BIG)
    return tab.transpose(2, 3, 0, 1)


def kernel(x, norm_mix_pre, norm_mix_post, norm_mlp_pre, norm_mlp_post, w_in,
           cmp_pos_k, cmp_w1_k, cmp_w2_k, cmp_pos_v, cmp_w1_v, cmp_w2_v,
           attn_sinks, rel_bias, w_up_nsa, w_up_swa, w_out, w_mlp_in, w_mlp_out):
    b, s, d = x.shape
    assert b == 1 and s % SEL_TILE == 0 and w_in.shape[0] == 1
    qd = HEADS * HEAD_DIM
    kvd = GROUPS * HEAD_DIM
    sizes = (qd,) + (kvd,) * 6 + (3 * HEADS, qd, kvd, kvd, d, d)
    offs = [0]
    for z in sizes:
        offs.append(offs[-1] + z)
    w = w_in[0]
    col = lambda i: w[:, offs[i]:offs[i + 1]]
    (w_qn, w_kc, w_vc, w_ks, w_vs, w_kw, w_vw, w_gn, w_qs, w_k_s, w_v_s, w_ga, w_gb) = [col(i) for i in range(13)]
    scale = HEAD_DIM ** -0.5
    w_gn = w_gn.reshape(d, HEADS, 3).transpose(0, 2, 1).reshape(d, 3 * HEADS)
    w_gn = jnp.pad(w_gn, ((0, 0), (0, 8)))
    wn = jnp.concatenate([w_kc, w_vc, w_ks, w_kw, w_k_s], axis=1).astype(BF16)
    wg = jnp.concatenate([w_ga, w_gb], axis=1).astype(BF16)
    wt = jnp.concatenate([w_qn * scale, w_qs * scale, w_vs, w_vw, w_v_s, w_gn], axis=1).T.astype(BF16)
    nqv = 2 * qd + 3 * kvd

    x2 = x[0]
    kv, gab, qv, gn = _project(x2, norm_mix_pre, wn, wg, wt, nqv, ts=512)
    q_nsa = qv[0:qd].reshape(GROUPS, REP * HEAD_DIM, s)
    q_swa = qv[qd:2 * qd].reshape(GROUPS, REP * HEAD_DIM, s)
    vs_t = qv[2 * qd:2 * qd + kvd].reshape(GROUPS, HEAD_DIM, s)
    vw_t = qv[2 * qd + kvd:2 * qd + 2 * kvd].reshape(GROUPS, HEAD_DIM, s)
    vswa_t = qv[2 * qd + 2 * kvd:].reshape(GROUPS, HEAD_DIM, s)

    n16 = s // CMP_STRIDE
    kb = kv[:, :2 * kvd].reshape(n16, CMP_STRIDE, 2, GROUPS, HEAD_DIM).transpose(2, 3, 0, 1, 4)
    kb = kb.reshape(2, GROUPS, n16, CMP_STRIDE * HEAD_DIM)
    pos8 = jnp.stack([cmp_pos_k[0], cmp_pos_v[0]]).reshape(2, 1, CMP_BLOCK * HEAD_DIM)
    pos8 = jnp.broadcast_to(pos8, (2, 8, CMP_BLOCK * HEAD_DIM)).astype(BF16)
    w1 = jnp.stack([cmp_w1_k[0], cmp_w1_v[0]]).astype(BF16)
    w2 = jnp.stack([cmp_w2_k[0], cmp_w2_v[0]]).astype(BF16)
    cmp_n, cmp_t = _compress(kb, pos8, w1, w2, w2.transpose(0, 2, 1))

    oc, sel = _cmp_select(q_nsa, cmp_n, cmp_t, s)

    bias_nsa = rel_bias[:, :HEADS].reshape(NUM_BUCKETS, GROUPS, REP)
    bias_swa = rel_bias[:, HEADS:].reshape(NUM_BUCKETS, GROUPS, REP)

    tk = SEL_TILE
    nb = tk // SEL_BLOCK
    posn = jnp.arange(s)
    onehot = ((posn // SEL_BLOCK) % nb)[:, None] == jnp.arange(nb)[None, :]
    aug = jnp.concatenate([onehot.astype(BF16), jnp.ones((s, 2), BF16),
                           jnp.zeros((s, kvd - nb - 2), BF16)], axis=1)
    kaug = jnp.concatenate([kv[:, 2 * kvd:3 * kvd], aug], axis=1)
    far = bias_nsa[NUM_BUCKETS - 1]
    far_hi = far.astype(BF16)
    far_lo = (far - far_hi.astype(F32)).astype(BF16)
    cst = jnp.stack([far_hi, far_lo], axis=1)
    cst = jnp.broadcast_to(cst[:, :, :, None], (GROUPS, 2, REP, QBLOCK)).reshape(GROUPS, 2, HQ)
    cst = jnp.pad(cst, ((0, 0), (0, 14), (0, 0)))
    o_sel = _selected(q_nsa, sel, kaug, vs_t, _sel_strip(bias_nsa, far, tk), cst, s, tk)

    pad_rows = lambda a, wdw: jnp.pad(a, ((wdw, 0), (0, 0)))
    pad_lanes = lambda a, wdw: jnp.pad(a, ((0, 0), (0, 0), (wdw, 0)))
    no_sink = jnp.zeros((GROUPS, 1, HQ), F32)
    o_win = _banded(q_nsa, pad_rows(kv[:, 3 * kvd:4 * kvd], NSA_WINDOW), pad_lanes(vw_t, NSA_WINDOW),
                    _band_strip(bias_nsa, NSA_WINDOW), no_sink, s, NSA_WINDOW, False, "nsa_window")
    sink = attn_sinks[0].reshape(GROUPS, 1, REP, 1).astype(F32)
    sink = jnp.broadcast_to(sink, (GROUPS, 1, REP, QBLOCK)).reshape(GROUPS, 1, HQ)
    o_swa = _banded(q_swa, pad_rows(kv[:, 4 * kvd:5 * kvd], SWA_WINDOW), pad_lanes(vswa_t, SWA_WINDOW),
                    _band_strip(bias_swa, SWA_WINDOW), sink, s, SWA_WINDOW, True, "swa_sink")

    flat = lambda a: a.reshape(qd, s)
    h1 = _merge(flat(oc), flat(o_sel), flat(o_win), flat(o_swa), gn, gab, x2,
                w_up_nsa[0].astype(BF16), w_up_swa[0].astype(BF16), w_out[0].astype(BF16),
                norm_mix_post, ts=512)
    out = _mlp(h1, norm_mlp_pre, w_mlp_in[0].astype(BF16), w_mlp_out[0].astype(BF16), norm_mlp_post,
               ts=1024, tf=1024)
    return out[None]
```

```python
import functools
import math

import jax
import jax.numpy as jnp
from jax import lax
from jax.experimental import pallas as pl
from jax.experimental.pallas import tpu as pltpu

F32 = jnp.float32
BF16 = jnp.bfloat16

HEAD_DIM = 64
GROUPS = 2
REP = 4
HEADS = GROUPS * REP
CMP_BLOCK = 32
CMP_STRIDE = 16
SEL_BLOCK = 64
SEL_TOPK = 16
NSA_WINDOW = 512
SWA_WINDOW = 128
QBLOCK = 128
NUM_BUCKETS = 32
MAX_DISTANCE = 1024
RMS_EPS = 1e-6
MASK_BIG = 2.0 ** 99
SEL_TILE = 1024
HQ = REP * QBLOCK
VMEM_LIMIT = 56 * 1024 * 1024

_NT = (((1,), (1,)), ((), ()))
_TN = (((0,), (0,)), ((), ()))


def _params(*sem):
    return pltpu.CompilerParams(dimension_semantics=sem, vmem_limit_bytes=VMEM_LIMIT)


def _t5_bucket(dist):
    max_exact = NUM_BUCKETS // 2
    d = jnp.maximum(dist, 0)
    df = jnp.maximum(d, 1).astype(jnp.float32)
    large = max_exact + (jnp.log(df / max_exact) / math.log(MAX_DISTANCE / max_exact)
                         * (NUM_BUCKETS - max_exact)).astype(jnp.int32)
    large = jnp.minimum(large, NUM_BUCKETS - 1)
    return jnp.where(d < max_exact, d, large)


def _rms(x, gain):
    return x * lax.rsqrt(jnp.mean(x * x, axis=-1, keepdims=True) + RMS_EPS) * gain


def _proj_kernel(x_ref, g_ref, wn_ref, wg_ref, wt_ref, kv_ref, gab_ref, qv_ref, gn_ref, *, nqv):
    u = _rms(x_ref[...], g_ref[...]).astype(BF16)
    kv_ref[...] = jnp.dot(u, wn_ref[...], preferred_element_type=F32).astype(BF16)
    gab_ref[...] = jnp.dot(u, wg_ref[...], preferred_element_type=F32)
    t = lax.dot_general(wt_ref[...], u, _NT, preferred_element_type=F32)
    qv_ref[...] = t[:nqv].astype(BF16)
    gn_ref[...] = t[nqv:]


def _project(x, gain, wn, wg, wt, nqv, ts):
    s, d = x.shape
    nn, ng, ntr = wn.shape[1], wg.shape[1], wt.shape[0]
    full = lambda shp: pl.BlockSpec(shp, lambda i: (0, 0))
    return pl.pallas_call(
        functools.partial(_proj_kernel, nqv=nqv),
        grid=(s // ts,),
        in_specs=[pl.BlockSpec((ts, d), lambda i: (i, 0)), full((1, d)),
                  full((d, nn)), full((d, ng)), full((ntr, d))],
        out_specs=[pl.BlockSpec((ts, nn), lambda i: (i, 0)),
                   pl.BlockSpec((ts, ng), lambda i: (i, 0)),
                   pl.BlockSpec((nqv, ts), lambda i: (0, i)),
                   pl.BlockSpec((ntr - nqv, ts), lambda i: (0, i))],
        out_shape=[jax.ShapeDtypeStruct((s, nn), BF16), jax.ShapeDtypeStruct((s, ng), F32),
                   jax.ShapeDtypeStruct((nqv, s), BF16), jax.ShapeDtypeStruct((ntr - nqv, s), F32)],
        compiler_params=_params("arbitrary"),
        name="in_proj",
    )(x, gain, wn, wg, wt)


def _gelu_tanh(x):
    return x * (0.5 * (1.0 + jnp.tanh(math.sqrt(2.0 / math.pi) * (x + 0.044715 * (x * x * x)))))


def _compress_kernel(kb_ref, pos_ref, w1_ref, w2_ref, w2t_ref, o_ref, ot_ref):
    kb = kb_ref[0, 0]
    half = kb.shape[1]
    n16 = kb.shape[0]
    first = jnp.dot(kb, w1_ref[0, :half, :], preferred_element_type=F32)
    second = jnp.dot(kb, w1_ref[0, half:, :], preferred_element_type=F32)
    posb = jnp.dot(pos_ref[0], w1_ref[0], preferred_element_type=F32)[0:1]
    pre = first + pltpu.roll(second, n16 - 1, 0) + posb
    h = _gelu_tanh(pre).astype(BF16)
    o_ref[0, 0] = jnp.dot(h, w2_ref[0], preferred_element_type=F32).astype(BF16)
    ot_ref[0, 0] = lax.dot_general(w2t_ref[0], h, _NT, preferred_element_type=F32).astype(BF16)


def _compress(kb, pos8, w1, w2, w2t):
    _, g, n16, cin2 = kb.shape
    hid = w1.shape[2]
    return pl.pallas_call(
        _compress_kernel,
        grid=(2, g),
        in_specs=[pl.BlockSpec((1, 1, n16, cin2), lambda a, b: (a, b, 0, 0)),
                  pl.BlockSpec((1, 8, 2 * cin2), lambda a, b: (a, 0, 0)),
                  pl.BlockSpec((1, 2 * cin2, hid), lambda a, b: (a, 0, 0)),
                  pl.BlockSpec((1, hid, HEAD_DIM), lambda a, b: (a, 0, 0)),
                  pl.BlockSpec((1, HEAD_DIM, hid), lambda a, b: (a, 0, 0))],
        out_specs=[pl.BlockSpec((1, 1, n16, HEAD_DIM), lambda a, b: (a, b, 0, 0)),
                   pl.BlockSpec((1, 1, HEAD_DIM, n16), lambda a, b: (a, b, 0, 0))],
        out_shape=[jax.ShapeDtypeStruct((2, g, n16, HEAD_DIM), BF16),
                   jax.ShapeDtypeStruct((2, g, HEAD_DIM, n16), BF16)],
        compiler_params=_params("arbitrary", "arbitrary"),
        name="kv_compress",
    )(kb, pos8, w1, w2, w2t)


def _heads_to_lanes(q_ref):
    return jnp.concatenate([q_ref[0, r * HEAD_DIM:(r + 1) * HEAD_DIM, :] for r in range(REP)], axis=1)


def _store_heads(o_ref, o_t):
    for r in range(REP):
        o_ref[0, r * HEAD_DIM:(r + 1) * HEAD_DIM, :] = o_t[:, r * QBLOCK:(r + 1) * QBLOCK]


def _cmpsel_kernel(q_ref, kc_ref, vct_ref, oc_ref, sel_ref, pg_ref, *, n16, nsel, topk):
    t0 = pl.program_id(1) * QBLOCK
    q4 = _heads_to_lanes(q_ref)
    s = jnp.dot(kc_ref[0, 0], q4, preferred_element_type=F32)
    n_io = lax.broadcasted_iota(jnp.int32, (n16, HQ), 0)
    t_io = t0 + (lax.broadcasted_iota(jnp.int32, (n16, HQ), 1) & (QBLOCK - 1))
    s = jnp.where(n_io * CMP_STRIDE + (CMP_BLOCK - 1) <= t_io, s, -jnp.inf)
    m = jnp.max(s, axis=0, keepdims=True)
    m = jnp.where(m == -jnp.inf, 0.0, m)
    e = jnp.exp(s - m)
    den = jnp.maximum(jnp.sum(e, axis=0, keepdims=True), 1e-30)
    p = e * (1.0 / den)
    _store_heads(oc_ref, jnp.dot(vct_ref[0, 0], p.astype(BF16), preferred_element_type=F32))

    pg = ((p[:, 0:QBLOCK] + p[:, QBLOCK:2 * QBLOCK]) + p[:, 2 * QBLOCK:3 * QBLOCK]) + p[:, 3 * QBLOCK:]
    pg_ref[0:8, :] = jnp.zeros((8, QBLOCK), F32)
    pg_ref[8:8 + n16, :] = pg
    ratio = SEL_BLOCK // CMP_STRIDE
    ps = pg_ref[pl.ds(7, nsel, stride=ratio), :]
    for o in range(1, ratio + 1):
        ps = ps + pg_ref[pl.ds(7 + o, nsel, stride=ratio), :]

    blk = lax.broadcasted_iota(jnp.int32, (nsel, QBLOCK), 0)
    cur = (t0 + lax.broadcasted_iota(jnp.int32, (nsel, QBLOCK), 1)) // SEL_BLOCK
    valid = blk <= cur
    forced = (blk == 0) | (blk == cur) | (blk == cur - 1)
    score = jnp.where(valid, jnp.where(forced, jnp.inf, ps), -jnp.inf)

    def pick_one(_, carry):
        rem, chosen = carry
        top = jnp.max(rem, axis=0, keepdims=True)
        first = jnp.min(jnp.where(rem == top, blk, nsel), axis=0, keepdims=True)
        hit = blk == first
        return jnp.where(hit, -jnp.inf, rem), jnp.where(hit, 1.0, chosen)

    _, chosen = lax.fori_loop(0, topk, pick_one, (score, jnp.zeros((nsel, QBLOCK), F32)))
    sel_ref[0] = jnp.where((chosen > 0.0) & valid, 0.0, -MASK_BIG).astype(BF16)


def _cmp_select(q, kc_all, vct_all, s):
    n16 = s // CMP_STRIDE
    nsel = s // SEL_BLOCK
    topk = min(SEL_TOPK, nsel)
    return pl.pallas_call(
        functools.partial(_cmpsel_kernel, n16=n16, nsel=nsel, topk=topk),
        grid=(GROUPS, s // QBLOCK),
        in_specs=[pl.BlockSpec((1, REP * HEAD_DIM, QBLOCK), lambda g, c: (g, 0, c)),
                  pl.BlockSpec((1, 1, n16, HEAD_DIM), lambda g, c: (0, g, 0, 0)),
                  pl.BlockSpec((1, 1, HEAD_DIM, n16), lambda g, c: (1, g, 0, 0))],
        out_specs=[pl.BlockSpec((1, REP * HEAD_DIM, QBLOCK), lambda g, c: (g, 0, c)),
                   pl.BlockSpec((1, nsel, QBLOCK), lambda g, c: (g, 0, c))],
        out_shape=[jax.ShapeDtypeStruct((GROUPS, REP * HEAD_DIM, s), F32),
                   jax.ShapeDtypeStruct((GROUPS, nsel, s), BF16)],
        scratch_shapes=[pltpu.VMEM((n16 + 8, QBLOCK), F32)],
        compiler_params=_params("arbitrary", "arbitrary"),
        name="cmp_select",
    )(q, kc_all, vct_all)


def _sel_kernel(q_ref, sel_ref, kaug_ref, vt_ref, strip_ref, cst_ref, o_ref,
                qa_ref, s_ref, m_ref, l_ref, acc_ref, *, tk):
    g = pl.program_id(0)
    ci = pl.program_id(1)
    nb = tk // SEL_BLOCK
    cpt = tk // QBLOCK
    kd = GROUPS * HEAD_DIM
    qa_ref[...] = jnp.zeros(qa_ref.shape, BF16)
    row0 = pl.multiple_of(g * HEAD_DIM, HEAD_DIM)
    for r in range(REP):
        qa_ref[pl.ds(row0, HEAD_DIM), r * QBLOCK:(r + 1) * QBLOCK] = q_ref[0, r * HEAD_DIM:(r + 1) * HEAD_DIM, :]
    qa_ref[kd + nb:kd + nb + 16, :] = cst_ref[0]
    m_ref[...] = jnp.full(m_ref.shape, -jnp.inf, F32)
    l_ref[...] = jnp.zeros(l_ref.shape, F32)
    acc_ref[...] = jnp.zeros(acc_ref.shape, F32)
    nt = ci // cpt + 1

    def tile(j, carry):
        sb = sel_ref[0, pl.ds(pl.multiple_of(j * nb, nb), nb), :]
        for r in range(REP):
            qa_ref[kd:kd + nb, r * QBLOCK:(r + 1) * QBLOCK] = sb
        k0 = pl.multiple_of(j * tk, tk)
        s_ref[...] = jnp.dot(kaug_ref[pl.ds(k0, tk), :], qa_ref[...], preferred_element_type=F32)

        @pl.when(j >= nt - 2)
        def _():
            off = pl.multiple_of((2 * tk - QBLOCK) - (ci * QBLOCK - j * tk), QBLOCK)
            for r in range(REP):
                s_ref[:, r * QBLOCK:(r + 1) * QBLOCK] += strip_ref[0, r, pl.ds(off, tk), :]

        s = s_ref[...]
        m_prev = m_ref[...]
        m_new = jnp.maximum(m_prev, jnp.max(s, axis=0, keepdims=True))
        alpha = jnp.exp(m_prev - m_new)
        p = jnp.exp(s - m_new)
        l_ref[...] = alpha * l_ref[...] + jnp.sum(p, axis=0, keepdims=True)
        acc_ref[...] = alpha * acc_ref[...] + jnp.dot(vt_ref[0, :, pl.ds(k0, tk)], p.astype(BF16),
                                                      preferred_element_type=F32)
        m_ref[...] = m_new
        return carry

    lax.fori_loop(0, nt, tile, 0)
    _store_heads(o_ref, acc_ref[...] * (1.0 / jnp.maximum(l_ref[...], 1e-30)))


def _selected(q, sel, kaug, vt, strip, cst, s, tk):
    nsel = s // SEL_BLOCK
    nb = tk // SEL_BLOCK
    sl = strip.shape[2]
    return pl.pallas_call(
        functools.partial(_sel_kernel, tk=tk),
        grid=(GROUPS, s // QBLOCK),
        in_specs=[pl.BlockSpec((1, REP * HEAD_DIM, QBLOCK), lambda g, c: (g, 0, c)),
                  pl.BlockSpec((1, nsel, QBLOCK), lambda g, c: (g, 0, c)),
                  pl.BlockSpec(kaug.shape, lambda g, c: (0, 0)),
                  pl.BlockSpec((1, HEAD_DIM, s), lambda g, c: (g, 0, 0)),
                  pl.BlockSpec((1, REP, sl, QBLOCK), lambda g, c: (g, 0, 0, 0)),
                  pl.BlockSpec((1, 16, HQ), lambda g, c: (g, 0, 0))],
        out_specs=pl.BlockSpec((1, REP * HEAD_DIM, QBLOCK), lambda g, c: (g, 0, c)),
        out_shape=jax.ShapeDtypeStruct((GROUPS, REP * HEAD_DIM, s), F32),
        scratch_shapes=[pltpu.VMEM((kaug.shape[1], HQ), BF16), pltpu.VMEM((tk, HQ), F32),
                        pltpu.VMEM((1, HQ), F32), pltpu.VMEM((1, HQ), F32), pltpu.VMEM((HEAD_DIM, HQ), F32)],
        compiler_params=_params("arbitrary", "arbitrary"),
        name="selected_attn",
    )(q, sel, kaug, vt, strip, cst)


def _band_kernel(q_ref, k_ref, vt_ref, strip_ref, sink_ref, o_ref, qa_ref, *, window, use_sink):
    g = pl.program_id(0)
    t0 = pl.multiple_of(pl.program_id(1) * QBLOCK, QBLOCK)
    nk = window + QBLOCK
    qa_ref[...] = jnp.zeros(qa_ref.shape, BF16)
    row0 = pl.multiple_of(g * HEAD_DIM, HEAD_DIM)
    for r in range(REP):
        qa_ref[pl.ds(row0, HEAD_DIM), r * QBLOCK:(r + 1) * QBLOCK] = q_ref[0, r * HEAD_DIM:(r + 1) * HEAD_DIM, :]
    s = jnp.dot(k_ref[pl.ds(t0, nk), :], qa_ref[...], preferred_element_type=F32) + strip_ref[0]
    row = lax.broadcasted_iota(jnp.int32, (nk, HQ), 0)
    s = jnp.where(row + (t0 - window) >= 0, s, -MASK_BIG)
    m = jnp.max(s, axis=0, keepdims=True)
    if use_sink:
        m = jnp.maximum(m, sink_ref[0])
    e = jnp.exp(s - m)
    den = jnp.sum(e, axis=0, keepdims=True)
    if use_sink:
        den = den + jnp.exp(sink_ref[0] - m)
    else:
        den = jnp.maximum(den, 1e-30)
    o_t = jnp.dot(vt_ref[0, :, pl.ds(t0, nk)], e.astype(BF16), preferred_element_type=F32)
    _store_heads(o_ref, o_t * (1.0 / den))


def _banded(q, kpad, vtpad, strip, sink, s, window, use_sink, name):
    nk = window + QBLOCK
    return pl.pallas_call(
        functools.partial(_band_kernel, window=window, use_sink=use_sink),
        grid=(GROUPS, s // QBLOCK),
        in_specs=[pl.BlockSpec((1, REP * HEAD_DIM, QBLOCK), lambda g, c: (g, 0, c)),
                  pl.BlockSpec(kpad.shape, lambda g, c: (0, 0)),
                  pl.BlockSpec((1, HEAD_DIM, s + window), lambda g, c: (g, 0, 0)),
                  pl.BlockSpec((1, nk, HQ), lambda g, c: (g, 0, 0)),
                  pl.BlockSpec((1, 1, HQ), lambda g, c: (g, 0, 0))],
        out_specs=pl.BlockSpec((1, REP * HEAD_DIM, QBLOCK), lambda g, c: (g, 0, c)),
        out_shape=jax.ShapeDtypeStruct((GROUPS, REP * HEAD_DIM, s), F32),
        scratch_shapes=[pltpu.VMEM((GROUPS * HEAD_DIM, HQ), BF16)],
        compiler_params=_params("arbitrary", "arbitrary"),
        name=name,
    )(q, kpad, vtpad, strip, sink)


def _merge_kernel(oc_ref, os_ref, ow_ref, ob_ref, gn_ref, gab_ref, x_ref, wa_ref, wb_ref, wo_ref, gp_ref, o_ref):
    d = x_ref.shape[1]
    gn = jax.nn.sigmoid(gn_ref[...])
    parts = []
    for h in range(HEADS):
        rows = slice(h * HEAD_DIM, (h + 1) * HEAD_DIM)
        parts.append(gn[h:h + 1] * oc_ref[rows, :] + gn[HEADS + h:HEADS + h + 1] * os_ref[rows, :]
                     + gn[2 * HEADS + h:2 * HEADS + h + 1] * ow_ref[rows, :])
    oa_t = jnp.concatenate(parts, axis=0).astype(BF16)
    ya = lax.dot_general(oa_t, wa_ref[...], _TN, preferred_element_type=F32)
    yb = lax.dot_general(ob_ref[...].astype(BF16), wb_ref[...], _TN, preferred_element_type=F32)
    y = jax.nn.sigmoid(gab_ref[:, :d]) * ya + jax.nn.sigmoid(gab_ref[:, d:]) * yb
    z = jnp.dot(y.astype(BF16), wo_ref[...], preferred_element_type=F32)
    o_ref[...] = x_ref[...] + _rms(z, gp_ref[...])


def _merge(oc, os_, ow, ob, gn, gab, x, wa, wb, wo, gpost, ts):
    s, d = x.shape
    hd = oc.shape[0]
    tok = lambda rows: pl.BlockSpec((rows, ts), lambda i: (0, i))
    full = lambda shp: pl.BlockSpec(shp, lambda i: (0, 0))
    return pl.pallas_call(
        _merge_kernel,
        grid=(s // ts,),
        in_specs=[tok(hd), tok(hd), tok(hd), tok(hd), tok(gn.shape[0]),
                  pl.BlockSpec((ts, 2 * d), lambda i: (i, 0)), pl.BlockSpec((ts, d), lambda i: (i, 0)),
                  full((hd, d)), full((hd, d)), full((d, d)), full((1, d))],
        out_specs=pl.BlockSpec((ts, d), lambda i: (i, 0)),
        out_shape=jax.ShapeDtypeStruct((s, d), F32),
        compiler_params=_params("arbitrary"),
        name="merge_out_proj",
    )(oc, os_, ow, ob, gn, gab, x, wa, wb, wo, gpost)


def _mlp_kernel(h_ref, gpre_ref, w1_ref, w2_ref, gpost_ref, o_ref, u_ref, acc_ref):
    f = pl.program_id(1)

    @pl.when(f == 0)
    def _():
        u_ref[...] = _rms(h_ref[...], gpre_ref[...]).astype(BF16)
        acc_ref[...] = jnp.zeros(acc_ref.shape, F32)

    a = jnp.maximum(jnp.dot(u_ref[...], w1_ref[...], preferred_element_type=F32), 0.0)
    acc_ref[...] += jnp.dot((a * a).astype(BF16), w2_ref[...], preferred_element_type=F32)

    @pl.when(f == pl.num_programs(1) - 1)
    def _():
        o_ref[...] = h_ref[...] + _rms(acc_ref[...], gpost_ref[...])


def _mlp(h, gpre, w1, w2, gpost, ts, tf):
    s, d = h.shape
    dff = w1.shape[1]
    return pl.pallas_call(
        _mlp_kernel,
        grid=(s // ts, dff // tf),
        in_specs=[pl.BlockSpec((ts, d), lambda i, f: (i, 0)), pl.BlockSpec((1, d), lambda i, f: (0, 0)),
                  pl.BlockSpec((d, tf), lambda i, f: (0, f)), pl.BlockSpec((tf, d), lambda i, f: (f, 0)),
                  pl.BlockSpec((1, d), lambda i, f: (0, 0))],
        out_specs=pl.BlockSpec((ts, d), lambda i, f: (i, 0)),
        out_shape=jax.ShapeDtypeStruct((s, d), F32),
        scratch_shapes=[pltpu.VMEM((ts, d), BF16), pltpu.VMEM((ts, d), F32)],
        compiler_params=_params("arbitrary", "arbitrary"),
        name="mlp_relu2",
    )(h, gpre, w1, w2, gpost)


def _bias_of(dist, bias_heads):
    bucket = _t5_bucket(dist)[..., None, None]
    out = jnp.zeros(dist.shape + bias_heads.shape[1:], F32)
    for b in range(NUM_BUCKETS):
        out = jnp.where(bucket == b, bias_heads[b], out)
    return out


def _band_strip(bias_heads, window):
    nk = window + QBLOCK
    dist = jnp.arange(QBLOCK)[None, :] - jnp.arange(nk)[:, None] + window
    ok = (dist >= 0) & (dist < window)
    tab = _bias_of(dist, bias_heads)
    tab = jnp.where(ok[:, :, None, None], tab, -MASK_BIG)
    return tab.transpose(2, 0, 3, 1).reshape(GROUPS, nk, HQ)


def _sel_strip(bias_heads, far, tk):
    rows = 3 * tk - QBLOCK
    dist = jnp.arange(QBLOCK)[None, :] - jnp.arange(rows)[:, None] + (2 * tk - QBLOCK)
    tab = _bias_of(dist, bias_heads) - far[None, None]
    tab = jnp.where((dist >= 0)[:, :, None, None], tab, -MASK_BIG)
    return tab.transpose(2, 3, 0, 1)


def kernel(x, norm_mix_pre, norm_mix_post, norm_mlp_pre, norm_mlp_post, w_in,
           cmp_pos_k, cmp_w1_k, cmp_w2_k, cmp_pos_v, cmp_w1_v, cmp_w2_v,
           attn_sinks, rel_bias, w_up_nsa, w_up_swa, w_out, w_mlp_in, w_mlp_out):
    b, s, d = x.shape
    assert b == 1 and s % SEL_TILE == 0 and w_in.shape[0] == 1
    qd = HEADS * HEAD_DIM
    kvd = GROUPS * HEAD_DIM
    sizes = (qd,) + (kvd,) * 6 + (3 * HEADS, qd, kvd, kvd, d, d)
    offs = [0]
    for z in sizes:
        offs.append(offs[-1] + z)
    w = w_in[0]
    col = lambda i: w[:, offs[i]:offs[i + 1]]
    (w_qn, w_kc, w_vc, w_ks, w_vs, w_kw, w_vw, w_gn, w_qs, w_k_s, w_v_s, w_ga, w_gb) = [col(i) for i in range(13)]
    scale = HEAD_DIM ** -0.5
    w_gn = w_gn.reshape(d, HEADS, 3).transpose(0, 2, 1).reshape(d, 3 * HEADS)
    w_gn = jnp.pad(w_gn, ((0, 0), (0, 8)))
    wn = jnp.concatenate([w_kc, w_vc, w_ks, w_kw, w_k_s], axis=1).astype(BF16)
    wg = jnp.concatenate([w_ga, w_gb], axis=1).astype(BF16)
    wt = jnp.concatenate([w_qn * scale, w_qs * scale, w_vs, w_vw, w_v_s, w_gn], axis=1).T.astype(BF16)
    nqv = 2 * qd + 3 * kvd

    x2 = x[0]
    kv, gab, qv, gn = _project(x2, norm_mix_pre, wn, wg, wt, nqv, ts=512)
    q_nsa = qv[0:qd].reshape(GROUPS, REP * HEAD_DIM, s)
    q_swa = qv[qd:2 * qd].reshape(GROUPS, REP * HEAD_DIM, s)
    vs_t = qv[2 * qd:2 * qd + kvd].reshape(GROUPS, HEAD_DIM, s)
    vw_t = qv[2 * qd + kvd:2 * qd + 2 * kvd].reshape(GROUPS, HEAD_DIM, s)
    vswa_t = qv[2 * qd + 2 * kvd:].reshape(GROUPS, HEAD_DIM, s)

    n16 = s // CMP_STRIDE
    kb = kv[:, :2 * kvd].reshape(n16, CMP_STRIDE, 2, GROUPS, HEAD_DIM).transpose(2, 3, 0, 1, 4)
    kb = kb.reshape(2, GROUPS, n16, CMP_STRIDE * HEAD_DIM)
    pos8 = jnp.stack([cmp_pos_k[0], cmp_pos_v[0]]).reshape(2, 1, CMP_BLOCK * HEAD_DIM)
    pos8 = jnp.broadcast_to(pos8, (2, 8, CMP_BLOCK * HEAD_DIM)).astype(BF16)
    w1 = jnp.stack([cmp_w1_k[0], cmp_w1_v[0]]).astype(BF16)
    w2 = jnp.stack([cmp_w2_k[0], cmp_w2_v[0]]).astype(BF16)
    cmp_n, cmp_t = _compress(kb, pos8, w1, w2, w2.transpose(0, 2, 1))

    oc, sel = _cmp_select(q_nsa, cmp_n, cmp_t, s)

    bias_nsa = rel_bias[:, :HEADS].reshape(NUM_BUCKETS, GROUPS, REP)
    bias_swa = rel_bias[:, HEADS:].reshape(NUM_BUCKETS, GROUPS, REP)

    tk = SEL_TILE
    nb = tk // SEL_BLOCK
    posn = jnp.arange(s)
    onehot = ((posn // SEL_BLOCK) % nb)[:, None] == jnp.arange(nb)[None, :]
    aug = jnp.concatenate([onehot.astype(BF16), jnp.ones((s, 2), BF16),
                           jnp.zeros((s, kvd - nb - 2), BF16)], axis=1)
    kaug = jnp.concatenate([kv[:, 2 * kvd:3 * kvd], aug], axis=1)
    far = bias_nsa[NUM_BUCKETS - 1]
    far_hi = far.astype(BF16)
    far_lo = (far - far_hi.astype(F32)).astype(BF16)
    cst = jnp.stack([far_hi, far_lo], axis=1)
    cst = jnp.broadcast_to(cst[:, :, :, None], (GROUPS, 2, REP, QBLOCK)).reshape(GROUPS, 2, HQ)
    cst = jnp.pad(cst, ((0, 0), (0, 14), (0, 0)))
    o_sel = _selected(q_nsa, sel, kaug, vs_t, _sel_strip(bias_nsa, far, tk), cst, s, tk)

    pad_rows = lambda a, wdw: jnp.pad(a, ((wdw, 0), (0, 0)))
    pad_lanes = lambda a, wdw: jnp.pad(a, ((0, 0), (0, 0), (wdw, 0)))
    no_sink = jnp.zeros((GROUPS, 1, HQ), F32)
    o_win = _banded(q_nsa, pad_rows(kv[:, 3 * kvd:4 * kvd], NSA_WINDOW), pad_lanes(vw_t, NSA_WINDOW),
                    _band_strip(bias_nsa, NSA_WINDOW), no_sink, s, NSA_WINDOW, False, "nsa_window")
    sink = attn_sinks[0].reshape(GROUPS, 1, REP, 1).astype(F32)
    sink = jnp.broadcast_to(sink, (GROUPS, 1, REP, QBLOCK)).reshape(GROUPS, 1, HQ)
    o_swa = _banded(q_swa, pad_rows(kv[:, 4 * kvd:5 * kvd], SWA_WINDOW), pad_lanes(vswa_t, SWA_WINDOW),
                    _band_strip(bias_swa, SWA_WINDOW), sink, s, SWA_WINDOW, True, "swa_sink")

    flat = lambda a: a.reshape(qd, s)
    h1 = _merge(flat(oc), flat(o_sel), flat(o_win), flat(o_swa), gn, gab, x2,
                w_up_nsa[0].astype(BF16), w_up_swa[0].astype(BF16), w_out[0].astype(BF16),
                norm_mix_post, ts=512)
    out = _mlp(h1, norm_mlp_pre, w_mlp_in[0].astype(BF16), w_mlp_out[0].astype(BF16), norm_mlp_post,
               ts=1024, tf=1024)
    return out[None]
```

```python
import functools
import math

import jax
import jax.numpy as jnp
from jax import lax
from jax.experimental import pallas as pl
from jax.experimental.pallas import tpu as pltpu

F32 = jnp.float32
BF16 = jnp.bfloat16

HEAD_DIM = 64
GROUPS = 2
REP = 4
HEADS = GROUPS * REP
CMP_BLOCK = 32
CMP_STRIDE = 16
SEL_BLOCK = 64
SEL_TOPK = 16
NSA_WINDOW = 512
SWA_WINDOW = 128
QBLOCK = 128
NUM_BUCKETS = 32
MAX_DISTANCE = 1024
RMS_EPS = 1e-6
MASK_BIG = 2.0 ** 99
SEL_TILE = 1024
HQ = REP * QBLOCK
VMEM_LIMIT = 56 * 1024 * 1024

_NT = (((1,), (1,)), ((), ()))
_TN = (((0,), (0,)), ((), ()))


def _params(*sem):
    return pltpu.CompilerParams(dimension_semantics=sem, vmem_limit_bytes=VMEM_LIMIT)


def _t5_bucket(dist):
    max_exact = NUM_BUCKETS // 2
    d = jnp.maximum(dist, 0)
    df = jnp.maximum(d, 1).astype(jnp.float32)
    large = max_exact + (jnp.log(df / max_exact) / math.log(MAX_DISTANCE / max_exact)
                         * (NUM_BUCKETS - max_exact)).astype(jnp.int32)
    large = jnp.minimum(large, NUM_BUCKETS - 1)
    return jnp.where(d < max_exact, d, large)


def _rms(x, gain):
    return x * lax.rsqrt(jnp.mean(x * x, axis=-1, keepdims=True) + RMS_EPS) * gain


def _proj_kernel(x_ref, g_ref, wn_ref, wg_ref, wt_ref, kv_ref, gab_ref, qv_ref, gn_ref, *, nqv):
    u = _rms(x_ref[...], g_ref[...]).astype(BF16)
    kv_ref[...] = jnp.dot(u, wn_ref[...], preferred_element_type=F32).astype(BF16)
    gab_ref[...] = jnp.dot(u, wg_ref[...], preferred_element_type=F32)
    t = lax.dot_general(wt_ref[...], u, _NT, preferred_element_type=F32)
    qv_ref[...] = t[:nqv].astype(BF16)
    gn_ref[...] = t[nqv:]


def _project(x, gain, wn, wg, wt, nqv, ts):
    s, d = x.shape
    nn, ng, ntr = wn.shape[1], wg.shape[1], wt.shape[0]
    full = lambda shp: pl.BlockSpec(shp, lambda i: (0, 0))
    return pl.pallas_call(
        functools.partial(_proj_kernel, nqv=nqv),
        grid=(s // ts,),
        in_specs=[pl.BlockSpec((ts, d), lambda i: (i, 0)), full((1, d)),
                  full((d, nn)), full((d, ng)), full((ntr, d))],
        out_specs=[pl.BlockSpec((ts, nn), lambda i: (i, 0)),
                   pl.BlockSpec((ts, ng), lambda i: (i, 0)),
                   pl.BlockSpec((nqv, ts), lambda i: (0, i)),
                   pl.BlockSpec((ntr - nqv, ts), lambda i: (0, i))],
        out_shape=[jax.ShapeDtypeStruct((s, nn), BF16), jax.ShapeDtypeStruct((s, ng), F32),
                   jax.ShapeDtypeStruct((nqv, s), BF16), jax.ShapeDtypeStruct((ntr - nqv, s), F32)],
        compiler_params=_params("arbitrary"),
        name="in_proj",
    )(x, gain, wn, wg, wt)


def _gelu_tanh(x):
    return x * (0.5 * (1.0 + jnp.tanh(math.sqrt(2.0 / math.pi) * (x + 0.044715 * (x * x * x)))))


def _compress_kernel(kb_ref, pos_ref, w1_ref, w2_ref, w2t_ref, o_ref, ot_ref):
    kb = kb_ref[0, 0]
    half = kb.shape[1]
    n16 = kb.shape[0]
    first = jnp.dot(kb, w1_ref[0, :half, :], preferred_element_type=F32)
    second = jnp.dot(kb, w1_ref[0, half:, :], preferred_element_type=F32)
    posb = jnp.dot(pos_ref[0], w1_ref[0], preferred_element_type=F32)[0:1]
    pre = first + pltpu.roll(second, n16 - 1, 0) + posb
    h = _gelu_tanh(pre).astype(BF16)
    o_ref[0, 0] = jnp.dot(h, w2_ref[0], preferred_element_type=F32).astype(BF16)
    ot_ref[0, 0] = lax.dot_general(w2t_ref[0], h, _NT, preferred_element_type=F32).astype(BF16)


def _compress(kb, pos8, w1, w2, w2t):
    _, g, n16, cin2 = kb.shape
    hid = w1.shape[2]
    return pl.pallas_call(
        _compress_kernel,
        grid=(2, g),
        in_specs=[pl.BlockSpec((1, 1, n16, cin2), lambda a, b: (a, b, 0, 0)),
                  pl.BlockSpec((1, 8, 2 * cin2), lambda a, b: (a, 0, 0)),
                  pl.BlockSpec((1, 2 * cin2, hid), lambda a, b: (a, 0, 0)),
                  pl.BlockSpec((1, hid, HEAD_DIM), lambda a, b: (a, 0, 0)),
                  pl.BlockSpec((1, HEAD_DIM, hid), lambda a, b: (a, 0, 0))],
        out_specs=[pl.BlockSpec((1, 1, n16, HEAD_DIM), lambda a, b: (a, b, 0, 0)),
                   pl.BlockSpec((1, 1, HEAD_DIM, n16), lambda a, b: (a, b, 0, 0))],
        out_shape=[jax.ShapeDtypeStruct((2, g, n16, HEAD_DIM), BF16),
                   jax.ShapeDtypeStruct((2, g, HEAD_DIM, n16), BF16)],
        compiler_params=_params("arbitrary", "arbitrary"),
        name="kv_compress",
    )(kb, pos8, w1, w2, w2t)


def _heads_to_lanes(q_ref):
    return jnp.concatenate([q_ref[0, r * HEAD_DIM:(r + 1) * HEAD_DIM, :] for r in range(REP)], axis=1)


def _store_heads(o_ref, o_t):
    for r in range(REP):
        o_ref[0, r * HEAD_DIM:(r + 1) * HEAD_DIM, :] = o_t[:, r * QBLOCK:(r + 1) * QBLOCK]


def _cmpsel_kernel(q_ref, kc_ref, vct_ref, oc_ref, sel_ref, pg_ref, *, n16, nsel, topk):
    t0 = pl.program_id(1) * QBLOCK
    q4 = _heads_to_lanes(q_ref)
    s = jnp.dot(kc_ref[0, 0], q4, preferred_element_type=F32)
    n_io = lax.broadcasted_iota(jnp.int32, (n16, HQ), 0)
    t_io = t0 + (lax.broadcasted_iota(jnp.int32, (n16, HQ), 1) & (QBLOCK - 1))
    s = jnp.where(n_io * CMP_STRIDE + (CMP_BLOCK - 1) <= t_io, s, -jnp.inf)
    m = jnp.max(s, axis=0, keepdims=True)
    m = jnp.where(m == -jnp.inf, 0.0, m)
    e = jnp.exp(s - m)
    den = jnp.maximum(jnp.sum(e, axis=0, keepdims=True), 1e-30)
    p = e * (1.0 / den)
    _store_heads(oc_ref, jnp.dot(vct_ref[0, 0], p.astype(BF16), preferred_element_type=F32))

    pg = ((p[:, 0:QBLOCK] + p[:, QBLOCK:2 * QBLOCK]) + p[:, 2 * QBLOCK:3 * QBLOCK]) + p[:, 3 * QBLOCK:]
    pg_ref[0:8, :] = jnp.zeros((8, QBLOCK), F32)
    pg_ref[8:8 + n16, :] = pg
    ratio = SEL_BLOCK // CMP_STRIDE
    ps = pg_ref[pl.ds(7, nsel, stride=ratio), :]
    for o in range(1, ratio + 1):
        ps = ps + pg_ref[pl.ds(7 + o, nsel, stride=ratio), :]

    blk = lax.broadcasted_iota(jnp.int32, (nsel, QBLOCK), 0)
    cur = (t0 + lax.broadcasted_iota(jnp.int32, (nsel, QBLOCK), 1)) // SEL_BLOCK
    valid = blk <= cur
    forced = (blk == 0) | (blk == cur) | (blk == cur - 1)
    score = jnp.where(valid, jnp.where(forced, jnp.inf, ps), -jnp.inf)

    def pick_one(_, carry):
        rem, chosen = carry
        top = jnp.max(rem, axis=0, keepdims=True)
        first = jnp.min(jnp.where(rem == top, blk, nsel), axis=0, keepdims=True)
        hit = blk == first
        return jnp.where(hit, -jnp.inf, rem), jnp.where(hit, 1.0, chosen)

    _, chosen = lax.fori_loop(0, topk, pick_one, (score, jnp.zeros((nsel, QBLOCK), F32)))
    sel_ref[0] = jnp.where((chosen > 0.0) & valid, 0.0, -MASK_BIG).astype(BF16)


def _cmp_select(q, kc_all, vct_all, s):
    n16 = s // CMP_STRIDE
    nsel = s // SEL_BLOCK
    topk = min(SEL_TOPK, nsel)
    return pl.pallas_call(
        functools.partial(_cmpsel_kernel, n16=n16, nsel=nsel, topk=topk),
        grid=(GROUPS, s // QBLOCK),
        in_specs=[pl.BlockSpec((1, REP * HEAD_DIM, QBLOCK), lambda g, c: (g, 0, c)),
                  pl.BlockSpec((1, 1, n16, HEAD_DIM), lambda g, c: (0, g, 0, 0)),
                  pl.BlockSpec((1, 1, HEAD_DIM, n16), lambda g, c: (1, g, 0, 0))],
        out_specs=[pl.BlockSpec((1, REP * HEAD_DIM, QBLOCK), lambda g, c: (g, 0, c)),
                   pl.BlockSpec((1, nsel, QBLOCK), lambda g, c: (g, 0, c))],
        out_shape=[jax.ShapeDtypeStruct((GROUPS, REP * HEAD_DIM, s), F32),
                   jax.ShapeDtypeStruct((GROUPS, nsel, s), BF16)],
        scratch_shapes=[pltpu.VMEM((n16 + 8, QBLOCK), F32)],
        compiler_params=_params("arbitrary", "arbitrary"),
        name="cmp_select",
    )(q, kc_all, vct_all)


def _sel_kernel(q_ref, sel_ref, kaug_ref, vt_ref, strip_ref, cst_ref, o_ref,
                qa_ref, s_ref, m_ref, l_ref, acc_ref, *, tk):
    g = pl.program_id(0)
    ci = pl.program_id(1)
    nb = tk // SEL_BLOCK
    cpt = tk // QBLOCK
    kd = GROUPS * HEAD_DIM
    qa_ref[...] = jnp.zeros(qa_ref.shape, BF16)
    row0 = pl.multiple_of(g * HEAD_DIM, HEAD_DIM)
    for b in range(2):
        for r in range(REP):
            qa_ref[b, pl.ds(row0, HEAD_DIM), r * QBLOCK:(r + 1) * QBLOCK] = (
                q_ref[0, r * HEAD_DIM:(r + 1) * HEAD_DIM, :])
        qa_ref[b, kd + nb:kd + nb + 16, :] = cst_ref[0]
    m_ref[...] = jnp.full(m_ref.shape, -jnp.inf, F32)
    l_ref[...] = jnp.zeros(l_ref.shape, F32)
    acc_ref[...] = jnp.zeros(acc_ref.shape, F32)

    nt = ci // cpt + 1
    seq = jnp.maximum(nt, 2)
    lead = seq - nt

    def qk(i, buf):
        j = jnp.maximum(i - lead, 0)
        sb = sel_ref[0, pl.ds(pl.multiple_of(j * nb, nb), nb), :]
        sb = jnp.where(i < lead, -MASK_BIG, sb.astype(F32)).astype(BF16)
        for r in range(REP):
            qa_ref[buf, kd:kd + nb, r * QBLOCK:(r + 1) * QBLOCK] = sb
        k0 = pl.multiple_of(j * tk, tk)
        s_ref[buf] = jnp.dot(kaug_ref[pl.ds(k0, tk), :], qa_ref[buf], preferred_element_type=F32)

    def softmax_pv(i, buf, near):
        j = jnp.maximum(i - lead, 0)
        k0 = pl.multiple_of(j * tk, tk)
        s = s_ref[buf]
        if near:
            off = pl.multiple_of((2 * tk - QBLOCK) - (ci * QBLOCK - j * tk), QBLOCK)
            s = s + strip_ref[0, pl.ds(off, tk), :]
        m_prev = m_ref[...]
        m_new = jnp.maximum(m_prev, jnp.max(s, axis=0, keepdims=True))
        alpha = jnp.exp(m_prev - m_new)
        p = jnp.exp(s - m_new)
        l_ref[...] = alpha * l_ref[...] + jnp.sum(p, axis=0, keepdims=True)
        acc_ref[...] = alpha * acc_ref[...] + jnp.dot(vt_ref[0, :, pl.ds(k0, tk)], p.astype(BF16),
                                                      preferred_element_type=F32)
        m_ref[...] = m_new

    nfar = seq - 2
    qk(0, 0)

    def pair(p, carry):
        i = 2 * p
        qk(i + 1, 1)
        softmax_pv(i, 0, False)
        qk(i + 2, 0)
        softmax_pv(i + 1, 1, False)
        return carry

    lax.fori_loop(0, nfar // 2, pair, 0)
    i0 = 2 * (nfar // 2)

    @pl.when(nfar % 2 == 1)
    def _():
        qk(i0 + 1, 1)
        softmax_pv(i0, 0, False)
        qk(i0 + 2, 0)
        softmax_pv(i0 + 1, 1, True)
        softmax_pv(i0 + 2, 0, True)

    @pl.when(nfar % 2 == 0)
    def _():
        qk(i0 + 1, 1)
        softmax_pv(i0, 0, True)
        softmax_pv(i0 + 1, 1, True)

    _store_heads(o_ref, acc_ref[...] * (1.0 / jnp.maximum(l_ref[...], 1e-30)))


def _selected(q, sel, kaug, vt, strip, cst, s, tk):
    nsel = s // SEL_BLOCK
    nb = tk // SEL_BLOCK
    sl = strip.shape[1]
    return pl.pallas_call(
        functools.partial(_sel_kernel, tk=tk),
        grid=(GROUPS, s // QBLOCK),
        in_specs=[pl.BlockSpec((1, REP * HEAD_DIM, QBLOCK), lambda g, c: (g, 0, c)),
                  pl.BlockSpec((1, nsel, QBLOCK), lambda g, c: (g, 0, c)),
                  pl.BlockSpec(kaug.shape, lambda g, c: (0, 0)),
                  pl.BlockSpec((1, HEAD_DIM, s), lambda g, c: (g, 0, 0)),
                  pl.BlockSpec((1, sl, HQ), lambda g, c: (g, 0, 0)),
                  pl.BlockSpec((1, 16, HQ), lambda g, c: (g, 0, 0))],
        out_specs=pl.BlockSpec((1, REP * HEAD_DIM, QBLOCK), lambda g, c: (g, 0, c)),
        out_shape=jax.ShapeDtypeStruct((GROUPS, REP * HEAD_DIM, s), F32),
        scratch_shapes=[pltpu.VMEM((2, kaug.shape[1], HQ), BF16), pltpu.VMEM((2, tk, HQ), F32),
                        pltpu.VMEM((1, HQ), F32), pltpu.VMEM((1, HQ), F32), pltpu.VMEM((HEAD_DIM, HQ), F32)],
        compiler_params=_params("arbitrary", "arbitrary"),
        name="selected_attn",
    )(q, sel, kaug, vt, strip, cst)


def _band_kernel(q_ref, k_ref, vt_ref, strip_ref, sink_ref, o_ref, qa_ref, *, window, use_sink):
    g = pl.program_id(0)
    t0 = pl.multiple_of(pl.program_id(1) * QBLOCK, QBLOCK)
    nk = window + QBLOCK
    qa_ref[...] = jnp.zeros(qa_ref.shape, BF16)
    row0 = pl.multiple_of(g * HEAD_DIM, HEAD_DIM)
    for r in range(REP):
        qa_ref[pl.ds(row0, HEAD_DIM), r * QBLOCK:(r + 1) * QBLOCK] = q_ref[0, r * HEAD_DIM:(r + 1) * HEAD_DIM, :]
    s = jnp.dot(k_ref[pl.ds(t0, nk), :], qa_ref[...], preferred_element_type=F32) + strip_ref[0]
    row = lax.broadcasted_iota(jnp.int32, (nk, HQ), 0)
    s = jnp.where(row + (t0 - window) >= 0, s, -MASK_BIG)
    m = jnp.max(s, axis=0, keepdims=True)
    if use_sink:
        m = jnp.maximum(m, sink_ref[0])
    e = jnp.exp(s - m)
    den = jnp.sum(e, axis=0, keepdims=True)
    if use_sink:
        den = den + jnp.exp(sink_ref[0] - m)
    else:
        den = jnp.maximum(den, 1e-30)
    o_t = jnp.dot(vt_ref[0, :, pl.ds(t0, nk)], e.astype(BF16), preferred_element_type=F32)
    _store_heads(o_ref, o_t * (1.0 / den))


def _banded(q, kpad, vtpad, strip, sink, s, window, use_sink, name):
    nk = window + QBLOCK
    return pl.pallas_call(
        functools.partial(_band_kernel, window=window, use_sink=use_sink),
        grid=(GROUPS, s // QBLOCK),
        in_specs=[pl.BlockSpec((1, REP * HEAD_DIM, QBLOCK), lambda g, c: (g, 0, c)),
                  pl.BlockSpec(kpad.shape, lambda g, c: (0, 0)),
                  pl.BlockSpec((1, HEAD_DIM, s + window), lambda g, c: (g, 0, 0)),
                  pl.BlockSpec((1, nk, HQ), lambda g, c: (g, 0, 0)),
                  pl.BlockSpec((1, 1, HQ), lambda g, c: (g, 0, 0))],
        out_specs=pl.BlockSpec((1, REP * HEAD_DIM, QBLOCK), lambda g, c: (g, 0, c)),
        out_shape=jax.ShapeDtypeStruct((GROUPS, REP * HEAD_DIM, s), F32),
        scratch_shapes=[pltpu.VMEM((GROUPS * HEAD_DIM, HQ), BF16)],
        compiler_params=_params("arbitrary", "arbitrary"),
        name=name,
    )(q, kpad, vtpad, strip, sink)


def _merge_kernel(oc_ref, os_ref, ow_ref, ob_ref, gn_ref, gab_ref, x_ref, wa_ref, wb_ref, wo_ref, gp_ref, o_ref):
    d = x_ref.shape[1]
    gn = jax.nn.sigmoid(gn_ref[...])
    parts = []
    for h in range(HEADS):
        rows = slice(h * HEAD_DIM, (h + 1) * HEAD_DIM)
        parts.append(gn[h:h + 1] * oc_ref[rows, :] + gn[HEADS + h:HEADS + h + 1] * os_ref[rows, :]
                     + gn[2 * HEADS + h:2 * HEADS + h + 1] * ow_ref[rows, :])
    oa_t = jnp.concatenate(parts, axis=0).astype(BF16)
    ya = lax.dot_general(oa_t, wa_ref[...], _TN, preferred_element_type=F32)
    yb = lax.dot_general(ob_ref[...].astype(BF16), wb_ref[...], _TN, preferred_element_type=F32)
    y = jax.nn.sigmoid(gab_ref[:, :d]) * ya + jax.nn.sigmoid(gab_ref[:, d:]) * yb
    z = jnp.dot(y.astype(BF16), wo_ref[...], preferred_element_type=F32)
    o_ref[...] = x_ref[...] + _rms(z, gp_ref[...])


def _merge(oc, os_, ow, ob, gn, gab, x, wa, wb, wo, gpost, ts):
    s, d = x.shape
    hd = oc.shape[0]
    tok = lambda rows: pl.BlockSpec((rows, ts), lambda i: (0, i))
    full = lambda shp: pl.BlockSpec(shp, lambda i: (0, 0))
    return pl.pallas_call(
        _merge_kernel,
        grid=(s // ts,),
        in_specs=[tok(hd), tok(hd), tok(hd), tok(hd), tok(gn.shape[0]),
                  pl.BlockSpec((ts, 2 * d), lambda i: (i, 0)), pl.BlockSpec((ts, d), lambda i: (i, 0)),
                  full((hd, d)), full((hd, d)), full((d, d)), full((1, d))],
        out_specs=pl.BlockSpec((ts, d), lambda i: (i, 0)),
        out_shape=jax.ShapeDtypeStruct((s, d), F32),
        compiler_params=_params("arbitrary"),
        name="merge_out_proj",
    )(oc, os_, ow, ob, gn, gab, x, wa, wb, wo, gpost)


def _mlp_kernel(h_ref, gpre_ref, w1_ref, w2_ref, gpost_ref, o_ref, u_ref, acc_ref):
    f = pl.program_id(1)

    @pl.when(f == 0)
    def _():
        u_ref[...] = _rms(h_ref[...], gpre_ref[...]).astype(BF16)
        acc_ref[...] = jnp.zeros(acc_ref.shape, F32)

    a = jnp.maximum(jnp.dot(u_ref[...], w1_ref[...], preferred_element_type=F32), 0.0)
    acc_ref[...] += jnp.dot((a * a).astype(BF16), w2_ref[...], preferred_element_type=F32)

    @pl.when(f == pl.num_programs(1) - 1)
    def _():
        o_ref[...] = h_ref[...] + _rms(acc_ref[...], gpost_ref[...])


def _mlp(h, gpre, w1, w2, gpost, ts, tf):
    s, d = h.shape
    dff = w1.shape[1]
    return pl.pallas_call(
        _mlp_kernel,
        grid=(s // ts, dff // tf),
        in_specs=[pl.BlockSpec((ts, d), lambda i, f: (i, 0)), pl.BlockSpec((1, d), lambda i, f: (0, 0)),
                  pl.BlockSpec((d, tf), lambda i, f: (0, f)), pl.BlockSpec((tf, d), lambda i, f: (f, 0)),
                  pl.BlockSpec((1, d), lambda i, f: (0, 0))],
        out_specs=pl.BlockSpec((ts, d), lambda i, f: (i, 0)),
        out_shape=jax.ShapeDtypeStruct((s, d), F32),
        scratch_shapes=[pltpu.VMEM((ts, d), BF16), pltpu.VMEM((ts, d), F32)],
        compiler_params=_params("arbitrary", "arbitrary"),
        name="mlp_relu2",
    )(h, gpre, w1, w2, gpost)


def _bias_of(dist, bias_heads):
    bucket = _t5_bucket(dist)[..., None, None]
    out = jnp.zeros(dist.shape + bias_heads.shape[1:], F32)
    for b in range(NUM_BUCKETS):
        out = jnp.where(bucket == b, bias_heads[b], out)
    return out


def _band_strip(bias_heads, window):
    nk = window + QBLOCK
    dist = jnp.arange(QBLOCK)[None, :] - jnp.arange(nk)[:, None] + window
    ok = (dist >= 0) & (dist < window)
    tab = _bias_of(dist, bias_heads)
    tab = jnp.where(ok[:, :, None, None], tab, -MASK_BIG)
    return tab.transpose(2, 0, 3, 1).reshape(GROUPS, nk, HQ)


def _sel_strip(bias_heads, far, tk):
    rows = 3 * tk - QBLOCK
    dist = jnp.arange(QBLOCK)[None, :] - jnp.arange(rows)[:, None] + (2 * tk - QBLOCK)
    tab = _bias_of(dist, bias_heads) - far[None, None]
    tab = jnp.where((dist >= 0)[:, :, None, None], tab, -MASK_BIG)
    return tab.transpose(2, 0, 3, 1).reshape(GROUPS, rows, HQ)


def kernel(x, norm_mix_pre, norm_mix_post, norm_mlp_pre, norm_mlp_post, w_in,
           cmp_pos_k, cmp_w1_k, cmp_w2_k, cmp_pos_v, cmp_w1_v, cmp_w2_v,
           attn_sinks, rel_bias, w_up_nsa, w_up_swa, w_out, w_mlp_in, w_mlp_out):
    b, s, d = x.shape
    assert b == 1 and s % SEL_TILE == 0 and w_in.shape[0] == 1
    qd = HEADS * HEAD_DIM
    kvd = GROUPS * HEAD_DIM
    sizes = (qd,) + (kvd,) * 6 + (3 * HEADS, qd, kvd, kvd, d, d)
    offs = [0]
    for z in sizes:
        offs.append(offs[-1] + z)
    w = w_in[0]
    col = lambda i: w[:, offs[i]:offs[i + 1]]
    (w_qn, w_kc, w_vc, w_ks, w_vs, w_kw, w_vw, w_gn, w_qs, w_k_s, w_v_s, w_ga, w_gb) = [col(i) for i in range(13)]
    scale = HEAD_DIM ** -0.5
    w_gn = w_gn.reshape(d, HEADS, 3).transpose(0, 2, 1).reshape(d, 3 * HEADS)
    w_gn = jnp.pad(w_gn, ((0, 0), (0, 8)))
    wn = jnp.concatenate([w_kc, w_vc, w_ks, w_kw, w_k_s], axis=1).astype(BF16)
    wg = jnp.concatenate([w_ga, w_gb], axis=1).astype(BF16)
    wt = jnp.concatenate([w_qn * scale, w_qs * scale, w_vs, w_vw, w_v_s, w_gn], axis=1).T.astype(BF16)
    nqv = 2 * qd + 3 * kvd

    x2 = x[0]
    kv, gab, qv, gn = _project(x2, norm_mix_pre, wn, wg, wt, nqv, ts=512)
    q_nsa = qv[0:qd].reshape(GROUPS, REP * HEAD_DIM, s)
    q_swa = qv[qd:2 * qd].reshape(GROUPS, REP * HEAD_DIM, s)
    vs_t = qv[2 * qd:2 * qd + kvd].reshape(GROUPS, HEAD_DIM, s)
    vw_t = qv[2 * qd + kvd:2 * qd + 2 * kvd].reshape(GROUPS, HEAD_DIM, s)
    vswa_t = qv[2 * qd + 2 * kvd:].reshape(GROUPS, HEAD_DIM, s)

    n16 = s // CMP_STRIDE
    kb = kv[:, :2 * kvd].reshape(n16, CMP_STRIDE, 2, GROUPS, HEAD_DIM).transpose(2, 3, 0, 1, 4)
    kb = kb.reshape(2, GROUPS, n16, CMP_STRIDE * HEAD_DIM)
    pos8 = jnp.stack([cmp_pos_k[0], cmp_pos_v[0]]).reshape(2, 1, CMP_BLOCK * HEAD_DIM)
    pos8 = jnp.broadcast_to(pos8, (2, 8, CMP_BLOCK * HEAD_DIM)).astype(BF16)
    w1 = jnp.stack([cmp_w1_k[0], cmp_w1_v[0]]).astype(BF16)
    w2 = jnp.stack([cmp_w2_k[0], cmp_w2_v[0]]).astype(BF16)
    cmp_n, cmp_t = _compress(kb, pos8, w1, w2, w2.transpose(0, 2, 1))

    oc, sel = _cmp_select(q_nsa, cmp_n, cmp_t, s)

    bias_nsa = rel_bias[:, :HEADS].reshape(NUM_BUCKETS, GROUPS, REP)
    bias_swa = rel_bias[:, HEADS:].reshape(NUM_BUCKETS, GROUPS, REP)

    tk = SEL_TILE
    nb = tk // SEL_BLOCK
    posn = jnp.arange(s)
    onehot = ((posn // SEL_BLOCK) % nb)[:, None] == jnp.arange(nb)[None, :]
    aug = jnp.concatenate([onehot.astype(BF16), jnp.ones((s, 2), BF16),
                           jnp.zeros((s, kvd - nb - 2), BF16)], axis=1)
    kaug = jnp.concatenate([kv[:, 2 * kvd:3 * kvd], aug], axis=1)
    far = bias_nsa[NUM_BUCKETS - 1]
    far_hi = far.astype(BF16)
    far_lo = (far - far_hi.astype(F32)).astype(BF16)
    cst = jnp.stack([far_hi, far_lo], axis=1)
    cst = jnp.broadcast_to(cst[:, :, :, None], (GROUPS, 2, REP, QBLOCK)).reshape(GROUPS, 2, HQ)
    cst = jnp.pad(cst, ((0, 0), (0, 14), (0, 0)))
    o_sel = _selected(q_nsa, sel, kaug, vs_t, _sel_strip(bias_nsa, far, tk), cst, s, tk)

    pad_rows = lambda a, wdw: jnp.pad(a, ((wdw, 0), (0, 0)))
    pad_lanes = lambda a, wdw: jnp.pad(a, ((0, 0), (0, 0), (wdw, 0)))
    no_sink = jnp.zeros((GROUPS, 1, HQ), F32)
    o_win = _banded(q_nsa, pad_rows(kv[:, 3 * kvd:4 * kvd], NSA_WINDOW), pad_lanes(vw_t, NSA_WINDOW),
                    _band_strip(bias_nsa, NSA_WINDOW), no_sink, s, NSA_WINDOW, False, "nsa_window")
    sink = attn_sinks[0].reshape(GROUPS, 1, REP, 1).astype(F32)
    sink = jnp.broadcast_to(sink, (GROUPS, 1, REP, QBLOCK)).reshape(GROUPS, 1, HQ)
    o_swa = _banded(q_swa, pad_rows(kv[:, 4 * kvd:5 * kvd], SWA_WINDOW), pad_lanes(vswa_t, SWA_WINDOW),
                    _band_strip(bias_swa, SWA_WINDOW), sink, s, SWA_WINDOW, True, "swa_sink")

    flat = lambda a: a.reshape(qd, s)
    h1 = _merge(flat(oc), flat(o_sel), flat(o_win), flat(o_swa), gn, gab, x2,
                w_up_nsa[0].astype(BF16), w_up_swa[0].astype(BF16), w_out[0].astype(BF16),
                norm_mix_post, ts=512)
    out = _mlp(h1, norm_mlp_pre, w_mlp_in[0].astype(BF16), w_mlp_out[0].astype(BF16), norm_mlp_post,
               ts=1024, tf=1024)
    return out[None]
```

```python
import functools
import math

import jax
import jax.numpy as jnp
from jax import lax
from jax.experimental import pallas as pl
from jax.experimental.pallas import tpu as pltpu

F32 = jnp.float32
BF16 = jnp.bfloat16

HEAD_DIM = 64
GROUPS = 2
REP = 4
HEADS = GROUPS * REP
CMP_BLOCK = 32
CMP_STRIDE = 16
SEL_BLOCK = 64
SEL_TOPK = 16
NSA_WINDOW = 512
SWA_WINDOW = 128
QBLOCK = 128
NUM_BUCKETS = 32
MAX_DISTANCE = 1024
RMS_EPS = 1e-6
LOG2E = math.log2(math.e)
MASK_BIG = 2.0 ** 99
SEL_TILE = 1024
HQ = REP * QBLOCK
VMEM_LIMIT = 56 * 1024 * 1024

_NT = (((1,), (1,)), ((), ()))
_TN = (((0,), (0,)), ((), ()))


def _params(*sem):
    return pltpu.CompilerParams(dimension_semantics=sem, vmem_limit_bytes=VMEM_LIMIT)


def _t5_bucket(dist):
    max_exact = NUM_BUCKETS // 2
    d = jnp.maximum(dist, 0)
    df = jnp.maximum(d, 1).astype(jnp.float32)
    large = max_exact + (jnp.log(df / max_exact) / math.log(MAX_DISTANCE / max_exact)
                         * (NUM_BUCKETS - max_exact)).astype(jnp.int32)
    large = jnp.minimum(large, NUM_BUCKETS - 1)
    return jnp.where(d < max_exact, d, large)


def _rms(x, gain):
    return x * lax.rsqrt(jnp.mean(x * x, axis=-1, keepdims=True) + RMS_EPS) * gain


def _proj_kernel(x_ref, g_ref, wn_ref, wg_ref, wt_ref, kv_ref, gab_ref, qv_ref, gn_ref, *, nqv):
    u = _rms(x_ref[...], g_ref[...]).astype(BF16)
    kv_ref[...] = jnp.dot(u, wn_ref[...], preferred_element_type=F32).astype(BF16)
    gab_ref[...] = jnp.dot(u, wg_ref[...], preferred_element_type=F32)
    t = lax.dot_general(wt_ref[...], u, _NT, preferred_element_type=F32)
    qv_ref[...] = t[:nqv].astype(BF16)
    gn_ref[...] = t[nqv:]


def _project(x, gain, wn, wg, wt, nqv, ts):
    s, d = x.shape
    nn, ng, ntr = wn.shape[1], wg.shape[1], wt.shape[0]
    full = lambda shp: pl.BlockSpec(shp, lambda i: (0, 0))
    return pl.pallas_call(
        functools.partial(_proj_kernel, nqv=nqv),
        grid=(s // ts,),
        in_specs=[pl.BlockSpec((ts, d), lambda i: (i, 0)), full((1, d)),
                  full((d, nn)), full((d, ng)), full((ntr, d))],
        out_specs=[pl.BlockSpec((ts, nn), lambda i: (i, 0)),
                   pl.BlockSpec((ts, ng), lambda i: (i, 0)),
                   pl.BlockSpec((nqv, ts), lambda i: (0, i)),
                   pl.BlockSpec((ntr - nqv, ts), lambda i: (0, i))],
        out_shape=[jax.ShapeDtypeStruct((s, nn), BF16), jax.ShapeDtypeStruct((s, ng), F32),
                   jax.ShapeDtypeStruct((nqv, s), BF16), jax.ShapeDtypeStruct((ntr - nqv, s), F32)],
        compiler_params=_params("arbitrary"),
        name="in_proj",
    )(x, gain, wn, wg, wt)


def _gelu_tanh(x):
    return x * (0.5 * (1.0 + jnp.tanh(math.sqrt(2.0 / math.pi) * (x + 0.044715 * (x * x * x)))))


def _compress_kernel(kb_ref, pos_ref, w1_ref, w2_ref, w2t_ref, o_ref, ot_ref):
    kb = kb_ref[0, 0]
    half = kb.shape[1]
    n16 = kb.shape[0]
    first = jnp.dot(kb, w1_ref[0, :half, :], preferred_element_type=F32)
    second = jnp.dot(kb, w1_ref[0, half:, :], preferred_element_type=F32)
    posb = jnp.dot(pos_ref[0], w1_ref[0], preferred_element_type=F32)[0:1]
    pre = first + pltpu.roll(second, n16 - 1, 0) + posb
    h = _gelu_tanh(pre).astype(BF16)
    o_ref[0, 0] = jnp.dot(h, w2_ref[0], preferred_element_type=F32).astype(BF16)
    ot_ref[0, 0] = lax.dot_general(w2t_ref[0], h, _NT, preferred_element_type=F32).astype(BF16)


def _compress(kb, pos8, w1, w2, w2t):
    _, g, n16, cin2 = kb.shape
    hid = w1.shape[2]
    return pl.pallas_call(
        _compress_kernel,
        grid=(2, g),
        in_specs=[pl.BlockSpec((1, 1, n16, cin2), lambda a, b: (a, b, 0, 0)),
                  pl.BlockSpec((1, 8, 2 * cin2), lambda a, b: (a, 0, 0)),
                  pl.BlockSpec((1, 2 * cin2, hid), lambda a, b: (a, 0, 0)),
                  pl.BlockSpec((1, hid, HEAD_DIM), lambda a, b: (a, 0, 0)),
                  pl.BlockSpec((1, HEAD_DIM, hid), lambda a, b: (a, 0, 0))],
        out_specs=[pl.BlockSpec((1, 1, n16, HEAD_DIM), lambda a, b: (a, b, 0, 0)),
                   pl.BlockSpec((1, 1, HEAD_DIM, n16), lambda a, b: (a, b, 0, 0))],
        out_shape=[jax.ShapeDtypeStruct((2, g, n16, HEAD_DIM), BF16),
                   jax.ShapeDtypeStruct((2, g, HEAD_DIM, n16), BF16)],
        compiler_params=_params("arbitrary", "arbitrary"),
        name="kv_compress",
    )(kb, pos8, w1, w2, w2t)


def _heads_to_lanes(q_ref):
    return jnp.concatenate([q_ref[0, r * HEAD_DIM:(r + 1) * HEAD_DIM, :] for r in range(REP)], axis=1)


def _store_heads(o_ref, o_t):
    for r in range(REP):
        o_ref[0, r * HEAD_DIM:(r + 1) * HEAD_DIM, :] = o_t[:, r * QBLOCK:(r + 1) * QBLOCK]


def _cmpsel_kernel(q_ref, kc_ref, vct_ref, oc_ref, sel_ref, pg_ref, *, n16, nsel, topk):
    t0 = pl.program_id(1) * QBLOCK
    q4 = _heads_to_lanes(q_ref)
    s = jnp.dot(kc_ref[0, 0], q4, preferred_element_type=F32)
    n_io = lax.broadcasted_iota(jnp.int32, (n16, HQ), 0)
    t_io = t0 + (lax.broadcasted_iota(jnp.int32, (n16, HQ), 1) & (QBLOCK - 1))
    s = jnp.where(n_io * CMP_STRIDE + (CMP_BLOCK - 1) <= t_io, s, -jnp.inf)
    m = jnp.max(s, axis=0, keepdims=True)
    m = jnp.where(m == -jnp.inf, 0.0, m)
    e = jnp.exp2(s - m)
    den = jnp.maximum(jnp.sum(e, axis=0, keepdims=True), 1e-30)
    p = e * (1.0 / den)
    _store_heads(oc_ref, jnp.dot(vct_ref[0, 0], p.astype(BF16), preferred_element_type=F32))

    pg = ((p[:, 0:QBLOCK] + p[:, QBLOCK:2 * QBLOCK]) + p[:, 2 * QBLOCK:3 * QBLOCK]) + p[:, 3 * QBLOCK:]
    pg_ref[0:8, :] = jnp.zeros((8, QBLOCK), F32)
    pg_ref[8:8 + n16, :] = pg
    ratio = SEL_BLOCK // CMP_STRIDE
    ps = pg_ref[pl.ds(7, nsel, stride=ratio), :]
    for o in range(1, ratio + 1):
        ps = ps + pg_ref[pl.ds(7 + o, nsel, stride=ratio), :]

    blk = lax.broadcasted_iota(jnp.int32, (nsel, QBLOCK), 0)
    cur = (t0 + lax.broadcasted_iota(jnp.int32, (nsel, QBLOCK), 1)) // SEL_BLOCK
    valid = blk <= cur
    forced = (blk == 0) | (blk == cur) | (blk == cur - 1)
    score = jnp.where(valid, jnp.where(forced, jnp.inf, ps), -jnp.inf)

    def pick_one(_, carry):
        rem, chosen = carry
        top = jnp.max(rem, axis=0, keepdims=True)
        first = jnp.min(jnp.where(rem == top, blk, nsel), axis=0, keepdims=True)
        hit = blk == first
        return jnp.where(hit, -jnp.inf, rem), jnp.where(hit, 1.0, chosen)

    _, chosen = lax.fori_loop(0, topk, pick_one, (score, jnp.zeros((nsel, QBLOCK), F32)))
    sel_ref[0] = jnp.where((chosen > 0.0) & valid, 0.0, -MASK_BIG).astype(BF16)


def _cmp_select(q, kc_all, vct_all, s):
    n16 = s // CMP_STRIDE
    nsel = s // SEL_BLOCK
    topk = min(SEL_TOPK, nsel)
    return pl.pallas_call(
        functools.partial(_cmpsel_kernel, n16=n16, nsel=nsel, topk=topk),
        grid=(GROUPS, s // QBLOCK),
        in_specs=[pl.BlockSpec((1, REP * HEAD_DIM, QBLOCK), lambda g, c: (g, 0, c)),
                  pl.BlockSpec((1, 1, n16, HEAD_DIM), lambda g, c: (0, g, 0, 0)),
                  pl.BlockSpec((1, 1, HEAD_DIM, n16), lambda g, c: (1, g, 0, 0))],
        out_specs=[pl.BlockSpec((1, REP * HEAD_DIM, QBLOCK), lambda g, c: (g, 0, c)),
                   pl.BlockSpec((1, nsel, QBLOCK), lambda g, c: (g, 0, c))],
        out_shape=[jax.ShapeDtypeStruct((GROUPS, REP * HEAD_DIM, s), F32),
                   jax.ShapeDtypeStruct((GROUPS, nsel, s), BF16)],
        scratch_shapes=[pltpu.VMEM((n16 + 8, QBLOCK), F32)],
        compiler_params=_params("arbitrary", "arbitrary"),
        name="cmp_select",
    )(q, kc_all, vct_all)


def _sel_kernel(q_ref, sel_ref, kaug_ref, vt_ref, strip_ref, cst_ref, o_ref,
                qa_ref, s_ref, tmax_ref, m_ref, acc_ref, *, tk):
    g = pl.program_id(0)
    ci = pl.program_id(1)
    nb = tk // SEL_BLOCK
    cpt = tk // QBLOCK
    kd = GROUPS * HEAD_DIM
    qa_ref[...] = jnp.zeros(qa_ref.shape, BF16)
    row0 = pl.multiple_of(g * HEAD_DIM, HEAD_DIM)
    for b in range(2):
        for r in range(REP):
            qa_ref[b, pl.ds(row0, HEAD_DIM), r * QBLOCK:(r + 1) * QBLOCK] = (
                q_ref[0, r * HEAD_DIM:(r + 1) * HEAD_DIM, :])
        qa_ref[b, kd + nb:kd + nb + 16, :] = cst_ref[0]
    m_ref[...] = jnp.full(m_ref.shape, -jnp.inf, F32)
    acc_ref[...] = jnp.zeros(acc_ref.shape, F32)

    nt = ci // cpt + 1
    seq = jnp.maximum(nt, 2)
    lead = seq - nt

    def qk(i, buf):
        j = jnp.maximum(i - lead, 0)
        sb = sel_ref[0, pl.ds(pl.multiple_of(j * nb, nb), nb), :]
        sb = jnp.where(i < lead, -MASK_BIG, sb.astype(F32)).astype(BF16)
        for r in range(REP):
            qa_ref[buf, kd:kd + nb, r * QBLOCK:(r + 1) * QBLOCK] = sb
        k0 = pl.multiple_of(j * tk, tk)
        s = jnp.dot(kaug_ref[pl.ds(k0, tk), :], qa_ref[buf], preferred_element_type=F32)
        s_ref[buf] = s
        tmax_ref[buf] = jnp.max(s, axis=0, keepdims=True)

    def softmax_pv(i, buf, near):
        j = jnp.maximum(i - lead, 0)
        k0 = pl.multiple_of(j * tk, tk)
        s = s_ref[buf]
        if near:
            off = pl.multiple_of((2 * tk - QBLOCK) - (ci * QBLOCK - j * tk), QBLOCK)
            s = s + strip_ref[0, pl.ds(off, tk), :]
            tile_max = jnp.max(s, axis=0, keepdims=True)
        else:
            tile_max = tmax_ref[buf]
        m_prev = m_ref[...]
        m_new = jnp.maximum(m_prev, tile_max)
        alpha = jnp.exp2(m_prev - m_new)
        p = jnp.exp2(s - m_new)
        acc_ref[...] = alpha * acc_ref[...] + jnp.dot(vt_ref[0, :, pl.ds(k0, tk)], p.astype(BF16),
                                                      preferred_element_type=F32)
        m_ref[...] = m_new

    nfar = seq - 2
    qk(0, 0)

    def pair(p, carry):
        i = 2 * p
        qk(i + 1, 1)
        softmax_pv(i, 0, False)
        qk(i + 2, 0)
        softmax_pv(i + 1, 1, False)
        return carry

    lax.fori_loop(0, nfar // 2, pair, 0)
    i0 = 2 * (nfar // 2)

    @pl.when(nfar % 2 == 1)
    def _():
        qk(i0 + 1, 1)
        softmax_pv(i0, 0, False)
        qk(i0 + 2, 0)
        softmax_pv(i0 + 1, 1, True)
        softmax_pv(i0 + 2, 0, True)

    @pl.when(nfar % 2 == 0)
    def _():
        qk(i0 + 1, 1)
        softmax_pv(i0, 0, True)
        softmax_pv(i0 + 1, 1, True)

    den = jnp.maximum(acc_ref[HEAD_DIM:HEAD_DIM + 1, :], 1e-30)
    _store_heads(o_ref, acc_ref[0:HEAD_DIM, :] * (1.0 / den))


def _selected(q, sel, kaug, vt, strip, cst, s, tk):
    nsel = s // SEL_BLOCK
    nb = tk // SEL_BLOCK
    sl = strip.shape[1]
    return pl.pallas_call(
        functools.partial(_sel_kernel, tk=tk),
        grid=(GROUPS, s // QBLOCK),
        in_specs=[pl.BlockSpec((1, REP * HEAD_DIM, QBLOCK), lambda g, c: (g, 0, c)),
                  pl.BlockSpec((1, nsel, QBLOCK), lambda g, c: (g, 0, c)),
                  pl.BlockSpec(kaug.shape, lambda g, c: (0, 0)),
                  pl.BlockSpec((1, vt.shape[1], s), lambda g, c: (g, 0, 0)),
                  pl.BlockSpec((1, sl, HQ), lambda g, c: (g, 0, 0)),
                  pl.BlockSpec((1, 16, HQ), lambda g, c: (g, 0, 0))],
        out_specs=pl.BlockSpec((1, REP * HEAD_DIM, QBLOCK), lambda g, c: (g, 0, c)),
        out_shape=jax.ShapeDtypeStruct((GROUPS, REP * HEAD_DIM, s), F32),
        scratch_shapes=[pltpu.VMEM((2, kaug.shape[1], HQ), BF16), pltpu.VMEM((2, tk, HQ), F32),
                        pltpu.VMEM((2, 1, HQ), F32),
                        pltpu.VMEM((1, HQ), F32), pltpu.VMEM((vt.shape[1], HQ), F32)],
        compiler_params=_params("arbitrary", "arbitrary"),
        name="selected_attn",
    )(q, sel, kaug, vt, strip, cst)


def _band_kernel(q_ref, k_ref, vt_ref, strip_ref, sink_ref, o_ref, qa_ref, *, window, use_sink):
    g = pl.program_id(0)
    t0 = pl.multiple_of(pl.program_id(1) * QBLOCK, QBLOCK)
    nk = window + QBLOCK
    qa_ref[...] = jnp.zeros(qa_ref.shape, BF16)
    row0 = pl.multiple_of(g * HEAD_DIM, HEAD_DIM)
    for r in range(REP):
        qa_ref[pl.ds(row0, HEAD_DIM), r * QBLOCK:(r + 1) * QBLOCK] = q_ref[0, r * HEAD_DIM:(r + 1) * HEAD_DIM, :]
    s = jnp.dot(k_ref[pl.ds(t0, nk), :], qa_ref[...], preferred_element_type=F32) + strip_ref[0]
    row = lax.broadcasted_iota(jnp.int32, (nk, HQ), 0)
    s = jnp.where(row + (t0 - window) >= 0, s, -MASK_BIG)
    m = jnp.max(s, axis=0, keepdims=True)
    if use_sink:
        m = jnp.maximum(m, sink_ref[0])
    e = jnp.exp2(s - m)
    den = jnp.sum(e, axis=0, keepdims=True)
    if use_sink:
        den = den + jnp.exp2(sink_ref[0] - m)
    else:
        den = jnp.maximum(den, 1e-30)
    o_t = jnp.dot(vt_ref[0, :, pl.ds(t0, nk)], e.astype(BF16), preferred_element_type=F32)
    _store_heads(o_ref, o_t * (1.0 / den))


def _banded(q, kpad, vtpad, strip, sink, s, window, use_sink, name):
    nk = window + QBLOCK
    return pl.pallas_call(
        functools.partial(_band_kernel, window=window, use_sink=use_sink),
        grid=(GROUPS, s // QBLOCK),
        in_specs=[pl.BlockSpec((1, REP * HEAD_DIM, QBLOCK), lambda g, c: (g, 0, c)),
                  pl.BlockSpec(kpad.shape, lambda g, c: (0, 0)),
                  pl.BlockSpec((1, HEAD_DIM, s + window), lambda g, c: (g, 0, 0)),
                  pl.BlockSpec((1, nk, HQ), lambda g, c: (g, 0, 0)),
                  pl.BlockSpec((1, 1, HQ), lambda g, c: (g, 0, 0))],
        out_specs=pl.BlockSpec((1, REP * HEAD_DIM, QBLOCK), lambda g, c: (g, 0, c)),
        out_shape=jax.ShapeDtypeStruct((GROUPS, REP * HEAD_DIM, s), F32),
        scratch_shapes=[pltpu.VMEM((GROUPS * HEAD_DIM, HQ), BF16)],
        compiler_params=_params("arbitrary", "arbitrary"),
        name=name,
    )(q, kpad, vtpad, strip, sink)


def _merge_kernel(oc_ref, os_ref, ow_ref, ob_ref, gn_ref, gab_ref, x_ref, wa_ref, wb_ref, wo_ref, gp_ref, o_ref):
    d = x_ref.shape[1]
    gn = jax.nn.sigmoid(gn_ref[...])
    parts = []
    for h in range(HEADS):
        rows = slice(h * HEAD_DIM, (h + 1) * HEAD_DIM)
        parts.append(gn[h:h + 1] * oc_ref[rows, :] + gn[HEADS + h:HEADS + h + 1] * os_ref[rows, :]
                     + gn[2 * HEADS + h:2 * HEADS + h + 1] * ow_ref[rows, :])
    oa_t = jnp.concatenate(parts, axis=0).astype(BF16)
    ya = lax.dot_general(oa_t, wa_ref[...], _TN, preferred_element_type=F32)
    yb = lax.dot_general(ob_ref[...].astype(BF16), wb_ref[...], _TN, preferred_element_type=F32)
    y = jax.nn.sigmoid(gab_ref[:, :d]) * ya + jax.nn.sigmoid(gab_ref[:, d:]) * yb
    z = jnp.dot(y.astype(BF16), wo_ref[...], preferred_element_type=F32)
    o_ref[...] = x_ref[...] + _rms(z, gp_ref[...])


def _merge(oc, os_, ow, ob, gn, gab, x, wa, wb, wo, gpost, ts):
    s, d = x.shape
    hd = oc.shape[0]
    tok = lambda rows: pl.BlockSpec((rows, ts), lambda i: (0, i))
    full = lambda shp: pl.BlockSpec(shp, lambda i: (0, 0))
    return pl.pallas_call(
        _merge_kernel,
        grid=(s // ts,),
        in_specs=[tok(hd), tok(hd), tok(hd), tok(hd), tok(gn.shape[0]),
                  pl.BlockSpec((ts, 2 * d), lambda i: (i, 0)), pl.BlockSpec((ts, d), lambda i: (i, 0)),
                  full((hd, d)), full((hd, d)), full((d, d)), full((1, d))],
        out_specs=pl.BlockSpec((ts, d), lambda i: (i, 0)),
        out_shape=jax.ShapeDtypeStruct((s, d), F32),
        compiler_params=_params("arbitrary"),
        name="merge_out_proj",
    )(oc, os_, ow, ob, gn, gab, x, wa, wb, wo, gpost)


def _mlp_kernel(h_ref, gpre_ref, w1_ref, w2_ref, gpost_ref, o_ref, u_ref, acc_ref):
    f = pl.program_id(1)

    @pl.when(f == 0)
    def _():
        u_ref[...] = _rms(h_ref[...], gpre_ref[...]).astype(BF16)
        acc_ref[...] = jnp.zeros(acc_ref.shape, F32)

    a = jnp.maximum(jnp.dot(u_ref[...], w1_ref[...], preferred_element_type=F32), 0.0)
    acc_ref[...] += jnp.dot((a * a).astype(BF16), w2_ref[...], preferred_element_type=F32)

    @pl.when(f == pl.num_programs(1) - 1)
    def _():
        o_ref[...] = h_ref[...] + _rms(acc_ref[...], gpost_ref[...])


def _mlp(h, gpre, w1, w2, gpost, ts, tf):
    s, d = h.shape
    dff = w1.shape[1]
    return pl.pallas_call(
        _mlp_kernel,
        grid=(s // ts, dff // tf),
        in_specs=[pl.BlockSpec((ts, d), lambda i, f: (i, 0)), pl.BlockSpec((1, d), lambda i, f: (0, 0)),
                  pl.BlockSpec((d, tf), lambda i, f: (0, f)), pl.BlockSpec((tf, d), lambda i, f: (f, 0)),
                  pl.BlockSpec((1, d), lambda i, f: (0, 0))],
        out_specs=pl.BlockSpec((ts, d), lambda i, f: (i, 0)),
        out_shape=jax.ShapeDtypeStruct((s, d), F32),
        scratch_shapes=[pltpu.VMEM((ts, d), BF16), pltpu.VMEM((ts, d), F32)],
        compiler_params=_params("arbitrary", "arbitrary"),
        name="mlp_relu2",
    )(h, gpre, w1, w2, gpost)


def _bias_of(dist, bias_heads):
    bucket = _t5_bucket(dist)[..., None, None]
    out = jnp.zeros(dist.shape + bias_heads.shape[1:], F32)
    for b in range(NUM_BUCKETS):
        out = jnp.where(bucket == b, bias_heads[b], out)
    return out


def _band_strip(bias_heads, window):
    nk = window + QBLOCK
    dist = jnp.arange(QBLOCK)[None, :] - jnp.arange(nk)[:, None] + window
    ok = (dist >= 0) & (dist < window)
    tab = _bias_of(dist, bias_heads)
    tab = jnp.where(ok[:, :, None, None], tab, -MASK_BIG)
    return tab.transpose(2, 0, 3, 1).reshape(GROUPS, nk, HQ)


def _sel_strip(bias_heads, far, tk):
    rows = 3 * tk - QBLOCK
    dist = jnp.arange(QBLOCK)[None, :] - jnp.arange(rows)[:, None] + (2 * tk - QBLOCK)
    tab = _bias_of(dist, bias_heads) - far[None, None]
    tab = jnp.where((dist >= 0)[:, :, None, None], tab, -MASK_BIG)
    return tab.transpose(2, 0, 3, 1).reshape(GROUPS, rows, HQ)


def kernel(x, norm_mix_pre, norm_mix_post, norm_mlp_pre, norm_mlp_post, w_in,
           cmp_pos_k, cmp_w1_k, cmp_w2_k, cmp_pos_v, cmp_w1_v, cmp_w2_v,
           attn_sinks, rel_bias, w_up_nsa, w_up_swa, w_out, w_mlp_in, w_mlp_out):
    b, s, d = x.shape
    assert b == 1 and s % SEL_TILE == 0 and w_in.shape[0] == 1
    qd = HEADS * HEAD_DIM
    kvd = GROUPS * HEAD_DIM
    sizes = (qd,) + (kvd,) * 6 + (3 * HEADS, qd, kvd, kvd, d, d)
    offs = [0]
    for z in sizes:
        offs.append(offs[-1] + z)
    w = w_in[0]
    col = lambda i: w[:, offs[i]:offs[i + 1]]
    (w_qn, w_kc, w_vc, w_ks, w_vs, w_kw, w_vw, w_gn, w_qs, w_k_s, w_v_s, w_ga, w_gb) = [col(i) for i in range(13)]
    scale = HEAD_DIM ** -0.5 * LOG2E
    w_gn = w_gn.reshape(d, HEADS, 3).transpose(0, 2, 1).reshape(d, 3 * HEADS)
    w_gn = jnp.pad(w_gn, ((0, 0), (0, 8)))
    wn = jnp.concatenate([w_kc, w_vc, w_ks, w_kw, w_k_s], axis=1).astype(BF16)
    wg = jnp.concatenate([w_ga, w_gb], axis=1).astype(BF16)
    wt = jnp.concatenate([w_qn * scale, w_qs * scale, w_vs, w_vw, w_v_s, w_gn], axis=1).T.astype(BF16)
    nqv = 2 * qd + 3 * kvd

    x2 = x[0]
    kv, gab, qv, gn = _project(x2, norm_mix_pre, wn, wg, wt, nqv, ts=512)
    q_nsa = qv[0:qd].reshape(GROUPS, REP * HEAD_DIM, s)
    q_swa = qv[qd:2 * qd].reshape(GROUPS, REP * HEAD_DIM, s)
    vs_t = qv[2 * qd:2 * qd + kvd].reshape(GROUPS, HEAD_DIM, s)
    vw_t = qv[2 * qd + kvd:2 * qd + 2 * kvd].reshape(GROUPS, HEAD_DIM, s)
    vswa_t = qv[2 * qd + 2 * kvd:].reshape(GROUPS, HEAD_DIM, s)

    n16 = s // CMP_STRIDE
    kb = kv[:, :2 * kvd].reshape(n16, CMP_STRIDE, 2, GROUPS, HEAD_DIM).transpose(2, 3, 0, 1, 4)
    kb = kb.reshape(2, GROUPS, n16, CMP_STRIDE * HEAD_DIM)
    pos8 = jnp.stack([cmp_pos_k[0], cmp_pos_v[0]]).reshape(2, 1, CMP_BLOCK * HEAD_DIM)
    pos8 = jnp.broadcast_to(pos8, (2, 8, CMP_BLOCK * HEAD_DIM)).astype(BF16)
    w1 = jnp.stack([cmp_w1_k[0], cmp_w1_v[0]]).astype(BF16)
    w2 = jnp.stack([cmp_w2_k[0], cmp_w2_v[0]]).astype(BF16)
    cmp_n, cmp_t = _compress(kb, pos8, w1, w2, w2.transpose(0, 2, 1))

    oc, sel = _cmp_select(q_nsa, cmp_n, cmp_t, s)

    bias_nsa = rel_bias[:, :HEADS].reshape(NUM_BUCKETS, GROUPS, REP) * LOG2E
    bias_swa = rel_bias[:, HEADS:].reshape(NUM_BUCKETS, GROUPS, REP) * LOG2E

    tk = SEL_TILE
    nb = tk // SEL_BLOCK
    posn = jnp.arange(s)
    onehot = ((posn // SEL_BLOCK) % nb)[:, None] == jnp.arange(nb)[None, :]
    aug = jnp.concatenate([onehot.astype(BF16), jnp.ones((s, 2), BF16),
                           jnp.zeros((s, kvd - nb - 2), BF16)], axis=1)
    kaug = jnp.concatenate([kv[:, 2 * kvd:3 * kvd], aug], axis=1)
    far = bias_nsa[NUM_BUCKETS - 1]
    far_hi = far.astype(BF16)
    far_lo = (far - far_hi.astype(F32)).astype(BF16)
    cst = jnp.stack([far_hi, far_lo], axis=1)
    cst = jnp.broadcast_to(cst[:, :, :, None], (GROUPS, 2, REP, QBLOCK)).reshape(GROUPS, 2, HQ)
    cst = jnp.pad(cst, ((0, 0), (0, 14), (0, 0)))
    ones_row = jnp.concatenate([jnp.ones((GROUPS, 1, s), BF16), jnp.zeros((GROUPS, 15, s), BF16)], axis=1)
    vs_aug = jnp.concatenate([vs_t, ones_row], axis=1)
    o_sel = _selected(q_nsa, sel, kaug, vs_aug, _sel_strip(bias_nsa, far, tk), cst, s, tk)

    pad_rows = lambda a, wdw: jnp.pad(a, ((wdw, 0), (0, 0)))
    pad_lanes = lambda a, wdw: jnp.pad(a, ((0, 0), (0, 0), (wdw, 0)))
    no_sink = jnp.zeros((GROUPS, 1, HQ), F32)
    o_win = _banded(q_nsa, pad_rows(kv[:, 3 * kvd:4 * kvd], NSA_WINDOW), pad_lanes(vw_t, NSA_WINDOW),
                    _band_strip(bias_nsa, NSA_WINDOW), no_sink, s, NSA_WINDOW, False, "nsa_window")
    sink = attn_sinks[0].reshape(GROUPS, 1, REP, 1).astype(F32) * LOG2E
    sink = jnp.broadcast_to(sink, (GROUPS, 1, REP, QBLOCK)).reshape(GROUPS, 1, HQ)
    o_swa = _banded(q_swa, pad_rows(kv[:, 4 * kvd:5 * kvd], SWA_WINDOW), pad_lanes(vswa_t, SWA_WINDOW),
                    _band_strip(bias_swa, SWA_WINDOW), sink, s, SWA_WINDOW, True, "swa_sink")

    flat = lambda a: a.reshape(qd, s)
    h1 = _merge(flat(oc), flat(o_sel), flat(o_win), flat(o_swa), gn, gab, x2,
                w_up_nsa[0].astype(BF16), w_up_swa[0].astype(BF16), w_out[0].astype(BF16),
                norm_mix_post, ts=512)
    out = _mlp(h1, norm_mlp_pre, w_mlp_in[0].astype(BF16), w_mlp_out[0].astype(BF16), norm_mlp_post,
               ts=1024, tf=1024)
    return out[None]
```

```python
import functools
import math

import jax
import jax.numpy as jnp
from jax import lax
from jax.experimental import pallas as pl
from jax.experimental.pallas import tpu as pltpu

F32 = jnp.float32
BF16 = jnp.bfloat16

HEAD_DIM = 64
GROUPS = 2
REP = 4
HEADS = GROUPS * REP
CMP_BLOCK = 32
CMP_STRIDE = 16
SEL_BLOCK = 64
SEL_TOPK = 16
NSA_WINDOW = 512
SWA_WINDOW = 128
QBLOCK = 128
NUM_BUCKETS = 32
MAX_DISTANCE = 1024
RMS_EPS = 1e-6
LOG2E = math.log2(math.e)
MASK_BIG = 2.0 ** 99
SEL_TILE = 1024
HQ = REP * QBLOCK
VMEM_LIMIT = 56 * 1024 * 1024

_NT = (((1,), (1,)), ((), ()))
_TN = (((0,), (0,)), ((), ()))


def _params(*sem):
    return pltpu.CompilerParams(dimension_semantics=sem, vmem_limit_bytes=VMEM_LIMIT)


def _t5_bucket(dist):
    max_exact = NUM_BUCKETS // 2
    d = jnp.maximum(dist, 0)
    df = jnp.maximum(d, 1).astype(jnp.float32)
    large = max_exact + (jnp.log(df / max_exact) / math.log(MAX_DISTANCE / max_exact)
                         * (NUM_BUCKETS - max_exact)).astype(jnp.int32)
    large = jnp.minimum(large, NUM_BUCKETS - 1)
    return jnp.where(d < max_exact, d, large)


def _rms(x, gain):
    return x * lax.rsqrt(jnp.mean(x * x, axis=-1, keepdims=True) + RMS_EPS) * gain


def _proj_kernel(x_ref, g_ref, wn_ref, wg_ref, wt_ref, kv_ref, gab_ref, qv_ref, gn_ref, *, nqv):
    u = _rms(x_ref[...], g_ref[...]).astype(BF16)
    kv_ref[...] = jnp.dot(u, wn_ref[...], preferred_element_type=F32).astype(BF16)
    gab_ref[...] = jnp.dot(u, wg_ref[...], preferred_element_type=F32)
    t = lax.dot_general(wt_ref[...], u, _NT, preferred_element_type=F32)
    qv_ref[...] = t[:nqv].astype(BF16)
    gn_ref[...] = t[nqv:]


def _project(x, gain, wn, wg, wt, nqv, ts):
    s, d = x.shape
    nn, ng, ntr = wn.shape[1], wg.shape[1], wt.shape[0]
    full = lambda shp: pl.BlockSpec(shp, lambda i: (0, 0))
    return pl.pallas_call(
        functools.partial(_proj_kernel, nqv=nqv),
        grid=(s // ts,),
        in_specs=[pl.BlockSpec((ts, d), lambda i: (i, 0)), full((1, d)),
                  full((d, nn)), full((d, ng)), full((ntr, d))],
        out_specs=[pl.BlockSpec((ts, nn), lambda i: (i, 0)),
                   pl.BlockSpec((ts, ng), lambda i: (i, 0)),
                   pl.BlockSpec((nqv, ts), lambda i: (0, i)),
                   pl.BlockSpec((ntr - nqv, ts), lambda i: (0, i))],
        out_shape=[jax.ShapeDtypeStruct((s, nn), BF16), jax.ShapeDtypeStruct((s, ng), F32),
                   jax.ShapeDtypeStruct((nqv, s), BF16), jax.ShapeDtypeStruct((ntr - nqv, s), F32)],
        compiler_params=_params("arbitrary"),
        name="in_proj",
    )(x, gain, wn, wg, wt)


def _gelu_tanh(x):
    return x * (0.5 * (1.0 + jnp.tanh(math.sqrt(2.0 / math.pi) * (x + 0.044715 * (x * x * x)))))


def _compress_kernel(kb_ref, pos_ref, w1_ref, w2_ref, w2t_ref, o_ref, ot_ref):
    kb = kb_ref[0, 0]
    half = kb.shape[1]
    n16 = kb.shape[0]
    first = jnp.dot(kb, w1_ref[0, :half, :], preferred_element_type=F32)
    second = jnp.dot(kb, w1_ref[0, half:, :], preferred_element_type=F32)
    posb = jnp.dot(pos_ref[0], w1_ref[0], preferred_element_type=F32)[0:1]
    pre = first + pltpu.roll(second, n16 - 1, 0) + posb
    h = _gelu_tanh(pre).astype(BF16)
    o_ref[0, 0] = jnp.dot(h, w2_ref[0], preferred_element_type=F32).astype(BF16)
    ot_ref[0, 0] = lax.dot_general(w2t_ref[0], h, _NT, preferred_element_type=F32).astype(BF16)


def _compress(kb, pos8, w1, w2, w2t):
    _, g, n16, cin2 = kb.shape
    hid = w1.shape[2]
    return pl.pallas_call(
        _compress_kernel,
        grid=(2, g),
        in_specs=[pl.BlockSpec((1, 1, n16, cin2), lambda a, b: (a, b, 0, 0)),
                  pl.BlockSpec((1, 8, 2 * cin2), lambda a, b: (a, 0, 0)),
                  pl.BlockSpec((1, 2 * cin2, hid), lambda a, b: (a, 0, 0)),
                  pl.BlockSpec((1, hid, HEAD_DIM), lambda a, b: (a, 0, 0)),
                  pl.BlockSpec((1, HEAD_DIM, hid), lambda a, b: (a, 0, 0))],
        out_specs=[pl.BlockSpec((1, 1, n16, HEAD_DIM), lambda a, b: (a, b, 0, 0)),
                   pl.BlockSpec((1, 1, HEAD_DIM, n16), lambda a, b: (a, b, 0, 0))],
        out_shape=[jax.ShapeDtypeStruct((2, g, n16, HEAD_DIM), BF16),
                   jax.ShapeDtypeStruct((2, g, HEAD_DIM, n16), BF16)],
        compiler_params=_params("arbitrary", "arbitrary"),
        name="kv_compress",
    )(kb, pos8, w1, w2, w2t)


def _heads_to_lanes(q_ref):
    return jnp.concatenate([q_ref[0, r * HEAD_DIM:(r + 1) * HEAD_DIM, :] for r in range(REP)], axis=1)


def _store_heads(o_ref, o_t):
    for r in range(REP):
        o_ref[0, r * HEAD_DIM:(r + 1) * HEAD_DIM, :] = o_t[:, r * QBLOCK:(r + 1) * QBLOCK]


def _cmpsel_body(q4, kc_ref, vct_ref, oc_ref, sel_ref, pg_ref, t0, *, rows, nomask, nsel, topk):
    s = jnp.dot(kc_ref[0, 0, 0:rows, :], q4, preferred_element_type=F32)
    mrows = rows - nomask
    n_io = nomask + lax.broadcasted_iota(jnp.int32, (mrows, HQ), 0)
    t_io = t0 + (lax.broadcasted_iota(jnp.int32, (mrows, HQ), 1) & (QBLOCK - 1))
    edge = jnp.where(n_io * CMP_STRIDE + (CMP_BLOCK - 1) <= t_io, s[nomask:], -jnp.inf)
    s = jnp.concatenate([s[:nomask], edge], axis=0) if nomask else edge
    m = jnp.max(s, axis=0, keepdims=True)
    m = jnp.where(m == -jnp.inf, 0.0, m)
    e = jnp.exp2(s - m)
    den = jnp.maximum(jnp.sum(e, axis=0, keepdims=True), 1e-30)
    p = e * (1.0 / den)
    _store_heads(oc_ref, jnp.dot(vct_ref[0, 0, :, 0:rows], p.astype(BF16), preferred_element_type=F32))

    pg = ((p[:, 0:QBLOCK] + p[:, QBLOCK:2 * QBLOCK]) + p[:, 2 * QBLOCK:3 * QBLOCK]) + p[:, 3 * QBLOCK:]
    pg_ref[0:8, :] = jnp.zeros((8, QBLOCK), F32)
    pg_ref[8:8 + rows, :] = pg
    ratio = SEL_BLOCK // CMP_STRIDE
    nblk = rows // ratio
    ps = pg_ref[pl.ds(7, nblk, stride=ratio), :]
    for o in range(1, ratio + 1):
        ps = ps + pg_ref[pl.ds(7 + o, nblk, stride=ratio), :]

    blk = lax.broadcasted_iota(jnp.int32, (nblk, QBLOCK), 0)
    cur = (t0 + lax.broadcasted_iota(jnp.int32, (nblk, QBLOCK), 1)) // SEL_BLOCK
    valid = blk <= cur
    forced = (blk == 0) | (blk == cur) | (blk == cur - 1)
    score = jnp.where(valid, jnp.where(forced, jnp.inf, ps), -jnp.inf)

    def pick_one(_, rem):
        top = jnp.max(rem, axis=0, keepdims=True)
        first = jnp.min(jnp.where(rem == top, blk, nblk), axis=0, keepdims=True)
        return jnp.where(blk == first, -jnp.inf, rem)

    rem = lax.fori_loop(0, topk, pick_one, score)
    sel_ref[0, 0:nblk, :] = jnp.where((rem == -jnp.inf) & valid, 0.0, -MASK_BIG).astype(BF16)
    if nblk < nsel:
        sel_ref[0, nblk:, :] = jnp.full((nsel - nblk, QBLOCK), -MASK_BIG, BF16)


def _cmpsel_kernel(q_ref, kc_ref, vct_ref, oc_ref, sel_ref, pg_ref, *, n16, nsel, topk, nvar):
    ci = pl.program_id(1)
    t0 = ci * QBLOCK
    q4 = _heads_to_lanes(q_ref)
    per = n16 // nvar
    cpv = per // (QBLOCK // CMP_STRIDE)
    for k in range(1, nvar + 1):
        @pl.when((ci >= (k - 1) * cpv) & (ci < k * cpv))
        def _(k=k):
            _cmpsel_body(q4, kc_ref, vct_ref, oc_ref, sel_ref, pg_ref, t0, rows=per * k,
                         nomask=max(per * (k - 1) - 8, 0), nsel=nsel, topk=topk)


def _cmp_select(q, kc_all, vct_all, s):
    n16 = s // CMP_STRIDE
    nsel = s // SEL_BLOCK
    topk = min(SEL_TOPK, nsel)
    nvar = max(n16 // 256, 1)
    return pl.pallas_call(
        functools.partial(_cmpsel_kernel, n16=n16, nsel=nsel, topk=topk, nvar=nvar),
        grid=(GROUPS, s // QBLOCK),
        in_specs=[pl.BlockSpec((1, REP * HEAD_DIM, QBLOCK), lambda g, c: (g, 0, c)),
                  pl.BlockSpec((1, 1, n16, HEAD_DIM), lambda g, c: (0, g, 0, 0)),
                  pl.BlockSpec((1, 1, HEAD_DIM, n16), lambda g, c: (1, g, 0, 0))],
        out_specs=[pl.BlockSpec((1, REP * HEAD_DIM, QBLOCK), lambda g, c: (g, 0, c)),
                   pl.BlockSpec((1, nsel, QBLOCK), lambda g, c: (g, 0, c))],
        out_shape=[jax.ShapeDtypeStruct((GROUPS, REP * HEAD_DIM, s), F32),
                   jax.ShapeDtypeStruct((GROUPS, nsel, s), BF16)],
        scratch_shapes=[pltpu.VMEM((n16 + 8, QBLOCK), F32)],
        compiler_params=_params("arbitrary", "arbitrary"),
        name="cmp_select",
    )(q, kc_all, vct_all)


def _sel_kernel(q_ref, sel_ref, kaug_ref, vt_ref, strip_ref, cst_ref, o_ref,
                qa_ref, s_ref, tmax_ref, m_ref, acc_ref, *, tk):
    g = pl.program_id(0)
    ci = pl.program_id(1)
    nb = tk // SEL_BLOCK
    cpt = tk // QBLOCK
    kd = GROUPS * HEAD_DIM
    qa_ref[...] = jnp.zeros(qa_ref.shape, BF16)
    row0 = pl.multiple_of(g * HEAD_DIM, HEAD_DIM)
    for b in range(2):
        for r in range(REP):
            qa_ref[b, pl.ds(row0, HEAD_DIM), r * QBLOCK:(r + 1) * QBLOCK] = (
                q_ref[0, r * HEAD_DIM:(r + 1) * HEAD_DIM, :])
        qa_ref[b, kd + nb:kd + nb + 16, :] = cst_ref[0]
    m_ref[...] = jnp.full(m_ref.shape, -jnp.inf, F32)
    acc_ref[...] = jnp.zeros(acc_ref.shape, F32)

    nt = ci // cpt + 1
    seq = jnp.maximum(nt, 2)
    lead = seq - nt

    def qk(i, buf):
        j = jnp.maximum(i - lead, 0)
        sb = sel_ref[0, pl.ds(pl.multiple_of(j * nb, nb), nb), :]
        sb = jnp.where(i < lead, -MASK_BIG, sb.astype(F32)).astype(BF16)
        for r in range(REP):
            qa_ref[buf, kd:kd + nb, r * QBLOCK:(r + 1) * QBLOCK] = sb
        k0 = pl.multiple_of(j * tk, tk)
        s = jnp.dot(kaug_ref[pl.ds(k0, tk), :], qa_ref[buf], preferred_element_type=F32)
        s_ref[buf] = s
        tmax_ref[buf] = jnp.max(s, axis=0, keepdims=True)

    def softmax_pv(i, buf, near):
        j = jnp.maximum(i - lead, 0)
        k0 = pl.multiple_of(j * tk, tk)
        s = s_ref[buf]
        if near:
            off = pl.multiple_of((2 * tk - QBLOCK) - (ci * QBLOCK - j * tk), QBLOCK)
            s = s + strip_ref[0, pl.ds(off, tk), :]
            tile_max = jnp.max(s, axis=0, keepdims=True)
        else:
            tile_max = tmax_ref[buf]
        m_prev = m_ref[...]
        m_new = jnp.maximum(m_prev, tile_max)
        alpha = jnp.exp2(m_prev - m_new)
        p = jnp.exp2(s - m_new)
        acc_ref[...] = alpha * acc_ref[...] + jnp.dot(vt_ref[0, :, pl.ds(k0, tk)], p.astype(BF16),
                                                      preferred_element_type=F32)
        m_ref[...] = m_new

    nfar = seq - 2
    qk(0, 0)

    def pair(p, carry):
        i = 2 * p
        qk(i + 1, 1)
        softmax_pv(i, 0, False)
        qk(i + 2, 0)
        softmax_pv(i + 1, 1, False)
        return carry

    lax.fori_loop(0, nfar // 2, pair, 0)
    i0 = 2 * (nfar // 2)

    @pl.when(nfar % 2 == 1)
    def _():
        qk(i0 + 1, 1)
        softmax_pv(i0, 0, False)
        qk(i0 + 2, 0)
        softmax_pv(i0 + 1, 1, True)
        softmax_pv(i0 + 2, 0, True)

    @pl.when(nfar % 2 == 0)
    def _():
        qk(i0 + 1, 1)
        softmax_pv(i0, 0, True)
        softmax_pv(i0 + 1, 1, True)

    den = jnp.maximum(acc_ref[HEAD_DIM:HEAD_DIM + 1, :], 1e-30)
    _store_heads(o_ref, acc_ref[0:HEAD_DIM, :] * (1.0 / den))


def _selected(q, sel, kaug, vt, strip, cst, s, tk):
    nsel = s // SEL_BLOCK
    nb = tk // SEL_BLOCK
    sl = strip.shape[1]
    return pl.pallas_call(
        functools.partial(_sel_kernel, tk=tk),
        grid=(GROUPS, s // QBLOCK),
        in_specs=[pl.BlockSpec((1, REP * HEAD_DIM, QBLOCK), lambda g, c: (g, 0, c)),
                  pl.BlockSpec((1, nsel, QBLOCK), lambda g, c: (g, 0, c)),
                  pl.BlockSpec(kaug.shape, lambda g, c: (0, 0)),
                  pl.BlockSpec((1, vt.shape[1], s), lambda g, c: (g, 0, 0)),
                  pl.BlockSpec((1, sl, HQ), lambda g, c: (g, 0, 0)),
                  pl.BlockSpec((1, 16, HQ), lambda g, c: (g, 0, 0))],
        out_specs=pl.BlockSpec((1, REP * HEAD_DIM, QBLOCK), lambda g, c: (g, 0, c)),
        out_shape=jax.ShapeDtypeStruct((GROUPS, REP * HEAD_DIM, s), F32),
        scratch_shapes=[pltpu.VMEM((2, kaug.shape[1], HQ), BF16), pltpu.VMEM((2, tk, HQ), F32),
                        pltpu.VMEM((2, 1, HQ), F32),
                        pltpu.VMEM((1, HQ), F32), pltpu.VMEM((vt.shape[1], HQ), F32)],
        compiler_params=_params("arbitrary", "arbitrary"),
        name="selected_attn",
    )(q, sel, kaug, vt, strip, cst)


def _band_kernel(q_ref, k_ref, vt_ref, strip_ref, sink_ref, o_ref, qa_ref, *, window, use_sink, cb):
    g = pl.program_id(0)
    base = pl.multiple_of(pl.program_id(1) * (QBLOCK * cb), QBLOCK * cb)
    nk = window + QBLOCK
    qa_ref[...] = jnp.zeros(qa_ref.shape, BF16)
    row0 = pl.multiple_of(g * HEAD_DIM, HEAD_DIM)
    for c in range(cb):
        for r in range(REP):
            qa_ref[c, pl.ds(row0, HEAD_DIM), r * QBLOCK:(r + 1) * QBLOCK] = (
                q_ref[0, r * HEAD_DIM:(r + 1) * HEAD_DIM, c * QBLOCK:(c + 1) * QBLOCK])
    row = lax.broadcasted_iota(jnp.int32, (nk, HQ), 0)
    for c in range(cb):
        t0 = base + c * QBLOCK
        s = jnp.dot(k_ref[pl.ds(t0, nk), :], qa_ref[c], preferred_element_type=F32) + strip_ref[0]
        s = jnp.where(row + (t0 - window) >= 0, s, -MASK_BIG)
        m = jnp.max(s, axis=0, keepdims=True)
        if use_sink:
            m = jnp.maximum(m, sink_ref[0])
        e = jnp.exp2(s - m)
        den = jnp.sum(e, axis=0, keepdims=True)
        if use_sink:
            den = den + jnp.exp2(sink_ref[0] - m)
        else:
            den = jnp.maximum(den, 1e-30)
        o_t = jnp.dot(vt_ref[0, :, pl.ds(t0, nk)], e.astype(BF16), preferred_element_type=F32) * (1.0 / den)
        for r in range(REP):
            o_ref[0, r * HEAD_DIM:(r + 1) * HEAD_DIM, c * QBLOCK:(c + 1) * QBLOCK] = (
                o_t[:, r * QBLOCK:(r + 1) * QBLOCK])


def _banded(q, kpad, vtpad, strip, sink, s, window, use_sink, name, cb=4):
    nk = window + QBLOCK
    return pl.pallas_call(
        functools.partial(_band_kernel, window=window, use_sink=use_sink, cb=cb),
        grid=(GROUPS, s // (QBLOCK * cb)),
        in_specs=[pl.BlockSpec((1, REP * HEAD_DIM, QBLOCK * cb), lambda g, c: (g, 0, c)),
                  pl.BlockSpec(kpad.shape, lambda g, c: (0, 0)),
                  pl.BlockSpec((1, HEAD_DIM, s + window), lambda g, c: (g, 0, 0)),
                  pl.BlockSpec((1, nk, HQ), lambda g, c: (g, 0, 0)),
                  pl.BlockSpec((1, 1, HQ), lambda g, c: (g, 0, 0))],
        out_specs=pl.BlockSpec((1, REP * HEAD_DIM, QBLOCK * cb), lambda g, c: (g, 0, c)),
        out_shape=jax.ShapeDtypeStruct((GROUPS, REP * HEAD_DIM, s), F32),
        scratch_shapes=[pltpu.VMEM((cb, GROUPS * HEAD_DIM, HQ), BF16)],
        compiler_params=_params("arbitrary", "arbitrary"),
        name=name,
    )(q, kpad, vtpad, strip, sink)


def _merge_kernel(oc_ref, os_ref, ow_ref, ob_ref, gn_ref, gab_ref, x_ref, wa_ref, wb_ref, wo_ref, gp_ref, o_ref):
    d = x_ref.shape[1]
    gn = jax.nn.sigmoid(gn_ref[...])
    parts = []
    for h in range(HEADS):
        rows = slice(h * HEAD_DIM, (h + 1) * HEAD_DIM)
        parts.append(gn[h:h + 1] * oc_ref[rows, :] + gn[HEADS + h:HEADS + h + 1] * os_ref[rows, :]
                     + gn[2 * HEADS + h:2 * HEADS + h + 1] * ow_ref[rows, :])
    oa_t = jnp.concatenate(parts, axis=0).astype(BF16)
    ya = lax.dot_general(oa_t, wa_ref[...], _TN, preferred_element_type=F32)
    yb = lax.dot_general(ob_ref[...].astype(BF16), wb_ref[...], _TN, preferred_element_type=F32)
    y = jax.nn.sigmoid(gab_ref[:, :d]) * ya + jax.nn.sigmoid(gab_ref[:, d:]) * yb
    z = jnp.dot(y.astype(BF16), wo_ref[...], preferred_element_type=F32)
    o_ref[...] = x_ref[...] + _rms(z, gp_ref[...])


def _merge(oc, os_, ow, ob, gn, gab, x, wa, wb, wo, gpost, ts):
    s, d = x.shape
    hd = oc.shape[0]
    tok = lambda rows: pl.BlockSpec((rows, ts), lambda i: (0, i))
    full = lambda shp: pl.BlockSpec(shp, lambda i: (0, 0))
    return pl.pallas_call(
        _merge_kernel,
        grid=(s // ts,),
        in_specs=[tok(hd), tok(hd), tok(hd), tok(hd), tok(gn.shape[0]),
                  pl.BlockSpec((ts, 2 * d), lambda i: (i, 0)), pl.BlockSpec((ts, d), lambda i: (i, 0)),
                  full((hd, d)), full((hd, d)), full((d, d)), full((1, d))],
        out_specs=pl.BlockSpec((ts, d), lambda i: (i, 0)),
        out_shape=jax.ShapeDtypeStruct((s, d), F32),
        compiler_params=_params("arbitrary"),
        name="merge_out_proj",
    )(oc, os_, ow, ob, gn, gab, x, wa, wb, wo, gpost)


def _mlp_kernel(h_ref, gpre_ref, w1_ref, w2_ref, gpost_ref, o_ref, u_ref, acc_ref):
    f = pl.program_id(1)

    @pl.when(f == 0)
    def _():
        u_ref[...] = _rms(h_ref[...], gpre_ref[...]).astype(BF16)
        acc_ref[...] = jnp.zeros(acc_ref.shape, F32)

    a = jnp.maximum(jnp.dot(u_ref[...], w1_ref[...], preferred_element_type=F32), 0.0)
    acc_ref[...] += jnp.dot((a * a).astype(BF16), w2_ref[...], preferred_element_type=F32)

    @pl.when(f == pl.num_programs(1) - 1)
    def _():
        o_ref[...] = h_ref[...] + _rms(acc_ref[...], gpost_ref[...])


def _mlp(h, gpre, w1, w2, gpost, ts, tf):
    s, d = h.shape
    dff = w1.shape[1]
    return pl.pallas_call(
        _mlp_kernel,
        grid=(s // ts, dff // tf),
        in_specs=[pl.BlockSpec((ts, d), lambda i, f: (i, 0)), pl.BlockSpec((1, d), lambda i, f: (0, 0)),
                  pl.BlockSpec((d, tf), lambda i, f: (0, f)), pl.BlockSpec((tf, d), lambda i, f: (f, 0)),
                  pl.BlockSpec((1, d), lambda i, f: (0, 0))],
        out_specs=pl.BlockSpec((ts, d), lambda i, f: (i, 0)),
        out_shape=jax.ShapeDtypeStruct((s, d), F32),
        scratch_shapes=[pltpu.VMEM((ts, d), BF16), pltpu.VMEM((ts, d), F32)],
        compiler_params=_params("arbitrary", "arbitrary"),
        name="mlp_relu2",
    )(h, gpre, w1, w2, gpost)


def _bias_of(dist, bias_heads):
    bucket = _t5_bucket(dist)[..., None, None]
    out = jnp.zeros(dist.shape + bias_heads.shape[1:], F32)
    for b in range(NUM_BUCKETS):
        out = jnp.where(bucket == b, bias_heads[b], out)
    return out


def _band_strip(bias_heads, window):
    nk = window + QBLOCK
    dist = jnp.arange(QBLOCK)[None, :] - jnp.arange(nk)[:, None] + window
    ok = (dist >= 0) & (dist < window)
    tab = _bias_of(dist, bias_heads)
    tab = jnp.where(ok[:, :, None, None], tab, -MASK_BIG)
    return tab.transpose(2, 0, 3, 1).reshape(GROUPS, nk, HQ)


def _sel_strip(bias_heads, far, tk):
    rows = 3 * tk - QBLOCK
    dist = jnp.arange(QBLOCK)[None, :] - jnp.arange(rows)[:, None] + (2 * tk - QBLOCK)
    tab = _bias_of(dist, bias_heads) - far[None, None]
    tab = jnp.where((dist >= 0)[:, :, None, None], tab, -MASK_BIG)
    return tab.transpose(2, 0, 3, 1).reshape(GROUPS, rows, HQ)


def kernel(x, norm_mix_pre, norm_mix_post, norm_mlp_pre, norm_mlp_post, w_in,
           cmp_pos_k, cmp_w1_k, cmp_w2_k, cmp_pos_v, cmp_w1_v, cmp_w2_v,
           attn_sinks, rel_bias, w_up_nsa, w_up_swa, w_out, w_mlp_in, w_mlp_out):
    b, s, d = x.shape
    assert b == 1 and s % SEL_TILE == 0 and w_in.shape[0] == 1
    qd = HEADS * HEAD_DIM
    kvd = GROUPS * HEAD_DIM
    sizes = (qd,) + (kvd,) * 6 + (3 * HEADS, qd, kvd, kvd, d, d)
    offs = [0]
    for z in sizes:
        offs.append(offs[-1] + z)
    w = w_in[0]
    col = lambda i: w[:, offs[i]:offs[i + 1]]
    (w_qn, w_kc, w_vc, w_ks, w_vs, w_kw, w_vw, w_gn, w_qs, w_k_s, w_v_s, w_ga, w_gb) = [col(i) for i in range(13)]
    scale = HEAD_DIM ** -0.5 * LOG2E
    w_gn = w_gn.reshape(d, HEADS, 3).transpose(0, 2, 1).reshape(d, 3 * HEADS)
    w_gn = jnp.pad(w_gn, ((0, 0), (0, 8)))
    wn = jnp.concatenate([w_kc, w_vc, w_ks, w_kw, w_k_s], axis=1).astype(BF16)
    wg = jnp.concatenate([w_ga, w_gb], axis=1).astype(BF16)
    wt = jnp.concatenate([w_qn * scale, w_qs * scale, w_vs, w_vw, w_v_s, w_gn], axis=1).T.astype(BF16)
    nqv = 2 * qd + 3 * kvd

    x2 = x[0]
    kv, gab, qv, gn = _project(x2, norm_mix_pre, wn, wg, wt, nqv, ts=512)
    q_nsa = qv[0:qd].reshape(GROUPS, REP * HEAD_DIM, s)
    q_swa = qv[qd:2 * qd].reshape(GROUPS, REP * HEAD_DIM, s)
    vs_t = qv[2 * qd:2 * qd + kvd].reshape(GROUPS, HEAD_DIM, s)
    vw_t = qv[2 * qd + kvd:2 * qd + 2 * kvd].reshape(GROUPS, HEAD_DIM, s)
    vswa_t = qv[2 * qd + 2 * kvd:].reshape(GROUPS, HEAD_DIM, s)

    n16 = s // CMP_STRIDE
    kb = kv[:, :2 * kvd].reshape(n16, CMP_STRIDE, 2, GROUPS, HEAD_DIM).transpose(2, 3, 0, 1, 4)
    kb = kb.reshape(2, GROUPS, n16, CMP_STRIDE * HEAD_DIM)
    pos8 = jnp.stack([cmp_pos_k[0], cmp_pos_v[0]]).reshape(2, 1, CMP_BLOCK * HEAD_DIM)
    pos8 = jnp.broadcast_to(pos8, (2, 8, CMP_BLOCK * HEAD_DIM)).astype(BF16)
    w1 = jnp.stack([cmp_w1_k[0], cmp_w1_v[0]]).astype(BF16)
    w2 = jnp.stack([cmp_w2_k[0], cmp_w2_v[0]]).astype(BF16)
    cmp_n, cmp_t = _compress(kb, pos8, w1, w2, w2.transpose(0, 2, 1))

    oc, sel = _cmp_select(q_nsa, cmp_n, cmp_t, s)

    bias_nsa = rel_bias[:, :HEADS].reshape(NUM_BUCKETS, GROUPS, REP) * LOG2E
    bias_swa = rel_bias[:, HEADS:].reshape(NUM_BUCKETS, GROUPS, REP) * LOG2E

    tk = SEL_TILE
    nb = tk // SEL_BLOCK
    posn = jnp.arange(s)
    onehot = ((posn // SEL_BLOCK) % nb)[:, None] == jnp.arange(nb)[None, :]
    aug = jnp.concatenate([onehot.astype(BF16), jnp.ones((s, 2), BF16),
                           jnp.zeros((s, kvd - nb - 2), BF16)], axis=1)
    kaug = jnp.concatenate([kv[:, 2 * kvd:3 * kvd], aug], axis=1)
    far = bias_nsa[NUM_BUCKETS - 1]
    far_hi = far.astype(BF16)
    far_lo = (far - far_hi.astype(F32)).astype(BF16)
    cst = jnp.stack([far_hi, far_lo], axis=1)
    cst = jnp.broadcast_to(cst[:, :, :, None], (GROUPS, 2, REP, QBLOCK)).reshape(GROUPS, 2, HQ)
    cst = jnp.pad(cst, ((0, 0), (0, 14), (0, 0)))
    ones_row = jnp.concatenate([jnp.ones((GROUPS, 1, s), BF16), jnp.zeros((GROUPS, 15, s), BF16)], axis=1)
    vs_aug = jnp.concatenate([vs_t, ones_row], axis=1)
    o_sel = _selected(q_nsa, sel, kaug, vs_aug, _sel_strip(bias_nsa, far, tk), cst, s, tk)

    pad_rows = lambda a, wdw: jnp.pad(a, ((wdw, 0), (0, 0)))
    pad_lanes = lambda a, wdw: jnp.pad(a, ((0, 0), (0, 0), (wdw, 0)))
    no_sink = jnp.zeros((GROUPS, 1, HQ), F32)
    o_win = _banded(q_nsa, pad_rows(kv[:, 3 * kvd:4 * kvd], NSA_WINDOW), pad_lanes(vw_t, NSA_WINDOW),
                    _band_strip(bias_nsa, NSA_WINDOW), no_sink, s, NSA_WINDOW, False, "nsa_window")
    sink = attn_sinks[0].reshape(GROUPS, 1, REP, 1).astype(F32) * LOG2E
    sink = jnp.broadcast_to(sink, (GROUPS, 1, REP, QBLOCK)).reshape(GROUPS, 1, HQ)
    o_swa = _banded(q_swa, pad_rows(kv[:, 4 * kvd:5 * kvd], SWA_WINDOW), pad_lanes(vswa_t, SWA_WINDOW),
                    _band_strip(bias_swa, SWA_WINDOW), sink, s, SWA_WINDOW, True, "swa_sink")

    flat = lambda a: a.reshape(qd, s)
    h1 = _merge(flat(oc), flat(o_sel), flat(o_win), flat(o_swa), gn, gab, x2,
                w_up_nsa[0].astype(BF16), w_up_swa[0].astype(BF16), w_out[0].astype(BF16),
                norm_mix_post, ts=512)
    out = _mlp(h1, norm_mlp_pre, w_mlp_in[0].astype(BF16), w_mlp_out[0].astype(BF16), norm_mlp_post,
               ts=1024, tf=1024)
    return out[None]
```

```python
import functools
import math

import jax
import jax.numpy as jnp
from jax import lax
from jax.experimental import pallas as pl
from jax.experimental.pallas import tpu as pltpu

F32 = jnp.float32
BF16 = jnp.bfloat16

HEAD_DIM = 64
GROUPS = 2
REP = 4
HEADS = GROUPS * REP
CMP_BLOCK = 32
CMP_STRIDE = 16
SEL_BLOCK = 64
SEL_TOPK = 16
NSA_WINDOW = 512
SWA_WINDOW = 128
QBLOCK = 128
NUM_BUCKETS = 32
MAX_DISTANCE = 1024
RMS_EPS = 1e-6
LOG2E = math.log2(math.e)
MASK_BIG = 2.0 ** 99
SEL_TILE = 1024
HQ = REP * QBLOCK
VMEM_LIMIT = 56 * 1024 * 1024

_NT = (((1,), (1,)), ((), ()))
_TN = (((0,), (0,)), ((), ()))


def _params(*sem):
    return pltpu.CompilerParams(dimension_semantics=sem, vmem_limit_bytes=VMEM_LIMIT)


def _t5_bucket(dist):
    max_exact = NUM_BUCKETS // 2
    d = jnp.maximum(dist, 0)
    df = jnp.maximum(d, 1).astype(jnp.float32)
    large = max_exact + (jnp.log(df / max_exact) / math.log(MAX_DISTANCE / max_exact)
                         * (NUM_BUCKETS - max_exact)).astype(jnp.int32)
    large = jnp.minimum(large, NUM_BUCKETS - 1)
    return jnp.where(d < max_exact, d, large)


def _rms(x, gain):
    return x * lax.rsqrt(jnp.mean(x * x, axis=-1, keepdims=True) + RMS_EPS) * gain


def _proj_kernel(x_ref, g_ref, wn_ref, wg_ref, wt_ref, kv_ref, gab_ref, qv_ref, gn_ref, *, nqv):
    u = _rms(x_ref[...], g_ref[...]).astype(BF16)
    kv_ref[...] = jnp.dot(u, wn_ref[...], preferred_element_type=F32).astype(BF16)
    gab_ref[...] = jnp.dot(u, wg_ref[...], preferred_element_type=F32).astype(BF16)
    t = lax.dot_general(wt_ref[...], u, _NT, preferred_element_type=F32)
    qv_ref[...] = t[:nqv].astype(BF16)
    gn_ref[...] = t[nqv:]


def _project(x, gain, wn, wg, wt, nqv, ts):
    s, d = x.shape
    nn, ng, ntr = wn.shape[1], wg.shape[1], wt.shape[0]
    full = lambda shp: pl.BlockSpec(shp, lambda i: (0, 0))
    return pl.pallas_call(
        functools.partial(_proj_kernel, nqv=nqv),
        grid=(s // ts,),
        in_specs=[pl.BlockSpec((ts, d), lambda i: (i, 0)), full((1, d)),
                  full((d, nn)), full((d, ng)), full((ntr, d))],
        out_specs=[pl.BlockSpec((ts, nn), lambda i: (i, 0)),
                   pl.BlockSpec((ts, ng), lambda i: (i, 0)),
                   pl.BlockSpec((nqv, ts), lambda i: (0, i)),
                   pl.BlockSpec((ntr - nqv, ts), lambda i: (0, i))],
        out_shape=[jax.ShapeDtypeStruct((s, nn), BF16), jax.ShapeDtypeStruct((s, ng), BF16),
                   jax.ShapeDtypeStruct((nqv, s), BF16), jax.ShapeDtypeStruct((ntr - nqv, s), F32)],
        compiler_params=_params("arbitrary"),
        name="in_proj",
    )(x, gain, wn, wg, wt)


def _gelu_tanh(x):
    return x * (0.5 * (1.0 + jnp.tanh(math.sqrt(2.0 / math.pi) * (x + 0.044715 * (x * x * x)))))


def _compress_kernel(kb_ref, pos_ref, w1_ref, w2_ref, w2t_ref, o_ref, ot_ref):
    kb = kb_ref[0, 0]
    half = kb.shape[1]
    n16 = kb.shape[0]
    first = jnp.dot(kb, w1_ref[0, :half, :], preferred_element_type=F32)
    second = jnp.dot(kb, w1_ref[0, half:, :], preferred_element_type=F32)
    posb = jnp.dot(pos_ref[0], w1_ref[0], preferred_element_type=F32)[0:1]
    pre = first + pltpu.roll(second, n16 - 1, 0) + posb
    h = _gelu_tanh(pre).astype(BF16)
    o_ref[0, 0] = jnp.dot(h, w2_ref[0], preferred_element_type=F32).astype(BF16)
    ot_ref[0, 0] = lax.dot_general(w2t_ref[0], h, _NT, preferred_element_type=F32).astype(BF16)


def _compress(kb, pos8, w1, w2, w2t):
    _, g, n16, cin2 = kb.shape
    hid = w1.shape[2]
    return pl.pallas_call(
        _compress_kernel,
        grid=(2, g),
        in_specs=[pl.BlockSpec((1, 1, n16, cin2), lambda a, b: (a, b, 0, 0)),
                  pl.BlockSpec((1, 8, 2 * cin2), lambda a, b: (a, 0, 0)),
                  pl.BlockSpec((1, 2 * cin2, hid), lambda a, b: (a, 0, 0)),
                  pl.BlockSpec((1, hid, HEAD_DIM), lambda a, b: (a, 0, 0)),
                  pl.BlockSpec((1, HEAD_DIM, hid), lambda a, b: (a, 0, 0))],
        out_specs=[pl.BlockSpec((1, 1, n16, HEAD_DIM), lambda a, b: (a, b, 0, 0)),
                   pl.BlockSpec((1, 1, HEAD_DIM, n16), lambda a, b: (a, b, 0, 0))],
        out_shape=[jax.ShapeDtypeStruct((2, g, n16, HEAD_DIM), BF16),
                   jax.ShapeDtypeStruct((2, g, HEAD_DIM, n16), BF16)],
        compiler_params=_params("arbitrary", "arbitrary"),
        name="kv_compress",
    )(kb, pos8, w1, w2, w2t)


def _heads_to_lanes(q_ref):
    return jnp.concatenate([q_ref[0, r * HEAD_DIM:(r + 1) * HEAD_DIM, :] for r in range(REP)], axis=1)


def _store_heads(o_ref, o_t):
    for r in range(REP):
        o_ref[0, r * HEAD_DIM:(r + 1) * HEAD_DIM, :] = o_t[:, r * QBLOCK:(r + 1) * QBLOCK].astype(o_ref.dtype)


def _cmpsel_body(q4, kc_ref, vct_ref, oc_ref, sel_ref, pg_ref, t0, *, rows, nomask, nsel, topk):
    s = jnp.dot(kc_ref[0, 0, 0:rows, :], q4, preferred_element_type=F32)
    mrows = rows - nomask
    n_io = nomask + lax.broadcasted_iota(jnp.int32, (mrows, HQ), 0)
    t_io = t0 + (lax.broadcasted_iota(jnp.int32, (mrows, HQ), 1) & (QBLOCK - 1))
    edge = jnp.where(n_io * CMP_STRIDE + (CMP_BLOCK - 1) <= t_io, s[nomask:], -jnp.inf)
    s = jnp.concatenate([s[:nomask], edge], axis=0) if nomask else edge
    m = jnp.max(s, axis=0, keepdims=True)
    m = jnp.where(m == -jnp.inf, 0.0, m)
    e = jnp.exp2(s - m)
    den = jnp.maximum(jnp.sum(e, axis=0, keepdims=True), 1e-30)
    p = e * (1.0 / den)
    _store_heads(oc_ref, jnp.dot(vct_ref[0, 0, :, 0:rows], p.astype(BF16), preferred_element_type=F32))

    pg = ((p[:, 0:QBLOCK] + p[:, QBLOCK:2 * QBLOCK]) + p[:, 2 * QBLOCK:3 * QBLOCK]) + p[:, 3 * QBLOCK:]
    pg_ref[0:8, :] = jnp.zeros((8, QBLOCK), F32)
    pg_ref[8:8 + rows, :] = pg
    ratio = SEL_BLOCK // CMP_STRIDE
    nblk = rows // ratio
    ps = pg_ref[pl.ds(7, nblk, stride=ratio), :]
    for o in range(1, ratio + 1):
        ps = ps + pg_ref[pl.ds(7 + o, nblk, stride=ratio), :]

    blk = lax.broadcasted_iota(jnp.int32, (nblk, QBLOCK), 0)
    cur = (t0 + lax.broadcasted_iota(jnp.int32, (nblk, QBLOCK), 1)) // SEL_BLOCK
    valid = blk <= cur
    forced = (blk == 0) | (blk == cur) | (blk == cur - 1)
    score = jnp.where(valid, jnp.where(forced, jnp.inf, ps), -jnp.inf)

    def pick_one(_, rem):
        top = jnp.max(rem, axis=0, keepdims=True)
        first = jnp.min(jnp.where(rem == top, blk, nblk), axis=0, keepdims=True)
        return jnp.where(blk == first, -jnp.inf, rem)

    rem = lax.fori_loop(0, topk, pick_one, score)
    sel_ref[0, 0:nblk, :] = jnp.where((rem == -jnp.inf) & valid, 0.0, -MASK_BIG).astype(BF16)
    if nblk < nsel:
        sel_ref[0, nblk:, :] = jnp.full((nsel - nblk, QBLOCK), -MASK_BIG, BF16)


def _cmpsel_kernel(q_ref, kc_ref, vct_ref, oc_ref, sel_ref, pg_ref, *, n16, nsel, topk, nvar):
    ci = pl.program_id(1)
    t0 = ci * QBLOCK
    q4 = _heads_to_lanes(q_ref)
    per = n16 // nvar
    cpv = per // (QBLOCK // CMP_STRIDE)
    for k in range(1, nvar + 1):
        @pl.when((ci >= (k - 1) * cpv) & (ci < k * cpv))
        def _(k=k):
            _cmpsel_body(q4, kc_ref, vct_ref, oc_ref, sel_ref, pg_ref, t0, rows=per * k,
                         nomask=max(per * (k - 1) - 8, 0), nsel=nsel, topk=topk)


def _cmp_select(q, kc_all, vct_all, s):
    n16 = s // CMP_STRIDE
    nsel = s // SEL_BLOCK
    topk = min(SEL_TOPK, nsel)
    nvar = max(n16 // 256, 1)
    return pl.pallas_call(
        functools.partial(_cmpsel_kernel, n16=n16, nsel=nsel, topk=topk, nvar=nvar),
        grid=(GROUPS, s // QBLOCK),
        in_specs=[pl.BlockSpec((1, REP * HEAD_DIM, QBLOCK), lambda g, c: (g, 0, c)),
                  pl.BlockSpec((1, 1, n16, HEAD_DIM), lambda g, c: (0, g, 0, 0)),
                  pl.BlockSpec((1, 1, HEAD_DIM, n16), lambda g, c: (1, g, 0, 0))],
        out_specs=[pl.BlockSpec((1, REP * HEAD_DIM, QBLOCK), lambda g, c: (g, 0, c)),
                   pl.BlockSpec((1, nsel, QBLOCK), lambda g, c: (g, 0, c))],
        out_shape=[jax.ShapeDtypeStruct((GROUPS, REP * HEAD_DIM, s), BF16),
                   jax.ShapeDtypeStruct((GROUPS, nsel, s), BF16)],
        scratch_shapes=[pltpu.VMEM((n16 + 8, QBLOCK), F32)],
        compiler_params=_params("arbitrary", "arbitrary"),
        name="cmp_select",
    )(q, kc_all, vct_all)


def _sel_kernel(q_ref, sel_ref, kaug_ref, vt_ref, strip_ref, cst_ref, o_ref,
                qa_ref, s_ref, tmax_ref, m_ref, acc_ref, *, tk):
    g = pl.program_id(0)
    ci = pl.program_id(1)
    nb = tk // SEL_BLOCK
    cpt = tk // QBLOCK
    kd = GROUPS * HEAD_DIM
    qa_ref[...] = jnp.zeros(qa_ref.shape, BF16)
    row0 = pl.multiple_of(g * HEAD_DIM, HEAD_DIM)
    for b in range(2):
        for r in range(REP):
            qa_ref[b, pl.ds(row0, HEAD_DIM), r * QBLOCK:(r + 1) * QBLOCK] = (
                q_ref[0, r * HEAD_DIM:(r + 1) * HEAD_DIM, :])
        qa_ref[b, kd + nb:kd + nb + 16, :] = cst_ref[0]
    m_ref[...] = jnp.full(m_ref.shape, -jnp.inf, F32)
    acc_ref[...] = jnp.zeros(acc_ref.shape, F32)

    nt = ci // cpt + 1
    seq = jnp.maximum(nt, 2)
    lead = seq - nt

    def qk(i, buf):
        j = jnp.maximum(i - lead, 0)
        sb = sel_ref[0, pl.ds(pl.multiple_of(j * nb, nb), nb), :]
        sb = jnp.where(i < lead, -MASK_BIG, sb.astype(F32)).astype(BF16)
        for r in range(REP):
            qa_ref[buf, kd:kd + nb, r * QBLOCK:(r + 1) * QBLOCK] = sb
        k0 = pl.multiple_of(j * tk, tk)
        s = jnp.dot(kaug_ref[pl.ds(k0, tk), :], qa_ref[buf], preferred_element_type=F32)
        s_ref[buf] = s
        tmax_ref[buf] = jnp.max(s, axis=0, keepdims=True)

    def softmax_pv(i, buf, near):
        j = jnp.maximum(i - lead, 0)
        k0 = pl.multiple_of(j * tk, tk)
        s = s_ref[buf]
        if near:
            off = pl.multiple_of((2 * tk - QBLOCK) - (ci * QBLOCK - j * tk), QBLOCK)
            s = s + strip_ref[0, pl.ds(off, tk), :]
            tile_max = jnp.max(s, axis=0, keepdims=True)
        else:
            tile_max = tmax_ref[buf]
        m_prev = m_ref[...]
        m_new = jnp.maximum(m_prev, tile_max)
        alpha = jnp.exp2(m_prev - m_new)
        p = jnp.exp2(s - m_new)
        acc_ref[...] = alpha * acc_ref[...] + jnp.dot(vt_ref[0, :, pl.ds(k0, tk)], p.astype(BF16),
                                                      preferred_element_type=F32)
        m_ref[...] = m_new

    nfar = seq - 2
    qk(0, 0)

    def pair(p, carry):
        i = 2 * p
        qk(i + 1, 1)
        softmax_pv(i, 0, False)
        qk(i + 2, 0)
        softmax_pv(i + 1, 1, False)
        return carry

    lax.fori_loop(0, nfar // 2, pair, 0)
    i0 = 2 * (nfar // 2)

    @pl.when(nfar % 2 == 1)
    def _():
        qk(i0 + 1, 1)
        softmax_pv(i0, 0, False)
        qk(i0 + 2, 0)
        softmax_pv(i0 + 1, 1, True)
        softmax_pv(i0 + 2, 0, True)

    @pl.when(nfar % 2 == 0)
    def _():
        qk(i0 + 1, 1)
        softmax_pv(i0, 0, True)
        softmax_pv(i0 + 1, 1, True)

    den = jnp.maximum(acc_ref[HEAD_DIM:HEAD_DIM + 1, :], 1e-30)
    _store_heads(o_ref, acc_ref[0:HEAD_DIM, :] * (1.0 / den))


def _selected(q, sel, kaug, vt, strip, cst, s, tk):
    nsel = s // SEL_BLOCK
    nb = tk // SEL_BLOCK
    sl = strip.shape[1]
    return pl.pallas_call(
        functools.partial(_sel_kernel, tk=tk),
        grid=(GROUPS, s // QBLOCK),
        in_specs=[pl.BlockSpec((1, REP * HEAD_DIM, QBLOCK), lambda g, c: (g, 0, c)),
                  pl.BlockSpec((1, nsel, QBLOCK), lambda g, c: (g, 0, c)),
                  pl.BlockSpec(kaug.shape, lambda g, c: (0, 0)),
                  pl.BlockSpec((1, vt.shape[1], s), lambda g, c: (g, 0, 0)),
                  pl.BlockSpec((1, sl, HQ), lambda g, c: (g, 0, 0)),
                  pl.BlockSpec((1, 16, HQ), lambda g, c: (g, 0, 0))],
        out_specs=pl.BlockSpec((1, REP * HEAD_DIM, QBLOCK), lambda g, c: (g, 0, c)),
        out_shape=jax.ShapeDtypeStruct((GROUPS, REP * HEAD_DIM, s), BF16),
        scratch_shapes=[pltpu.VMEM((2, kaug.shape[1], HQ), BF16), pltpu.VMEM((2, tk, HQ), F32),
                        pltpu.VMEM((2, 1, HQ), F32),
                        pltpu.VMEM((1, HQ), F32), pltpu.VMEM((vt.shape[1], HQ), F32)],
        compiler_params=_params("arbitrary", "arbitrary"),
        name="selected_attn",
    )(q, sel, kaug, vt, strip, cst)


def _band_kernel(q_ref, k_ref, vt_ref, strip_ref, sink_ref, o_ref, qa_ref, *, window, use_sink, cb):
    g = pl.program_id(0)
    base = pl.multiple_of(pl.program_id(1) * (QBLOCK * cb), QBLOCK * cb)
    nk = window + QBLOCK
    qa_ref[...] = jnp.zeros(qa_ref.shape, BF16)
    row0 = pl.multiple_of(g * HEAD_DIM, HEAD_DIM)
    for c in range(cb):
        for r in range(REP):
            qa_ref[c, pl.ds(row0, HEAD_DIM), r * QBLOCK:(r + 1) * QBLOCK] = (
                q_ref[0, r * HEAD_DIM:(r + 1) * HEAD_DIM, c * QBLOCK:(c + 1) * QBLOCK])
    row = lax.broadcasted_iota(jnp.int32, (nk, HQ), 0)
    for c in range(cb):
        t0 = base + c * QBLOCK
        s = jnp.dot(k_ref[pl.ds(t0, nk), :], qa_ref[c], preferred_element_type=F32) + strip_ref[0]
        s = jnp.where(row + (t0 - window) >= 0, s, -MASK_BIG)
        m = jnp.max(s, axis=0, keepdims=True)
        if use_sink:
            m = jnp.maximum(m, sink_ref[0])
        e = jnp.exp2(s - m)
        den = jnp.sum(e, axis=0, keepdims=True)
        if use_sink:
            den = den + jnp.exp2(sink_ref[0] - m)
        else:
            den = jnp.maximum(den, 1e-30)
        o_t = jnp.dot(vt_ref[0, :, pl.ds(t0, nk)], e.astype(BF16), preferred_element_type=F32) * (1.0 / den)
        for r in range(REP):
            o_ref[0, r * HEAD_DIM:(r + 1) * HEAD_DIM, c * QBLOCK:(c + 1) * QBLOCK] = (
                o_t[:, r * QBLOCK:(r + 1) * QBLOCK].astype(o_ref.dtype))


def _banded(q, kpad, vtpad, strip, sink, s, window, use_sink, name, cb=4):
    nk = window + QBLOCK
    return pl.pallas_call(
        functools.partial(_band_kernel, window=window, use_sink=use_sink, cb=cb),
        grid=(GROUPS, s // (QBLOCK * cb)),
        in_specs=[pl.BlockSpec((1, REP * HEAD_DIM, QBLOCK * cb), lambda g, c: (g, 0, c)),
                  pl.BlockSpec(kpad.shape, lambda g, c: (0, 0)),
                  pl.BlockSpec((1, HEAD_DIM, s + window), lambda g, c: (g, 0, 0)),
                  pl.BlockSpec((1, nk, HQ), lambda g, c: (g, 0, 0)),
                  pl.BlockSpec((1, 1, HQ), lambda g, c: (g, 0, 0))],
        out_specs=pl.BlockSpec((1, REP * HEAD_DIM, QBLOCK * cb), lambda g, c: (g, 0, c)),
        out_shape=jax.ShapeDtypeStruct((GROUPS, REP * HEAD_DIM, s), BF16),
        scratch_shapes=[pltpu.VMEM((cb, GROUPS * HEAD_DIM, HQ), BF16)],
        compiler_params=_params("arbitrary", "arbitrary"),
        name=name,
    )(q, kpad, vtpad, strip, sink)


def _merge_kernel(oc_ref, os_ref, ow_ref, ob_ref, gn_ref, gab_ref, x_ref, wa_ref, wb_ref, wo_ref, gp_ref, o_ref):
    d = x_ref.shape[1]
    gn = jax.nn.sigmoid(gn_ref[...])
    parts = []
    for h in range(HEADS):
        rows = slice(h * HEAD_DIM, (h + 1) * HEAD_DIM)
        parts.append(gn[h:h + 1] * oc_ref[rows, :].astype(F32)
                     + gn[HEADS + h:HEADS + h + 1] * os_ref[rows, :].astype(F32)
                     + gn[2 * HEADS + h:2 * HEADS + h + 1] * ow_ref[rows, :].astype(F32))
    oa_t = jnp.concatenate(parts, axis=0).astype(BF16)
    ya = lax.dot_general(oa_t, wa_ref[...], _TN, preferred_element_type=F32)
    yb = lax.dot_general(ob_ref[...], wb_ref[...], _TN, preferred_element_type=F32)
    y = (jax.nn.sigmoid(gab_ref[:, :d].astype(F32)) * ya
         + jax.nn.sigmoid(gab_ref[:, d:].astype(F32)) * yb)
    z = jnp.dot(y.astype(BF16), wo_ref[...], preferred_element_type=F32)
    o_ref[...] = x_ref[...] + _rms(z, gp_ref[...])


def _merge(oc, os_, ow, ob, gn, gab, x, wa, wb, wo, gpost, ts):
    s, d = x.shape
    hd = oc.shape[0]
    tok = lambda rows: pl.BlockSpec((rows, ts), lambda i: (0, i))
    full = lambda shp: pl.BlockSpec(shp, lambda i: (0, 0))
    return pl.pallas_call(
        _merge_kernel,
        grid=(s // ts,),
        in_specs=[tok(hd), tok(hd), tok(hd), tok(hd), tok(gn.shape[0]),
                  pl.BlockSpec((ts, 2 * d), lambda i: (i, 0)), pl.BlockSpec((ts, d), lambda i: (i, 0)),
                  full((hd, d)), full((hd, d)), full((d, d)), full((1, d))],
        out_specs=pl.BlockSpec((ts, d), lambda i: (i, 0)),
        out_shape=jax.ShapeDtypeStruct((s, d), F32),
        compiler_params=_params("arbitrary"),
        name="merge_out_proj",
    )(oc, os_, ow, ob, gn, gab, x, wa, wb, wo, gpost)


def _mlp_kernel(h_ref, gpre_ref, w1_ref, w2_ref, gpost_ref, o_ref, u_ref, acc_ref):
    f = pl.program_id(1)

    @pl.when(f == 0)
    def _():
        u_ref[...] = _rms(h_ref[...], gpre_ref[...]).astype(BF16)
        acc_ref[...] = jnp.zeros(acc_ref.shape, F32)

    a = jnp.maximum(jnp.dot(u_ref[...], w1_ref[...], preferred_element_type=F32), 0.0)
    acc_ref[...] += jnp.dot((a * a).astype(BF16), w2_ref[...], preferred_element_type=F32)

    @pl.when(f == pl.num_programs(1) - 1)
    def _():
        o_ref[...] = h_ref[...] + _rms(acc_ref[...], gpost_ref[...])


def _mlp(h, gpre, w1, w2, gpost, ts, tf):
    s, d = h.shape
    dff = w1.shape[1]
    return pl.pallas_call(
        _mlp_kernel,
        grid=(s // ts, dff // tf),
        in_specs=[pl.BlockSpec((ts, d), lambda i, f: (i, 0)), pl.BlockSpec((1, d), lambda i, f: (0, 0)),
                  pl.BlockSpec((d, tf), lambda i, f: (0, f)), pl.BlockSpec((tf, d), lambda i, f: (f, 0)),
                  pl.BlockSpec((1, d), lambda i, f: (0, 0))],
        out_specs=pl.BlockSpec((ts, d), lambda i, f: (i, 0)),
        out_shape=jax.ShapeDtypeStruct((s, d), F32),
        scratch_shapes=[pltpu.VMEM((ts, d), BF16), pltpu.VMEM((ts, d), F32)],
        compiler_params=_params("arbitrary", "arbitrary"),
        name="mlp_relu2",
    )(h, gpre, w1, w2, gpost)


def _bias_of(dist, bias_heads, ok, shift):
    bucket = _t5_bucket(dist)
    groups = []
    for g in range(GROUPS):
        heads = []
        for r in range(REP):
            out = jnp.zeros(dist.shape, F32)
            for b in range(NUM_BUCKETS):
                out = jnp.where(bucket == b, bias_heads[b, g, r], out)
            heads.append(jnp.where(ok, out - shift[g, r], -MASK_BIG))
        groups.append(jnp.concatenate(heads, axis=1))
    return jnp.stack(groups)


def _band_strip(bias_heads, window):
    nk = window + QBLOCK
    dist = jnp.arange(QBLOCK)[None, :] - jnp.arange(nk)[:, None] + window
    return _bias_of(dist, bias_heads, (dist >= 0) & (dist < window), jnp.zeros((GROUPS, REP), F32))


def _sel_strip(bias_heads, far, tk):
    rows = 3 * tk - QBLOCK
    dist = jnp.arange(QBLOCK)[None, :] - jnp.arange(rows)[:, None] + (2 * tk - QBLOCK)
    return _bias_of(dist, bias_heads, dist >= 0, far)


def kernel(x, norm_mix_pre, norm_mix_post, norm_mlp_pre, norm_mlp_post, w_in,
           cmp_pos_k, cmp_w1_k, cmp_w2_k, cmp_pos_v, cmp_w1_v, cmp_w2_v,
           attn_sinks, rel_bias, w_up_nsa, w_up_swa, w_out, w_mlp_in, w_mlp_out):
    b, s, d = x.shape
    assert b == 1 and s % SEL_TILE == 0 and w_in.shape[0] == 1
    qd = HEADS * HEAD_DIM
    kvd = GROUPS * HEAD_DIM
    sizes = (qd,) + (kvd,) * 6 + (3 * HEADS, qd, kvd, kvd, d, d)
    offs = [0]
    for z in sizes:
        offs.append(offs[-1] + z)
    w = w_in[0]
    col = lambda i: w[:, offs[i]:offs[i + 1]]
    (w_qn, w_kc, w_vc, w_ks, w_vs, w_kw, w_vw, w_gn, w_qs, w_k_s, w_v_s, w_ga, w_gb) = [col(i) for i in range(13)]
    scale = HEAD_DIM ** -0.5 * LOG2E
    w_gn = w_gn.reshape(d, HEADS, 3).transpose(0, 2, 1).reshape(d, 3 * HEADS)
    w_gn = jnp.pad(w_gn, ((0, 0), (0, 8)))
    wn = jnp.concatenate([w_kc, w_vc, w_ks, w_kw, w_k_s], axis=1).astype(BF16)
    wg = jnp.concatenate([w_ga, w_gb], axis=1).astype(BF16)
    wt = jnp.concatenate([w_qn * scale, w_qs * scale, w_vs, w_vw, w_v_s, w_gn], axis=1).T.astype(BF16)
    nqv = 2 * qd + 3 * kvd

    x2 = x[0]
    kv, gab, qv, gn = _project(x2, norm_mix_pre, wn, wg, wt, nqv, ts=512)
    q_nsa = qv[0:qd].reshape(GROUPS, REP * HEAD_DIM, s)
    q_swa = qv[qd:2 * qd].reshape(GROUPS, REP * HEAD_DIM, s)
    vs_t = qv[2 * qd:2 * qd + kvd].reshape(GROUPS, HEAD_DIM, s)
    vw_t = qv[2 * qd + kvd:2 * qd + 2 * kvd].reshape(GROUPS, HEAD_DIM, s)
    vswa_t = qv[2 * qd + 2 * kvd:].reshape(GROUPS, HEAD_DIM, s)

    n16 = s // CMP_STRIDE
    kb = kv[:, :2 * kvd].reshape(n16, CMP_STRIDE, 2, GROUPS, HEAD_DIM).transpose(2, 3, 0, 1, 4)
    kb = kb.reshape(2, GROUPS, n16, CMP_STRIDE * HEAD_DIM)
    pos8 = jnp.stack([cmp_pos_k[0], cmp_pos_v[0]]).reshape(2, 1, CMP_BLOCK * HEAD_DIM)
    pos8 = jnp.broadcast_to(pos8, (2, 8, CMP_BLOCK * HEAD_DIM)).astype(BF16)
    w1 = jnp.stack([cmp_w1_k[0], cmp_w1_v[0]]).astype(BF16)
    w2 = jnp.stack([cmp_w2_k[0], cmp_w2_v[0]]).astype(BF16)
    cmp_n, cmp_t = _compress(kb, pos8, w1, w2, w2.transpose(0, 2, 1))

    oc, sel = _cmp_select(q_nsa, cmp_n, cmp_t, s)

    bias_nsa = rel_bias[:, :HEADS].reshape(NUM_BUCKETS, GROUPS, REP) * LOG2E
    bias_swa = rel_bias[:, HEADS:].reshape(NUM_BUCKETS, GROUPS, REP) * LOG2E

    tk = SEL_TILE
    nb = tk // SEL_BLOCK
    posn = jnp.arange(s)
    onehot = ((posn // SEL_BLOCK) % nb)[:, None] == jnp.arange(nb)[None, :]
    aug = jnp.concatenate([onehot.astype(BF16), jnp.ones((s, 2), BF16),
                           jnp.zeros((s, kvd - nb - 2), BF16)], axis=1)
    kaug = jnp.concatenate([kv[:, 2 * kvd:3 * kvd], aug], axis=1)
    far = bias_nsa[NUM_BUCKETS - 1]
    far_hi = far.astype(BF16)
    far_lo = (far - far_hi.astype(F32)).astype(BF16)
    cst = jnp.stack([far_hi, far_lo], axis=1)
    cst = jnp.broadcast_to(cst[:, :, :, None], (GROUPS, 2, REP, QBLOCK)).reshape(GROUPS, 2, HQ)
    cst = jnp.pad(cst, ((0, 0), (0, 14), (0, 0)))
    ones_row = jnp.concatenate([jnp.ones((GROUPS, 1, s), BF16), jnp.zeros((GROUPS, 15, s), BF16)], axis=1)
    vs_aug = jnp.concatenate([vs_t, ones_row], axis=1)
    o_sel = _selected(q_nsa, sel, kaug, vs_aug, _sel_strip(bias_nsa, far, tk), cst, s, tk)

    pad_rows = lambda a, wdw: jnp.pad(a, ((wdw, 0), (0, 0)))
    pad_lanes = lambda a, wdw: jnp.pad(a, ((0, 0), (0, 0), (wdw, 0)))
    no_sink = jnp.zeros((GROUPS, 1, HQ), F32)
    o_win = _banded(q_nsa, pad_rows(kv[:, 3 * kvd:4 * kvd], NSA_WINDOW), pad_lanes(vw_t, NSA_WINDOW),
                    _band_strip(bias_nsa, NSA_WINDOW), no_sink, s, NSA_WINDOW, False, "nsa_window")
    sink = attn_sinks[0].reshape(GROUPS, 1, REP, 1).astype(F32) * LOG2E
    sink = jnp.broadcast_to(sink, (GROUPS, 1, REP, QBLOCK)).reshape(GROUPS, 1, HQ)
    o_swa = _banded(q_swa, pad_rows(kv[:, 4 * kvd:5 * kvd], SWA_WINDOW), pad_lanes(vswa_t, SWA_WINDOW),
                    _band_strip(bias_swa, SWA_WINDOW), sink, s, SWA_WINDOW, True, "swa_sink")

    flat = lambda a: a.reshape(qd, s)
    h1 = _merge(flat(oc), flat(o_sel), flat(o_win), flat(o_swa), gn, gab, x2,
                w_up_nsa[0].astype(BF16), w_up_swa[0].astype(BF16), w_out[0].astype(BF16),
                norm_mix_post, ts=512)
    out = _mlp(h1, norm_mlp_pre, w_mlp_in[0].astype(BF16), w_mlp_out[0].astype(BF16), norm_mlp_post,
               ts=1024, tf=1024)
    return out[None]
```

```python
import functools
import math

import jax
import jax.numpy as jnp
import numpy as np
from jax import lax
from jax.experimental import pallas as pl
from jax.experimental.pallas import tpu as pltpu

F32 = jnp.float32
BF16 = jnp.bfloat16

HEAD_DIM = 64
GROUPS = 2
REP = 4
HEADS = GROUPS * REP
CMP_BLOCK = 32
CMP_STRIDE = 16
SEL_BLOCK = 64
SEL_TOPK = 16
NSA_WINDOW = 512
SWA_WINDOW = 128
QBLOCK = 128
NUM_BUCKETS = 32
MAX_DISTANCE = 1024
RMS_EPS = 1e-6
LOG2E = math.log2(math.e)
MASK_BIG = 2.0 ** 99
SEL_TILE = 1024
HQ = REP * QBLOCK
VMEM_LIMIT = 56 * 1024 * 1024

_NT = (((1,), (1,)), ((), ()))
_TN = (((0,), (0,)), ((), ()))


def _params(*sem):
    return pltpu.CompilerParams(dimension_semantics=sem, vmem_limit_bytes=VMEM_LIMIT)


def _bucket_starts():
    max_exact = NUM_BUCKETS // 2
    d = np.arange(2 * MAX_DISTANCE)
    large = max_exact + (np.log(np.maximum(d, 1).astype(np.float64) / max_exact)
                         / math.log(MAX_DISTANCE / max_exact) * (NUM_BUCKETS - max_exact)).astype(np.int64)
    bucket = np.where(d < max_exact, d, np.minimum(large, NUM_BUCKETS - 1))
    return (bucket[None, :] < np.arange(NUM_BUCKETS)[:, None]).sum(axis=1).astype(np.int32)


def _rms(x, gain):
    return x * lax.rsqrt(jnp.mean(x * x, axis=-1, keepdims=True) + RMS_EPS) * gain


def _proj_kernel(x_ref, g_ref, wn_ref, wg_ref, wt_ref, kv_ref, gab_ref, qv_ref, gn_ref, *, nqv):
    u = _rms(x_ref[...], g_ref[...]).astype(BF16)
    kv_ref[...] = jnp.dot(u, wn_ref[...], preferred_element_type=F32).astype(BF16)
    gab_ref[...] = jnp.dot(u, wg_ref[...], preferred_element_type=F32).astype(BF16)
    t = lax.dot_general(wt_ref[...], u, _NT, preferred_element_type=F32)
    qv_ref[...] = t[:nqv].astype(BF16)
    gn_ref[...] = t[nqv:]


def _project(x, gain, wn, wg, wt, nqv, ts):
    s, d = x.shape
    nn, ng, ntr = wn.shape[1], wg.shape[1], wt.shape[0]
    full = lambda shp: pl.BlockSpec(shp, lambda i: (0, 0))
    return pl.pallas_call(
        functools.partial(_proj_kernel, nqv=nqv),
        grid=(s // ts,),
        in_specs=[pl.BlockSpec((ts, d), lambda i: (i, 0)), full((1, d)),
                  full((d, nn)), full((d, ng)), full((ntr, d))],
        out_specs=[pl.BlockSpec((ts, nn), lambda i: (i, 0)),
                   pl.BlockSpec((ts, ng), lambda i: (i, 0)),
                   pl.BlockSpec((nqv, ts), lambda i: (0, i)),
                   pl.BlockSpec((ntr - nqv, ts), lambda i: (0, i))],
        out_shape=[jax.ShapeDtypeStruct((s, nn), BF16), jax.ShapeDtypeStruct((s, ng), BF16),
                   jax.ShapeDtypeStruct((nqv, s), BF16), jax.ShapeDtypeStruct((ntr - nqv, s), F32)],
        compiler_params=_params("arbitrary"),
        name="in_proj",
    )(x, gain, wn, wg, wt)


def _gelu_tanh(x):
    return x * (0.5 * (1.0 + jnp.tanh(math.sqrt(2.0 / math.pi) * (x + 0.044715 * (x * x * x)))))


def _compress_kernel(kb_ref, pos_ref, w1_ref, w2_ref, w2t_ref, o_ref, ot_ref):
    kb = kb_ref[0, 0]
    half = kb.shape[1]
    n16 = kb.shape[0]
    first = jnp.dot(kb, w1_ref[0, :half, :], preferred_element_type=F32)
    second = jnp.dot(kb, w1_ref[0, half:, :], preferred_element_type=F32)
    posb = jnp.dot(pos_ref[0], w1_ref[0], preferred_element_type=F32)[0:1]
    pre = first + pltpu.roll(second, n16 - 1, 0) + posb
    h = _gelu_tanh(pre).astype(BF16)
    o_ref[0, 0] = jnp.dot(h, w2_ref[0], preferred_element_type=F32).astype(BF16)
    ot_ref[0, 0] = lax.dot_general(w2t_ref[0], h, _NT, preferred_element_type=F32).astype(BF16)


def _compress(kb, pos8, w1, w2, w2t):
    _, g, n16, cin2 = kb.shape
    hid = w1.shape[2]
    return pl.pallas_call(
        _compress_kernel,
        grid=(2, g),
        in_specs=[pl.BlockSpec((1, 1, n16, cin2), lambda a, b: (a, b, 0, 0)),
                  pl.BlockSpec((1, 8, 2 * cin2), lambda a, b: (a, 0, 0)),
                  pl.BlockSpec((1, 2 * cin2, hid), lambda a, b: (a, 0, 0)),
                  pl.BlockSpec((1, hid, HEAD_DIM), lambda a, b: (a, 0, 0)),
                  pl.BlockSpec((1, HEAD_DIM, hid), lambda a, b: (a, 0, 0))],
        out_specs=[pl.BlockSpec((1, 1, n16, HEAD_DIM), lambda a, b: (a, b, 0, 0)),
                   pl.BlockSpec((1, 1, HEAD_DIM, n16), lambda a, b: (a, b, 0, 0))],
        out_shape=[jax.ShapeDtypeStruct((2, g, n16, HEAD_DIM), BF16),
                   jax.ShapeDtypeStruct((2, g, HEAD_DIM, n16), BF16)],
        compiler_params=_params("arbitrary", "arbitrary"),
        name="kv_compress",
    )(kb, pos8, w1, w2, w2t)


def _heads_to_lanes(q_ref):
    return jnp.concatenate([q_ref[0, r * HEAD_DIM:(r + 1) * HEAD_DIM, :] for r in range(REP)], axis=1)


def _store_heads(o_ref, o_t):
    for r in range(REP):
        o_ref[0, r * HEAD_DIM:(r + 1) * HEAD_DIM, :] = o_t[:, r * QBLOCK:(r + 1) * QBLOCK].astype(o_ref.dtype)


def _cmpsel_body(q4, kc_ref, vct_ref, oc_ref, sel_ref, pg_ref, t0, *, rows, nomask, nsel, topk):
    s = jnp.dot(kc_ref[0, 0, 0:rows, :], q4, preferred_element_type=F32)
    mrows = rows - nomask
    n_io = nomask + lax.broadcasted_iota(jnp.int32, (mrows, HQ), 0)
    t_io = t0 + (lax.broadcasted_iota(jnp.int32, (mrows, HQ), 1) & (QBLOCK - 1))
    edge = jnp.where(n_io * CMP_STRIDE + (CMP_BLOCK - 1) <= t_io, s[nomask:], -jnp.inf)
    s = jnp.concatenate([s[:nomask], edge], axis=0) if nomask else edge
    m = jnp.max(s, axis=0, keepdims=True)
    m = jnp.where(m == -jnp.inf, 0.0, m)
    e = jnp.exp2(s - m)
    den = jnp.maximum(jnp.sum(e, axis=0, keepdims=True), 1e-30)
    p = e * (1.0 / den)
    _store_heads(oc_ref, jnp.dot(vct_ref[0, 0, :, 0:rows], p.astype(BF16), preferred_element_type=F32))

    pg = ((p[:, 0:QBLOCK] + p[:, QBLOCK:2 * QBLOCK]) + p[:, 2 * QBLOCK:3 * QBLOCK]) + p[:, 3 * QBLOCK:]
    pg_ref[0:8, :] = jnp.zeros((8, QBLOCK), F32)
    pg_ref[8:8 + rows, :] = pg
    ratio = SEL_BLOCK // CMP_STRIDE
    nblk = rows // ratio
    ps = pg_ref[pl.ds(7, nblk, stride=ratio), :]
    for o in range(1, ratio + 1):
        ps = ps + pg_ref[pl.ds(7 + o, nblk, stride=ratio), :]

    blk = lax.broadcasted_iota(jnp.int32, (nblk, QBLOCK), 0)
    cur = (t0 + lax.broadcasted_iota(jnp.int32, (nblk, QBLOCK), 1)) // SEL_BLOCK
    valid = blk <= cur
    forced = (blk == 0) | (blk == cur) | (blk == cur - 1)
    score = jnp.where(valid, jnp.where(forced, jnp.inf, ps), -jnp.inf)

    def pick_one(_, rem):
        top = jnp.max(rem, axis=0, keepdims=True)
        first = jnp.min(jnp.where(rem == top, blk, nblk), axis=0, keepdims=True)
        return jnp.where(blk == first, -jnp.inf, rem)

    rem = lax.fori_loop(0, topk, pick_one, score)
    sel_ref[0, 0:nblk, :] = jnp.where((rem == -jnp.inf) & valid, 0.0, -MASK_BIG).astype(BF16)
    if nblk < nsel:
        sel_ref[0, nblk:, :] = jnp.full((nsel - nblk, QBLOCK), -MASK_BIG, BF16)


def _cmpsel_kernel(q_ref, kc_ref, vct_ref, oc_ref, sel_ref, pg_ref, *, n16, nsel, topk, nvar):
    ci = pl.program_id(1)
    t0 = ci * QBLOCK
    q4 = _heads_to_lanes(q_ref)
    per = n16 // nvar
    cpv = per // (QBLOCK // CMP_STRIDE)
    for k in range(1, nvar + 1):
        @pl.when((ci >= (k - 1) * cpv) & (ci < k * cpv))
        def _(k=k):
            _cmpsel_body(q4, kc_ref, vct_ref, oc_ref, sel_ref, pg_ref, t0, rows=per * k,
                         nomask=max(per * (k - 1) - 8, 0), nsel=nsel, topk=topk)


def _cmp_select(q, kc_all, vct_all, s):
    n16 = s // CMP_STRIDE
    nsel = s // SEL_BLOCK
    topk = min(SEL_TOPK, nsel)
    nvar = max(n16 // 256, 1)
    return pl.pallas_call(
        functools.partial(_cmpsel_kernel, n16=n16, nsel=nsel, topk=topk, nvar=nvar),
        grid=(GROUPS, s // QBLOCK),
        in_specs=[pl.BlockSpec((1, REP * HEAD_DIM, QBLOCK), lambda g, c: (g, 0, c)),
                  pl.BlockSpec((1, 1, n16, HEAD_DIM), lambda g, c: (0, g, 0, 0)),
                  pl.BlockSpec((1, 1, HEAD_DIM, n16), lambda g, c: (1, g, 0, 0))],
        out_specs=[pl.BlockSpec((1, REP * HEAD_DIM, QBLOCK), lambda g, c: (g, 0, c)),
                   pl.BlockSpec((1, nsel, QBLOCK), lambda g, c: (g, 0, c))],
        out_shape=[jax.ShapeDtypeStruct((GROUPS, REP * HEAD_DIM, s), BF16),
                   jax.ShapeDtypeStruct((GROUPS, nsel, s), BF16)],
        scratch_shapes=[pltpu.VMEM((n16 + 8, QBLOCK), F32)],
        compiler_params=_params("arbitrary", "arbitrary"),
        name="cmp_select",
    )(q, kc_all, vct_all)


def _fill_bias_strip(strip_ref, thr_ref, tab_ref, g, *, dist0, lo, hi, minus_far):
    nrows = strip_ref.shape[0]

    def block(bi, carry):
        i0 = pl.multiple_of(bi * QBLOCK, QBLOCK)
        d = (lax.broadcasted_iota(jnp.int32, (QBLOCK, QBLOCK), 1)
             - lax.broadcasted_iota(jnp.int32, (QBLOCK, QBLOCK), 0) + (dist0 - i0))
        ok = d >= lo if hi is None else (d >= lo) & (d < hi)
        for r in range(REP):
            base = (g * REP + r) * NUM_BUCKETS
            out = jnp.full((QBLOCK, QBLOCK), tab_ref[base], F32)
            for b in range(1, NUM_BUCKETS):
                out = jnp.where(d >= thr_ref[b], tab_ref[base + b], out)
            if minus_far:
                out = out - tab_ref[base + NUM_BUCKETS - 1]
            strip_ref[pl.ds(i0, QBLOCK), r * QBLOCK:(r + 1) * QBLOCK] = jnp.where(ok, out, -MASK_BIG)
        return carry

    lax.fori_loop(0, nrows // QBLOCK, block, 0)


def _sel_kernel(thr_ref, tab_ref, q_ref, sel_ref, kaug_ref, vt_ref, cst_ref, o_ref,
                qa_ref, s_ref, tmax_ref, m_ref, acc_ref, strip_ref, *, tk):
    g = pl.program_id(0)
    st = pl.program_id(1)
    nb = tk // SEL_BLOCK
    cpt = tk // QBLOCK
    kd = GROUPS * HEAD_DIM
    dmax = 3 * tk - QBLOCK

    @pl.when(st == 0)
    def _():
        _fill_bias_strip(strip_ref, thr_ref, tab_ref, g, dist0=dmax, lo=0, hi=None, minus_far=True)

    qa_ref[...] = jnp.zeros(qa_ref.shape, BF16)
    for b in range(2):
        qa_ref[b, kd + nb:kd + nb + 16, :] = cst_ref[0]
    acc_ref[...] = jnp.zeros(acc_ref.shape, F32)
    row0 = pl.multiple_of(g * HEAD_DIM, HEAD_DIM)
    nt = st + 1

    def advance(c, j):
        wrap = j + 1 == nt
        return jnp.where(wrap, c + 1, c), jnp.where(wrap, 0, j + 1)

    def qk(c, j, buf):
        cc = jnp.minimum(c, cpt - 1)
        lane0 = pl.multiple_of(cc * QBLOCK, QBLOCK)
        for r in range(REP):
            qa_ref[buf, pl.ds(row0, HEAD_DIM), r * QBLOCK:(r + 1) * QBLOCK] = (
                q_ref[0, r * HEAD_DIM:(r + 1) * HEAD_DIM, pl.ds(lane0, QBLOCK)])
        sb = sel_ref[0, pl.ds(pl.multiple_of(j * nb, nb), nb), pl.ds(lane0, QBLOCK)]
        for r in range(REP):
            qa_ref[buf, kd:kd + nb, r * QBLOCK:(r + 1) * QBLOCK] = sb
        k0 = pl.multiple_of(j * tk, tk)
        delta = (st - j) * tk + cc * QBLOCK
        off = pl.multiple_of(jnp.maximum(dmax - delta, 0), QBLOCK)
        s = (jnp.dot(kaug_ref[pl.ds(k0, tk), :], qa_ref[buf], preferred_element_type=F32)
             + strip_ref[pl.ds(off, tk), :])
        s_ref[buf] = s
        tmax_ref[buf] = jnp.max(s, axis=0, keepdims=True)

    def softmax_pv(c, j, buf):
        lane0 = pl.multiple_of(c * QBLOCK, QBLOCK)
        k0 = pl.multiple_of(j * tk, tk)
        m_prev = jnp.where(j == 0, -jnp.inf, m_ref[...])
        m_new = jnp.maximum(m_prev, tmax_ref[buf])
        alpha = jnp.exp2(m_prev - m_new)
        p = jnp.exp2(s_ref[buf] - m_new)
        acc = alpha * acc_ref[...] + jnp.dot(vt_ref[0, :, pl.ds(k0, tk)], p.astype(BF16),
                                             preferred_element_type=F32)
        acc_ref[...] = acc
        m_ref[...] = m_new
        out = acc[0:HEAD_DIM] * (1.0 / jnp.maximum(acc[HEAD_DIM:HEAD_DIM + 1], 1e-30))
        for r in range(REP):
            o_ref[0, r * HEAD_DIM:(r + 1) * HEAD_DIM, pl.ds(lane0, QBLOCK)] = (
                out[:, r * QBLOCK:(r + 1) * QBLOCK].astype(o_ref.dtype))

    zero = jnp.int32(0)
    qk(zero, zero, 0)

    def pair(_, carry):
        ca, ja = carry
        cb, jb = advance(ca, ja)
        qk(cb, jb, 1)
        softmax_pv(ca, ja, 0)
        cn, jn = advance(cb, jb)
        qk(cn, jn, 0)
        softmax_pv(cb, jb, 1)
        return cn, jn

    lax.fori_loop(0, (cpt * nt) // 2, pair, (zero, zero))


def _selected(thr, tab, q, sel, kaug, vt, cst, s, tk):
    nsel = s // SEL_BLOCK
    sl = 4 * tk - QBLOCK
    smem = pl.BlockSpec(memory_space=pltpu.SMEM)
    return pl.pallas_call(
        functools.partial(_sel_kernel, tk=tk),
        grid=(GROUPS, s // tk),
        in_specs=[smem, smem,
                  pl.BlockSpec((1, REP * HEAD_DIM, tk), lambda g, c: (g, 0, c)),
                  pl.BlockSpec((1, nsel, tk), lambda g, c: (g, 0, c)),
                  pl.BlockSpec(kaug.shape, lambda g, c: (0, 0)),
                  pl.BlockSpec((1, vt.shape[1], s), lambda g, c: (g, 0, 0)),
                  pl.BlockSpec((1, 16, HQ), lambda g, c: (g, 0, 0))],
        out_specs=pl.BlockSpec((1, REP * HEAD_DIM, tk), lambda g, c: (g, 0, c)),
        out_shape=jax.ShapeDtypeStruct((GROUPS, REP * HEAD_DIM, s), BF16),
        scratch_shapes=[pltpu.VMEM((2, kaug.shape[1], HQ), BF16), pltpu.VMEM((2, tk, HQ), F32),
                        pltpu.VMEM((2, 1, HQ), F32),
                        pltpu.VMEM((1, HQ), F32), pltpu.VMEM((vt.shape[1], HQ), F32),
                        pltpu.VMEM((sl, HQ), F32)],
        compiler_params=_params("arbitrary", "arbitrary"),
        name="selected_attn",
    )(thr, tab, q, sel, kaug, vt, cst)


def _band_kernel(thr_ref, tab_ref, q_ref, k_ref, vt_ref, sink_ref, o_ref, qa_ref, strip_ref, *,
                 window, use_sink, cb):
    g = pl.program_id(0)

    @pl.when(pl.program_id(1) == 0)
    def _():
        _fill_bias_strip(strip_ref, thr_ref, tab_ref, g, dist0=window, lo=0, hi=window, minus_far=False)

    base = pl.multiple_of(pl.program_id(1) * (QBLOCK * cb), QBLOCK * cb)
    nk = window + QBLOCK
    qa_ref[...] = jnp.zeros(qa_ref.shape, BF16)
    row0 = pl.multiple_of(g * HEAD_DIM, HEAD_DIM)
    for c in range(cb):
        for r in range(REP):
            qa_ref[c, pl.ds(row0, HEAD_DIM), r * QBLOCK:(r + 1) * QBLOCK] = (
                q_ref[0, r * HEAD_DIM:(r + 1) * HEAD_DIM, c * QBLOCK:(c + 1) * QBLOCK])
    row = lax.broadcasted_iota(jnp.int32, (nk, HQ), 0)
    for c in range(cb):
        t0 = base + c * QBLOCK
        s = jnp.dot(k_ref[pl.ds(t0, nk), :], qa_ref[c], preferred_element_type=F32) + strip_ref[...]
        s = jnp.where(row + (t0 - window) >= 0, s, -MASK_BIG)
        m = jnp.max(s, axis=0, keepdims=True)
        if use_sink:
            m = jnp.maximum(m, sink_ref[0])
        e = jnp.exp2(s - m)
        den = jnp.sum(e, axis=0, keepdims=True)
        if use_sink:
            den = den + jnp.exp2(sink_ref[0] - m)
        else:
            den = jnp.maximum(den, 1e-30)
        o_t = jnp.dot(vt_ref[0, :, pl.ds(t0, nk)], e.astype(BF16), preferred_element_type=F32) * (1.0 / den)
        for r in range(REP):
            o_ref[0, r * HEAD_DIM:(r + 1) * HEAD_DIM, c * QBLOCK:(c + 1) * QBLOCK] = (
                o_t[:, r * QBLOCK:(r + 1) * QBLOCK].astype(o_ref.dtype))


def _banded(thr, tab, q, kpad, vtpad, sink, s, window, use_sink, name, cb=4):
    nk = window + QBLOCK
    smem = pl.BlockSpec(memory_space=pltpu.SMEM)
    return pl.pallas_call(
        functools.partial(_band_kernel, window=window, use_sink=use_sink, cb=cb),
        grid=(GROUPS, s // (QBLOCK * cb)),
        in_specs=[smem, smem,
                  pl.BlockSpec((1, REP * HEAD_DIM, QBLOCK * cb), lambda g, c: (g, 0, c)),
                  pl.BlockSpec(kpad.shape, lambda g, c: (0, 0)),
                  pl.BlockSpec((1, HEAD_DIM, s + window), lambda g, c: (g, 0, 0)),
                  pl.BlockSpec((1, 1, HQ), lambda g, c: (g, 0, 0))],
        out_specs=pl.BlockSpec((1, REP * HEAD_DIM, QBLOCK * cb), lambda g, c: (g, 0, c)),
        out_shape=jax.ShapeDtypeStruct((GROUPS, REP * HEAD_DIM, s), BF16),
        scratch_shapes=[pltpu.VMEM((cb, GROUPS * HEAD_DIM, HQ), BF16), pltpu.VMEM((nk, HQ), F32)],
        compiler_params=_params("arbitrary", "arbitrary"),
        name=name,
    )(thr, tab, q, kpad, vtpad, sink)


def _merge_kernel(oc_ref, os_ref, ow_ref, ob_ref, gn_ref, gab_ref, x_ref, wa_ref, wb_ref, wo_ref, gp_ref, o_ref):
    d = x_ref.shape[1]
    gn = jax.nn.sigmoid(gn_ref[...])
    parts = []
    for h in range(HEADS):
        rows = slice(h * HEAD_DIM, (h + 1) * HEAD_DIM)
        parts.append(gn[h:h + 1] * oc_ref[rows, :].astype(F32)
                     + gn[HEADS + h:HEADS + h + 1] * os_ref[rows, :].astype(F32)
                     + gn[2 * HEADS + h:2 * HEADS + h + 1] * ow_ref[rows, :].astype(F32))
    oa_t = jnp.concatenate(parts, axis=0).astype(BF16)
    ya = lax.dot_general(oa_t, wa_ref[...], _TN, preferred_element_type=F32)
    yb = lax.dot_general(ob_ref[...], wb_ref[...], _TN, preferred_element_type=F32)
    y = (jax.nn.sigmoid(gab_ref[:, :d].astype(F32)) * ya
         + jax.nn.sigmoid(gab_ref[:, d:].astype(F32)) * yb)
    z = jnp.dot(y.astype(BF16), wo_ref[...], preferred_element_type=F32)
    o_ref[...] = x_ref[...] + _rms(z, gp_ref[...])


def _merge(oc, os_, ow, ob, gn, gab, x, wa, wb, wo, gpost, ts):
    s, d = x.shape
    hd = oc.shape[0]
    tok = lambda rows: pl.BlockSpec((rows, ts), lambda i: (0, i))
    full = lambda shp: pl.BlockSpec(shp, lambda i: (0, 0))
    return pl.pallas_call(
        _merge_kernel,
        grid=(s // ts,),
        in_specs=[tok(hd), tok(hd), tok(hd), tok(hd), tok(gn.shape[0]),
                  pl.BlockSpec((ts, 2 * d), lambda i: (i, 0)), pl.BlockSpec((ts, d), lambda i: (i, 0)),
                  full((hd, d)), full((hd, d)), full((d, d)), full((1, d))],
        out_specs=pl.BlockSpec((ts, d), lambda i: (i, 0)),
        out_shape=jax.ShapeDtypeStruct((s, d), F32),
        compiler_params=_params("arbitrary"),
        name="merge_out_proj",
    )(oc, os_, ow, ob, gn, gab, x, wa, wb, wo, gpost)


def _mlp_kernel(h_ref, gpre_ref, w1_ref, w2_ref, gpost_ref, o_ref, u_ref, acc_ref):
    f = pl.program_id(1)

    @pl.when(f == 0)
    def _():
        u_ref[...] = _rms(h_ref[...], gpre_ref[...]).astype(BF16)
        acc_ref[...] = jnp.zeros(acc_ref.shape, F32)

    a = jnp.maximum(jnp.dot(u_ref[...], w1_ref[...], preferred_element_type=F32), 0.0)
    acc_ref[...] += jnp.dot((a * a).astype(BF16), w2_ref[...], preferred_element_type=F32)

    @pl.when(f == pl.num_programs(1) - 1)
    def _():
        o_ref[...] = h_ref[...] + _rms(acc_ref[...], gpost_ref[...])


def _mlp(h, gpre, w1, w2, gpost, ts, tf):
    s, d = h.shape
    dff = w1.shape[1]
    return pl.pallas_call(
        _mlp_kernel,
        grid=(s // ts, dff // tf),
        in_specs=[pl.BlockSpec((ts, d), lambda i, f: (i, 0)), pl.BlockSpec((1, d), lambda i, f: (0, 0)),
                  pl.BlockSpec((d, tf), lambda i, f: (0, f)), pl.BlockSpec((tf, d), lambda i, f: (f, 0)),
                  pl.BlockSpec((1, d), lambda i, f: (0, 0))],
        out_specs=pl.BlockSpec((ts, d), lambda i, f: (i, 0)),
        out_shape=jax.ShapeDtypeStruct((s, d), F32),
        scratch_shapes=[pltpu.VMEM((ts, d), BF16), pltpu.VMEM((ts, d), F32)],
        compiler_params=_params("arbitrary", "arbitrary"),
        name="mlp_relu2",
    )(h, gpre, w1, w2, gpost)


def kernel(x, norm_mix_pre, norm_mix_post, norm_mlp_pre, norm_mlp_post, w_in,
           cmp_pos_k, cmp_w1_k, cmp_w2_k, cmp_pos_v, cmp_w1_v, cmp_w2_v,
           attn_sinks, rel_bias, w_up_nsa, w_up_swa, w_out, w_mlp_in, w_mlp_out):
    b, s, d = x.shape
    assert b == 1 and s % SEL_TILE == 0 and w_in.shape[0] == 1
    qd = HEADS * HEAD_DIM
    kvd = GROUPS * HEAD_DIM
    sizes = (qd,) + (kvd,) * 6 + (3 * HEADS, qd, kvd, kvd, d, d)
    offs = [0]
    for z in sizes:
        offs.append(offs[-1] + z)
    w = w_in[0]
    col = lambda i: w[:, offs[i]:offs[i + 1]]
    (w_qn, w_kc, w_vc, w_ks, w_vs, w_kw, w_vw, w_gn, w_qs, w_k_s, w_v_s, w_ga, w_gb) = [col(i) for i in range(13)]
    scale = HEAD_DIM ** -0.5 * LOG2E
    w_gn = w_gn.reshape(d, HEADS, 3).transpose(0, 2, 1).reshape(d, 3 * HEADS)
    w_gn = jnp.pad(w_gn, ((0, 0), (0, 8)))
    wn = jnp.concatenate([w_kc, w_vc, w_ks, w_kw, w_k_s], axis=1).astype(BF16)
    wg = jnp.concatenate([w_ga, w_gb], axis=1).astype(BF16)
    wt = jnp.concatenate([w_qn * scale, w_qs * scale, w_vs, w_vw, w_v_s, w_gn], axis=1).T.astype(BF16)
    nqv = 2 * qd + 3 * kvd

    x2 = x[0]
    kv, gab, qv, gn = _project(x2, norm_mix_pre, wn, wg, wt, nqv, ts=512)
    q_nsa = qv[0:qd].reshape(GROUPS, REP * HEAD_DIM, s)
    q_swa = qv[qd:2 * qd].reshape(GROUPS, REP * HEAD_DIM, s)
    vs_t = qv[2 * qd:2 * qd + kvd].reshape(GROUPS, HEAD_DIM, s)
    vw_t = qv[2 * qd + kvd:2 * qd + 2 * kvd].reshape(GROUPS, HEAD_DIM, s)
    vswa_t = qv[2 * qd + 2 * kvd:].reshape(GROUPS, HEAD_DIM, s)

    n16 = s // CMP_STRIDE
    kb = kv[:, :2 * kvd].reshape(n16, CMP_STRIDE, 2, GROUPS, HEAD_DIM).transpose(2, 3, 0, 1, 4)
    kb = kb.reshape(2, GROUPS, n16, CMP_STRIDE * HEAD_DIM)
    pos8 = jnp.stack([cmp_pos_k[0], cmp_pos_v[0]]).reshape(2, 1, CMP_BLOCK * HEAD_DIM)
    pos8 = jnp.broadcast_to(pos8, (2, 8, CMP_BLOCK * HEAD_DIM)).astype(BF16)
    w1 = jnp.stack([cmp_w1_k[0], cmp_w1_v[0]]).astype(BF16)
    w2 = jnp.stack([cmp_w2_k[0], cmp_w2_v[0]]).astype(BF16)
    cmp_n, cmp_t = _compress(kb, pos8, w1, w2, w2.transpose(0, 2, 1))

    oc, sel = _cmp_select(q_nsa, cmp_n, cmp_t, s)

    bias_nsa = rel_bias[:, :HEADS].reshape(NUM_BUCKETS, GROUPS, REP) * LOG2E
    bias_swa = rel_bias[:, HEADS:].reshape(NUM_BUCKETS, GROUPS, REP) * LOG2E
    thr = jnp.asarray(_bucket_starts())
    tab_nsa = bias_nsa.transpose(1, 2, 0).reshape(-1)
    tab_swa = bias_swa.transpose(1, 2, 0).reshape(-1)

    tk = SEL_TILE
    nb = tk // SEL_BLOCK
    posn = jnp.arange(s)
    onehot = ((posn // SEL_BLOCK) % nb)[:, None] == jnp.arange(nb)[None, :]
    aug = jnp.concatenate([onehot.astype(BF16), jnp.ones((s, 2), BF16),
                           jnp.zeros((s, kvd - nb - 2), BF16)], axis=1)
    kaug = jnp.concatenate([kv[:, 2 * kvd:3 * kvd], aug], axis=1)
    far = bias_nsa[NUM_BUCKETS - 1]
    far_hi = far.astype(BF16)
    far_lo = (far - far_hi.astype(F32)).astype(BF16)
    cst = jnp.stack([far_hi, far_lo], axis=1)
    cst = jnp.broadcast_to(cst[:, :, :, None], (GROUPS, 2, REP, QBLOCK)).reshape(GROUPS, 2, HQ)
    cst = jnp.pad(cst, ((0, 0), (0, 14), (0, 0)))
    ones_row = jnp.concatenate([jnp.ones((GROUPS, 1, s), BF16), jnp.zeros((GROUPS, 15, s), BF16)], axis=1)
    vs_aug = jnp.concatenate([vs_t, ones_row], axis=1)
    o_sel = _selected(thr, tab_nsa, q_nsa, sel, kaug, vs_aug, cst, s, tk)

    pad_rows = lambda a, wdw: jnp.pad(a, ((wdw, 0), (0, 0)))
    pad_lanes = lambda a, wdw: jnp.pad(a, ((0, 0), (0, 0), (wdw, 0)))
    no_sink = jnp.zeros((GROUPS, 1, HQ), F32)
    o_win = _banded(thr, tab_nsa, q_nsa, pad_rows(kv[:, 3 * kvd:4 * kvd], NSA_WINDOW), pad_lanes(vw_t, NSA_WINDOW),
                    no_sink, s, NSA_WINDOW, False, "nsa_window")
    sink = attn_sinks[0].reshape(GROUPS, 1, REP, 1).astype(F32) * LOG2E
    sink = jnp.broadcast_to(sink, (GROUPS, 1, REP, QBLOCK)).reshape(GROUPS, 1, HQ)
    o_swa = _banded(thr, tab_swa, q_swa, pad_rows(kv[:, 4 * kvd:5 * kvd], SWA_WINDOW), pad_lanes(vswa_t, SWA_WINDOW),
                    sink, s, SWA_WINDOW, True, "swa_sink")

    flat = lambda a: a.reshape(qd, s)
    h1 = _merge(flat(oc), flat(o_sel), flat(o_win), flat(o_swa), gn, gab, x2,
                w_up_nsa[0].astype(BF16), w_up_swa[0].astype(BF16), w_out[0].astype(BF16),
                norm_mix_post, ts=512)
    out = _mlp(h1, norm_mlp_pre, w_mlp_in[0].astype(BF16), w_mlp_out[0].astype(BF16), norm_mlp_post,
               ts=1024, tf=1024)
    return out[None]
```

```python
import functools
import math

import jax
import jax.numpy as jnp
import numpy as np
from jax import lax
from jax.experimental import pallas as pl
from jax.experimental.pallas import tpu as pltpu

F32 = jnp.float32
BF16 = jnp.bfloat16

HEAD_DIM = 64
GROUPS = 2
REP = 4
HEADS = GROUPS * REP
CMP_BLOCK = 32
CMP_STRIDE = 16
SEL_BLOCK = 64
SEL_TOPK = 16
NSA_WINDOW = 512
SWA_WINDOW = 128
QBLOCK = 128
NUM_BUCKETS = 32
MAX_DISTANCE = 1024
RMS_EPS = 1e-6
LOG2E = math.log2(math.e)
MASK_BIG = 2.0 ** 99
SEL_TILE = 1024
HQ = REP * QBLOCK
VMEM_LIMIT = 56 * 1024 * 1024

_NT = (((1,), (1,)), ((), ()))
_TN = (((0,), (0,)), ((), ()))


def _params(*sem):
    return pltpu.CompilerParams(dimension_semantics=sem, vmem_limit_bytes=VMEM_LIMIT)


def _bucket_starts():
    max_exact = NUM_BUCKETS // 2
    d = np.arange(2 * MAX_DISTANCE)
    large = max_exact + (np.log(np.maximum(d, 1).astype(np.float64) / max_exact)
                         / math.log(MAX_DISTANCE / max_exact) * (NUM_BUCKETS - max_exact)).astype(np.int64)
    bucket = np.where(d < max_exact, d, np.minimum(large, NUM_BUCKETS - 1))
    return (bucket[None, :] < np.arange(NUM_BUCKETS)[:, None]).sum(axis=1).astype(np.int32)


def _rms(x, gain):
    return x * lax.rsqrt(jnp.mean(x * x, axis=-1, keepdims=True) + RMS_EPS) * gain


def _proj_kernel(x_ref, g_ref, wn_ref, wg_ref, wt_ref, kv_ref, gab_ref, qv_ref, gn_ref, *, nqv):
    u = _rms(x_ref[...], g_ref[...]).astype(BF16)
    kv_ref[...] = jnp.dot(u, wn_ref[...], preferred_element_type=F32).astype(BF16)
    gab_ref[...] = jnp.dot(u, wg_ref[...], preferred_element_type=F32).astype(BF16)
    t = lax.dot_general(wt_ref[...], u, _NT, preferred_element_type=F32)
    qv_ref[...] = t[:nqv].astype(BF16)
    gn_ref[...] = t[nqv:]


def _project(x, gain, wn, wg, wt, nqv, ts):
    s, d = x.shape
    nn, ng, ntr = wn.shape[1], wg.shape[1], wt.shape[0]
    full = lambda shp: pl.BlockSpec(shp, lambda i: (0, 0))
    return pl.pallas_call(
        functools.partial(_proj_kernel, nqv=nqv),
        grid=(s // ts,),
        in_specs=[pl.BlockSpec((ts, d), lambda i: (i, 0)), full((1, d)),
                  full((d, nn)), full((d, ng)), full((ntr, d))],
        out_specs=[pl.BlockSpec((ts, nn), lambda i: (i, 0)),
                   pl.BlockSpec((ts, ng), lambda i: (i, 0)),
                   pl.BlockSpec((nqv, ts), lambda i: (0, i)),
                   pl.BlockSpec((ntr - nqv, ts), lambda i: (0, i))],
        out_shape=[jax.ShapeDtypeStruct((s, nn), BF16), jax.ShapeDtypeStruct((s, ng), BF16),
                   jax.ShapeDtypeStruct((nqv, s), BF16), jax.ShapeDtypeStruct((ntr - nqv, s), F32)],
        compiler_params=_params("arbitrary"),
        name="in_proj",
    )(x, gain, wn, wg, wt)


def _gelu_tanh(x):
    return x * (0.5 * (1.0 + jnp.tanh(math.sqrt(2.0 / math.pi) * (x + 0.044715 * (x * x * x)))))


def _compress_kernel(kb_ref, pos_ref, w1_ref, w2_ref, w2t_ref, o_ref, ot_ref):
    kb = kb_ref[0, 0]
    half = kb.shape[1]
    n16 = kb.shape[0]
    first = jnp.dot(kb, w1_ref[0, :half, :], preferred_element_type=F32)
    second = jnp.dot(kb, w1_ref[0, half:, :], preferred_element_type=F32)
    posb = jnp.dot(pos_ref[0], w1_ref[0], preferred_element_type=F32)[0:1]
    pre = first + pltpu.roll(second, n16 - 1, 0) + posb
    h = _gelu_tanh(pre).astype(BF16)
    o_ref[0, 0] = jnp.dot(h, w2_ref[0], preferred_element_type=F32).astype(BF16)
    ot_ref[0, 0] = lax.dot_general(w2t_ref[0], h, _NT, preferred_element_type=F32).astype(BF16)


def _compress(kb, pos8, w1, w2, w2t):
    _, g, n16, cin2 = kb.shape
    hid = w1.shape[2]
    return pl.pallas_call(
        _compress_kernel,
        grid=(2, g),
        in_specs=[pl.BlockSpec((1, 1, n16, cin2), lambda a, b: (a, b, 0, 0)),
                  pl.BlockSpec((1, 8, 2 * cin2), lambda a, b: (a, 0, 0)),
                  pl.BlockSpec((1, 2 * cin2, hid), lambda a, b: (a, 0, 0)),
                  pl.BlockSpec((1, hid, HEAD_DIM), lambda a, b: (a, 0, 0)),
                  pl.BlockSpec((1, HEAD_DIM, hid), lambda a, b: (a, 0, 0))],
        out_specs=[pl.BlockSpec((1, 1, n16, HEAD_DIM), lambda a, b: (a, b, 0, 0)),
                   pl.BlockSpec((1, 1, HEAD_DIM, n16), lambda a, b: (a, b, 0, 0))],
        out_shape=[jax.ShapeDtypeStruct((2, g, n16, HEAD_DIM), BF16),
                   jax.ShapeDtypeStruct((2, g, HEAD_DIM, n16), BF16)],
        compiler_params=_params("arbitrary", "arbitrary"),
        name="kv_compress",
    )(kb, pos8, w1, w2, w2t)


def _heads_to_lanes(q_ref):
    return jnp.concatenate([q_ref[0, r * HEAD_DIM:(r + 1) * HEAD_DIM, :] for r in range(REP)], axis=1)


def _store_heads(o_ref, o_t):
    for r in range(REP):
        o_ref[0, r * HEAD_DIM:(r + 1) * HEAD_DIM, :] = o_t[:, r * QBLOCK:(r + 1) * QBLOCK].astype(o_ref.dtype)


def _cmpsel_body(q4, kc_ref, vct_ref, oc_ref, sel_ref, pg_ref, t0, *, rows, nomask, nsel, topk):
    s = jnp.dot(kc_ref[0, 0, 0:rows, :], q4, preferred_element_type=F32)
    mrows = rows - nomask
    n_io = nomask + lax.broadcasted_iota(jnp.int32, (mrows, HQ), 0)
    t_io = t0 + (lax.broadcasted_iota(jnp.int32, (mrows, HQ), 1) & (QBLOCK - 1))
    edge = jnp.where(n_io * CMP_STRIDE + (CMP_BLOCK - 1) <= t_io, s[nomask:], -jnp.inf)
    s = jnp.concatenate([s[:nomask], edge], axis=0) if nomask else edge
    m = jnp.max(s, axis=0, keepdims=True)
    m = jnp.where(m == -jnp.inf, 0.0, m)
    e = jnp.exp2(s - m)
    den = jnp.maximum(jnp.sum(e, axis=0, keepdims=True), 1e-30)
    p = e * (1.0 / den)
    _store_heads(oc_ref, jnp.dot(vct_ref[0, 0, :, 0:rows], p.astype(BF16), preferred_element_type=F32))

    pg = ((p[:, 0:QBLOCK] + p[:, QBLOCK:2 * QBLOCK]) + p[:, 2 * QBLOCK:3 * QBLOCK]) + p[:, 3 * QBLOCK:]
    pg_ref[0:8, :] = jnp.zeros((8, QBLOCK), F32)
    pg_ref[8:8 + rows, :] = pg
    ratio = SEL_BLOCK // CMP_STRIDE
    nblk = rows // ratio
    ps = pg_ref[pl.ds(7, nblk, stride=ratio), :]
    for o in range(1, ratio + 1):
        ps = ps + pg_ref[pl.ds(7 + o, nblk, stride=ratio), :]

    blk = lax.broadcasted_iota(jnp.int32, (nblk, QBLOCK), 0)
    cur = (t0 + lax.broadcasted_iota(jnp.int32, (nblk, QBLOCK), 1)) // SEL_BLOCK
    valid = blk <= cur
    forced = (blk == 0) | (blk == cur) | (blk == cur - 1)
    score = jnp.where(valid, jnp.where(forced, jnp.inf, ps), -jnp.inf)

    def pick_one(_, rem):
        top = jnp.max(rem, axis=0, keepdims=True)
        first = jnp.min(jnp.where(rem == top, blk, nblk), axis=0, keepdims=True)
        return jnp.where(blk == first, -jnp.inf, rem)

    rem = lax.fori_loop(0, topk, pick_one, score)
    sel_ref[0, 0:nblk, :] = jnp.where((rem == -jnp.inf) & valid, 0.0, -MASK_BIG).astype(BF16)
    if nblk < nsel:
        sel_ref[0, nblk:, :] = jnp.full((nsel - nblk, QBLOCK), -MASK_BIG, BF16)


def _cmpsel_kernel(q_ref, kc_ref, vct_ref, oc_ref, sel_ref, pg_ref, *, n16, nsel, topk, nvar):
    ci = pl.program_id(1)
    t0 = ci * QBLOCK
    q4 = _heads_to_lanes(q_ref)
    per = n16 // nvar
    cpv = per // (QBLOCK // CMP_STRIDE)
    for k in range(1, nvar + 1):
        @pl.when((ci >= (k - 1) * cpv) & (ci < k * cpv))
        def _(k=k):
            _cmpsel_body(q4, kc_ref, vct_ref, oc_ref, sel_ref, pg_ref, t0, rows=per * k,
                         nomask=max(per * (k - 1) - 8, 0), nsel=nsel, topk=topk)


def _cmp_select(q, kc_all, vct_all, s):
    n16 = s // CMP_STRIDE
    nsel = s // SEL_BLOCK
    topk = min(SEL_TOPK, nsel)
    nvar = max(n16 // 256, 1)
    return pl.pallas_call(
        functools.partial(_cmpsel_kernel, n16=n16, nsel=nsel, topk=topk, nvar=nvar),
        grid=(GROUPS, s // QBLOCK),
        in_specs=[pl.BlockSpec((1, REP * HEAD_DIM, QBLOCK), lambda g, c: (g, 0, c)),
                  pl.BlockSpec((1, 1, n16, HEAD_DIM), lambda g, c: (0, g, 0, 0)),
                  pl.BlockSpec((1, 1, HEAD_DIM, n16), lambda g, c: (1, g, 0, 0))],
        out_specs=[pl.BlockSpec((1, REP * HEAD_DIM, QBLOCK), lambda g, c: (g, 0, c)),
                   pl.BlockSpec((1, nsel, QBLOCK), lambda g, c: (g, 0, c))],
        out_shape=[jax.ShapeDtypeStruct((GROUPS, REP * HEAD_DIM, s), BF16),
                   jax.ShapeDtypeStruct((GROUPS, nsel, s), BF16)],
        scratch_shapes=[pltpu.VMEM((n16 + 8, QBLOCK), F32)],
        compiler_params=_params("arbitrary", "arbitrary"),
        name="cmp_select",
    )(q, kc_all, vct_all)


def _fill_bias_strip(strip_ref, thr_ref, tab_ref, g, *, dist0, lo, hi, minus_far):
    nrows = strip_ref.shape[0]

    def block(bi, carry):
        i0 = pl.multiple_of(bi * QBLOCK, QBLOCK)
        d = (lax.broadcasted_iota(jnp.int32, (QBLOCK, QBLOCK), 1)
             - lax.broadcasted_iota(jnp.int32, (QBLOCK, QBLOCK), 0) + (dist0 - i0))
        ok = d >= lo if hi is None else (d >= lo) & (d < hi)
        for r in range(REP):
            base = (g * REP + r) * NUM_BUCKETS
            out = jnp.full((QBLOCK, QBLOCK), tab_ref[base], F32)
            for b in range(1, NUM_BUCKETS):
                out = jnp.where(d >= thr_ref[b], tab_ref[base + b], out)
            if minus_far:
                out = out - tab_ref[base + NUM_BUCKETS - 1]
            strip_ref[pl.ds(i0, QBLOCK), r * QBLOCK:(r + 1) * QBLOCK] = jnp.where(ok, out, -MASK_BIG)
        return carry

    lax.fori_loop(0, nrows // QBLOCK, block, 0)


def _sel_kernel(thr_ref, tab_ref, q_ref, sel_ref, kaug_ref, vt_ref, cst_ref, o_ref,
                qa_ref, s_ref, tmax_ref, m_ref, acc_ref, strip_ref, *, tk):
    g = pl.program_id(0)
    st = pl.program_id(1)
    nb = tk // SEL_BLOCK
    cpt = tk // QBLOCK
    kd = GROUPS * HEAD_DIM
    dmax = 2 * tk - QBLOCK

    @pl.when(st == 0)
    def _():
        _fill_bias_strip(strip_ref, thr_ref, tab_ref, g, dist0=dmax, lo=0, hi=None, minus_far=True)

    qa_ref[...] = jnp.zeros(qa_ref.shape, BF16)
    for b in range(2):
        qa_ref[b, kd + nb:kd + nb + 16, :] = cst_ref[0]
    m_ref[...] = jnp.full(m_ref.shape, -jnp.inf, F32)
    acc_ref[...] = jnp.zeros(acc_ref.shape, F32)
    row0 = pl.multiple_of(g * HEAD_DIM, HEAD_DIM)
    nnear = jnp.minimum(st + 1, 2)
    nfar = st + 1 - nnear

    def run(count, first_tile, near):

        def advance(c, i):
            wrap = i + 1 == count
            return jnp.where(wrap, c + 1, c), jnp.where(wrap, 0, i + 1)

        def qk(c, i, buf):
            cc = jnp.minimum(c, cpt - 1)
            j = first_tile + i
            lane0 = pl.multiple_of(cc * QBLOCK, QBLOCK)
            for r in range(REP):
                qa_ref[buf, pl.ds(row0, HEAD_DIM), r * QBLOCK:(r + 1) * QBLOCK] = (
                    q_ref[0, r * HEAD_DIM:(r + 1) * HEAD_DIM, pl.ds(lane0, QBLOCK)])
            sb = sel_ref[0, pl.ds(pl.multiple_of(j * nb, nb), nb), pl.ds(lane0, QBLOCK)]
            for r in range(REP):
                qa_ref[buf, kd:kd + nb, r * QBLOCK:(r + 1) * QBLOCK] = sb
            k0 = pl.multiple_of(j * tk, tk)
            s = jnp.dot(kaug_ref[pl.ds(k0, tk), :], qa_ref[buf], preferred_element_type=F32)
            if near:
                delta = (st - j) * tk + cc * QBLOCK
                s = s + strip_ref[pl.ds(pl.multiple_of(dmax - delta, QBLOCK), tk), :]
            s_ref[buf] = s
            tmax_ref[buf] = jnp.max(s, axis=0, keepdims=True)

        def softmax_pv(c, i, buf):
            k0 = pl.multiple_of((first_tile + i) * tk, tk)
            m_prev = m_ref[c]
            m_new = jnp.maximum(m_prev, tmax_ref[buf])
            alpha = jnp.exp2(m_prev - m_new)
            p = jnp.exp2(s_ref[buf] - m_new)
            acc = alpha * acc_ref[c] + jnp.dot(vt_ref[0, :, pl.ds(k0, tk)], p.astype(BF16),
                                               preferred_element_type=F32)
            acc_ref[c] = acc
            m_ref[c] = m_new
            if near:
                lane0 = pl.multiple_of(c * QBLOCK, QBLOCK)
                out = acc[0:HEAD_DIM] * (1.0 / jnp.maximum(acc[HEAD_DIM:HEAD_DIM + 1], 1e-30))
                for r in range(REP):
                    o_ref[0, r * HEAD_DIM:(r + 1) * HEAD_DIM, pl.ds(lane0, QBLOCK)] = (
                        out[:, r * QBLOCK:(r + 1) * QBLOCK].astype(o_ref.dtype))

        zero = jnp.int32(0)
        qk(zero, zero, 0)

        def pair(_, carry):
            ca, ia = carry
            cb, ib = advance(ca, ia)
            qk(cb, ib, 1)
            softmax_pv(ca, ia, 0)
            cn, inx = advance(cb, ib)
            qk(cn, inx, 0)
            softmax_pv(cb, ib, 1)
            return cn, inx

        lax.fori_loop(0, (cpt * count) // 2, pair, (zero, zero))

    @pl.when(nfar > 0)
    def _():
        run(nfar, 0, False)

    run(nnear, nfar, True)


def _selected(thr, tab, q, sel, kaug, vt, cst, s, tk):
    nsel = s // SEL_BLOCK
    sl = 3 * tk - QBLOCK
    cpt = tk // QBLOCK
    smem = pl.BlockSpec(memory_space=pltpu.SMEM)
    return pl.pallas_call(
        functools.partial(_sel_kernel, tk=tk),
        grid=(GROUPS, s // tk),
        in_specs=[smem, smem,
                  pl.BlockSpec((1, REP * HEAD_DIM, tk), lambda g, c: (g, 0, c)),
                  pl.BlockSpec((1, nsel, tk), lambda g, c: (g, 0, c)),
                  pl.BlockSpec(kaug.shape, lambda g, c: (0, 0)),
                  pl.BlockSpec((1, vt.shape[1], s), lambda g, c: (g, 0, 0)),
                  pl.BlockSpec((1, 16, HQ), lambda g, c: (g, 0, 0))],
        out_specs=pl.BlockSpec((1, REP * HEAD_DIM, tk), lambda g, c: (g, 0, c)),
        out_shape=jax.ShapeDtypeStruct((GROUPS, REP * HEAD_DIM, s), BF16),
        scratch_shapes=[pltpu.VMEM((2, kaug.shape[1], HQ), BF16), pltpu.VMEM((2, tk, HQ), F32),
                        pltpu.VMEM((2, 1, HQ), F32),
                        pltpu.VMEM((cpt, 1, HQ), F32), pltpu.VMEM((cpt, vt.shape[1], HQ), F32),
                        pltpu.VMEM((sl, HQ), F32)],
        compiler_params=_params("arbitrary", "arbitrary"),
        name="selected_attn",
    )(thr, tab, q, sel, kaug, vt, cst)


def _band_kernel(thr_ref, tab_ref, q_ref, k_ref, vt_ref, sink_ref, o_ref, qa_ref, strip_ref, *,
                 window, use_sink, cb):
    g = pl.program_id(0)

    @pl.when(pl.program_id(1) == 0)
    def _():
        _fill_bias_strip(strip_ref, thr_ref, tab_ref, g, dist0=window, lo=0, hi=window, minus_far=False)

    base = pl.multiple_of(pl.program_id(1) * (QBLOCK * cb), QBLOCK * cb)
    nk = window + QBLOCK
    qa_ref[...] = jnp.zeros(qa_ref.shape, BF16)
    row0 = pl.multiple_of(g * HEAD_DIM, HEAD_DIM)
    for c in range(cb):
        for r in range(REP):
            qa_ref[c, pl.ds(row0, HEAD_DIM), r * QBLOCK:(r + 1) * QBLOCK] = (
                q_ref[0, r * HEAD_DIM:(r + 1) * HEAD_DIM, c * QBLOCK:(c + 1) * QBLOCK])
    row = lax.broadcasted_iota(jnp.int32, (nk, HQ), 0)
    for c in range(cb):
        t0 = base + c * QBLOCK
        s = jnp.dot(k_ref[pl.ds(t0, nk), :], qa_ref[c], preferred_element_type=F32) + strip_ref[...]
        s = jnp.where(row + (t0 - window) >= 0, s, -MASK_BIG)
        m = jnp.max(s, axis=0, keepdims=True)
        if use_sink:
            m = jnp.maximum(m, sink_ref[0])
        e = jnp.exp2(s - m)
        den = jnp.sum(e, axis=0, keepdims=True)
        if use_sink:
            den = den + jnp.exp2(sink_ref[0] - m)
        else:
            den = jnp.maximum(den, 1e-30)
        o_t = jnp.dot(vt_ref[0, :, pl.ds(t0, nk)], e.astype(BF16), preferred_element_type=F32) * (1.0 / den)
        for r in range(REP):
            o_ref[0, r * HEAD_DIM:(r + 1) * HEAD_DIM, c * QBLOCK:(c + 1) * QBLOCK] = (
                o_t[:, r * QBLOCK:(r + 1) * QBLOCK].astype(o_ref.dtype))


def _banded(thr, tab, q, kpad, vtpad, sink, s, window, use_sink, name, cb=4):
    nk = window + QBLOCK
    smem = pl.BlockSpec(memory_space=pltpu.SMEM)
    return pl.pallas_call(
        functools.partial(_band_kernel, window=window, use_sink=use_sink, cb=cb),
        grid=(GROUPS, s // (QBLOCK * cb)),
        in_specs=[smem, smem,
                  pl.BlockSpec((1, REP * HEAD_DIM, QBLOCK * cb), lambda g, c: (g, 0, c)),
                  pl.BlockSpec(kpad.shape, lambda g, c: (0, 0)),
                  pl.BlockSpec((1, HEAD_DIM, s + window), lambda g, c: (g, 0, 0)),
                  pl.BlockSpec((1, 1, HQ), lambda g, c: (g, 0, 0))],
        out_specs=pl.BlockSpec((1, REP * HEAD_DIM, QBLOCK * cb), lambda g, c: (g, 0, c)),
        out_shape=jax.ShapeDtypeStruct((GROUPS, REP * HEAD_DIM, s), BF16),
        scratch_shapes=[pltpu.VMEM((cb, GROUPS * HEAD_DIM, HQ), BF16), pltpu.VMEM((nk, HQ), F32)],
        compiler_params=_params("arbitrary", "arbitrary"),
        name=name,
    )(thr, tab, q, kpad, vtpad, sink)


def _merge_kernel(oc_ref, os_ref, ow_ref, ob_ref, gn_ref, gab_ref, x_ref, wa_ref, wb_ref, wo_ref, gp_ref, o_ref):
    d = x_ref.shape[1]
    gn = jax.nn.sigmoid(gn_ref[...])
    parts = []
    for h in range(HEADS):
        rows = slice(h * HEAD_DIM, (h + 1) * HEAD_DIM)
        parts.append(gn[h:h + 1] * oc_ref[rows, :].astype(F32)
                     + gn[HEADS + h:HEADS + h + 1] * os_ref[rows, :].astype(F32)
                     + gn[2 * HEADS + h:2 * HEADS + h + 1] * ow_ref[rows, :].astype(F32))
    oa_t = jnp.concatenate(parts, axis=0).astype(BF16)
    ya = lax.dot_general(oa_t, wa_ref[...], _TN, preferred_element_type=F32)
    yb = lax.dot_general(ob_ref[...], wb_ref[...], _TN, preferred_element_type=F32)
    y = (jax.nn.sigmoid(gab_ref[:, :d].astype(F32)) * ya
         + jax.nn.sigmoid(gab_ref[:, d:].astype(F32)) * yb)
    z = jnp.dot(y.astype(BF16), wo_ref[...], preferred_element_type=F32)
    o_ref[...] = x_ref[...] + _rms(z, gp_ref[...])


def _merge(oc, os_, ow, ob, gn, gab, x, wa, wb, wo, gpost, ts):
    s, d = x.shape
    hd = oc.shape[0]
    tok = lambda rows: pl.BlockSpec((rows, ts), lambda i: (0, i))
    full = lambda shp: pl.BlockSpec(shp, lambda i: (0, 0))
    return pl.pallas_call(
        _merge_kernel,
        grid=(s // ts,),
        in_specs=[tok(hd), tok(hd), tok(hd), tok(hd), tok(gn.shape[0]),
                  pl.BlockSpec((ts, 2 * d), lambda i: (i, 0)), pl.BlockSpec((ts, d), lambda i: (i, 0)),
                  full((hd, d)), full((hd, d)), full((d, d)), full((1, d))],
        out_specs=pl.BlockSpec((ts, d), lambda i: (i, 0)),
        out_shape=jax.ShapeDtypeStruct((s, d), F32),
        compiler_params=_params("arbitrary"),
        name="merge_out_proj",
    )(oc, os_, ow, ob, gn, gab, x, wa, wb, wo, gpost)


def _mlp_kernel(h_ref, gpre_ref, w1_ref, w2_ref, gpost_ref, o_ref, u_ref, acc_ref):
    f = pl.program_id(1)

    @pl.when(f == 0)
    def _():
        u_ref[...] = _rms(h_ref[...], gpre_ref[...]).astype(BF16)
        acc_ref[...] = jnp.zeros(acc_ref.shape, F32)

    a = jnp.maximum(jnp.dot(u_ref[...], w1_ref[...], preferred_element_type=F32), 0.0)
    acc_ref[...] += jnp.dot((a * a).astype(BF16), w2_ref[...], preferred_element_type=F32)

    @pl.when(f == pl.num_programs(1) - 1)
    def _():
        o_ref[...] = h_ref[...] + _rms(acc_ref[...], gpost_ref[...])


def _mlp(h, gpre, w1, w2, gpost, ts, tf):
    s, d = h.shape
    dff = w1.shape[1]
    return pl.pallas_call(
        _mlp_kernel,
        grid=(s // ts, dff // tf),
        in_specs=[pl.BlockSpec((ts, d), lambda i, f: (i, 0)), pl.BlockSpec((1, d), lambda i, f: (0, 0)),
                  pl.BlockSpec((d, tf), lambda i, f: (0, f)), pl.BlockSpec((tf, d), lambda i, f: (f, 0)),
                  pl.BlockSpec((1, d), lambda i, f: (0, 0))],
        out_specs=pl.BlockSpec((ts, d), lambda i, f: (i, 0)),
        out_shape=jax.ShapeDtypeStruct((s, d), F32),
        scratch_shapes=[pltpu.VMEM((ts, d), BF16), pltpu.VMEM((ts, d), F32)],
        compiler_params=_params("arbitrary", "arbitrary"),
        name="mlp_relu2",
    )(h, gpre, w1, w2, gpost)


def kernel(x, norm_mix_pre, norm_mix_post, norm_mlp_pre, norm_mlp_post, w_in,
           cmp_pos_k, cmp_w1_k, cmp_w2_k, cmp_pos_v, cmp_w1_v, cmp_w2_v,
           attn_sinks, rel_bias, w_up_nsa, w_up_swa, w_out, w_mlp_in, w_mlp_out):
    b, s, d = x.shape
    assert b == 1 and s % SEL_TILE == 0 and w_in.shape[0] == 1
    qd = HEADS * HEAD_DIM
    kvd = GROUPS * HEAD_DIM
    sizes = (qd,) + (kvd,) * 6 + (3 * HEADS, qd, kvd, kvd, d, d)
    offs = [0]
    for z in sizes:
        offs.append(offs[-1] + z)
    w = w_in[0]
    col = lambda i: w[:, offs[i]:offs[i + 1]]
    (w_qn, w_kc, w_vc, w_ks, w_vs, w_kw, w_vw, w_gn, w_qs, w_k_s, w_v_s, w_ga, w_gb) = [col(i) for i in range(13)]
    scale = HEAD_DIM ** -0.5 * LOG2E
    w_gn = w_gn.reshape(d, HEADS, 3).transpose(0, 2, 1).reshape(d, 3 * HEADS)
    w_gn = jnp.pad(w_gn, ((0, 0), (0, 8)))
    wn = jnp.concatenate([w_kc, w_vc, w_ks, w_kw, w_k_s], axis=1).astype(BF16)
    wg = jnp.concatenate([w_ga, w_gb], axis=1).astype(BF16)
    wt = jnp.concatenate([w_qn * scale, w_qs * scale, w_vs, w_vw, w_v_s, w_gn], axis=1).T.astype(BF16)
    nqv = 2 * qd + 3 * kvd

    x2 = x[0]
    kv, gab, qv, gn = _project(x2, norm_mix_pre, wn, wg, wt, nqv, ts=512)
    q_nsa = qv[0:qd].reshape(GROUPS, REP * HEAD_DIM, s)
    q_swa = qv[qd:2 * qd].reshape(GROUPS, REP * HEAD_DIM, s)
    vs_t = qv[2 * qd:2 * qd + kvd].reshape(GROUPS, HEAD_DIM, s)
    vw_t = qv[2 * qd + kvd:2 * qd + 2 * kvd].reshape(GROUPS, HEAD_DIM, s)
    vswa_t = qv[2 * qd + 2 * kvd:].reshape(GROUPS, HEAD_DIM, s)

    n16 = s // CMP_STRIDE
    kb = kv[:, :2 * kvd].reshape(n16, CMP_STRIDE, 2, GROUPS, HEAD_DIM).transpose(2, 3, 0, 1, 4)
    kb = kb.reshape(2, GROUPS, n16, CMP_STRIDE * HEAD_DIM)
    pos8 = jnp.stack([cmp_pos_k[0], cmp_pos_v[0]]).reshape(2, 1, CMP_BLOCK * HEAD_DIM)
    pos8 = jnp.broadcast_to(pos8, (2, 8, CMP_BLOCK * HEAD_DIM)).astype(BF16)
    w1 = jnp.stack([cmp_w1_k[0], cmp_w1_v[0]]).astype(BF16)
    w2 = jnp.stack([cmp_w2_k[0], cmp_w2_v[0]]).astype(BF16)
    cmp_n, cmp_t = _compress(kb, pos8, w1, w2, w2.transpose(0, 2, 1))

    oc, sel = _cmp_select(q_nsa, cmp_n, cmp_t, s)

    bias_nsa = rel_bias[:, :HEADS].reshape(NUM_BUCKETS, GROUPS, REP) * LOG2E
    bias_swa = rel_bias[:, HEADS:].reshape(NUM_BUCKETS, GROUPS, REP) * LOG2E
    thr = jnp.asarray(_bucket_starts())
    tab_nsa = bias_nsa.transpose(1, 2, 0).reshape(-1)
    tab_swa = bias_swa.transpose(1, 2, 0).reshape(-1)

    tk = SEL_TILE
    nb = tk // SEL_BLOCK
    posn = jnp.arange(s)
    onehot = ((posn // SEL_BLOCK) % nb)[:, None] == jnp.arange(nb)[None, :]
    aug = jnp.concatenate([onehot.astype(BF16), jnp.ones((s, 2), BF16),
                           jnp.zeros((s, kvd - nb - 2), BF16)], axis=1)
    kaug = jnp.concatenate([kv[:, 2 * kvd:3 * kvd], aug], axis=1)
    far = bias_nsa[NUM_BUCKETS - 1]
    far_hi = far.astype(BF16)
    far_lo = (far - far_hi.astype(F32)).astype(BF16)
    cst = jnp.stack([far_hi, far_lo], axis=1)
    cst = jnp.broadcast_to(cst[:, :, :, None], (GROUPS, 2, REP, QBLOCK)).reshape(GROUPS, 2, HQ)
    cst = jnp.pad(cst, ((0, 0), (0, 14), (0, 0)))
    ones_row = jnp.concatenate([jnp.ones((GROUPS, 1, s), BF16), jnp.zeros((GROUPS, 15, s), BF16)], axis=1)
    vs_aug = jnp.concatenate([vs_t, ones_row], axis=1)
    o_sel = _selected(thr, tab_nsa, q_nsa, sel, kaug, vs_aug, cst, s, tk)

    pad_rows = lambda a, wdw: jnp.pad(a, ((wdw, 0), (0, 0)))
    pad_lanes = lambda a, wdw: jnp.pad(a, ((0, 0), (0, 0), (wdw, 0)))
    no_sink = jnp.zeros((GROUPS, 1, HQ), F32)
    o_win = _banded(thr, tab_nsa, q_nsa, pad_rows(kv[:, 3 * kvd:4 * kvd], NSA_WINDOW), pad_lanes(vw_t, NSA_WINDOW),
                    no_sink, s, NSA_WINDOW, False, "nsa_window")
    sink = attn_sinks[0].reshape(GROUPS, 1, REP, 1).astype(F32) * LOG2E
    sink = jnp.broadcast_to(sink, (GROUPS, 1, REP, QBLOCK)).reshape(GROUPS, 1, HQ)
    o_swa = _banded(thr, tab_swa, q_swa, pad_rows(kv[:, 4 * kvd:5 * kvd], SWA_WINDOW), pad_lanes(vswa_t, SWA_WINDOW),
                    sink, s, SWA_WINDOW, True, "swa_sink")

    flat = lambda a: a.reshape(qd, s)
    h1 = _merge(flat(oc), flat(o_sel), flat(o_win), flat(o_swa), gn, gab, x2,
                w_up_nsa[0].astype(BF16), w_up_swa[0].astype(BF16), w_out[0].astype(BF16),
                norm_mix_post, ts=512)
    out = _mlp(h1, norm_mlp_pre, w_mlp_in[0].astype(BF16), w_mlp_out[0].astype(BF16), norm_mlp_post,
               ts=1024, tf=1024)
    return out[None]
```

```python
import functools
import math

import jax
import jax.numpy as jnp
import numpy as np
from jax import lax
from jax.experimental import pallas as pl
from jax.experimental.pallas import tpu as pltpu

F32 = jnp.float32
BF16 = jnp.bfloat16

HEAD_DIM = 64
GROUPS = 2
REP = 4
HEADS = GROUPS * REP
CMP_BLOCK = 32
CMP_STRIDE = 16
SEL_BLOCK = 64
SEL_TOPK = 16
NSA_WINDOW = 512
SWA_WINDOW = 128
QBLOCK = 128
NUM_BUCKETS = 32
MAX_DISTANCE = 1024
RMS_EPS = 1e-6
LOG2E = math.log2(math.e)
MASK_BIG = 2.0 ** 99
SEL_TILE = 1024
HQ = REP * QBLOCK
VMEM_LIMIT = 56 * 1024 * 1024

_NT = (((1,), (1,)), ((), ()))
_TN = (((0,), (0,)), ((), ()))


def _params(*sem):
    return pltpu.CompilerParams(dimension_semantics=sem, vmem_limit_bytes=VMEM_LIMIT)


def _bucket_starts():
    max_exact = NUM_BUCKETS // 2
    d = np.arange(2 * MAX_DISTANCE)
    large = max_exact + (np.log(np.maximum(d, 1).astype(np.float64) / max_exact)
                         / math.log(MAX_DISTANCE / max_exact) * (NUM_BUCKETS - max_exact)).astype(np.int64)
    bucket = np.where(d < max_exact, d, np.minimum(large, NUM_BUCKETS - 1))
    return (bucket[None, :] < np.arange(NUM_BUCKETS)[:, None]).sum(axis=1).astype(np.int32)


def _rms(x, gain):
    return x * lax.rsqrt(jnp.mean(x * x, axis=-1, keepdims=True) + RMS_EPS) * gain


def _proj_kernel(x_ref, g_ref, wn_ref, wg_ref, wt_ref, kv_ref, gab_ref, qv_ref, gn_ref, *, nqv):
    u = _rms(x_ref[...], g_ref[...]).astype(BF16)
    kv_ref[...] = jnp.dot(u, wn_ref[...], preferred_element_type=F32).astype(BF16)
    gab_ref[...] = jnp.dot(u, wg_ref[...], preferred_element_type=F32).astype(BF16)
    t = lax.dot_general(wt_ref[...], u, _NT, preferred_element_type=F32)
    qv_ref[...] = t[:nqv].astype(BF16)
    gn_ref[...] = t[nqv:]


def _project(x, gain, wn, wg, wt, nqv, ts):
    s, d = x.shape
    nn, ng, ntr = wn.shape[1], wg.shape[1], wt.shape[0]
    full = lambda shp: pl.BlockSpec(shp, lambda i: (0, 0))
    return pl.pallas_call(
        functools.partial(_proj_kernel, nqv=nqv),
        grid=(s // ts,),
        in_specs=[pl.BlockSpec((ts, d), lambda i: (i, 0)), full((1, d)),
                  full((d, nn)), full((d, ng)), full((ntr, d))],
        out_specs=[pl.BlockSpec((ts, nn), lambda i: (i, 0)),
                   pl.BlockSpec((ts, ng), lambda i: (i, 0)),
                   pl.BlockSpec((nqv, ts), lambda i: (0, i)),
                   pl.BlockSpec((ntr - nqv, ts), lambda i: (0, i))],
        out_shape=[jax.ShapeDtypeStruct((s, nn), BF16), jax.ShapeDtypeStruct((s, ng), BF16),
                   jax.ShapeDtypeStruct((nqv, s), BF16), jax.ShapeDtypeStruct((ntr - nqv, s), F32)],
        compiler_params=_params("arbitrary"),
        name="in_proj",
    )(x, gain, wn, wg, wt)


def _gelu_tanh(x):
    return x * (0.5 * (1.0 + jnp.tanh(math.sqrt(2.0 / math.pi) * (x + 0.044715 * (x * x * x)))))


def _compress_kernel(kb_ref, pos_ref, w1_ref, w2_ref, w2t_ref, o_ref, ot_ref):
    kb = kb_ref[0, 0]
    half = kb.shape[1]
    n16 = kb.shape[0]
    first = jnp.dot(kb, w1_ref[0, :half, :], preferred_element_type=F32)
    second = jnp.dot(kb, w1_ref[0, half:, :], preferred_element_type=F32)
    posb = jnp.dot(pos_ref[0], w1_ref[0], preferred_element_type=F32)[0:1]
    pre = first + pltpu.roll(second, n16 - 1, 0) + posb
    h = _gelu_tanh(pre).astype(BF16)
    o_ref[0, 0] = jnp.dot(h, w2_ref[0], preferred_element_type=F32).astype(BF16)
    ot_ref[0, 0] = lax.dot_general(w2t_ref[0], h, _NT, preferred_element_type=F32).astype(BF16)


def _compress(kb, pos8, w1, w2, w2t):
    _, g, n16, cin2 = kb.shape
    hid = w1.shape[2]
    return pl.pallas_call(
        _compress_kernel,
        grid=(2, g),
        in_specs=[pl.BlockSpec((1, 1, n16, cin2), lambda a, b: (a, b, 0, 0)),
                  pl.BlockSpec((1, 8, 2 * cin2), lambda a, b: (a, 0, 0)),
                  pl.BlockSpec((1, 2 * cin2, hid), lambda a, b: (a, 0, 0)),
                  pl.BlockSpec((1, hid, HEAD_DIM), lambda a, b: (a, 0, 0)),
                  pl.BlockSpec((1, HEAD_DIM, hid), lambda a, b: (a, 0, 0))],
        out_specs=[pl.BlockSpec((1, 1, n16, HEAD_DIM), lambda a, b: (a, b, 0, 0)),
                   pl.BlockSpec((1, 1, HEAD_DIM, n16), lambda a, b: (a, b, 0, 0))],
        out_shape=[jax.ShapeDtypeStruct((2, g, n16, HEAD_DIM), BF16),
                   jax.ShapeDtypeStruct((2, g, HEAD_DIM, n16), BF16)],
        compiler_params=_params("arbitrary", "arbitrary"),
        name="kv_compress",
    )(kb, pos8, w1, w2, w2t)


def _cmpsel_body(q_ref, kc_ref, vct_ref, oc_ref, sel_ref, pg_ref, t_base, *, cb, rows, nomask, nsel, topk):
    mrows = rows - nomask
    ratio = SEL_BLOCK // CMP_STRIDE
    nblk = rows // ratio
    n_io = nomask + lax.broadcasted_iota(jnp.int32, (mrows, HQ), 0)
    lane_t = lax.broadcasted_iota(jnp.int32, (mrows, HQ), 1) & (QBLOCK - 1)
    blk = lax.broadcasted_iota(jnp.int32, (nblk, QBLOCK), 0)
    scores, valids = [], []
    for c in range(cb):
        t0 = t_base + c * QBLOCK
        lanes = slice(c * QBLOCK, (c + 1) * QBLOCK)
        q4 = jnp.concatenate([q_ref[0, r * HEAD_DIM:(r + 1) * HEAD_DIM, lanes] for r in range(REP)], axis=1)
        s = jnp.dot(kc_ref[0, 0, 0:rows, :], q4, preferred_element_type=F32)
        edge = jnp.where(n_io * CMP_STRIDE + (CMP_BLOCK - 1) <= t0 + lane_t, s[nomask:], -jnp.inf)
        s = jnp.concatenate([s[:nomask], edge], axis=0) if nomask else edge
        m = jnp.max(s, axis=0, keepdims=True)
        m = jnp.where(m == -jnp.inf, 0.0, m)
        e = jnp.exp2(s - m)
        den = jnp.maximum(jnp.sum(e, axis=0, keepdims=True), 1e-30)
        p = e * (1.0 / den)
        o_t = jnp.dot(vct_ref[0, 0, :, 0:rows], p.astype(BF16), preferred_element_type=F32)
        for r in range(REP):
            oc_ref[0, r * HEAD_DIM:(r + 1) * HEAD_DIM, lanes] = (
                o_t[:, r * QBLOCK:(r + 1) * QBLOCK].astype(oc_ref.dtype))

        pg = ((p[:, 0:QBLOCK] + p[:, QBLOCK:2 * QBLOCK]) + p[:, 2 * QBLOCK:3 * QBLOCK]) + p[:, 3 * QBLOCK:]
        pg_ref[c, 0:8, :] = jnp.zeros((8, QBLOCK), F32)
        pg_ref[c, 8:8 + rows, :] = pg
        ps = pg_ref[c, pl.ds(7, nblk, stride=ratio), :]
        for o in range(1, ratio + 1):
            ps = ps + pg_ref[c, pl.ds(7 + o, nblk, stride=ratio), :]
        cur = (t0 + lax.broadcasted_iota(jnp.int32, (nblk, QBLOCK), 1)) // SEL_BLOCK
        valid = blk <= cur
        forced = (blk == 0) | (blk == cur) | (blk == cur - 1)
        scores.append(jnp.where(valid, jnp.where(forced, jnp.inf, ps), -jnp.inf))
        valids.append(valid)

    def pick_one(_, rems):
        out = []
        for rem in rems:
            top = jnp.max(rem, axis=0, keepdims=True)
            first = jnp.min(jnp.where(rem == top, blk, nblk), axis=0, keepdims=True)
            out.append(jnp.where(blk == first, -jnp.inf, rem))
        return tuple(out)

    rems = lax.fori_loop(0, topk, pick_one, tuple(scores))
    for c in range(cb):
        lanes = slice(c * QBLOCK, (c + 1) * QBLOCK)
        sel_ref[0, 0:nblk, lanes] = jnp.where((rems[c] == -jnp.inf) & valids[c], 0.0, -MASK_BIG).astype(BF16)
        if nblk < nsel:
            sel_ref[0, nblk:, lanes] = jnp.full((nsel - nblk, QBLOCK), -MASK_BIG, BF16)


def _cmpsel_kernel(q_ref, kc_ref, vct_ref, oc_ref, sel_ref, pg_ref, *, n16, nsel, topk, nvar, cb):
    step = pl.program_id(1)
    per = n16 // nvar
    spv = per // (QBLOCK // CMP_STRIDE) // cb
    for k in range(1, nvar + 1):
        @pl.when((step >= (k - 1) * spv) & (step < k * spv))
        def _(k=k):
            _cmpsel_body(q_ref, kc_ref, vct_ref, oc_ref, sel_ref, pg_ref, step * (cb * QBLOCK), cb=cb,
                         rows=per * k, nomask=max(per * (k - 1) - 8, 0), nsel=nsel, topk=topk)


def _cmp_select(q, kc_all, vct_all, s, cb=2):
    n16 = s // CMP_STRIDE
    nsel = s // SEL_BLOCK
    topk = min(SEL_TOPK, nsel)
    nvar = max(n16 // 256, 1)
    return pl.pallas_call(
        functools.partial(_cmpsel_kernel, n16=n16, nsel=nsel, topk=topk, nvar=nvar, cb=cb),
        grid=(GROUPS, s // (QBLOCK * cb)),
        in_specs=[pl.BlockSpec((1, REP * HEAD_DIM, QBLOCK * cb), lambda g, c: (g, 0, c)),
                  pl.BlockSpec((1, 1, n16, HEAD_DIM), lambda g, c: (0, g, 0, 0)),
                  pl.BlockSpec((1, 1, HEAD_DIM, n16), lambda g, c: (1, g, 0, 0))],
        out_specs=[pl.BlockSpec((1, REP * HEAD_DIM, QBLOCK * cb), lambda g, c: (g, 0, c)),
                   pl.BlockSpec((1, nsel, QBLOCK * cb), lambda g, c: (g, 0, c))],
        out_shape=[jax.ShapeDtypeStruct((GROUPS, REP * HEAD_DIM, s), BF16),
                   jax.ShapeDtypeStruct((GROUPS, nsel, s), BF16)],
        scratch_shapes=[pltpu.VMEM((cb, n16 + 8, QBLOCK), F32)],
        compiler_params=_params("arbitrary", "arbitrary"),
        name="cmp_select",
    )(q, kc_all, vct_all)


def _fill_bias_strip(strip_ref, thr_ref, tab_ref, g, *, dist0, lo, hi, minus_far):
    nrows = strip_ref.shape[0]

    def block(bi, carry):
        i0 = pl.multiple_of(bi * QBLOCK, QBLOCK)
        d = (lax.broadcasted_iota(jnp.int32, (QBLOCK, QBLOCK), 1)
             - lax.broadcasted_iota(jnp.int32, (QBLOCK, QBLOCK), 0) + (dist0 - i0))
        ok = d >= lo if hi is None else (d >= lo) & (d < hi)
        for r in range(REP):
            base = (g * REP + r) * NUM_BUCKETS
            out = jnp.full((QBLOCK, QBLOCK), tab_ref[base], F32)
            for b in range(1, NUM_BUCKETS):
                out = jnp.where(d >= thr_ref[b], tab_ref[base + b], out)
            if minus_far:
                out = out - tab_ref[base + NUM_BUCKETS - 1]
            strip_ref[pl.ds(i0, QBLOCK), r * QBLOCK:(r + 1) * QBLOCK] = jnp.where(ok, out, -MASK_BIG)
        return carry

    lax.fori_loop(0, nrows // QBLOCK, block, 0)


def _sel_kernel(thr_ref, tab_ref, q_ref, sel_ref, kaug_ref, vt_ref, cst_ref, o_ref,
                qa_ref, s_ref, tmax_ref, m_ref, acc_ref, strip_ref, *, tk):
    g = pl.program_id(0)
    st = pl.program_id(1)
    nb = tk // SEL_BLOCK
    cpt = tk // QBLOCK
    kd = GROUPS * HEAD_DIM
    dmax = 2 * tk - QBLOCK

    @pl.when(st == 0)
    def _():
        _fill_bias_strip(strip_ref, thr_ref, tab_ref, g, dist0=dmax, lo=0, hi=None, minus_far=True)

    qa_ref[...] = jnp.zeros(qa_ref.shape, BF16)
    for b in range(2):
        qa_ref[b, kd + nb:kd + nb + 16, :] = cst_ref[0]
    m_ref[...] = jnp.full(m_ref.shape, -jnp.inf, F32)
    acc_ref[...] = jnp.zeros(acc_ref.shape, F32)
    row0 = pl.multiple_of(g * HEAD_DIM, HEAD_DIM)
    nnear = jnp.minimum(st + 1, 2)
    nfar = st + 1 - nnear

    def run(count, first_tile, near):

        def advance(c, i):
            wrap = i + 1 == count
            return jnp.where(wrap, c + 1, c), jnp.where(wrap, 0, i + 1)

        def qk(c, i, buf):
            cc = jnp.minimum(c, cpt - 1)
            j = first_tile + i
            lane0 = pl.multiple_of(cc * QBLOCK, QBLOCK)
            for r in range(REP):
                qa_ref[buf, pl.ds(row0, HEAD_DIM), r * QBLOCK:(r + 1) * QBLOCK] = (
                    q_ref[0, r * HEAD_DIM:(r + 1) * HEAD_DIM, pl.ds(lane0, QBLOCK)])
            sb = sel_ref[0, pl.ds(pl.multiple_of(j * nb, nb), nb), pl.ds(lane0, QBLOCK)]
            for r in range(REP):
                qa_ref[buf, kd:kd + nb, r * QBLOCK:(r + 1) * QBLOCK] = sb
            k0 = pl.multiple_of(j * tk, tk)
            s = jnp.dot(kaug_ref[pl.ds(k0, tk), :], qa_ref[buf], preferred_element_type=F32)
            if near:
                delta = (st - j) * tk + cc * QBLOCK
                s = s + strip_ref[pl.ds(pl.multiple_of(dmax - delta, QBLOCK), tk), :]
            s_ref[buf] = s
            tmax_ref[buf] = jnp.max(s, axis=0, keepdims=True)

        def softmax_pv(c, i, buf):
            k0 = pl.multiple_of((first_tile + i) * tk, tk)
            m_prev = m_ref[c]
            m_new = jnp.maximum(m_prev, tmax_ref[buf])
            alpha = jnp.exp2(m_prev - m_new)
            p = jnp.exp2(s_ref[buf] - m_new)
            acc = alpha * acc_ref[c] + jnp.dot(vt_ref[0, :, pl.ds(k0, tk)], p.astype(BF16),
                                               preferred_element_type=F32)
            acc_ref[c] = acc
            m_ref[c] = m_new
            if near:
                lane0 = pl.multiple_of(c * QBLOCK, QBLOCK)
                out = acc[0:HEAD_DIM] * (1.0 / jnp.maximum(acc[HEAD_DIM:HEAD_DIM + 1], 1e-30))
                for r in range(REP):
                    o_ref[0, r * HEAD_DIM:(r + 1) * HEAD_DIM, pl.ds(lane0, QBLOCK)] = (
                        out[:, r * QBLOCK:(r + 1) * QBLOCK].astype(o_ref.dtype))

        zero = jnp.int32(0)
        qk(zero, zero, 0)

        def pair(_, carry):
            ca, ia = carry
            cb, ib = advance(ca, ia)
            qk(cb, ib, 1)
            softmax_pv(ca, ia, 0)
            cn, inx = advance(cb, ib)
            qk(cn, inx, 0)
            softmax_pv(cb, ib, 1)
            return cn, inx

        lax.fori_loop(0, (cpt * count) // 2, pair, (zero, zero))

    @pl.when(nfar > 0)
    def _():
        run(nfar, 0, False)

    run(nnear, nfar, True)


def _selected(thr, tab, q, sel, kaug, vt, cst, s, tk):
    nsel = s // SEL_BLOCK
    sl = 3 * tk - QBLOCK
    cpt = tk // QBLOCK
    smem = pl.BlockSpec(memory_space=pltpu.SMEM)
    return pl.pallas_call(
        functools.partial(_sel_kernel, tk=tk),
        grid=(GROUPS, s // tk),
        in_specs=[smem, smem,
                  pl.BlockSpec((1, REP * HEAD_DIM, tk), lambda g, c: (g, 0, c)),
                  pl.BlockSpec((1, nsel, tk), lambda g, c: (g, 0, c)),
                  pl.BlockSpec(kaug.shape, lambda g, c: (0, 0)),
                  pl.BlockSpec((1, vt.shape[1], s), lambda g, c: (g, 0, 0)),
                  pl.BlockSpec((1, 16, HQ), lambda g, c: (g, 0, 0))],
        out_specs=pl.BlockSpec((1, REP * HEAD_DIM, tk), lambda g, c: (g, 0, c)),
        out_shape=jax.ShapeDtypeStruct((GROUPS, REP * HEAD_DIM, s), BF16),
        scratch_shapes=[pltpu.VMEM((2, kaug.shape[1], HQ), BF16), pltpu.VMEM((2, tk, HQ), F32),
                        pltpu.VMEM((2, 1, HQ), F32),
                        pltpu.VMEM((cpt, 1, HQ), F32), pltpu.VMEM((cpt, vt.shape[1], HQ), F32),
                        pltpu.VMEM((sl, HQ), F32)],
        compiler_params=_params("arbitrary", "arbitrary"),
        name="selected_attn",
    )(thr, tab, q, sel, kaug, vt, cst)


def _band_kernel(thr_ref, tab_ref, q_ref, k_ref, vt_ref, sink_ref, o_ref, qa_ref, strip_ref, *,
                 window, use_sink, cb):
    g = pl.program_id(0)

    @pl.when(pl.program_id(1) == 0)
    def _():
        _fill_bias_strip(strip_ref, thr_ref, tab_ref, g, dist0=window, lo=0, hi=window, minus_far=False)

    base = pl.multiple_of(pl.program_id(1) * (QBLOCK * cb), QBLOCK * cb)
    nk = window + QBLOCK
    qa_ref[...] = jnp.zeros(qa_ref.shape, BF16)
    row0 = pl.multiple_of(g * HEAD_DIM, HEAD_DIM)
    for c in range(cb):
        for r in range(REP):
            qa_ref[c, pl.ds(row0, HEAD_DIM), r * QBLOCK:(r + 1) * QBLOCK] = (
                q_ref[0, r * HEAD_DIM:(r + 1) * HEAD_DIM, c * QBLOCK:(c + 1) * QBLOCK])

    def chunks(first_step):
        for c in range(cb):
            t0 = base + c * QBLOCK
            s = jnp.dot(k_ref[pl.ds(t0, nk), :], qa_ref[c], preferred_element_type=F32) + strip_ref[...]
            if first_step and c * QBLOCK < window:
                row = lax.broadcasted_iota(jnp.int32, (nk, HQ), 0)
                s = jnp.where(row >= window - c * QBLOCK, s, -MASK_BIG)
            m = jnp.max(s, axis=0, keepdims=True)
            if use_sink:
                m = jnp.maximum(m, sink_ref[0])
            e = jnp.exp2(s - m)
            o_t = jnp.dot(vt_ref[0, :, pl.ds(t0, nk)], e.astype(BF16), preferred_element_type=F32)
            den = o_t[HEAD_DIM:HEAD_DIM + 1]
            if use_sink:
                den = den + jnp.exp2(sink_ref[0] - m)
            else:
                den = jnp.maximum(den, 1e-30)
            o_t = o_t[0:HEAD_DIM] * (1.0 / den)
            for r in range(REP):
                o_ref[0, r * HEAD_DIM:(r + 1) * HEAD_DIM, c * QBLOCK:(c + 1) * QBLOCK] = (
                    o_t[:, r * QBLOCK:(r + 1) * QBLOCK].astype(o_ref.dtype))

    @pl.when(pl.program_id(1) == 0)
    def _():
        chunks(True)

    @pl.when(pl.program_id(1) > 0)
    def _():
        chunks(False)


def _banded(thr, tab, q, kpad, vtpad, sink, s, window, use_sink, name, cb=4):
    assert QBLOCK * cb >= window
    nk = window + QBLOCK
    smem = pl.BlockSpec(memory_space=pltpu.SMEM)
    return pl.pallas_call(
        functools.partial(_band_kernel, window=window, use_sink=use_sink, cb=cb),
        grid=(GROUPS, s // (QBLOCK * cb)),
        in_specs=[smem, smem,
                  pl.BlockSpec((1, REP * HEAD_DIM, QBLOCK * cb), lambda g, c: (g, 0, c)),
                  pl.BlockSpec(kpad.shape, lambda g, c: (0, 0)),
                  pl.BlockSpec((1, vtpad.shape[1], s + window), lambda g, c: (g, 0, 0)),
                  pl.BlockSpec((1, 1, HQ), lambda g, c: (g, 0, 0))],
        out_specs=pl.BlockSpec((1, REP * HEAD_DIM, QBLOCK * cb), lambda g, c: (g, 0, c)),
        out_shape=jax.ShapeDtypeStruct((GROUPS, REP * HEAD_DIM, s), BF16),
        scratch_shapes=[pltpu.VMEM((cb, GROUPS * HEAD_DIM, HQ), BF16), pltpu.VMEM((nk, HQ), F32)],
        compiler_params=_params("arbitrary", "arbitrary"),
        name=name,
    )(thr, tab, q, kpad, vtpad, sink)


def _merge_kernel(oc_ref, os_ref, ow_ref, ob_ref, gn_ref, gab_ref, x_ref, wa_ref, wb_ref, wo_ref, gp_ref, o_ref):
    d = x_ref.shape[1]
    gn = jax.nn.sigmoid(gn_ref[...])
    parts = []
    for h in range(HEADS):
        rows = slice(h * HEAD_DIM, (h + 1) * HEAD_DIM)
        parts.append(gn[h:h + 1] * oc_ref[rows, :].astype(F32)
                     + gn[HEADS + h:HEADS + h + 1] * os_ref[rows, :].astype(F32)
                     + gn[2 * HEADS + h:2 * HEADS + h + 1] * ow_ref[rows, :].astype(F32))
    oa_t = jnp.concatenate(parts, axis=0).astype(BF16)
    ya = lax.dot_general(oa_t, wa_ref[...], _TN, preferred_element_type=F32)
    yb = lax.dot_general(ob_ref[...], wb_ref[...], _TN, preferred_element_type=F32)
    y = (jax.nn.sigmoid(gab_ref[:, :d].astype(F32)) * ya
         + jax.nn.sigmoid(gab_ref[:, d:].astype(F32)) * yb)
    z = jnp.dot(y.astype(BF16), wo_ref[...], preferred_element_type=F32)
    o_ref[...] = x_ref[...] + _rms(z, gp_ref[...])


def _merge(oc, os_, ow, ob, gn, gab, x, wa, wb, wo, gpost, ts):
    s, d = x.shape
    hd = oc.shape[0]
    tok = lambda rows: pl.BlockSpec((rows, ts), lambda i: (0, i))
    full = lambda shp: pl.BlockSpec(shp, lambda i: (0, 0))
    return pl.pallas_call(
        _merge_kernel,
        grid=(s // ts,),
        in_specs=[tok(hd), tok(hd), tok(hd), tok(hd), tok(gn.shape[0]),
                  pl.BlockSpec((ts, 2 * d), lambda i: (i, 0)), pl.BlockSpec((ts, d), lambda i: (i, 0)),
                  full((hd, d)), full((hd, d)), full((d, d)), full((1, d))],
        out_specs=pl.BlockSpec((ts, d), lambda i: (i, 0)),
        out_shape=jax.ShapeDtypeStruct((s, d), F32),
        compiler_params=_params("arbitrary"),
        name="merge_out_proj",
    )(oc, os_, ow, ob, gn, gab, x, wa, wb, wo, gpost)


def _mlp_kernel(h_ref, gpre_ref, w1_ref, w2_ref, gpost_ref, o_ref, u_ref, acc_ref):
    f = pl.program_id(1)

    @pl.when(f == 0)
    def _():
        u_ref[...] = _rms(h_ref[...], gpre_ref[...]).astype(BF16)
        acc_ref[...] = jnp.zeros(acc_ref.shape, F32)

    a = jnp.maximum(jnp.dot(u_ref[...], w1_ref[...], preferred_element_type=F32), 0.0)
    acc_ref[...] += jnp.dot((a * a).astype(BF16), w2_ref[...], preferred_element_type=F32)

    @pl.when(f == pl.num_programs(1) - 1)
    def _():
        o_ref[...] = h_ref[...] + _rms(acc_ref[...], gpost_ref[...])


def _mlp(h, gpre, w1, w2, gpost, ts, tf):
    s, d = h.shape
    dff = w1.shape[1]
    return pl.pallas_call(
        _mlp_kernel,
        grid=(s // ts, dff // tf),
        in_specs=[pl.BlockSpec((ts, d), lambda i, f: (i, 0)), pl.BlockSpec((1, d), lambda i, f: (0, 0)),
                  pl.BlockSpec((d, tf), lambda i, f: (0, f)), pl.BlockSpec((tf, d), lambda i, f: (f, 0)),
                  pl.BlockSpec((1, d), lambda i, f: (0, 0))],
        out_specs=pl.BlockSpec((ts, d), lambda i, f: (i, 0)),
        out_shape=jax.ShapeDtypeStruct((s, d), F32),
        scratch_shapes=[pltpu.VMEM((ts, d), BF16), pltpu.VMEM((ts, d), F32)],
        compiler_params=_params("arbitrary", "arbitrary"),
        name="mlp_relu2",
    )(h, gpre, w1, w2, gpost)


def kernel(x, norm_mix_pre, norm_mix_post, norm_mlp_pre, norm_mlp_post, w_in,
           cmp_pos_k, cmp_w1_k, cmp_w2_k, cmp_pos_v, cmp_w1_v, cmp_w2_v,
           attn_sinks, rel_bias, w_up_nsa, w_up_swa, w_out, w_mlp_in, w_mlp_out):
    b, s, d = x.shape
    assert b == 1 and s % SEL_TILE == 0 and w_in.shape[0] == 1
    qd = HEADS * HEAD_DIM
    kvd = GROUPS * HEAD_DIM
    sizes = (qd,) + (kvd,) * 6 + (3 * HEADS, qd, kvd, kvd, d, d)
    offs = [0]
    for z in sizes:
        offs.append(offs[-1] + z)
    w = w_in[0]
    col = lambda i: w[:, offs[i]:offs[i + 1]]
    (w_qn, w_kc, w_vc, w_ks, w_vs, w_kw, w_vw, w_gn, w_qs, w_k_s, w_v_s, w_ga, w_gb) = [col(i) for i in range(13)]
    scale = HEAD_DIM ** -0.5 * LOG2E
    w_gn = w_gn.reshape(d, HEADS, 3).transpose(0, 2, 1).reshape(d, 3 * HEADS)
    w_gn = jnp.pad(w_gn, ((0, 0), (0, 8)))
    wn = jnp.concatenate([w_kc, w_vc, w_ks, w_kw, w_k_s], axis=1).astype(BF16)
    wg = jnp.concatenate([w_ga, w_gb], axis=1).astype(BF16)
    wt = jnp.concatenate([w_qn * scale, w_qs * scale, w_vs, w_vw, w_v_s, w_gn], axis=1).T.astype(BF16)
    nqv = 2 * qd + 3 * kvd

    x2 = x[0]
    kv, gab, qv, gn = _project(x2, norm_mix_pre, wn, wg, wt, nqv, ts=512)
    q_nsa = qv[0:qd].reshape(GROUPS, REP * HEAD_DIM, s)
    q_swa = qv[qd:2 * qd].reshape(GROUPS, REP * HEAD_DIM, s)
    vs_t = qv[2 * qd:2 * qd + kvd].reshape(GROUPS, HEAD_DIM, s)
    vw_t = qv[2 * qd + kvd:2 * qd + 2 * kvd].reshape(GROUPS, HEAD_DIM, s)
    vswa_t = qv[2 * qd + 2 * kvd:].reshape(GROUPS, HEAD_DIM, s)

    n16 = s // CMP_STRIDE
    kb = kv[:, :2 * kvd].reshape(n16, CMP_STRIDE, 2, GROUPS, HEAD_DIM).transpose(2, 3, 0, 1, 4)
    kb = kb.reshape(2, GROUPS, n16, CMP_STRIDE * HEAD_DIM)
    pos8 = jnp.stack([cmp_pos_k[0], cmp_pos_v[0]]).reshape(2, 1, CMP_BLOCK * HEAD_DIM)
    pos8 = jnp.broadcast_to(pos8, (2, 8, CMP_BLOCK * HEAD_DIM)).astype(BF16)
    w1 = jnp.stack([cmp_w1_k[0], cmp_w1_v[0]]).astype(BF16)
    w2 = jnp.stack([cmp_w2_k[0], cmp_w2_v[0]]).astype(BF16)
    cmp_n, cmp_t = _compress(kb, pos8, w1, w2, w2.transpose(0, 2, 1))

    oc, sel = _cmp_select(q_nsa, cmp_n, cmp_t, s)

    bias_nsa = rel_bias[:, :HEADS].reshape(NUM_BUCKETS, GROUPS, REP) * LOG2E
    bias_swa = rel_bias[:, HEADS:].reshape(NUM_BUCKETS, GROUPS, REP) * LOG2E
    thr = jnp.asarray(_bucket_starts())
    tab_nsa = bias_nsa.transpose(1, 2, 0).reshape(-1)
    tab_swa = bias_swa.transpose(1, 2, 0).reshape(-1)

    tk = SEL_TILE
    nb = tk // SEL_BLOCK
    posn = jnp.arange(s)
    onehot = ((posn // SEL_BLOCK) % nb)[:, None] == jnp.arange(nb)[None, :]
    aug = jnp.concatenate([onehot.astype(BF16), jnp.ones((s, 2), BF16),
                           jnp.zeros((s, kvd - nb - 2), BF16)], axis=1)
    kaug = jnp.concatenate([kv[:, 2 * kvd:3 * kvd], aug], axis=1)
    far = bias_nsa[NUM_BUCKETS - 1]
    far_hi = far.astype(BF16)
    far_lo = (far - far_hi.astype(F32)).astype(BF16)
    cst = jnp.stack([far_hi, far_lo], axis=1)
    cst = jnp.broadcast_to(cst[:, :, :, None], (GROUPS, 2, REP, QBLOCK)).reshape(GROUPS, 2, HQ)
    cst = jnp.pad(cst, ((0, 0), (0, 14), (0, 0)))
    ones_row = jnp.concatenate([jnp.ones((GROUPS, 1, s), BF16), jnp.zeros((GROUPS, 15, s), BF16)], axis=1)
    vs_aug = jnp.concatenate([vs_t, ones_row], axis=1)
    o_sel = _selected(thr, tab_nsa, q_nsa, sel, kaug, vs_aug, cst, s, tk)

    pad_rows = lambda a, wdw: jnp.pad(a, ((wdw, 0), (0, 0)))
    pad_lanes = lambda a, wdw: jnp.pad(jnp.concatenate([a, ones_row], axis=1), ((0, 0), (0, 0), (wdw, 0)))
    no_sink = jnp.zeros((GROUPS, 1, HQ), F32)
    o_win = _banded(thr, tab_nsa, q_nsa, pad_rows(kv[:, 3 * kvd:4 * kvd], NSA_WINDOW), pad_lanes(vw_t, NSA_WINDOW),
                    no_sink, s, NSA_WINDOW, False, "nsa_window")
    sink = attn_sinks[0].reshape(GROUPS, 1, REP, 1).astype(F32) * LOG2E
    sink = jnp.broadcast_to(sink, (GROUPS, 1, REP, QBLOCK)).reshape(GROUPS, 1, HQ)
    o_swa = _banded(thr, tab_swa, q_swa, pad_rows(kv[:, 4 * kvd:5 * kvd], SWA_WINDOW), pad_lanes(vswa_t, SWA_WINDOW),
                    sink, s, SWA_WINDOW, True, "swa_sink")

    flat = lambda a: a.reshape(qd, s)
    h1 = _merge(flat(oc), flat(o_sel), flat(o_win), flat(o_swa), gn, gab, x2,
                w_up_nsa[0].astype(BF16), w_up_swa[0].astype(BF16), w_out[0].astype(BF16),
                norm_mix_post, ts=512)
    out = _mlp(h1, norm_mlp_pre, w_mlp_in[0].astype(BF16), w_mlp_out[0].astype(BF16), norm_mlp_post,
               ts=1024, tf=1024)
    return out[None]
```

```python
import functools
import math

import jax
import jax.numpy as jnp
import numpy as np
from jax import lax
from jax.experimental import pallas as pl
from jax.experimental.pallas import tpu as pltpu

F32 = jnp.float32
BF16 = jnp.bfloat16

HEAD_DIM = 64
GROUPS = 2
REP = 4
HEADS = GROUPS * REP
CMP_BLOCK = 32
CMP_STRIDE = 16
SEL_BLOCK = 64
SEL_TOPK = 16
NSA_WINDOW = 512
SWA_WINDOW = 128
QBLOCK = 128
NUM_BUCKETS = 32
MAX_DISTANCE = 1024
RMS_EPS = 1e-6
LOG2E = math.log2(math.e)
MASK_BIG = 2.0 ** 99
SEL_TILE = 1024
TILES_PER_TRIP = 4
HQ = REP * QBLOCK
VMEM_LIMIT = 56 * 1024 * 1024

_NT = (((1,), (1,)), ((), ()))
_TN = (((0,), (0,)), ((), ()))


def _params(*sem):
    return pltpu.CompilerParams(dimension_semantics=sem, vmem_limit_bytes=VMEM_LIMIT)


def _bucket_starts():
    max_exact = NUM_BUCKETS // 2
    d = np.arange(2 * MAX_DISTANCE)
    large = max_exact + (np.log(np.maximum(d, 1).astype(np.float64) / max_exact)
                         / math.log(MAX_DISTANCE / max_exact) * (NUM_BUCKETS - max_exact)).astype(np.int64)
    bucket = np.where(d < max_exact, d, np.minimum(large, NUM_BUCKETS - 1))
    return (bucket[None, :] < np.arange(NUM_BUCKETS)[:, None]).sum(axis=1).astype(np.int32)


def _rms(x, gain):
    return x * lax.rsqrt(jnp.mean(x * x, axis=-1, keepdims=True) + RMS_EPS) * gain


def _proj_kernel(x_ref, g_ref, wn_ref, wg_ref, wt_ref, kv_ref, gab_ref, qv_ref, gn_ref, *, nqv):
    u = _rms(x_ref[...], g_ref[...]).astype(BF16)
    kv_ref[...] = jnp.dot(u, wn_ref[...], preferred_element_type=F32).astype(BF16)
    gab_ref[...] = jnp.dot(u, wg_ref[...], preferred_element_type=F32).astype(BF16)
    t = lax.dot_general(wt_ref[...], u, _NT, preferred_element_type=F32)
    qv_ref[...] = t[:nqv].astype(BF16)
    gn_ref[...] = t[nqv:]


def _project(x, gain, wn, wg, wt, nqv, ts):
    s, d = x.shape
    nn, ng, ntr = wn.shape[1], wg.shape[1], wt.shape[0]
    full = lambda shp: pl.BlockSpec(shp, lambda i: (0, 0))
    return pl.pallas_call(
        functools.partial(_proj_kernel, nqv=nqv),
        grid=(s // ts,),
        in_specs=[pl.BlockSpec((ts, d), lambda i: (i, 0)), full((1, d)),
                  full((d, nn)), full((d, ng)), full((ntr, d))],
        out_specs=[pl.BlockSpec((ts, nn), lambda i: (i, 0)),
                   pl.BlockSpec((ts, ng), lambda i: (i, 0)),
                   pl.BlockSpec((nqv, ts), lambda i: (0, i)),
                   pl.BlockSpec((ntr - nqv, ts), lambda i: (0, i))],
        out_shape=[jax.ShapeDtypeStruct((s, nn), BF16), jax.ShapeDtypeStruct((s, ng), BF16),
                   jax.ShapeDtypeStruct((nqv, s), BF16), jax.ShapeDtypeStruct((ntr - nqv, s), F32)],
        compiler_params=_params("arbitrary"),
        name="in_proj",
    )(x, gain, wn, wg, wt)


def _gelu_tanh(x):
    return x * (0.5 * (1.0 + jnp.tanh(math.sqrt(2.0 / math.pi) * (x + 0.044715 * (x * x * x)))))


def _compress_kernel(kb_ref, pos_ref, w1_ref, w2_ref, w2t_ref, o_ref, ot_ref):
    kb = kb_ref[0, 0]
    half = kb.shape[1]
    n16 = kb.shape[0]
    first = jnp.dot(kb, w1_ref[0, :half, :], preferred_element_type=F32)
    second = jnp.dot(kb, w1_ref[0, half:, :], preferred_element_type=F32)
    posb = jnp.dot(pos_ref[0], w1_ref[0], preferred_element_type=F32)[0:1]
    pre = first + pltpu.roll(second, n16 - 1, 0) + posb
    h = _gelu_tanh(pre).astype(BF16)
    o_ref[0, 0] = jnp.dot(h, w2_ref[0], preferred_element_type=F32).astype(BF16)
    ot_ref[0, 0] = lax.dot_general(w2t_ref[0], h, _NT, preferred_element_type=F32).astype(BF16)


def _compress(kb, pos8, w1, w2, w2t):
    _, g, n16, cin2 = kb.shape
    hid = w1.shape[2]
    return pl.pallas_call(
        _compress_kernel,
        grid=(2, g),
        in_specs=[pl.BlockSpec((1, 1, n16, cin2), lambda a, b: (a, b, 0, 0)),
                  pl.BlockSpec((1, 8, 2 * cin2), lambda a, b: (a, 0, 0)),
                  pl.BlockSpec((1, 2 * cin2, hid), lambda a, b: (a, 0, 0)),
                  pl.BlockSpec((1, hid, HEAD_DIM), lambda a, b: (a, 0, 0)),
                  pl.BlockSpec((1, HEAD_DIM, hid), lambda a, b: (a, 0, 0))],
        out_specs=[pl.BlockSpec((1, 1, n16, HEAD_DIM), lambda a, b: (a, b, 0, 0)),
                   pl.BlockSpec((1, 1, HEAD_DIM, n16), lambda a, b: (a, b, 0, 0))],
        out_shape=[jax.ShapeDtypeStruct((2, g, n16, HEAD_DIM), BF16),
                   jax.ShapeDtypeStruct((2, g, HEAD_DIM, n16), BF16)],
        compiler_params=_params("arbitrary", "arbitrary"),
        name="kv_compress",
    )(kb, pos8, w1, w2, w2t)


def _cmpsel_body(q_ref, kc_ref, vct_ref, oc_ref, sel_ref, pg_ref, t_base, *, cb, rows, nomask, nsel, topk):
    mrows = rows - nomask
    ratio = SEL_BLOCK // CMP_STRIDE
    nblk = rows // ratio
    n_io = nomask + lax.broadcasted_iota(jnp.int32, (mrows, HQ), 0)
    lane_t = lax.broadcasted_iota(jnp.int32, (mrows, HQ), 1) & (QBLOCK - 1)
    blk = lax.broadcasted_iota(jnp.int32, (nblk, QBLOCK), 0)
    scores, valids = [], []
    for c in range(cb):
        t0 = t_base + c * QBLOCK
        lanes = slice(c * QBLOCK, (c + 1) * QBLOCK)
        q4 = jnp.concatenate([q_ref[0, r * HEAD_DIM:(r + 1) * HEAD_DIM, lanes] for r in range(REP)], axis=1)
        s = jnp.dot(kc_ref[0, 0, 0:rows, :], q4, preferred_element_type=F32)
        edge = jnp.where(n_io * CMP_STRIDE + (CMP_BLOCK - 1) <= t0 + lane_t, s[nomask:], -jnp.inf)
        s = jnp.concatenate([s[:nomask], edge], axis=0) if nomask else edge
        m = jnp.max(s, axis=0, keepdims=True)
        m = jnp.where(m == -jnp.inf, 0.0, m)
        e = jnp.exp2(s - m)
        den = jnp.maximum(jnp.sum(e, axis=0, keepdims=True), 1e-30)
        p = e * (1.0 / den)
        o_t = jnp.dot(vct_ref[0, 0, :, 0:rows], p.astype(BF16), preferred_element_type=F32)
        for r in range(REP):
            oc_ref[0, r * HEAD_DIM:(r + 1) * HEAD_DIM, lanes] = (
                o_t[:, r * QBLOCK:(r + 1) * QBLOCK].astype(oc_ref.dtype))

        pg = ((p[:, 0:QBLOCK] + p[:, QBLOCK:2 * QBLOCK]) + p[:, 2 * QBLOCK:3 * QBLOCK]) + p[:, 3 * QBLOCK:]
        pg_ref[c, 0:8, :] = jnp.zeros((8, QBLOCK), F32)
        pg_ref[c, 8:8 + rows, :] = pg
        ps = pg_ref[c, pl.ds(7, nblk, stride=ratio), :]
        for o in range(1, ratio + 1):
            ps = ps + pg_ref[c, pl.ds(7 + o, nblk, stride=ratio), :]
        cur = (t0 + lax.broadcasted_iota(jnp.int32, (nblk, QBLOCK), 1)) // SEL_BLOCK
        valid = blk <= cur
        forced = (blk == 0) | (blk == cur) | (blk == cur - 1)
        scores.append(jnp.where(valid, jnp.where(forced, jnp.inf, ps), -jnp.inf))
        valids.append(valid)

    def pick_one(_, rems):
        out = []
        for rem in rems:
            top = jnp.max(rem, axis=0, keepdims=True)
            first = jnp.min(jnp.where(rem == top, blk, nblk), axis=0, keepdims=True)
            out.append(jnp.where(blk == first, -jnp.inf, rem))
        return tuple(out)

    rems = lax.fori_loop(0, topk, pick_one, tuple(scores))
    for c in range(cb):
        lanes = slice(c * QBLOCK, (c + 1) * QBLOCK)
        sel_ref[0, 0:nblk, lanes] = jnp.where((rems[c] == -jnp.inf) & valids[c], 0.0, -MASK_BIG).astype(BF16)
        if nblk < nsel:
            sel_ref[0, nblk:, lanes] = jnp.full((nsel - nblk, QBLOCK), -MASK_BIG, BF16)


def _cmpsel_kernel(q_ref, kc_ref, vct_ref, oc_ref, sel_ref, pg_ref, *, n16, nsel, topk, nvar, cb):
    step = pl.program_id(1)
    per = n16 // nvar
    spv = per // (QBLOCK // CMP_STRIDE) // cb
    for k in range(1, nvar + 1):
        @pl.when((step >= (k - 1) * spv) & (step < k * spv))
        def _(k=k):
            _cmpsel_body(q_ref, kc_ref, vct_ref, oc_ref, sel_ref, pg_ref, step * (cb * QBLOCK), cb=cb,
                         rows=per * k, nomask=max(per * (k - 1) - 8, 0), nsel=nsel, topk=topk)


def _cmp_select(q, kc_all, vct_all, s, cb=2):
    n16 = s // CMP_STRIDE
    nsel = s // SEL_BLOCK
    topk = min(SEL_TOPK, nsel)
    nvar = max(n16 // 256, 1)
    return pl.pallas_call(
        functools.partial(_cmpsel_kernel, n16=n16, nsel=nsel, topk=topk, nvar=nvar, cb=cb),
        grid=(GROUPS, s // (QBLOCK * cb)),
        in_specs=[pl.BlockSpec((1, REP * HEAD_DIM, QBLOCK * cb), lambda g, c: (g, 0, c)),
                  pl.BlockSpec((1, 1, n16, HEAD_DIM), lambda g, c: (0, g, 0, 0)),
                  pl.BlockSpec((1, 1, HEAD_DIM, n16), lambda g, c: (1, g, 0, 0))],
        out_specs=[pl.BlockSpec((1, REP * HEAD_DIM, QBLOCK * cb), lambda g, c: (g, 0, c)),
                   pl.BlockSpec((1, nsel, QBLOCK * cb), lambda g, c: (g, 0, c))],
        out_shape=[jax.ShapeDtypeStruct((GROUPS, REP * HEAD_DIM, s), BF16),
                   jax.ShapeDtypeStruct((GROUPS, nsel, s), BF16)],
        scratch_shapes=[pltpu.VMEM((cb, n16 + 8, QBLOCK), F32)],
        compiler_params=_params("arbitrary", "arbitrary"),
        name="cmp_select",
    )(q, kc_all, vct_all)


def _fill_bias_strip(strip_ref, thr_ref, tab_ref, g, *, dist0, lo, hi, minus_far):
    nrows = strip_ref.shape[0]

    def block(bi, carry):
        i0 = pl.multiple_of(bi * QBLOCK, QBLOCK)
        d = (lax.broadcasted_iota(jnp.int32, (QBLOCK, QBLOCK), 1)
             - lax.broadcasted_iota(jnp.int32, (QBLOCK, QBLOCK), 0) + (dist0 - i0))
        ok = d >= lo if hi is None else (d >= lo) & (d < hi)
        for r in range(REP):
            base = (g * REP + r) * NUM_BUCKETS
            out = jnp.full((QBLOCK, QBLOCK), tab_ref[base], F32)
            for b in range(1, NUM_BUCKETS):
                out = jnp.where(d >= thr_ref[b], tab_ref[base + b], out)
            if minus_far:
                out = out - tab_ref[base + NUM_BUCKETS - 1]
            strip_ref[pl.ds(i0, QBLOCK), r * QBLOCK:(r + 1) * QBLOCK] = jnp.where(ok, out, -MASK_BIG)
        return carry

    lax.fori_loop(0, nrows // QBLOCK, block, 0)


def _sel_kernel(thr_ref, tab_ref, q_ref, sel_ref, kaug_ref, vt_ref, cst_ref, o_ref,
                qa_ref, s_ref, tmax_ref, m_ref, acc_ref, strip_ref, *, tk):
    g = pl.program_id(0)
    st = pl.program_id(1)
    nb = tk // SEL_BLOCK
    cpt = tk // QBLOCK
    kd = GROUPS * HEAD_DIM
    dmax = 2 * tk - QBLOCK

    @pl.when(st == 0)
    def _():
        _fill_bias_strip(strip_ref, thr_ref, tab_ref, g, dist0=dmax, lo=0, hi=None, minus_far=True)

    qa_ref[...] = jnp.zeros(qa_ref.shape, BF16)
    for b in range(2):
        qa_ref[b, kd + nb:kd + nb + 16, :] = cst_ref[0]
    m_ref[...] = jnp.full(m_ref.shape, -jnp.inf, F32)
    acc_ref[...] = jnp.zeros(acc_ref.shape, F32)
    row0 = pl.multiple_of(g * HEAD_DIM, HEAD_DIM)
    nnear = jnp.minimum(st + 1, 2)
    nfar = st + 1 - nnear

    def run(count, first_tile, near):

        def advance(c, i):
            wrap = i + 1 == count
            return jnp.where(wrap, c + 1, c), jnp.where(wrap, 0, i + 1)

        def qk(c, i, buf):
            cc = jnp.minimum(c, cpt - 1)
            j = first_tile + i
            lane0 = pl.multiple_of(cc * QBLOCK, QBLOCK)
            for r in range(REP):
                qa_ref[buf, pl.ds(row0, HEAD_DIM), r * QBLOCK:(r + 1) * QBLOCK] = (
                    q_ref[0, r * HEAD_DIM:(r + 1) * HEAD_DIM, pl.ds(lane0, QBLOCK)])
            sb = sel_ref[0, pl.ds(pl.multiple_of(j * nb, nb), nb), pl.ds(lane0, QBLOCK)]
            for r in range(REP):
                qa_ref[buf, kd:kd + nb, r * QBLOCK:(r + 1) * QBLOCK] = sb
            k0 = pl.multiple_of(j * tk, tk)
            s = jnp.dot(kaug_ref[pl.ds(k0, tk), :], qa_ref[buf], preferred_element_type=F32)
            if near:
                delta = (st - j) * tk + cc * QBLOCK
                s = s + strip_ref[pl.ds(pl.multiple_of(dmax - delta, QBLOCK), tk), :]
            s_ref[buf] = s
            tmax_ref[buf] = jnp.max(s, axis=0, keepdims=True)

        def softmax_pv(c, i, buf):
            k0 = pl.multiple_of((first_tile + i) * tk, tk)
            m_prev = m_ref[c]
            m_new = jnp.maximum(m_prev, tmax_ref[buf])
            alpha = jnp.exp2(m_prev - m_new)
            p = jnp.exp2(s_ref[buf] - m_new)
            acc = alpha * acc_ref[c] + jnp.dot(vt_ref[0, :, pl.ds(k0, tk)], p.astype(BF16),
                                               preferred_element_type=F32)
            acc_ref[c] = acc
            m_ref[c] = m_new
            if near:
                lane0 = pl.multiple_of(c * QBLOCK, QBLOCK)
                out = acc[0:HEAD_DIM] * (1.0 / jnp.maximum(acc[HEAD_DIM:HEAD_DIM + 1], 1e-30))
                for r in range(REP):
                    o_ref[0, r * HEAD_DIM:(r + 1) * HEAD_DIM, pl.ds(lane0, QBLOCK)] = (
                        out[:, r * QBLOCK:(r + 1) * QBLOCK].astype(o_ref.dtype))

        zero = jnp.int32(0)
        qk(zero, zero, 0)

        def trip(_, carry):
            cur = carry
            for k in range(TILES_PER_TRIP):
                nxt = advance(*cur)
                qk(*nxt, (k + 1) % 2)
                softmax_pv(*cur, k % 2)
                cur = nxt
            return cur

        lax.fori_loop(0, (cpt * count) // TILES_PER_TRIP, trip, (zero, zero))

    @pl.when(nfar > 0)
    def _():
        run(nfar, 0, False)

    run(nnear, nfar, True)


def _selected(thr, tab, q, sel, kaug, vt, cst, s, tk):
    nsel = s // SEL_BLOCK
    sl = 3 * tk - QBLOCK
    cpt = tk // QBLOCK
    smem = pl.BlockSpec(memory_space=pltpu.SMEM)
    return pl.pallas_call(
        functools.partial(_sel_kernel, tk=tk),
        grid=(GROUPS, s // tk),
        in_specs=[smem, smem,
                  pl.BlockSpec((1, REP * HEAD_DIM, tk), lambda g, c: (g, 0, c)),
                  pl.BlockSpec((1, nsel, tk), lambda g, c: (g, 0, c)),
                  pl.BlockSpec(kaug.shape, lambda g, c: (0, 0)),
                  pl.BlockSpec((1, vt.shape[1], s), lambda g, c: (g, 0, 0)),
                  pl.BlockSpec((1, 16, HQ), lambda g, c: (g, 0, 0))],
        out_specs=pl.BlockSpec((1, REP * HEAD_DIM, tk), lambda g, c: (g, 0, c)),
        out_shape=jax.ShapeDtypeStruct((GROUPS, REP * HEAD_DIM, s), BF16),
        scratch_shapes=[pltpu.VMEM((2, kaug.shape[1], HQ), BF16), pltpu.VMEM((2, tk, HQ), F32),
                        pltpu.VMEM((2, 1, HQ), F32),
                        pltpu.VMEM((cpt, 1, HQ), F32), pltpu.VMEM((cpt, vt.shape[1], HQ), F32),
                        pltpu.VMEM((sl, HQ), F32)],
        compiler_params=_params("arbitrary", "arbitrary"),
        name="selected_attn",
    )(thr, tab, q, sel, kaug, vt, cst)


def _band_kernel(thr_ref, tab_ref, q_ref, k_ref, vt_ref, sink_ref, o_ref, qa_ref, strip_ref, s_ref, *,
                 window, use_sink, cb):
    g = pl.program_id(0)

    @pl.when(pl.program_id(1) == 0)
    def _():
        _fill_bias_strip(strip_ref, thr_ref, tab_ref, g, dist0=window, lo=0, hi=window, minus_far=False)

    base = pl.multiple_of(pl.program_id(1) * (QBLOCK * cb), QBLOCK * cb)
    nk = window + QBLOCK
    qa_ref[...] = jnp.zeros(qa_ref.shape, BF16)
    row0 = pl.multiple_of(g * HEAD_DIM, HEAD_DIM)
    for c in range(cb):
        for r in range(REP):
            qa_ref[c, pl.ds(row0, HEAD_DIM), r * QBLOCK:(r + 1) * QBLOCK] = (
                q_ref[0, r * HEAD_DIM:(r + 1) * HEAD_DIM, c * QBLOCK:(c + 1) * QBLOCK])

    def scores(c, first_step):
        t0 = base + c * QBLOCK
        s = jnp.dot(k_ref[pl.ds(t0, nk), :], qa_ref[c], preferred_element_type=F32) + strip_ref[...]
        if first_step and c * QBLOCK < window:
            row = lax.broadcasted_iota(jnp.int32, (nk, HQ), 0)
            s = jnp.where(row >= window - c * QBLOCK, s, -MASK_BIG)
        s_ref[c] = s
        m = jnp.max(s, axis=0, keepdims=True)
        return jnp.maximum(m, sink_ref[0]) if use_sink else m

    def attend(c, m):
        t0 = base + c * QBLOCK
        e = jnp.exp2(s_ref[c] - m)
        o_t = jnp.dot(vt_ref[0, :, pl.ds(t0, nk)], e.astype(BF16), preferred_element_type=F32)
        den = o_t[HEAD_DIM:HEAD_DIM + 1]
        den = den + jnp.exp2(sink_ref[0] - m) if use_sink else jnp.maximum(den, 1e-30)
        o_t = o_t[0:HEAD_DIM] * (1.0 / den)
        for r in range(REP):
            o_ref[0, r * HEAD_DIM:(r + 1) * HEAD_DIM, c * QBLOCK:(c + 1) * QBLOCK] = (
                o_t[:, r * QBLOCK:(r + 1) * QBLOCK].astype(o_ref.dtype))

    def chunks(first_step):
        m = scores(0, first_step)
        for c in range(cb):
            m_next = scores(c + 1, first_step) if c + 1 < cb else None
            attend(c, m)
            m = m_next

    @pl.when(pl.program_id(1) == 0)
    def _():
        chunks(True)

    @pl.when(pl.program_id(1) > 0)
    def _():
        chunks(False)


def _banded(thr, tab, q, kpad, vtpad, sink, s, window, use_sink, name, cb=4):
    assert QBLOCK * cb >= window
    nk = window + QBLOCK
    smem = pl.BlockSpec(memory_space=pltpu.SMEM)
    return pl.pallas_call(
        functools.partial(_band_kernel, window=window, use_sink=use_sink, cb=cb),
        grid=(GROUPS, s // (QBLOCK * cb)),
        in_specs=[smem, smem,
                  pl.BlockSpec((1, REP * HEAD_DIM, QBLOCK * cb), lambda g, c: (g, 0, c)),
                  pl.BlockSpec(kpad.shape, lambda g, c: (0, 0)),
                  pl.BlockSpec((1, vtpad.shape[1], s + window), lambda g, c: (g, 0, 0)),
                  pl.BlockSpec((1, 1, HQ), lambda g, c: (g, 0, 0))],
        out_specs=pl.BlockSpec((1, REP * HEAD_DIM, QBLOCK * cb), lambda g, c: (g, 0, c)),
        out_shape=jax.ShapeDtypeStruct((GROUPS, REP * HEAD_DIM, s), BF16),
        scratch_shapes=[pltpu.VMEM((cb, GROUPS * HEAD_DIM, HQ), BF16), pltpu.VMEM((nk, HQ), F32),
                        pltpu.VMEM((cb, nk, HQ), F32)],
        compiler_params=_params("arbitrary", "arbitrary"),
        name=name,
    )(thr, tab, q, kpad, vtpad, sink)


def _merge_kernel(oc_ref, os_ref, ow_ref, ob_ref, gn_ref, gab_ref, x_ref, wa_ref, wb_ref, wo_ref, gp_ref, o_ref):
    d = x_ref.shape[1]
    gn = jax.nn.sigmoid(gn_ref[...])
    parts = []
    for h in range(HEADS):
        rows = slice(h * HEAD_DIM, (h + 1) * HEAD_DIM)
        parts.append(gn[h:h + 1] * oc_ref[rows, :].astype(F32)
                     + gn[HEADS + h:HEADS + h + 1] * os_ref[rows, :].astype(F32)
                     + gn[2 * HEADS + h:2 * HEADS + h + 1] * ow_ref[rows, :].astype(F32))
    oa_t = jnp.concatenate(parts, axis=0).astype(BF16)
    ya = lax.dot_general(oa_t, wa_ref[...], _TN, preferred_element_type=F32)
    yb = lax.dot_general(ob_ref[...], wb_ref[...], _TN, preferred_element_type=F32)
    y = (jax.nn.sigmoid(gab_ref[:, :d].astype(F32)) * ya
         + jax.nn.sigmoid(gab_ref[:, d:].astype(F32)) * yb)
    z = jnp.dot(y.astype(BF16), wo_ref[...], preferred_element_type=F32)
    o_ref[...] = x_ref[...] + _rms(z, gp_ref[...])


def _merge(oc, os_, ow, ob, gn, gab, x, wa, wb, wo, gpost, ts):
    s, d = x.shape
    hd = oc.shape[0]
    tok = lambda rows: pl.BlockSpec((rows, ts), lambda i: (0, i))
    full = lambda shp: pl.BlockSpec(shp, lambda i: (0, 0))
    return pl.pallas_call(
        _merge_kernel,
        grid=(s // ts,),
        in_specs=[tok(hd), tok(hd), tok(hd), tok(hd), tok(gn.shape[0]),
                  pl.BlockSpec((ts, 2 * d), lambda i: (i, 0)), pl.BlockSpec((ts, d), lambda i: (i, 0)),
                  full((hd, d)), full((hd, d)), full((d, d)), full((1, d))],
        out_specs=pl.BlockSpec((ts, d), lambda i: (i, 0)),
        out_shape=jax.ShapeDtypeStruct((s, d), F32),
        compiler_params=_params("arbitrary"),
        name="merge_out_proj",
    )(oc, os_, ow, ob, gn, gab, x, wa, wb, wo, gpost)


def _mlp_kernel(h_ref, gpre_ref, w1_ref, w2_ref, gpost_ref, o_ref, u_ref, acc_ref):
    f = pl.program_id(1)

    @pl.when(f == 0)
    def _():
        u_ref[...] = _rms(h_ref[...], gpre_ref[...]).astype(BF16)
        acc_ref[...] = jnp.zeros(acc_ref.shape, F32)

    a = jnp.maximum(jnp.dot(u_ref[...], w1_ref[...], preferred_element_type=F32), 0.0)
    acc_ref[...] += jnp.dot((a * a).astype(BF16), w2_ref[...], preferred_element_type=F32)

    @pl.when(f == pl.num_programs(1) - 1)
    def _():
        o_ref[...] = h_ref[...] + _rms(acc_ref[...], gpost_ref[...])


def _mlp(h, gpre, w1, w2, gpost, ts, tf):
    s, d = h.shape
    dff = w1.shape[1]
    return pl.pallas_call(
        _mlp_kernel,
        grid=(s // ts, dff // tf),
        in_specs=[pl.BlockSpec((ts, d), lambda i, f: (i, 0)), pl.BlockSpec((1, d), lambda i, f: (0, 0)),
                  pl.BlockSpec((d, tf), lambda i, f: (0, f)), pl.BlockSpec((tf, d), lambda i, f: (f, 0)),
                  pl.BlockSpec((1, d), lambda i, f: (0, 0))],
        out_specs=pl.BlockSpec((ts, d), lambda i, f: (i, 0)),
        out_shape=jax.ShapeDtypeStruct((s, d), F32),
        scratch_shapes=[pltpu.VMEM((ts, d), BF16), pltpu.VMEM((ts, d), F32)],
        compiler_params=_params("arbitrary", "arbitrary"),
        name="mlp_relu2",
    )(h, gpre, w1, w2, gpost)


def kernel(x, norm_mix_pre, norm_mix_post, norm_mlp_pre, norm_mlp_post, w_in,
           cmp_pos_k, cmp_w1_k, cmp_w2_k, cmp_pos_v, cmp_w1_v, cmp_w2_v,
           attn_sinks, rel_bias, w_up_nsa, w_up_swa, w_out, w_mlp_in, w_mlp_out):
    b, s, d = x.shape
    assert b == 1 and s % SEL_TILE == 0 and w_in.shape[0] == 1
    qd = HEADS * HEAD_DIM
    kvd = GROUPS * HEAD_DIM
    sizes = (qd,) + (kvd,) * 6 + (3 * HEADS, qd, kvd, kvd, d, d)
    offs = [0]
    for z in sizes:
        offs.append(offs[-1] + z)
    w = w_in[0]
    col = lambda i: w[:, offs[i]:offs[i + 1]]
    (w_qn, w_kc, w_vc, w_ks, w_vs, w_kw, w_vw, w_gn, w_qs, w_k_s, w_v_s, w_ga, w_gb) = [col(i) for i in range(13)]
    scale = HEAD_DIM ** -0.5 * LOG2E
    w_gn = w_gn.reshape(d, HEADS, 3).transpose(0, 2, 1).reshape(d, 3 * HEADS)
    w_gn = jnp.pad(w_gn, ((0, 0), (0, 8)))
    wn = jnp.concatenate([w_kc, w_vc, w_ks, w_kw, w_k_s], axis=1).astype(BF16)
    wg = jnp.concatenate([w_ga, w_gb], axis=1).astype(BF16)
    wt = jnp.concatenate([w_qn * scale, w_qs * scale, w_vs, w_vw, w_v_s, w_gn], axis=1).T.astype(BF16)
    nqv = 2 * qd + 3 * kvd

    x2 = x[0]
    kv, gab, qv, gn = _project(x2, norm_mix_pre, wn, wg, wt, nqv, ts=512)
    q_nsa = qv[0:qd].reshape(GROUPS, REP * HEAD_DIM, s)
    q_swa = qv[qd:2 * qd].reshape(GROUPS, REP * HEAD_DIM, s)
    vs_t = qv[2 * qd:2 * qd + kvd].reshape(GROUPS, HEAD_DIM, s)
    vw_t = qv[2 * qd + kvd:2 * qd + 2 * kvd].reshape(GROUPS, HEAD_DIM, s)
    vswa_t = qv[2 * qd + 2 * kvd:].reshape(GROUPS, HEAD_DIM, s)

    n16 = s // CMP_STRIDE
    kb = kv[:, :2 * kvd].reshape(n16, CMP_STRIDE, 2, GROUPS, HEAD_DIM).transpose(2, 3, 0, 1, 4)
    kb = kb.reshape(2, GROUPS, n16, CMP_STRIDE * HEAD_DIM)
    pos8 = jnp.stack([cmp_pos_k[0], cmp_pos_v[0]]).reshape(2, 1, CMP_BLOCK * HEAD_DIM)
    pos8 = jnp.broadcast_to(pos8, (2, 8, CMP_BLOCK * HEAD_DIM)).astype(BF16)
    w1 = jnp.stack([cmp_w1_k[0], cmp_w1_v[0]]).astype(BF16)
    w2 = jnp.stack([cmp_w2_k[0], cmp_w2_v[0]]).astype(BF16)
    cmp_n, cmp_t = _compress(kb, pos8, w1, w2, w2.transpose(0, 2, 1))

    oc, sel = _cmp_select(q_nsa, cmp_n, cmp_t, s)

    bias_nsa = rel_bias[:, :HEADS].reshape(NUM_BUCKETS, GROUPS, REP) * LOG2E
    bias_swa = rel_bias[:, HEADS:].reshape(NUM_BUCKETS, GROUPS, REP) * LOG2E
    thr = jnp.asarray(_bucket_starts())
    tab_nsa = bias_nsa.transpose(1, 2, 0).reshape(-1)
    tab_swa = bias_swa.transpose(1, 2, 0).reshape(-1)

    tk = SEL_TILE
    nb = tk // SEL_BLOCK
    posn = jnp.arange(s)
    onehot = ((posn // SEL_BLOCK) % nb)[:, None] == jnp.arange(nb)[None, :]
    aug = jnp.concatenate([onehot.astype(BF16), jnp.ones((s, 2), BF16),
                           jnp.zeros((s, kvd - nb - 2), BF16)], axis=1)
    kaug = jnp.concatenate([kv[:, 2 * kvd:3 * kvd], aug], axis=1)
    far = bias_nsa[NUM_BUCKETS - 1]
    far_hi = far.astype(BF16)
    far_lo = (far - far_hi.astype(F32)).astype(BF16)
    cst = jnp.stack([far_hi, far_lo], axis=1)
    cst = jnp.broadcast_to(cst[:, :, :, None], (GROUPS, 2, REP, QBLOCK)).reshape(GROUPS, 2, HQ)
    cst = jnp.pad(cst, ((0, 0), (0, 14), (0, 0)))
    ones_row = jnp.concatenate([jnp.ones((GROUPS, 1, s), BF16), jnp.zeros((GROUPS, 15, s), BF16)], axis=1)
    vs_aug = jnp.concatenate([vs_t, ones_row], axis=1)
    o_sel = _selected(thr, tab_nsa, q_nsa, sel, kaug, vs_aug, cst, s, tk)

    pad_rows = lambda a, wdw: jnp.pad(a, ((wdw, 0), (0, 0)))
    pad_lanes = lambda a, wdw: jnp.pad(jnp.concatenate([a, ones_row], axis=1), ((0, 0), (0, 0), (wdw, 0)))
    no_sink = jnp.zeros((GROUPS, 1, HQ), F32)
    o_win = _banded(thr, tab_nsa, q_nsa, pad_rows(kv[:, 3 * kvd:4 * kvd], NSA_WINDOW), pad_lanes(vw_t, NSA_WINDOW),
                    no_sink, s, NSA_WINDOW, False, "nsa_window")
    sink = attn_sinks[0].reshape(GROUPS, 1, REP, 1).astype(F32) * LOG2E
    sink = jnp.broadcast_to(sink, (GROUPS, 1, REP, QBLOCK)).reshape(GROUPS, 1, HQ)
    o_swa = _banded(thr, tab_swa, q_swa, pad_rows(kv[:, 4 * kvd:5 * kvd], SWA_WINDOW), pad_lanes(vswa_t, SWA_WINDOW),
                    sink, s, SWA_WINDOW, True, "swa_sink")

    flat = lambda a: a.reshape(qd, s)
    h1 = _merge(flat(oc), flat(o_sel), flat(o_win), flat(o_swa), gn, gab, x2,
                w_up_nsa[0].astype(BF16), w_up_swa[0].astype(BF16), w_out[0].astype(BF16),
                norm_mix_post, ts=512)
    out = _mlp(h1, norm_mlp_pre, w_mlp_in[0].astype(BF16), w_mlp_out[0].astype(BF16), norm_mlp_post,
               ts=1024, tf=1024)
    return out[None]
```

```python
import functools
import math

import jax
import jax.numpy as jnp
import numpy as np
from jax import lax
from jax.experimental import pallas as pl
from jax.experimental.pallas import tpu as pltpu

F32 = jnp.float32
BF16 = jnp.bfloat16

HEAD_DIM = 64
GROUPS = 2
REP = 4
HEADS = GROUPS * REP
CMP_BLOCK = 32
CMP_STRIDE = 16
SEL_BLOCK = 64
SEL_TOPK = 16
NSA_WINDOW = 512
SWA_WINDOW = 128
QBLOCK = 128
NUM_BUCKETS = 32
MAX_DISTANCE = 1024
RMS_EPS = 1e-6
LOG2E = math.log2(math.e)
MASK_BIG = 2.0 ** 99
SEL_TILE = 1024
TILES_PER_TRIP = 8
HQ = REP * QBLOCK
VMEM_LIMIT = 56 * 1024 * 1024

_NT = (((1,), (1,)), ((), ()))
_TN = (((0,), (0,)), ((), ()))


def _params(*sem):
    return pltpu.CompilerParams(dimension_semantics=sem, vmem_limit_bytes=VMEM_LIMIT)


def _bucket_starts():
    max_exact = NUM_BUCKETS // 2
    d = np.arange(2 * MAX_DISTANCE)
    large = max_exact + (np.log(np.maximum(d, 1).astype(np.float64) / max_exact)
                         / math.log(MAX_DISTANCE / max_exact) * (NUM_BUCKETS - max_exact)).astype(np.int64)
    bucket = np.where(d < max_exact, d, np.minimum(large, NUM_BUCKETS - 1))
    return (bucket[None, :] < np.arange(NUM_BUCKETS)[:, None]).sum(axis=1).astype(np.int32)


def _rms(x, gain):
    return x * lax.rsqrt(jnp.mean(x * x, axis=-1, keepdims=True) + RMS_EPS) * gain


def _proj_kernel(x_ref, g_ref, wn_ref, wg_ref, wt_ref, kv_ref, gab_ref, qv_ref, gn_ref, kb_ref, y_ref, *, nqv):
    u = _rms(x_ref[...], g_ref[...]).astype(BF16)
    kvf = jnp.dot(u, wn_ref[...], preferred_element_type=F32)
    kv_ref[...] = kvf.astype(BF16)
    gw = GROUPS * HEAD_DIM
    nrow = y_ref.shape[1] // CMP_STRIDE
    for a in range(2):
        y_ref[a] = kvf[:, a * gw:(a + 1) * gw]
        both = [y_ref[a, pl.ds(p, nrow, stride=CMP_STRIDE), :] for p in range(CMP_STRIDE)]
        for gi in range(GROUPS):
            kb_ref[a, gi] = jnp.concatenate([t[:, gi * HEAD_DIM:(gi + 1) * HEAD_DIM] for t in both],
                                            axis=1).astype(BF16)
    gab_ref[...] = jnp.dot(u, wg_ref[...], preferred_element_type=F32).astype(BF16)
    t = lax.dot_general(wt_ref[...], u, _NT, preferred_element_type=F32)
    qv_ref[...] = t[:nqv].astype(BF16)
    gn_ref[...] = t[nqv:]


def _project(x, gain, wn, wg, wt, nqv, ts):
    s, d = x.shape
    nn, ng, ntr = wn.shape[1], wg.shape[1], wt.shape[0]
    full = lambda shp: pl.BlockSpec(shp, lambda i: (0, 0))
    return pl.pallas_call(
        functools.partial(_proj_kernel, nqv=nqv),
        grid=(s // ts,),
        in_specs=[pl.BlockSpec((ts, d), lambda i: (i, 0)), full((1, d)),
                  full((d, nn)), full((d, ng)), full((ntr, d))],
        out_specs=[pl.BlockSpec((ts, nn), lambda i: (i, 0)),
                   pl.BlockSpec((ts, ng), lambda i: (i, 0)),
                   pl.BlockSpec((nqv, ts), lambda i: (0, i)),
                   pl.BlockSpec((ntr - nqv, ts), lambda i: (0, i)),
                   pl.BlockSpec((2, GROUPS, ts // CMP_STRIDE, CMP_STRIDE * HEAD_DIM), lambda i: (0, 0, i, 0))],
        out_shape=[jax.ShapeDtypeStruct((s, nn), BF16), jax.ShapeDtypeStruct((s, ng), BF16),
                   jax.ShapeDtypeStruct((nqv, s), BF16), jax.ShapeDtypeStruct((ntr - nqv, s), F32),
                   jax.ShapeDtypeStruct((2, GROUPS, s // CMP_STRIDE, CMP_STRIDE * HEAD_DIM), BF16)],
        scratch_shapes=[pltpu.VMEM((2, ts, GROUPS * HEAD_DIM), F32)],
        compiler_params=_params("arbitrary"),
        name="in_proj",
    )(x, gain, wn, wg, wt)


def _gelu_tanh(x):
    return x * (0.5 * (1.0 + jnp.tanh(math.sqrt(2.0 / math.pi) * (x + 0.044715 * (x * x * x)))))


def _compress_kernel(kb_ref, pos_ref, w1_ref, w2_ref, w2t_ref, o_ref, ot_ref):
    kb = kb_ref[0, 0]
    half = kb.shape[1]
    n16 = kb.shape[0]
    first = jnp.dot(kb, w1_ref[0, :half, :], preferred_element_type=F32)
    second = jnp.dot(kb, w1_ref[0, half:, :], preferred_element_type=F32)
    posb = jnp.dot(pos_ref[0], w1_ref[0], preferred_element_type=F32)[0:1]
    pre = first + pltpu.roll(second, n16 - 1, 0) + posb
    h = _gelu_tanh(pre).astype(BF16)
    o_ref[0, 0] = jnp.dot(h, w2_ref[0], preferred_element_type=F32).astype(BF16)
    ot_ref[0, 0] = lax.dot_general(w2t_ref[0], h, _NT, preferred_element_type=F32).astype(BF16)


def _compress(kb, pos8, w1, w2, w2t):
    _, g, n16, cin2 = kb.shape
    hid = w1.shape[2]
    return pl.pallas_call(
        _compress_kernel,
        grid=(2, g),
        in_specs=[pl.BlockSpec((1, 1, n16, cin2), lambda a, b: (a, b, 0, 0)),
                  pl.BlockSpec((1, 8, 2 * cin2), lambda a, b: (a, 0, 0)),
                  pl.BlockSpec((1, 2 * cin2, hid), lambda a, b: (a, 0, 0)),
                  pl.BlockSpec((1, hid, HEAD_DIM), lambda a, b: (a, 0, 0)),
                  pl.BlockSpec((1, HEAD_DIM, hid), lambda a, b: (a, 0, 0))],
        out_specs=[pl.BlockSpec((1, 1, n16, HEAD_DIM), lambda a, b: (a, b, 0, 0)),
                   pl.BlockSpec((1, 1, HEAD_DIM, n16), lambda a, b: (a, b, 0, 0))],
        out_shape=[jax.ShapeDtypeStruct((2, g, n16, HEAD_DIM), BF16),
                   jax.ShapeDtypeStruct((2, g, HEAD_DIM, n16), BF16)],
        compiler_params=_params("arbitrary", "arbitrary"),
        name="kv_compress",
    )(kb, pos8, w1, w2, w2t)


def _cmpsel_body(q_ref, kc_ref, vct_ref, oc_ref, sel_ref, pg_ref, s_ref, t_base, *, cb, rows, nomask, nsel, topk):
    mrows = rows - nomask
    ratio = SEL_BLOCK // CMP_STRIDE
    nblk = rows // ratio
    n_io = nomask + lax.broadcasted_iota(jnp.int32, (mrows, HQ), 0)
    lane_t = lax.broadcasted_iota(jnp.int32, (mrows, HQ), 1) & (QBLOCK - 1)
    blk = lax.broadcasted_iota(jnp.int32, (nblk, QBLOCK), 0)

    def raw_scores(c):
        t0 = t_base + c * QBLOCK
        lanes = slice(c * QBLOCK, (c + 1) * QBLOCK)
        q4 = jnp.concatenate([q_ref[0, r * HEAD_DIM:(r + 1) * HEAD_DIM, lanes] for r in range(REP)], axis=1)
        s = jnp.dot(kc_ref[0, 0, 0:rows, :], q4, preferred_element_type=F32)
        edge = jnp.where(n_io * CMP_STRIDE + (CMP_BLOCK - 1) <= t0 + lane_t, s[nomask:], -jnp.inf)
        s = jnp.concatenate([s[:nomask], edge], axis=0) if nomask else edge
        s_ref[c, 0:rows, :] = s
        m = jnp.max(s, axis=0, keepdims=True)
        return jnp.where(m == -jnp.inf, 0.0, m)

    def attend(c, m):
        t0 = t_base + c * QBLOCK
        lanes = slice(c * QBLOCK, (c + 1) * QBLOCK)
        e = jnp.exp2(s_ref[c, 0:rows, :] - m)
        den = jnp.maximum(jnp.sum(e, axis=0, keepdims=True), 1e-30)
        p = e * (1.0 / den)
        o_t = jnp.dot(vct_ref[0, 0, :, 0:rows], p.astype(BF16), preferred_element_type=F32)
        for r in range(REP):
            oc_ref[0, r * HEAD_DIM:(r + 1) * HEAD_DIM, lanes] = (
                o_t[:, r * QBLOCK:(r + 1) * QBLOCK].astype(oc_ref.dtype))

        pg = ((p[:, 0:QBLOCK] + p[:, QBLOCK:2 * QBLOCK]) + p[:, 2 * QBLOCK:3 * QBLOCK]) + p[:, 3 * QBLOCK:]
        pg_ref[c, 0:8, :] = jnp.zeros((8, QBLOCK), F32)
        pg_ref[c, 8:8 + rows, :] = pg
        ps = pg_ref[c, pl.ds(7, nblk, stride=ratio), :]
        for o in range(1, ratio + 1):
            ps = ps + pg_ref[c, pl.ds(7 + o, nblk, stride=ratio), :]
        cur = (t0 + lax.broadcasted_iota(jnp.int32, (nblk, QBLOCK), 1)) // SEL_BLOCK
        valid = blk <= cur
        forced = (blk == 0) | (blk == cur) | (blk == cur - 1)
        return jnp.where(valid, jnp.where(forced, jnp.inf, ps), -jnp.inf), valid

    scores, valids = [], []
    m = raw_scores(0)
    for c in range(cb):
        m_next = raw_scores(c + 1) if c + 1 < cb else None
        score, valid = attend(c, m)
        scores.append(score)
        valids.append(valid)
        m = m_next

    def pick_one(_, rems):
        out = []
        for rem in rems:
            top = jnp.max(rem, axis=0, keepdims=True)
            first = jnp.min(jnp.where(rem == top, blk, nblk), axis=0, keepdims=True)
            out.append(jnp.where(blk == first, -jnp.inf, rem))
        return tuple(out)

    rems = lax.fori_loop(0, topk, pick_one, tuple(scores))
    for c in range(cb):
        lanes = slice(c * QBLOCK, (c + 1) * QBLOCK)
        sel_ref[0, 0:nblk, lanes] = jnp.where((rems[c] == -jnp.inf) & valids[c], 0.0, -MASK_BIG).astype(BF16)
        if nblk < nsel:
            sel_ref[0, nblk:, lanes] = jnp.full((nsel - nblk, QBLOCK), -MASK_BIG, BF16)


def _cmpsel_kernel(q_ref, kc_ref, vct_ref, oc_ref, sel_ref, pg_ref, s_ref, *, n16, nsel, topk, nvar, cb):
    step = pl.program_id(1)
    per = n16 // nvar
    spv = per // (QBLOCK // CMP_STRIDE) // cb
    for k in range(1, nvar + 1):
        @pl.when((step >= (k - 1) * spv) & (step < k * spv))
        def _(k=k):
            _cmpsel_body(q_ref, kc_ref, vct_ref, oc_ref, sel_ref, pg_ref, s_ref, step * (cb * QBLOCK), cb=cb,
                         rows=per * k, nomask=max(per * (k - 1) - 8, 0), nsel=nsel, topk=topk)


def _cmp_select(q, kc_all, vct_all, s, cb=2):
    n16 = s // CMP_STRIDE
    nsel = s // SEL_BLOCK
    topk = min(SEL_TOPK, nsel)
    nvar = max(n16 // 256, 1)
    return pl.pallas_call(
        functools.partial(_cmpsel_kernel, n16=n16, nsel=nsel, topk=topk, nvar=nvar, cb=cb),
        grid=(GROUPS, s // (QBLOCK * cb)),
        in_specs=[pl.BlockSpec((1, REP * HEAD_DIM, QBLOCK * cb), lambda g, c: (g, 0, c)),
                  pl.BlockSpec((1, 1, n16, HEAD_DIM), lambda g, c: (0, g, 0, 0)),
                  pl.BlockSpec((1, 1, HEAD_DIM, n16), lambda g, c: (1, g, 0, 0))],
        out_specs=[pl.BlockSpec((1, REP * HEAD_DIM, QBLOCK * cb), lambda g, c: (g, 0, c)),
                   pl.BlockSpec((1, nsel, QBLOCK * cb), lambda g, c: (g, 0, c))],
        out_shape=[jax.ShapeDtypeStruct((GROUPS, REP * HEAD_DIM, s), BF16),
                   jax.ShapeDtypeStruct((GROUPS, nsel, s), BF16)],
        scratch_shapes=[pltpu.VMEM((cb, n16 + 8, QBLOCK), F32), pltpu.VMEM((cb, n16, HQ), F32)],
        compiler_params=_params("arbitrary", "arbitrary"),
        name="cmp_select",
    )(q, kc_all, vct_all)


def _fill_bias_strip(strip_ref, thr_ref, tab_ref, g, *, dist0, lo, hi, minus_far):
    nrows = strip_ref.shape[0]

    def block(bi, carry):
        i0 = pl.multiple_of(bi * QBLOCK, QBLOCK)
        d = (lax.broadcasted_iota(jnp.int32, (QBLOCK, QBLOCK), 1)
             - lax.broadcasted_iota(jnp.int32, (QBLOCK, QBLOCK), 0) + (dist0 - i0))
        ok = d >= lo if hi is None else (d >= lo) & (d < hi)
        for r in range(REP):
            base = (g * REP + r) * NUM_BUCKETS
            out = jnp.full((QBLOCK, QBLOCK), tab_ref[base], F32)
            for b in range(1, NUM_BUCKETS):
                out = jnp.where(d >= thr_ref[b], tab_ref[base + b], out)
            if minus_far:
                out = out - tab_ref[base + NUM_BUCKETS - 1]
            strip_ref[pl.ds(i0, QBLOCK), r * QBLOCK:(r + 1) * QBLOCK] = jnp.where(ok, out, -MASK_BIG)
        return carry

    lax.fori_loop(0, nrows // QBLOCK, block, 0)


def _sel_kernel(thr_ref, tab_ref, q_ref, sel_ref, kaug_ref, vt_ref, cst_ref, o_ref,
                qa_ref, s_ref, tmax_ref, m_ref, acc_ref, strip_ref, *, tk):
    g = pl.program_id(0)
    st = pl.program_id(1)
    nb = tk // SEL_BLOCK
    cpt = tk // QBLOCK
    kd = GROUPS * HEAD_DIM
    dmax = 2 * tk - QBLOCK

    @pl.when(st == 0)
    def _():
        _fill_bias_strip(strip_ref, thr_ref, tab_ref, g, dist0=dmax, lo=0, hi=None, minus_far=True)

    qa_ref[...] = jnp.zeros(qa_ref.shape, BF16)
    for b in range(2):
        qa_ref[b, kd + nb:kd + nb + 16, :] = cst_ref[0]
    m_ref[...] = jnp.full(m_ref.shape, -jnp.inf, F32)
    acc_ref[...] = jnp.zeros(acc_ref.shape, F32)
    row0 = pl.multiple_of(g * HEAD_DIM, HEAD_DIM)
    nnear = jnp.minimum(st + 1, 2)
    nfar = st + 1 - nnear

    def run(count, first_tile, near):

        def advance(c, i):
            wrap = i + 1 == count
            return jnp.where(wrap, c + 1, c), jnp.where(wrap, 0, i + 1)

        def qk(c, i, buf):
            cc = jnp.minimum(c, cpt - 1)
            j = first_tile + i
            lane0 = pl.multiple_of(cc * QBLOCK, QBLOCK)
            for r in range(REP):
                qa_ref[buf, pl.ds(row0, HEAD_DIM), r * QBLOCK:(r + 1) * QBLOCK] = (
                    q_ref[0, r * HEAD_DIM:(r + 1) * HEAD_DIM, pl.ds(lane0, QBLOCK)])
            sb = sel_ref[0, pl.ds(pl.multiple_of(j * nb, nb), nb), pl.ds(lane0, QBLOCK)]
            for r in range(REP):
                qa_ref[buf, kd:kd + nb, r * QBLOCK:(r + 1) * QBLOCK] = sb
            k0 = pl.multiple_of(j * tk, tk)
            s = jnp.dot(kaug_ref[pl.ds(k0, tk), :], qa_ref[buf], preferred_element_type=F32)
            if near:
                delta = (st - j) * tk + cc * QBLOCK
                s = s + strip_ref[pl.ds(pl.multiple_of(dmax - delta, QBLOCK), tk), :]
            s_ref[buf] = s
            tmax_ref[buf] = jnp.max(s, axis=0, keepdims=True)

        def softmax_pv(c, i, buf):
            k0 = pl.multiple_of((first_tile + i) * tk, tk)
            m_prev = m_ref[c]
            m_new = jnp.maximum(m_prev, tmax_ref[buf])
            alpha = jnp.exp2(m_prev - m_new)
            p = jnp.exp2(s_ref[buf] - m_new)
            acc = alpha * acc_ref[c] + jnp.dot(vt_ref[0, :, pl.ds(k0, tk)], p.astype(BF16),
                                               preferred_element_type=F32)
            acc_ref[c] = acc
            m_ref[c] = m_new
            if near:
                lane0 = pl.multiple_of(c * QBLOCK, QBLOCK)
                out = acc[0:HEAD_DIM] * (1.0 / jnp.maximum(acc[HEAD_DIM:HEAD_DIM + 1], 1e-30))
                for r in range(REP):
                    o_ref[0, r * HEAD_DIM:(r + 1) * HEAD_DIM, pl.ds(lane0, QBLOCK)] = (
                        out[:, r * QBLOCK:(r + 1) * QBLOCK].astype(o_ref.dtype))

        zero = jnp.int32(0)
        qk(zero, zero, 0)

        def trip(_, carry):
            cur = carry
            for k in range(TILES_PER_TRIP):
                nxt = advance(*cur)
                qk(*nxt, (k + 1) % 2)
                softmax_pv(*cur, k % 2)
                cur = nxt
            return cur

        lax.fori_loop(0, (cpt * count) // TILES_PER_TRIP, trip, (zero, zero))

    @pl.when(nfar > 0)
    def _():
        run(nfar, 0, False)

    run(nnear, nfar, True)


def _selected(thr, tab, q, sel, kaug, vt, cst, s, tk):
    nsel = s // SEL_BLOCK
    sl = 3 * tk - QBLOCK
    cpt = tk // QBLOCK
    smem = pl.BlockSpec(memory_space=pltpu.SMEM)
    return pl.pallas_call(
        functools.partial(_sel_kernel, tk=tk),
        grid=(GROUPS, s // tk),
        in_specs=[smem, smem,
                  pl.BlockSpec((1, REP * HEAD_DIM, tk), lambda g, c: (g, 0, c)),
                  pl.BlockSpec((1, nsel, tk), lambda g, c: (g, 0, c)),
                  pl.BlockSpec(kaug.shape, lambda g, c: (0, 0)),
                  pl.BlockSpec((1, vt.shape[1], s), lambda g, c: (g, 0, 0)),
                  pl.BlockSpec((1, 16, HQ), lambda g, c: (g, 0, 0))],
        out_specs=pl.BlockSpec((1, REP * HEAD_DIM, tk), lambda g, c: (g, 0, c)),
        out_shape=jax.ShapeDtypeStruct((GROUPS, REP * HEAD_DIM, s), BF16),
        scratch_shapes=[pltpu.VMEM((2, kaug.shape[1], HQ), BF16), pltpu.VMEM((2, tk, HQ), F32),
                        pltpu.VMEM((2, 1, HQ), F32),
                        pltpu.VMEM((cpt, 1, HQ), F32), pltpu.VMEM((cpt, vt.shape[1], HQ), F32),
                        pltpu.VMEM((sl, HQ), F32)],
        compiler_params=_params("arbitrary", "arbitrary"),
        name="selected_attn",
    )(thr, tab, q, sel, kaug, vt, cst)


def _band_kernel(thr_ref, tab_ref, q_ref, k_ref, vt_ref, sink_ref, o_ref, qa_ref, strip_ref, s_ref, *,
                 window, use_sink, cb):
    g = pl.program_id(0)

    @pl.when(pl.program_id(1) == 0)
    def _():
        _fill_bias_strip(strip_ref, thr_ref, tab_ref, g, dist0=window, lo=0, hi=window, minus_far=False)

    base = pl.multiple_of(pl.program_id(1) * (QBLOCK * cb), QBLOCK * cb)
    nk = window + QBLOCK
    qa_ref[...] = jnp.zeros(qa_ref.shape, BF16)
    row0 = pl.multiple_of(g * HEAD_DIM, HEAD_DIM)
    for c in range(cb):
        for r in range(REP):
            qa_ref[c, pl.ds(row0, HEAD_DIM), r * QBLOCK:(r + 1) * QBLOCK] = (
                q_ref[0, r * HEAD_DIM:(r + 1) * HEAD_DIM, c * QBLOCK:(c + 1) * QBLOCK])

    def scores(c, first_step):
        t0 = base + c * QBLOCK
        s = jnp.dot(k_ref[pl.ds(t0, nk), :], qa_ref[c], preferred_element_type=F32) + strip_ref[...]
        if first_step and c * QBLOCK < window:
            row = lax.broadcasted_iota(jnp.int32, (nk, HQ), 0)
            s = jnp.where(row >= window - c * QBLOCK, s, -MASK_BIG)
        s_ref[c] = s
        m = jnp.max(s, axis=0, keepdims=True)
        return jnp.maximum(m, sink_ref[0]) if use_sink else m

    def attend(c, m):
        t0 = base + c * QBLOCK
        e = jnp.exp2(s_ref[c] - m)
        o_t = jnp.dot(vt_ref[0, :, pl.ds(t0, nk)], e.astype(BF16), preferred_element_type=F32)
        den = o_t[HEAD_DIM:HEAD_DIM + 1]
        den = den + jnp.exp2(sink_ref[0] - m) if use_sink else jnp.maximum(den, 1e-30)
        o_t = o_t[0:HEAD_DIM] * (1.0 / den)
        for r in range(REP):
            o_ref[0, r * HEAD_DIM:(r + 1) * HEAD_DIM, c * QBLOCK:(c + 1) * QBLOCK] = (
                o_t[:, r * QBLOCK:(r + 1) * QBLOCK].astype(o_ref.dtype))

    def chunks(first_step):
        m = scores(0, first_step)
        for c in range(cb):
            m_next = scores(c + 1, first_step) if c + 1 < cb else None
            attend(c, m)
            m = m_next

    @pl.when(pl.program_id(1) == 0)
    def _():
        chunks(True)

    @pl.when(pl.program_id(1) > 0)
    def _():
        chunks(False)


def _banded(thr, tab, q, kpad, vtpad, sink, s, window, use_sink, name, cb=4):
    assert QBLOCK * cb >= window
    nk = window + QBLOCK
    smem = pl.BlockSpec(memory_space=pltpu.SMEM)
    return pl.pallas_call(
        functools.partial(_band_kernel, window=window, use_sink=use_sink, cb=cb),
        grid=(GROUPS, s // (QBLOCK * cb)),
        in_specs=[smem, smem,
                  pl.BlockSpec((1, REP * HEAD_DIM, QBLOCK * cb), lambda g, c: (g, 0, c)),
                  pl.BlockSpec(kpad.shape, lambda g, c: (0, 0)),
                  pl.BlockSpec((1, vtpad.shape[1], s + window), lambda g, c: (g, 0, 0)),
                  pl.BlockSpec((1, 1, HQ), lambda g, c: (g, 0, 0))],
        out_specs=pl.BlockSpec((1, REP * HEAD_DIM, QBLOCK * cb), lambda g, c: (g, 0, c)),
        out_shape=jax.ShapeDtypeStruct((GROUPS, REP * HEAD_DIM, s), BF16),
        scratch_shapes=[pltpu.VMEM((cb, GROUPS * HEAD_DIM, HQ), BF16), pltpu.VMEM((nk, HQ), F32),
                        pltpu.VMEM((cb, nk, HQ), F32)],
        compiler_params=_params("arbitrary", "arbitrary"),
        name=name,
    )(thr, tab, q, kpad, vtpad, sink)


def _merge_kernel(oc_ref, os_ref, ow_ref, ob_ref, gn_ref, gab_ref, x_ref, wa_ref, wb_ref, wo_ref, gp_ref, o_ref):
    d = x_ref.shape[1]
    gn = jax.nn.sigmoid(gn_ref[...])
    parts = []
    for h in range(HEADS):
        rows = slice(h * HEAD_DIM, (h + 1) * HEAD_DIM)
        parts.append(gn[h:h + 1] * oc_ref[rows, :].astype(F32)
                     + gn[HEADS + h:HEADS + h + 1] * os_ref[rows, :].astype(F32)
                     + gn[2 * HEADS + h:2 * HEADS + h + 1] * ow_ref[rows, :].astype(F32))
    oa_t = jnp.concatenate(parts, axis=0).astype(BF16)
    ya = lax.dot_general(oa_t, wa_ref[...], _TN, preferred_element_type=F32)
    yb = lax.dot_general(ob_ref[...], wb_ref[...], _TN, preferred_element_type=F32)
    y = (jax.nn.sigmoid(gab_ref[:, :d].astype(F32)) * ya
         + jax.nn.sigmoid(gab_ref[:, d:].astype(F32)) * yb)
    z = jnp.dot(y.astype(BF16), wo_ref[...], preferred_element_type=F32)
    o_ref[...] = x_ref[...] + _rms(z, gp_ref[...])


def _merge(oc, os_, ow, ob, gn, gab, x, wa, wb, wo, gpost, ts):
    s, d = x.shape
    hd = oc.shape[0]
    tok = lambda rows: pl.BlockSpec((rows, ts), lambda i: (0, i))
    full = lambda shp: pl.BlockSpec(shp, lambda i: (0, 0))
    return pl.pallas_call(
        _merge_kernel,
        grid=(s // ts,),
        in_specs=[tok(hd), tok(hd), tok(hd), tok(hd), tok(gn.shape[0]),
                  pl.BlockSpec((ts, 2 * d), lambda i: (i, 0)), pl.BlockSpec((ts, d), lambda i: (i, 0)),
                  full((hd, d)), full((hd, d)), full((d, d)), full((1, d))],
        out_specs=pl.BlockSpec((ts, d), lambda i: (i, 0)),
        out_shape=jax.ShapeDtypeStruct((s, d), F32),
        compiler_params=_params("arbitrary"),
        name="merge_out_proj",
    )(oc, os_, ow, ob, gn, gab, x, wa, wb, wo, gpost)


def _mlp_kernel(h_ref, gpre_ref, w1_ref, w2_ref, gpost_ref, o_ref, u_ref, acc_ref):
    f = pl.program_id(1)

    @pl.when(f == 0)
    def _():
        u_ref[...] = _rms(h_ref[...], gpre_ref[...]).astype(BF16)
        acc_ref[...] = jnp.zeros(acc_ref.shape, F32)

    a = jnp.maximum(jnp.dot(u_ref[...], w1_ref[...], preferred_element_type=F32), 0.0)
    acc_ref[...] += jnp.dot((a * a).astype(BF16), w2_ref[...], preferred_element_type=F32)

    @pl.when(f == pl.num_programs(1) - 1)
    def _():
        o_ref[...] = h_ref[...] + _rms(acc_ref[...], gpost_ref[...])


def _mlp(h, gpre, w1, w2, gpost, ts, tf):
    s, d = h.shape
    dff = w1.shape[1]
    return pl.pallas_call(
        _mlp_kernel,
        grid=(s // ts, dff // tf),
        in_specs=[pl.BlockSpec((ts, d), lambda i, f: (i, 0)), pl.BlockSpec((1, d), lambda i, f: (0, 0)),
                  pl.BlockSpec((d, tf), lambda i, f: (0, f)), pl.BlockSpec((tf, d), lambda i, f: (f, 0)),
                  pl.BlockSpec((1, d), lambda i, f: (0, 0))],
        out_specs=pl.BlockSpec((ts, d), lambda i, f: (i, 0)),
        out_shape=jax.ShapeDtypeStruct((s, d), F32),
        scratch_shapes=[pltpu.VMEM((ts, d), BF16), pltpu.VMEM((ts, d), F32)],
        compiler_params=_params("arbitrary", "arbitrary"),
        name="mlp_relu2",
    )(h, gpre, w1, w2, gpost)


def kernel(x, norm_mix_pre, norm_mix_post, norm_mlp_pre, norm_mlp_post, w_in,
           cmp_pos_k, cmp_w1_k, cmp_w2_k, cmp_pos_v, cmp_w1_v, cmp_w2_v,
           attn_sinks, rel_bias, w_up_nsa, w_up_swa, w_out, w_mlp_in, w_mlp_out):
    b, s, d = x.shape
    assert b == 1 and s % SEL_TILE == 0 and w_in.shape[0] == 1
    qd = HEADS * HEAD_DIM
    kvd = GROUPS * HEAD_DIM
    sizes = (qd,) + (kvd,) * 6 + (3 * HEADS, qd, kvd, kvd, d, d)
    offs = [0]
    for z in sizes:
        offs.append(offs[-1] + z)
    w = w_in[0]
    col = lambda i: w[:, offs[i]:offs[i + 1]]
    (w_qn, w_kc, w_vc, w_ks, w_vs, w_kw, w_vw, w_gn, w_qs, w_k_s, w_v_s, w_ga, w_gb) = [col(i) for i in range(13)]
    scale = HEAD_DIM ** -0.5 * LOG2E
    w_gn = w_gn.reshape(d, HEADS, 3).transpose(0, 2, 1).reshape(d, 3 * HEADS)
    w_gn = jnp.pad(w_gn, ((0, 0), (0, 8)))
    wn = jnp.concatenate([w_kc, w_vc, w_ks, w_kw, w_k_s], axis=1).astype(BF16)
    wg = jnp.concatenate([w_ga, w_gb], axis=1).astype(BF16)
    wt = jnp.concatenate([w_qn * scale, w_qs * scale, w_vs, w_vw, w_v_s, w_gn], axis=1).T.astype(BF16)
    nqv = 2 * qd + 3 * kvd

    x2 = x[0]
    kv, gab, qv, gn, kb = _project(x2, norm_mix_pre, wn, wg, wt, nqv, ts=512)
    q_nsa = qv[0:qd].reshape(GROUPS, REP * HEAD_DIM, s)
    q_swa = qv[qd:2 * qd].reshape(GROUPS, REP * HEAD_DIM, s)
    vs_t = qv[2 * qd:2 * qd + kvd].reshape(GROUPS, HEAD_DIM, s)
    vw_t = qv[2 * qd + kvd:2 * qd + 2 * kvd].reshape(GROUPS, HEAD_DIM, s)
    vswa_t = qv[2 * qd + 2 * kvd:].reshape(GROUPS, HEAD_DIM, s)

    n16 = s // CMP_STRIDE
    pos8 = jnp.stack([cmp_pos_k[0], cmp_pos_v[0]]).reshape(2, 1, CMP_BLOCK * HEAD_DIM)
    pos8 = jnp.broadcast_to(pos8, (2, 8, CMP_BLOCK * HEAD_DIM)).astype(BF16)
    w1 = jnp.stack([cmp_w1_k[0], cmp_w1_v[0]]).astype(BF16)
    w2 = jnp.stack([cmp_w2_k[0], cmp_w2_v[0]]).astype(BF16)
    cmp_n, cmp_t = _compress(kb, pos8, w1, w2, w2.transpose(0, 2, 1))

    oc, sel = _cmp_select(q_nsa, cmp_n, cmp_t, s)

    bias_nsa = rel_bias[:, :HEADS].reshape(NUM_BUCKETS, GROUPS, REP) * LOG2E
    bias_swa = rel_bias[:, HEADS:].reshape(NUM_BUCKETS, GROUPS, REP) * LOG2E
    thr = jnp.asarray(_bucket_starts())
    tab_nsa = bias_nsa.transpose(1, 2, 0).reshape(-1)
    tab_swa = bias_swa.transpose(1, 2, 0).reshape(-1)

    tk = SEL_TILE
    nb = tk // SEL_BLOCK
    posn = jnp.arange(s)
    onehot = ((posn // SEL_BLOCK) % nb)[:, None] == jnp.arange(nb)[None, :]
    aug = jnp.concatenate([onehot.astype(BF16), jnp.ones((s, 2), BF16),
                           jnp.zeros((s, kvd - nb - 2), BF16)], axis=1)
    kaug = jnp.concatenate([kv[:, 2 * kvd:3 * kvd], aug], axis=1)
    far = bias_nsa[NUM_BUCKETS - 1]
    far_hi = far.astype(BF16)
    far_lo = (far - far_hi.astype(F32)).astype(BF16)
    cst = jnp.stack([far_hi, far_lo], axis=1)
    cst = jnp.broadcast_to(cst[:, :, :, None], (GROUPS, 2, REP, QBLOCK)).reshape(GROUPS, 2, HQ)
    cst = jnp.pad(cst, ((0, 0), (0, 14), (0, 0)))
    ones_row = jnp.concatenate([jnp.ones((GROUPS, 1, s), BF16), jnp.zeros((GROUPS, 15, s), BF16)], axis=1)
    vs_aug = jnp.concatenate([vs_t, ones_row], axis=1)
    o_sel = _selected(thr, tab_nsa, q_nsa, sel, kaug, vs_aug, cst, s, tk)

    pad_rows = lambda a, wdw: jnp.pad(a, ((wdw, 0), (0, 0)))
    pad_lanes = lambda a, wdw: jnp.pad(jnp.concatenate([a, ones_row], axis=1), ((0, 0), (0, 0), (wdw, 0)))
    no_sink = jnp.zeros((GROUPS, 1, HQ), F32)
    o_win = _banded(thr, tab_nsa, q_nsa, pad_rows(kv[:, 3 * kvd:4 * kvd], NSA_WINDOW), pad_lanes(vw_t, NSA_WINDOW),
                    no_sink, s, NSA_WINDOW, False, "nsa_window")
    sink = attn_sinks[0].reshape(GROUPS, 1, REP, 1).astype(F32) * LOG2E
    sink = jnp.broadcast_to(sink, (GROUPS, 1, REP, QBLOCK)).reshape(GROUPS, 1, HQ)
    o_swa = _banded(thr, tab_swa, q_swa, pad_rows(kv[:, 4 * kvd:5 * kvd], SWA_WINDOW), pad_lanes(vswa_t, SWA_WINDOW),
                    sink, s, SWA_WINDOW, True, "swa_sink")

    flat = lambda a: a.reshape(qd, s)
    h1 = _merge(flat(oc), flat(o_sel), flat(o_win), flat(o_swa), gn, gab, x2,
                w_up_nsa[0].astype(BF16), w_up_swa[0].astype(BF16), w_out[0].astype(BF16),
                norm_mix_post, ts=512)
    out = _mlp(h1, norm_mlp_pre, w_mlp_in[0].astype(BF16), w_mlp_out[0].astype(BF16), norm_mlp_post,
               ts=1024, tf=1024)
    return out[None]
```

```python
import functools
import math

import jax
import jax.numpy as jnp
import numpy as np
from jax import lax
from jax.experimental import pallas as pl
from jax.experimental.pallas import tpu as pltpu

F32 = jnp.float32
BF16 = jnp.bfloat16

HEAD_DIM = 64
GROUPS = 2
REP = 4
HEADS = GROUPS * REP
CMP_BLOCK = 32
CMP_STRIDE = 16
SEL_BLOCK = 64
SEL_TOPK = 16
NSA_WINDOW = 512
SWA_WINDOW = 128
QBLOCK = 128
NUM_BUCKETS = 32
MAX_DISTANCE = 1024
RMS_EPS = 1e-6
LOG2E = math.log2(math.e)
MASK_BIG = 2.0 ** 99
SEL_TILE = 1024
TILES_PER_TRIP = 8
HQ = REP * QBLOCK
VMEM_LIMIT = 56 * 1024 * 1024

_NT = (((1,), (1,)), ((), ()))
_TN = (((0,), (0,)), ((), ()))


def _params(*sem):
    return pltpu.CompilerParams(dimension_semantics=sem, vmem_limit_bytes=VMEM_LIMIT)


def _bucket_starts():
    max_exact = NUM_BUCKETS // 2
    d = np.arange(2 * MAX_DISTANCE)
    large = max_exact + (np.log(np.maximum(d, 1).astype(np.float64) / max_exact)
                         / math.log(MAX_DISTANCE / max_exact) * (NUM_BUCKETS - max_exact)).astype(np.int64)
    bucket = np.where(d < max_exact, d, np.minimum(large, NUM_BUCKETS - 1))
    return (bucket[None, :] < np.arange(NUM_BUCKETS)[:, None]).sum(axis=1).astype(np.int32)


def _rms(x, gain):
    return x * lax.rsqrt(jnp.mean(x * x, axis=-1, keepdims=True) + RMS_EPS) * gain


def _proj_kernel(x_ref, g_ref, wn_ref, wg_ref, wt_ref, kv_ref, gab_ref, qv_ref, gn_ref, kb_ref, y_ref, *, nqv):
    u = _rms(x_ref[...], g_ref[...]).astype(BF16)
    kvf = jnp.dot(u, wn_ref[...], preferred_element_type=F32)
    kv_ref[...] = kvf.astype(BF16)
    gw = GROUPS * HEAD_DIM
    nrow = y_ref.shape[1] // CMP_STRIDE
    for a in range(2):
        y_ref[a] = kvf[:, a * gw:(a + 1) * gw]
        both = [y_ref[a, pl.ds(p, nrow, stride=CMP_STRIDE), :] for p in range(CMP_STRIDE)]
        for gi in range(GROUPS):
            kb_ref[a, gi] = jnp.concatenate([t[:, gi * HEAD_DIM:(gi + 1) * HEAD_DIM] for t in both],
                                            axis=1).astype(BF16)
    gab_ref[...] = jnp.dot(u, wg_ref[...], preferred_element_type=F32).astype(BF16)
    t = lax.dot_general(wt_ref[...], u, _NT, preferred_element_type=F32)
    qv_ref[...] = t[:nqv].astype(BF16)
    gn_ref[...] = t[nqv:]


def _project(x, gain, wn, wg, wt, nqv, ts):
    s, d = x.shape
    nn, ng, ntr = wn.shape[1], wg.shape[1], wt.shape[0]
    full = lambda shp: pl.BlockSpec(shp, lambda i: (0, 0))
    return pl.pallas_call(
        functools.partial(_proj_kernel, nqv=nqv),
        grid=(s // ts,),
        in_specs=[pl.BlockSpec((ts, d), lambda i: (i, 0)), full((1, d)),
                  full((d, nn)), full((d, ng)), full((ntr, d))],
        out_specs=[pl.BlockSpec((ts, nn), lambda i: (i, 0)),
                   pl.BlockSpec((ts, ng), lambda i: (i, 0)),
                   pl.BlockSpec((nqv, ts), lambda i: (0, i)),
                   pl.BlockSpec((ntr - nqv, ts), lambda i: (0, i)),
                   pl.BlockSpec((2, GROUPS, ts // CMP_STRIDE, CMP_STRIDE * HEAD_DIM), lambda i: (0, 0, i, 0))],
        out_shape=[jax.ShapeDtypeStruct((s, nn), BF16), jax.ShapeDtypeStruct((s, ng), BF16),
                   jax.ShapeDtypeStruct((nqv, s), BF16), jax.ShapeDtypeStruct((ntr - nqv, s), F32),
                   jax.ShapeDtypeStruct((2, GROUPS, s // CMP_STRIDE, CMP_STRIDE * HEAD_DIM), BF16)],
        scratch_shapes=[pltpu.VMEM((2, ts, GROUPS * HEAD_DIM), F32)],
        compiler_params=_params("arbitrary"),
        name="in_proj",
    )(x, gain, wn, wg, wt)


def _gelu_tanh(x):
    return x * (0.5 * (1.0 + jnp.tanh(math.sqrt(2.0 / math.pi) * (x + 0.044715 * (x * x * x)))))


def _compress_kernel(kb_ref, pos_ref, w1_ref, w2_ref, w2t_ref, o_ref, ot_ref):
    kb = kb_ref[0, 0]
    half = kb.shape[1]
    n16 = kb.shape[0]
    first = jnp.dot(kb, w1_ref[0, :half, :], preferred_element_type=F32)
    second = jnp.dot(kb, w1_ref[0, half:, :], preferred_element_type=F32)
    posb = jnp.dot(pos_ref[0], w1_ref[0], preferred_element_type=F32)[0:1]
    pre = first + pltpu.roll(second, n16 - 1, 0) + posb
    h = _gelu_tanh(pre).astype(BF16)
    o_ref[0, 0] = jnp.dot(h, w2_ref[0], preferred_element_type=F32).astype(BF16)
    ot_ref[0, 0] = lax.dot_general(w2t_ref[0], h, _NT, preferred_element_type=F32).astype(BF16)


def _compress(kb, pos8, w1, w2, w2t):
    _, g, n16, cin2 = kb.shape
    hid = w1.shape[2]
    return pl.pallas_call(
        _compress_kernel,
        grid=(2, g),
        in_specs=[pl.BlockSpec((1, 1, n16, cin2), lambda a, b: (a, b, 0, 0)),
                  pl.BlockSpec((1, 8, 2 * cin2), lambda a, b: (a, 0, 0)),
                  pl.BlockSpec((1, 2 * cin2, hid), lambda a, b: (a, 0, 0)),
                  pl.BlockSpec((1, hid, HEAD_DIM), lambda a, b: (a, 0, 0)),
                  pl.BlockSpec((1, HEAD_DIM, hid), lambda a, b: (a, 0, 0))],
        out_specs=[pl.BlockSpec((1, 1, n16, HEAD_DIM), lambda a, b: (a, b, 0, 0)),
                   pl.BlockSpec((1, 1, HEAD_DIM, n16), lambda a, b: (a, b, 0, 0))],
        out_shape=[jax.ShapeDtypeStruct((2, g, n16, HEAD_DIM), BF16),
                   jax.ShapeDtypeStruct((2, g, HEAD_DIM, n16), BF16)],
        compiler_params=_params("arbitrary", "arbitrary"),
        name="kv_compress",
    )(kb, pos8, w1, w2, w2t)


def _cmpsel_body(q_ref, kc_ref, vct_ref, oc_ref, sel_ref, pg_ref, s_ref, t_base, *, cb, rows, nomask, nsel, topk):
    mrows = rows - nomask
    ratio = SEL_BLOCK // CMP_STRIDE
    nblk = rows // ratio
    n_io = nomask + lax.broadcasted_iota(jnp.int32, (mrows, HQ), 0)
    lane_t = lax.broadcasted_iota(jnp.int32, (mrows, HQ), 1) & (QBLOCK - 1)
    blk = lax.broadcasted_iota(jnp.int32, (nblk, QBLOCK), 0)

    def raw_scores(c):
        t0 = t_base + c * QBLOCK
        lanes = slice(c * QBLOCK, (c + 1) * QBLOCK)
        q4 = jnp.concatenate([q_ref[0, r * HEAD_DIM:(r + 1) * HEAD_DIM, lanes] for r in range(REP)], axis=1)
        s = jnp.dot(kc_ref[0, 0, 0:rows, :], q4, preferred_element_type=F32)
        edge = jnp.where(n_io * CMP_STRIDE + (CMP_BLOCK - 1) <= t0 + lane_t, s[nomask:], -jnp.inf)
        s = jnp.concatenate([s[:nomask], edge], axis=0) if nomask else edge
        s_ref[c, 0:rows, :] = s
        m = jnp.max(s, axis=0, keepdims=True)
        return jnp.where(m == -jnp.inf, 0.0, m)

    def attend(c, m):
        t0 = t_base + c * QBLOCK
        lanes = slice(c * QBLOCK, (c + 1) * QBLOCK)
        e = jnp.exp2(s_ref[c, 0:rows, :] - m)
        den = jnp.maximum(jnp.sum(e, axis=0, keepdims=True), 1e-30)
        p = e * (1.0 / den)
        o_t = jnp.dot(vct_ref[0, 0, :, 0:rows], p.astype(BF16), preferred_element_type=F32)
        for r in range(REP):
            oc_ref[0, r * HEAD_DIM:(r + 1) * HEAD_DIM, lanes] = (
                o_t[:, r * QBLOCK:(r + 1) * QBLOCK].astype(oc_ref.dtype))

        pg = ((p[:, 0:QBLOCK] + p[:, QBLOCK:2 * QBLOCK]) + p[:, 2 * QBLOCK:3 * QBLOCK]) + p[:, 3 * QBLOCK:]
        pg_ref[c, 0:8, :] = jnp.zeros((8, QBLOCK), F32)
        pg_ref[c, 8:8 + rows, :] = pg
        ps = pg_ref[c, pl.ds(7, nblk, stride=ratio), :]
        for o in range(1, ratio + 1):
            ps = ps + pg_ref[c, pl.ds(7 + o, nblk, stride=ratio), :]
        cur = (t0 + lax.broadcasted_iota(jnp.int32, (nblk, QBLOCK), 1)) // SEL_BLOCK
        valid = blk <= cur
        forced = (blk == 0) | (blk == cur) | (blk == cur - 1)
        return jnp.where(valid & ~forced, ps, -jnp.inf), valid

    scores, valids = [], []
    m = raw_scores(0)
    for c in range(cb):
        m_next = raw_scores(c + 1) if c + 1 < cb else None
        score, valid = attend(c, m)
        scores.append(score)
        valids.append(valid)
        m = m_next

    def pick_one(_, rems):
        out = []
        for rem in rems:
            top = jnp.max(rem, axis=0, keepdims=True)
            first = jnp.min(jnp.where(rem == top, blk, nblk), axis=0, keepdims=True)
            out.append(jnp.where(blk == first, -jnp.inf, rem))
        return tuple(out)

    rems = lax.fori_loop(0, max(topk - 3, 0), pick_one, tuple(scores))
    for c in range(cb):
        lanes = slice(c * QBLOCK, (c + 1) * QBLOCK)
        sel_ref[0, 0:nblk, lanes] = jnp.where((rems[c] == -jnp.inf) & valids[c], 0.0, -MASK_BIG).astype(BF16)
        if nblk < nsel:
            sel_ref[0, nblk:, lanes] = jnp.full((nsel - nblk, QBLOCK), -MASK_BIG, BF16)


def _cmpsel_kernel(q_ref, kc_ref, vct_ref, oc_ref, sel_ref, pg_ref, s_ref, *, n16, nsel, topk, nvar, cb):
    step = pl.program_id(1)
    per = n16 // nvar
    spv = per // (QBLOCK // CMP_STRIDE) // cb
    for k in range(1, nvar + 1):
        @pl.when((step >= (k - 1) * spv) & (step < k * spv))
        def _(k=k):
            _cmpsel_body(q_ref, kc_ref, vct_ref, oc_ref, sel_ref, pg_ref, s_ref, step * (cb * QBLOCK), cb=cb,
                         rows=per * k, nomask=max(per * (k - 1) - 8, 0), nsel=nsel, topk=topk)


def _cmp_select(q, kc_all, vct_all, s, cb=2):
    n16 = s // CMP_STRIDE
    nsel = s // SEL_BLOCK
    topk = min(SEL_TOPK, nsel)
    nvar = max(n16 // 256, 1)
    return pl.pallas_call(
        functools.partial(_cmpsel_kernel, n16=n16, nsel=nsel, topk=topk, nvar=nvar, cb=cb),
        grid=(GROUPS, s // (QBLOCK * cb)),
        in_specs=[pl.BlockSpec((1, REP * HEAD_DIM, QBLOCK * cb), lambda g, c: (g, 0, c)),
                  pl.BlockSpec((1, 1, n16, HEAD_DIM), lambda g, c: (0, g, 0, 0)),
                  pl.BlockSpec((1, 1, HEAD_DIM, n16), lambda g, c: (1, g, 0, 0))],
        out_specs=[pl.BlockSpec((1, REP * HEAD_DIM, QBLOCK * cb), lambda g, c: (g, 0, c)),
                   pl.BlockSpec((1, nsel, QBLOCK * cb), lambda g, c: (g, 0, c))],
        out_shape=[jax.ShapeDtypeStruct((GROUPS, REP * HEAD_DIM, s), BF16),
                   jax.ShapeDtypeStruct((GROUPS, nsel, s), BF16)],
        scratch_shapes=[pltpu.VMEM((cb, n16 + 8, QBLOCK), F32), pltpu.VMEM((cb, n16, HQ), F32)],
        compiler_params=_params("arbitrary", "arbitrary"),
        name="cmp_select",
    )(q, kc_all, vct_all)


def _fill_bias_strip(strip_ref, thr_ref, tab_ref, g, *, dist0, lo, hi, minus_far):
    nrows = strip_ref.shape[0]

    def block(bi, carry):
        i0 = pl.multiple_of(bi * QBLOCK, QBLOCK)
        d = (lax.broadcasted_iota(jnp.int32, (QBLOCK, QBLOCK), 1)
             - lax.broadcasted_iota(jnp.int32, (QBLOCK, QBLOCK), 0) + (dist0 - i0))
        ok = d >= lo if hi is None else (d >= lo) & (d < hi)
        for r in range(REP):
            base = (g * REP + r) * NUM_BUCKETS
            out = jnp.full((QBLOCK, QBLOCK), tab_ref[base], F32)
            for b in range(1, NUM_BUCKETS):
                out = jnp.where(d >= thr_ref[b], tab_ref[base + b], out)
            if minus_far:
                out = out - tab_ref[base + NUM_BUCKETS - 1]
            strip_ref[pl.ds(i0, QBLOCK), r * QBLOCK:(r + 1) * QBLOCK] = jnp.where(ok, out, -MASK_BIG)
        return carry

    lax.fori_loop(0, nrows // QBLOCK, block, 0)


def _sel_kernel(thr_ref, tab_ref, q_ref, sel_ref, kaug_ref, vt_ref, cst_ref, o_ref,
                qa_ref, s_ref, tmax_ref, m_ref, acc_ref, strip_ref, *, tk):
    g = pl.program_id(0)
    st = pl.program_id(1)
    nb = tk // SEL_BLOCK
    cpt = tk // QBLOCK
    kd = GROUPS * HEAD_DIM
    dmax = 2 * tk - QBLOCK

    @pl.when(st == 0)
    def _():
        _fill_bias_strip(strip_ref, thr_ref, tab_ref, g, dist0=dmax, lo=0, hi=None, minus_far=True)

    qa_ref[...] = jnp.zeros(qa_ref.shape, BF16)
    for b in range(2):
        qa_ref[b, kd + nb:kd + nb + 16, :] = cst_ref[0]
    m_ref[...] = jnp.full(m_ref.shape, -jnp.inf, F32)
    acc_ref[...] = jnp.zeros(acc_ref.shape, F32)
    row0 = pl.multiple_of(g * HEAD_DIM, HEAD_DIM)
    nnear = jnp.minimum(st + 1, 2)
    nfar = st + 1 - nnear

    def run(count, first_tile, near):

        def advance(c, i):
            wrap = i + 1 == count
            return jnp.where(wrap, c + 1, c), jnp.where(wrap, 0, i + 1)

        def qk(c, i, buf):
            cc = jnp.minimum(c, cpt - 1)
            j = first_tile + i
            lane0 = pl.multiple_of(cc * QBLOCK, QBLOCK)
            for r in range(REP):
                qa_ref[buf, pl.ds(row0, HEAD_DIM), r * QBLOCK:(r + 1) * QBLOCK] = (
                    q_ref[0, r * HEAD_DIM:(r + 1) * HEAD_DIM, pl.ds(lane0, QBLOCK)])
            sb = sel_ref[0, pl.ds(pl.multiple_of(j * nb, nb), nb), pl.ds(lane0, QBLOCK)]
            for r in range(REP):
                qa_ref[buf, kd:kd + nb, r * QBLOCK:(r + 1) * QBLOCK] = sb
            k0 = pl.multiple_of(j * tk, tk)
            s = jnp.dot(kaug_ref[pl.ds(k0, tk), :], qa_ref[buf], preferred_element_type=F32)
            if near:
                delta = (st - j) * tk + cc * QBLOCK
                s = s + strip_ref[pl.ds(pl.multiple_of(dmax - delta, QBLOCK), tk), :]
            s_ref[buf] = s
            tmax_ref[buf] = jnp.max(s, axis=0, keepdims=True)

        def softmax_pv(c, i, buf):
            k0 = pl.multiple_of((first_tile + i) * tk, tk)
            m_prev = m_ref[c]
            m_new = jnp.maximum(m_prev, tmax_ref[buf])
            alpha = jnp.exp2(m_prev - m_new)
            p = jnp.exp2(s_ref[buf] - m_new)
            acc = alpha * acc_ref[c] + jnp.dot(vt_ref[0, :, pl.ds(k0, tk)], p.astype(BF16),
                                               preferred_element_type=F32)
            acc_ref[c] = acc
            m_ref[c] = m_new
            if near:
                lane0 = pl.multiple_of(c * QBLOCK, QBLOCK)
                out = acc[0:HEAD_DIM] * (1.0 / jnp.maximum(acc[HEAD_DIM:HEAD_DIM + 1], 1e-30))
                for r in range(REP):
                    o_ref[0, r * HEAD_DIM:(r + 1) * HEAD_DIM, pl.ds(lane0, QBLOCK)] = (
                        out[:, r * QBLOCK:(r + 1) * QBLOCK].astype(o_ref.dtype))

        zero = jnp.int32(0)
        qk(zero, zero, 0)

        def trip(_, carry):
            cur = carry
            for k in range(TILES_PER_TRIP):
                nxt = advance(*cur)
                qk(*nxt, (k + 1) % 2)
                softmax_pv(*cur, k % 2)
                cur = nxt
            return cur

        lax.fori_loop(0, (cpt * count) // TILES_PER_TRIP, trip, (zero, zero))

    @pl.when(nfar > 0)
    def _():
        run(nfar, 0, False)

    run(nnear, nfar, True)


def _selected(thr, tab, q, sel, kaug, vt, cst, s, tk):
    nsel = s // SEL_BLOCK
    sl = 3 * tk - QBLOCK
    cpt = tk // QBLOCK
    smem = pl.BlockSpec(memory_space=pltpu.SMEM)
    return pl.pallas_call(
        functools.partial(_sel_kernel, tk=tk),
        grid=(GROUPS, s // tk),
        in_specs=[smem, smem,
                  pl.BlockSpec((1, REP * HEAD_DIM, tk), lambda g, c: (g, 0, c)),
                  pl.BlockSpec((1, nsel, tk), lambda g, c: (g, 0, c)),
                  pl.BlockSpec(kaug.shape, lambda g, c: (0, 0)),
                  pl.BlockSpec((1, vt.shape[1], s), lambda g, c: (g, 0, 0)),
                  pl.BlockSpec((1, 16, HQ), lambda g, c: (g, 0, 0))],
        out_specs=pl.BlockSpec((1, REP * HEAD_DIM, tk), lambda g, c: (g, 0, c)),
        out_shape=jax.ShapeDtypeStruct((GROUPS, REP * HEAD_DIM, s), BF16),
        scratch_shapes=[pltpu.VMEM((2, kaug.shape[1], HQ), BF16), pltpu.VMEM((2, tk, HQ), F32),
                        pltpu.VMEM((2, 1, HQ), F32),
                        pltpu.VMEM((cpt, 1, HQ), F32), pltpu.VMEM((cpt, vt.shape[1], HQ), F32),
                        pltpu.VMEM((sl, HQ), F32)],
        compiler_params=_params("arbitrary", "arbitrary"),
        name="selected_attn",
    )(thr, tab, q, sel, kaug, vt, cst)


def _band_kernel(thr_ref, tab_ref, q_ref, k_ref, vt_ref, sink_ref, o_ref, qa_ref, strip_ref, s_ref, *,
                 window, use_sink, cb):
    g = pl.program_id(0)

    @pl.when(pl.program_id(1) == 0)
    def _():
        _fill_bias_strip(strip_ref, thr_ref, tab_ref, g, dist0=window, lo=0, hi=window, minus_far=False)

    base = pl.multiple_of(pl.program_id(1) * (QBLOCK * cb), QBLOCK * cb)
    nk = window + QBLOCK
    qa_ref[...] = jnp.zeros(qa_ref.shape, BF16)
    row0 = pl.multiple_of(g * HEAD_DIM, HEAD_DIM)
    for c in range(cb):
        for r in range(REP):
            qa_ref[c, pl.ds(row0, HEAD_DIM), r * QBLOCK:(r + 1) * QBLOCK] = (
                q_ref[0, r * HEAD_DIM:(r + 1) * HEAD_DIM, c * QBLOCK:(c + 1) * QBLOCK])

    def scores(c, first_step):
        t0 = base + c * QBLOCK
        s = jnp.dot(k_ref[pl.ds(t0, nk), :], qa_ref[c], preferred_element_type=F32) + strip_ref[...]
        if first_step and c * QBLOCK < window:
            row = lax.broadcasted_iota(jnp.int32, (nk, HQ), 0)
            s = jnp.where(row >= window - c * QBLOCK, s, -MASK_BIG)
        s_ref[c] = s
        m = jnp.max(s, axis=0, keepdims=True)
        return jnp.maximum(m, sink_ref[0]) if use_sink else m

    def attend(c, m):
        t0 = base + c * QBLOCK
        e = jnp.exp2(s_ref[c] - m)
        o_t = jnp.dot(vt_ref[0, :, pl.ds(t0, nk)], e.astype(BF16), preferred_element_type=F32)
        den = o_t[HEAD_DIM:HEAD_DIM + 1]
        den = den + jnp.exp2(sink_ref[0] - m) if use_sink else jnp.maximum(den, 1e-30)
        o_t = o_t[0:HEAD_DIM] * (1.0 / den)
        for r in range(REP):
            o_ref[0, r * HEAD_DIM:(r + 1) * HEAD_DIM, c * QBLOCK:(c + 1) * QBLOCK] = (
                o_t[:, r * QBLOCK:(r + 1) * QBLOCK].astype(o_ref.dtype))

    def chunks(first_step):
        m = scores(0, first_step)
        for c in range(cb):
            m_next = scores(c + 1, first_step) if c + 1 < cb else None
            attend(c, m)
            m = m_next

    @pl.when(pl.program_id(1) == 0)
    def _():
        chunks(True)

    @pl.when(pl.program_id(1) > 0)
    def _():
        chunks(False)


def _banded(thr, tab, q, kpad, vtpad, sink, s, window, use_sink, name, cb=8):
    assert QBLOCK * cb >= window
    nk = window + QBLOCK
    smem = pl.BlockSpec(memory_space=pltpu.SMEM)
    return pl.pallas_call(
        functools.partial(_band_kernel, window=window, use_sink=use_sink, cb=cb),
        grid=(GROUPS, s // (QBLOCK * cb)),
        in_specs=[smem, smem,
                  pl.BlockSpec((1, REP * HEAD_DIM, QBLOCK * cb), lambda g, c: (g, 0, c)),
                  pl.BlockSpec(kpad.shape, lambda g, c: (0, 0)),
                  pl.BlockSpec((1, vtpad.shape[1], s + window), lambda g, c: (g, 0, 0)),
                  pl.BlockSpec((1, 1, HQ), lambda g, c: (g, 0, 0))],
        out_specs=pl.BlockSpec((1, REP * HEAD_DIM, QBLOCK * cb), lambda g, c: (g, 0, c)),
        out_shape=jax.ShapeDtypeStruct((GROUPS, REP * HEAD_DIM, s), BF16),
        scratch_shapes=[pltpu.VMEM((cb, GROUPS * HEAD_DIM, HQ), BF16), pltpu.VMEM((nk, HQ), F32),
                        pltpu.VMEM((cb, nk, HQ), F32)],
        compiler_params=_params("arbitrary", "arbitrary"),
        name=name,
    )(thr, tab, q, kpad, vtpad, sink)


def _merge_kernel(oc_ref, os_ref, ow_ref, ob_ref, gn_ref, gab_ref, x_ref, wa_ref, wb_ref, wo_ref, gp_ref, o_ref):
    d = x_ref.shape[1]
    gn = jax.nn.sigmoid(gn_ref[...])
    parts = []
    for h in range(HEADS):
        rows = slice(h * HEAD_DIM, (h + 1) * HEAD_DIM)
        parts.append(gn[h:h + 1] * oc_ref[rows, :].astype(F32)
                     + gn[HEADS + h:HEADS + h + 1] * os_ref[rows, :].astype(F32)
                     + gn[2 * HEADS + h:2 * HEADS + h + 1] * ow_ref[rows, :].astype(F32))
    oa_t = jnp.concatenate(parts, axis=0).astype(BF16)
    ya = lax.dot_general(oa_t, wa_ref[...], _TN, preferred_element_type=F32)
    yb = lax.dot_general(ob_ref[...], wb_ref[...], _TN, preferred_element_type=F32)
    y = (jax.nn.sigmoid(gab_ref[:, :d].astype(F32)) * ya
         + jax.nn.sigmoid(gab_ref[:, d:].astype(F32)) * yb)
    z = jnp.dot(y.astype(BF16), wo_ref[...], preferred_element_type=F32)
    o_ref[...] = x_ref[...] + _rms(z, gp_ref[...])


def _merge(oc, os_, ow, ob, gn, gab, x, wa, wb, wo, gpost, ts):
    s, d = x.shape
    hd = oc.shape[0]
    tok = lambda rows: pl.BlockSpec((rows, ts), lambda i: (0, i))
    full = lambda shp: pl.BlockSpec(shp, lambda i: (0, 0))
    return pl.pallas_call(
        _merge_kernel,
        grid=(s // ts,),
        in_specs=[tok(hd), tok(hd), tok(hd), tok(hd), tok(gn.shape[0]),
                  pl.BlockSpec((ts, 2 * d), lambda i: (i, 0)), pl.BlockSpec((ts, d), lambda i: (i, 0)),
                  full((hd, d)), full((hd, d)), full((d, d)), full((1, d))],
        out_specs=pl.BlockSpec((ts, d), lambda i: (i, 0)),
        out_shape=jax.ShapeDtypeStruct((s, d), F32),
        compiler_params=_params("arbitrary"),
        name="merge_out_proj",
    )(oc, os_, ow, ob, gn, gab, x, wa, wb, wo, gpost)


def _mlp_kernel(h_ref, gpre_ref, w1_ref, w2_ref, gpost_ref, o_ref, u_ref, acc_ref):
    f = pl.program_id(1)

    @pl.when(f == 0)
    def _():
        u_ref[...] = _rms(h_ref[...], gpre_ref[...]).astype(BF16)
        acc_ref[...] = jnp.zeros(acc_ref.shape, F32)

    a = jnp.maximum(jnp.dot(u_ref[...], w1_ref[...], preferred_element_type=F32), 0.0)
    acc_ref[...] += jnp.dot((a * a).astype(BF16), w2_ref[...], preferred_element_type=F32)

    @pl.when(f == pl.num_programs(1) - 1)
    def _():
        o_ref[...] = h_ref[...] + _rms(acc_ref[...], gpost_ref[...])


def _mlp(h, gpre, w1, w2, gpost, ts, tf):
    s, d = h.shape
    dff = w1.shape[1]
    return pl.pallas_call(
        _mlp_kernel,
        grid=(s // ts, dff // tf),
        in_specs=[pl.BlockSpec((ts, d), lambda i, f: (i, 0)), pl.BlockSpec((1, d), lambda i, f: (0, 0)),
                  pl.BlockSpec((d, tf), lambda i, f: (0, f)), pl.BlockSpec((tf, d), lambda i, f: (f, 0)),
                  pl.BlockSpec((1, d), lambda i, f: (0, 0))],
        out_specs=pl.BlockSpec((ts, d), lambda i, f: (i, 0)),
        out_shape=jax.ShapeDtypeStruct((s, d), F32),
        scratch_shapes=[pltpu.VMEM((ts, d), BF16), pltpu.VMEM((ts, d), F32)],
        compiler_params=_params("arbitrary", "arbitrary"),
        name="mlp_relu2",
    )(h, gpre, w1, w2, gpost)


def kernel(x, norm_mix_pre, norm_mix_post, norm_mlp_pre, norm_mlp_post, w_in,
           cmp_pos_k, cmp_w1_k, cmp_w2_k, cmp_pos_v, cmp_w1_v, cmp_w2_v,
           attn_sinks, rel_bias, w_up_nsa, w_up_swa, w_out, w_mlp_in, w_mlp_out):
    b, s, d = x.shape
    assert b == 1 and s % SEL_TILE == 0 and w_in.shape[0] == 1
    qd = HEADS * HEAD_DIM
    kvd = GROUPS * HEAD_DIM
    sizes = (qd,) + (kvd,) * 6 + (3 * HEADS, qd, kvd, kvd, d, d)
    offs = [0]
    for z in sizes:
        offs.append(offs[-1] + z)
    w = w_in[0]
    col = lambda i: w[:, offs[i]:offs[i + 1]]
    (w_qn, w_kc, w_vc, w_ks, w_vs, w_kw, w_vw, w_gn, w_qs, w_k_s, w_v_s, w_ga, w_gb) = [col(i) for i in range(13)]
    scale = HEAD_DIM ** -0.5 * LOG2E
    w_gn = w_gn.reshape(d, HEADS, 3).transpose(0, 2, 1).reshape(d, 3 * HEADS)
    w_gn = jnp.pad(w_gn, ((0, 0), (0, 8)))
    wn = jnp.concatenate([w_kc, w_vc, w_ks, w_kw, w_k_s], axis=1).astype(BF16)
    wg = jnp.concatenate([w_ga, w_gb], axis=1).astype(BF16)
    wt = jnp.concatenate([w_qn * scale, w_qs * scale, w_vs, w_vw, w_v_s, w_gn], axis=1).T.astype(BF16)
    nqv = 2 * qd + 3 * kvd

    x2 = x[0]
    kv, gab, qv, gn, kb = _project(x2, norm_mix_pre, wn, wg, wt, nqv, ts=512)
    q_nsa = qv[0:qd].reshape(GROUPS, REP * HEAD_DIM, s)
    q_swa = qv[qd:2 * qd].reshape(GROUPS, REP * HEAD_DIM, s)
    vs_t = qv[2 * qd:2 * qd + kvd].reshape(GROUPS, HEAD_DIM, s)
    vw_t = qv[2 * qd + kvd:2 * qd + 2 * kvd].reshape(GROUPS, HEAD_DIM, s)
    vswa_t = qv[2 * qd + 2 * kvd:].reshape(GROUPS, HEAD_DIM, s)

    n16 = s // CMP_STRIDE
    pos8 = jnp.stack([cmp_pos_k[0], cmp_pos_v[0]]).reshape(2, 1, CMP_BLOCK * HEAD_DIM)
    pos8 = jnp.broadcast_to(pos8, (2, 8, CMP_BLOCK * HEAD_DIM)).astype(BF16)
    w1 = jnp.stack([cmp_w1_k[0], cmp_w1_v[0]]).astype(BF16)
    w2 = jnp.stack([cmp_w2_k[0], cmp_w2_v[0]]).astype(BF16)
    cmp_n, cmp_t = _compress(kb, pos8, w1, w2, w2.transpose(0, 2, 1))

    oc, sel = _cmp_select(q_nsa, cmp_n, cmp_t, s)

    bias_nsa = rel_bias[:, :HEADS].reshape(NUM_BUCKETS, GROUPS, REP) * LOG2E
    bias_swa = rel_bias[:, HEADS:].reshape(NUM_BUCKETS, GROUPS, REP) * LOG2E
    thr = jnp.asarray(_bucket_starts())
    tab_nsa = bias_nsa.transpose(1, 2, 0).reshape(-1)
    tab_swa = bias_swa.transpose(1, 2, 0).reshape(-1)

    tk = SEL_TILE
    nb = tk // SEL_BLOCK
    posn = jnp.arange(s)
    onehot = ((posn // SEL_BLOCK) % nb)[:, None] == jnp.arange(nb)[None, :]
    aug = jnp.concatenate([onehot.astype(BF16), jnp.ones((s, 2), BF16),
                           jnp.zeros((s, kvd - nb - 2), BF16)], axis=1)
    kaug = jnp.concatenate([kv[:, 2 * kvd:3 * kvd], aug], axis=1)
    far = bias_nsa[NUM_BUCKETS - 1]
    far_hi = far.astype(BF16)
    far_lo = (far - far_hi.astype(F32)).astype(BF16)
    cst = jnp.stack([far_hi, far_lo], axis=1)
    cst = jnp.broadcast_to(cst[:, :, :, None], (GROUPS, 2, REP, QBLOCK)).reshape(GROUPS, 2, HQ)
    cst = jnp.pad(cst, ((0, 0), (0, 14), (0, 0)))
    ones_row = jnp.concatenate([jnp.ones((GROUPS, 1, s), BF16), jnp.zeros((GROUPS, 15, s), BF16)], axis=1)
    vs_aug = jnp.concatenate([vs_t, ones_row], axis=1)
    o_sel = _selected(thr, tab_nsa, q_nsa, sel, kaug, vs_aug, cst, s, tk)

    pad_rows = lambda a, wdw: jnp.pad(a, ((wdw, 0), (0, 0)))
    pad_lanes = lambda a, wdw: jnp.pad(jnp.concatenate([a, ones_row], axis=1), ((0, 0), (0, 0), (wdw, 0)))
    no_sink = jnp.zeros((GROUPS, 1, HQ), F32)
    o_win = _banded(thr, tab_nsa, q_nsa, pad_rows(kv[:, 3 * kvd:4 * kvd], NSA_WINDOW), pad_lanes(vw_t, NSA_WINDOW),
                    no_sink, s, NSA_WINDOW, False, "nsa_window")
    sink = attn_sinks[0].reshape(GROUPS, 1, REP, 1).astype(F32) * LOG2E
    sink = jnp.broadcast_to(sink, (GROUPS, 1, REP, QBLOCK)).reshape(GROUPS, 1, HQ)
    o_swa = _banded(thr, tab_swa, q_swa, pad_rows(kv[:, 4 * kvd:5 * kvd], SWA_WINDOW), pad_lanes(vswa_t, SWA_WINDOW),
                    sink, s, SWA_WINDOW, True, "swa_sink")

    flat = lambda a: a.reshape(qd, s)
    h1 = _merge(flat(oc), flat(o_sel), flat(o_win), flat(o_swa), gn, gab, x2,
                w_up_nsa[0].astype(BF16), w_up_swa[0].astype(BF16), w_out[0].astype(BF16),
                norm_mix_post, ts=512)
    out = _mlp(h1, norm_mlp_pre, w_mlp_in[0].astype(BF16), w_mlp_out[0].astype(BF16), norm_mlp_post,
               ts=1024, tf=2048)
    return out[None]
```

```python
import functools
import math

import jax
import jax.numpy as jnp
import numpy as np
from jax import lax
from jax.experimental import pallas as pl
from jax.experimental.pallas import tpu as pltpu

F32 = jnp.float32
BF16 = jnp.bfloat16

HEAD_DIM = 64
GROUPS = 2
REP = 4
HEADS = GROUPS * REP
CMP_BLOCK = 32
CMP_STRIDE = 16
SEL_BLOCK = 64
SEL_TOPK = 16
NSA_WINDOW = 512
SWA_WINDOW = 128
QBLOCK = 128
NUM_BUCKETS = 32
MAX_DISTANCE = 1024
RMS_EPS = 1e-6
LOG2E = math.log2(math.e)
MASK_BIG = 2.0 ** 99
SEL_TILE = 1024
TILES_PER_TRIP = 8
HQ = REP * QBLOCK
VMEM_LIMIT = 56 * 1024 * 1024

_NT = (((1,), (1,)), ((), ()))
_TN = (((0,), (0,)), ((), ()))


def _params(*sem):
    return pltpu.CompilerParams(dimension_semantics=sem, vmem_limit_bytes=VMEM_LIMIT)


def _bucket_starts():
    max_exact = NUM_BUCKETS // 2
    d = np.arange(2 * MAX_DISTANCE)
    large = max_exact + (np.log(np.maximum(d, 1).astype(np.float64) / max_exact)
                         / math.log(MAX_DISTANCE / max_exact) * (NUM_BUCKETS - max_exact)).astype(np.int64)
    bucket = np.where(d < max_exact, d, np.minimum(large, NUM_BUCKETS - 1))
    return (bucket[None, :] < np.arange(NUM_BUCKETS)[:, None]).sum(axis=1).astype(np.int32)


def _rms(x, gain):
    return x * lax.rsqrt(jnp.mean(x * x, axis=-1, keepdims=True) + RMS_EPS) * gain


def _proj_kernel(x_ref, g_ref, wn_ref, wg_ref, wt_ref, kv_ref, gab_ref, qv_ref, gn_ref, kb_ref, y_ref, *, nqv):
    u = _rms(x_ref[...], g_ref[...]).astype(BF16)
    kvf = jnp.dot(u, wn_ref[...], preferred_element_type=F32)
    kv_ref[...] = kvf.astype(BF16)
    gw = GROUPS * HEAD_DIM
    nrow = y_ref.shape[1] // CMP_STRIDE
    for a in range(2):
        y_ref[a] = kvf[:, a * gw:(a + 1) * gw]
        both = [y_ref[a, pl.ds(p, nrow, stride=CMP_STRIDE), :] for p in range(CMP_STRIDE)]
        for gi in range(GROUPS):
            kb_ref[a, gi] = jnp.concatenate([t[:, gi * HEAD_DIM:(gi + 1) * HEAD_DIM] for t in both],
                                            axis=1).astype(BF16)
    gab_ref[...] = jnp.dot(u, wg_ref[...], preferred_element_type=F32).astype(BF16)
    t = lax.dot_general(wt_ref[...], u, _NT, preferred_element_type=F32)
    qv_ref[...] = t[:nqv].astype(BF16)
    gn_ref[...] = t[nqv:]


def _project(x, gain, wn, wg, wt, nqv, ts):
    s, d = x.shape
    nn, ng, ntr = wn.shape[1], wg.shape[1], wt.shape[0]
    full = lambda shp: pl.BlockSpec(shp, lambda i: (0, 0))
    return pl.pallas_call(
        functools.partial(_proj_kernel, nqv=nqv),
        grid=(s // ts,),
        in_specs=[pl.BlockSpec((ts, d), lambda i: (i, 0)), full((1, d)),
                  full((d, nn)), full((d, ng)), full((ntr, d))],
        out_specs=[pl.BlockSpec((ts, nn), lambda i: (i, 0)),
                   pl.BlockSpec((ts, ng), lambda i: (i, 0)),
                   pl.BlockSpec((nqv, ts), lambda i: (0, i)),
                   pl.BlockSpec((ntr - nqv, ts), lambda i: (0, i)),
                   pl.BlockSpec((2, GROUPS, ts // CMP_STRIDE, CMP_STRIDE * HEAD_DIM), lambda i: (0, 0, i, 0))],
        out_shape=[jax.ShapeDtypeStruct((s, nn), BF16), jax.ShapeDtypeStruct((s, ng), BF16),
                   jax.ShapeDtypeStruct((nqv, s), BF16), jax.ShapeDtypeStruct((ntr - nqv, s), F32),
                   jax.ShapeDtypeStruct((2, GROUPS, s // CMP_STRIDE, CMP_STRIDE * HEAD_DIM), BF16)],
        scratch_shapes=[pltpu.VMEM((2, ts, GROUPS * HEAD_DIM), F32)],
        compiler_params=_params("arbitrary"),
        name="in_proj",
    )(x, gain, wn, wg, wt)


def _gelu_tanh(x):
    return x * (0.5 * (1.0 + jnp.tanh(math.sqrt(2.0 / math.pi) * (x + 0.044715 * (x * x * x)))))


def _compress_kernel(kb_ref, pos_ref, w1_ref, w2_ref, w2t_ref, o_ref, ot_ref):
    kb = kb_ref[0, 0]
    half = kb.shape[1]
    n16 = kb.shape[0]
    first = jnp.dot(kb, w1_ref[0, :half, :], preferred_element_type=F32)
    second = jnp.dot(kb, w1_ref[0, half:, :], preferred_element_type=F32)
    posb = jnp.dot(pos_ref[0], w1_ref[0], preferred_element_type=F32)[0:1]
    pre = first + pltpu.roll(second, n16 - 1, 0) + posb
    h = _gelu_tanh(pre).astype(BF16)
    o_ref[0, 0] = jnp.dot(h, w2_ref[0], preferred_element_type=F32).astype(BF16)
    ot_ref[0, 0] = lax.dot_general(w2t_ref[0], h, _NT, preferred_element_type=F32).astype(BF16)


def _compress(kb, pos8, w1, w2, w2t):
    _, g, n16, cin2 = kb.shape
    hid = w1.shape[2]
    return pl.pallas_call(
        _compress_kernel,
        grid=(2, g),
        in_specs=[pl.BlockSpec((1, 1, n16, cin2), lambda a, b: (a, b, 0, 0)),
                  pl.BlockSpec((1, 8, 2 * cin2), lambda a, b: (a, 0, 0)),
                  pl.BlockSpec((1, 2 * cin2, hid), lambda a, b: (a, 0, 0)),
                  pl.BlockSpec((1, hid, HEAD_DIM), lambda a, b: (a, 0, 0)),
                  pl.BlockSpec((1, HEAD_DIM, hid), lambda a, b: (a, 0, 0))],
        out_specs=[pl.BlockSpec((1, 1, n16, HEAD_DIM), lambda a, b: (a, b, 0, 0)),
                   pl.BlockSpec((1, 1, HEAD_DIM, n16), lambda a, b: (a, b, 0, 0))],
        out_shape=[jax.ShapeDtypeStruct((2, g, n16, HEAD_DIM), BF16),
                   jax.ShapeDtypeStruct((2, g, HEAD_DIM, n16), BF16)],
        compiler_params=_params("arbitrary", "arbitrary"),
        name="kv_compress",
    )(kb, pos8, w1, w2, w2t)


def _cmpsel_body(q_ref, kc_ref, vct_ref, oc_ref, sel_ref, pg_ref, s_ref, t_base, *, cb, rows, nomask, nsel, topk):
    mrows = rows - nomask
    ratio = SEL_BLOCK // CMP_STRIDE
    nblk = rows // ratio
    n_io = nomask + lax.broadcasted_iota(jnp.int32, (mrows, HQ), 0)
    lane_t = lax.broadcasted_iota(jnp.int32, (mrows, HQ), 1) & (QBLOCK - 1)
    blk = lax.broadcasted_iota(jnp.int32, (nblk, QBLOCK), 0)

    def raw_scores(c):
        t0 = t_base + c * QBLOCK
        lanes = slice(c * QBLOCK, (c + 1) * QBLOCK)
        q4 = jnp.concatenate([q_ref[0, r * HEAD_DIM:(r + 1) * HEAD_DIM, lanes] for r in range(REP)], axis=1)
        s = jnp.dot(kc_ref[0, 0, 0:rows, :], q4, preferred_element_type=F32)
        edge = jnp.where(n_io * CMP_STRIDE + (CMP_BLOCK - 1) <= t0 + lane_t, s[nomask:], -jnp.inf)
        s = jnp.concatenate([s[:nomask], edge], axis=0) if nomask else edge
        s_ref[c, 0:rows, :] = s
        m = jnp.max(s, axis=0, keepdims=True)
        return jnp.where(m == -jnp.inf, 0.0, m)

    def attend(c, m):
        t0 = t_base + c * QBLOCK
        lanes = slice(c * QBLOCK, (c + 1) * QBLOCK)
        e = jnp.exp2(s_ref[c, 0:rows, :] - m)
        den = jnp.maximum(jnp.sum(e, axis=0, keepdims=True), 1e-30)
        p = e * (1.0 / den)
        o_t = jnp.dot(vct_ref[0, 0, :, 0:rows], p.astype(BF16), preferred_element_type=F32)
        for r in range(REP):
            oc_ref[0, r * HEAD_DIM:(r + 1) * HEAD_DIM, lanes] = (
                o_t[:, r * QBLOCK:(r + 1) * QBLOCK].astype(oc_ref.dtype))

        pg = ((p[:, 0:QBLOCK] + p[:, QBLOCK:2 * QBLOCK]) + p[:, 2 * QBLOCK:3 * QBLOCK]) + p[:, 3 * QBLOCK:]
        pg_ref[c, 0:8, :] = jnp.zeros((8, QBLOCK), F32)
        pg_ref[c, 8:8 + rows, :] = pg
        ps = pg_ref[c, pl.ds(7, nblk, stride=ratio), :]
        for o in range(1, ratio + 1):
            ps = ps + pg_ref[c, pl.ds(7 + o, nblk, stride=ratio), :]
        cur = (t0 + lax.broadcasted_iota(jnp.int32, (nblk, QBLOCK), 1)) // SEL_BLOCK
        valid = blk <= cur
        forced = (blk == 0) | (blk == cur) | (blk == cur - 1)
        return jnp.where(valid & ~forced, ps, -jnp.inf), valid

    scores, valids = [], []
    m = raw_scores(0)
    for c in range(cb):
        m_next = raw_scores(c + 1) if c + 1 < cb else None
        score, valid = attend(c, m)
        scores.append(score)
        valids.append(valid)
        m = m_next

    def pick_one(_, rems):
        out = []
        for rem in rems:
            top = jnp.max(rem, axis=0, keepdims=True)
            first = jnp.min(jnp.where(rem == top, blk, nblk), axis=0, keepdims=True)
            out.append(jnp.where(blk == first, -jnp.inf, rem))
        return tuple(out)

    rems = lax.fori_loop(0, max(topk - 3, 0), pick_one, tuple(scores))
    for c in range(cb):
        lanes = slice(c * QBLOCK, (c + 1) * QBLOCK)
        sel_ref[0, 0:nblk, lanes] = jnp.where((rems[c] == -jnp.inf) & valids[c], 0.0, -MASK_BIG).astype(BF16)
        if nblk < nsel:
            sel_ref[0, nblk:, lanes] = jnp.full((nsel - nblk, QBLOCK), -MASK_BIG, BF16)


def _cmpsel_kernel(q_ref, kc_ref, vct_ref, oc_ref, sel_ref, pg_ref, s_ref, *, n16, nsel, topk, nvar, cb):
    step = pl.program_id(1)
    per = n16 // nvar
    spv = per // (QBLOCK // CMP_STRIDE) // cb
    for k in range(1, nvar + 1):
        @pl.when((step >= (k - 1) * spv) & (step < k * spv))
        def _(k=k):
            _cmpsel_body(q_ref, kc_ref, vct_ref, oc_ref, sel_ref, pg_ref, s_ref, step * (cb * QBLOCK), cb=cb,
                         rows=per * k, nomask=max(per * (k - 1) - 8, 0), nsel=nsel, topk=topk)


def _cmp_select(q, kc_all, vct_all, s, cb=4):
    n16 = s // CMP_STRIDE
    nsel = s // SEL_BLOCK
    topk = min(SEL_TOPK, nsel)
    nvar = max(n16 // 128, 1)
    return pl.pallas_call(
        functools.partial(_cmpsel_kernel, n16=n16, nsel=nsel, topk=topk, nvar=nvar, cb=cb),
        grid=(GROUPS, s // (QBLOCK * cb)),
        in_specs=[pl.BlockSpec((1, REP * HEAD_DIM, QBLOCK * cb), lambda g, c: (g, 0, c)),
                  pl.BlockSpec((1, 1, n16, HEAD_DIM), lambda g, c: (0, g, 0, 0)),
                  pl.BlockSpec((1, 1, HEAD_DIM, n16), lambda g, c: (1, g, 0, 0))],
        out_specs=[pl.BlockSpec((1, REP * HEAD_DIM, QBLOCK * cb), lambda g, c: (g, 0, c)),
                   pl.BlockSpec((1, nsel, QBLOCK * cb), lambda g, c: (g, 0, c))],
        out_shape=[jax.ShapeDtypeStruct((GROUPS, REP * HEAD_DIM, s), BF16),
                   jax.ShapeDtypeStruct((GROUPS, nsel, s), BF16)],
        scratch_shapes=[pltpu.VMEM((cb, n16 + 8, QBLOCK), F32), pltpu.VMEM((cb, n16, HQ), F32)],
        compiler_params=_params("arbitrary", "arbitrary"),
        name="cmp_select",
    )(q, kc_all, vct_all)


def _fill_bias_strip(strip_ref, thr_ref, tab_ref, g, *, dist0, lo, hi, minus_far):
    nrows = strip_ref.shape[0]

    def block(bi, carry):
        i0 = pl.multiple_of(bi * QBLOCK, QBLOCK)
        d = (lax.broadcasted_iota(jnp.int32, (QBLOCK, QBLOCK), 1)
             - lax.broadcasted_iota(jnp.int32, (QBLOCK, QBLOCK), 0) + (dist0 - i0))
        ok = d >= lo if hi is None else (d >= lo) & (d < hi)
        for r in range(REP):
            base = (g * REP + r) * NUM_BUCKETS
            out = jnp.full((QBLOCK, QBLOCK), tab_ref[base], F32)
            for b in range(1, NUM_BUCKETS):
                out = jnp.where(d >= thr_ref[b], tab_ref[base + b], out)
            if minus_far:
                out = out - tab_ref[base + NUM_BUCKETS - 1]
            strip_ref[pl.ds(i0, QBLOCK), r * QBLOCK:(r + 1) * QBLOCK] = jnp.where(ok, out, -MASK_BIG)
        return carry

    lax.fori_loop(0, nrows // QBLOCK, block, 0)


def _sel_kernel(thr_ref, tab_ref, q_ref, sel_ref, kaug_ref, vt_ref, cst_ref, o_ref,
                qa_ref, s_ref, tmax_ref, m_ref, acc_ref, strip_ref, *, tk):
    g = pl.program_id(0)
    st = pl.program_id(1)
    nb = tk // SEL_BLOCK
    cpt = tk // QBLOCK
    kd = GROUPS * HEAD_DIM
    dmax = 2 * tk - QBLOCK

    @pl.when(st == 0)
    def _():
        _fill_bias_strip(strip_ref, thr_ref, tab_ref, g, dist0=dmax, lo=0, hi=None, minus_far=True)

    qa_ref[...] = jnp.zeros(qa_ref.shape, BF16)
    for b in range(2):
        qa_ref[b, kd + nb:kd + nb + 16, :] = cst_ref[0]
    m_ref[...] = jnp.full(m_ref.shape, -jnp.inf, F32)
    acc_ref[...] = jnp.zeros(acc_ref.shape, F32)
    row0 = pl.multiple_of(g * HEAD_DIM, HEAD_DIM)
    nnear = jnp.minimum(st + 1, 2)
    nfar = st + 1 - nnear

    def run(count, first_tile, near):

        def advance(c, i):
            wrap = i + 1 == count
            return jnp.where(wrap, c + 1, c), jnp.where(wrap, 0, i + 1)

        def qk(c, i, buf):
            cc = jnp.minimum(c, cpt - 1)
            j = first_tile + i
            lane0 = pl.multiple_of(cc * QBLOCK, QBLOCK)
            for r in range(REP):
                qa_ref[buf, pl.ds(row0, HEAD_DIM), r * QBLOCK:(r + 1) * QBLOCK] = (
                    q_ref[0, r * HEAD_DIM:(r + 1) * HEAD_DIM, pl.ds(lane0, QBLOCK)])
            sb = sel_ref[0, pl.ds(pl.multiple_of(j * nb, nb), nb), pl.ds(lane0, QBLOCK)]
            for r in range(REP):
                qa_ref[buf, kd:kd + nb, r * QBLOCK:(r + 1) * QBLOCK] = sb
            k0 = pl.multiple_of(j * tk, tk)
            s = jnp.dot(kaug_ref[pl.ds(k0, tk), :], qa_ref[buf], preferred_element_type=F32)
            if near:
                delta = (st - j) * tk + cc * QBLOCK
                s = s + strip_ref[pl.ds(pl.multiple_of(dmax - delta, QBLOCK), tk), :]
            s_ref[buf] = s
            tmax_ref[buf] = jnp.max(s, axis=0, keepdims=True)

        def softmax_pv(c, i, buf):
            k0 = pl.multiple_of((first_tile + i) * tk, tk)
            m_prev = m_ref[c]
            m_new = jnp.maximum(m_prev, tmax_ref[buf])
            alpha = jnp.exp2(m_prev - m_new)
            p = jnp.exp2(s_ref[buf] - m_new)
            acc = alpha * acc_ref[c] + jnp.dot(vt_ref[0, :, pl.ds(k0, tk)], p.astype(BF16),
                                               preferred_element_type=F32)
            acc_ref[c] = acc
            m_ref[c] = m_new
            if near:
                lane0 = pl.multiple_of(c * QBLOCK, QBLOCK)
                out = acc[0:HEAD_DIM] * (1.0 / jnp.maximum(acc[HEAD_DIM:HEAD_DIM + 1], 1e-30))
                for r in range(REP):
                    o_ref[0, r * HEAD_DIM:(r + 1) * HEAD_DIM, pl.ds(lane0, QBLOCK)] = (
                        out[:, r * QBLOCK:(r + 1) * QBLOCK].astype(o_ref.dtype))

        zero = jnp.int32(0)
        qk(zero, zero, 0)

        def trip(_, carry):
            cur = carry
            for k in range(TILES_PER_TRIP):
                nxt = advance(*cur)
                qk(*nxt, (k + 1) % 2)
                softmax_pv(*cur, k % 2)
                cur = nxt
            return cur

        lax.fori_loop(0, (cpt * count) // TILES_PER_TRIP, trip, (zero, zero))

    @pl.when(nfar > 0)
    def _():
        run(nfar, 0, False)

    run(nnear, nfar, True)


def _selected(thr, tab, q, sel, kaug, vt, cst, s, tk):
    nsel = s // SEL_BLOCK
    sl = 3 * tk - QBLOCK
    cpt = tk // QBLOCK
    smem = pl.BlockSpec(memory_space=pltpu.SMEM)
    return pl.pallas_call(
        functools.partial(_sel_kernel, tk=tk),
        grid=(GROUPS, s // tk),
        in_specs=[smem, smem,
                  pl.BlockSpec((1, REP * HEAD_DIM, tk), lambda g, c: (g, 0, c)),
                  pl.BlockSpec((1, nsel, tk), lambda g, c: (g, 0, c)),
                  pl.BlockSpec(kaug.shape, lambda g, c: (0, 0)),
                  pl.BlockSpec((1, vt.shape[1], s), lambda g, c: (g, 0, 0)),
                  pl.BlockSpec((1, 16, HQ), lambda g, c: (g, 0, 0))],
        out_specs=pl.BlockSpec((1, REP * HEAD_DIM, tk), lambda g, c: (g, 0, c)),
        out_shape=jax.ShapeDtypeStruct((GROUPS, REP * HEAD_DIM, s), BF16),
        scratch_shapes=[pltpu.VMEM((2, kaug.shape[1], HQ), BF16), pltpu.VMEM((2, tk, HQ), F32),
                        pltpu.VMEM((2, 1, HQ), F32),
                        pltpu.VMEM((cpt, 1, HQ), F32), pltpu.VMEM((cpt, vt.shape[1], HQ), F32),
                        pltpu.VMEM((sl, HQ), F32)],
        compiler_params=_params("arbitrary", "arbitrary"),
        name="selected_attn",
    )(thr, tab, q, sel, kaug, vt, cst)


def _band_kernel(thr_ref, tab_ref, q_ref, k_ref, vt_ref, sink_ref, o_ref, qa_ref, strip_ref, s_ref, *,
                 window, use_sink, cb):
    g = pl.program_id(0)

    @pl.when(pl.program_id(1) == 0)
    def _():
        _fill_bias_strip(strip_ref, thr_ref, tab_ref, g, dist0=window, lo=0, hi=window, minus_far=False)

    base = pl.multiple_of(pl.program_id(1) * (QBLOCK * cb), QBLOCK * cb)
    nk = window + QBLOCK
    qa_ref[...] = jnp.zeros(qa_ref.shape, BF16)
    row0 = pl.multiple_of(g * HEAD_DIM, HEAD_DIM)
    for c in range(cb):
        for r in range(REP):
            qa_ref[c, pl.ds(row0, HEAD_DIM), r * QBLOCK:(r + 1) * QBLOCK] = (
                q_ref[0, r * HEAD_DIM:(r + 1) * HEAD_DIM, c * QBLOCK:(c + 1) * QBLOCK])

    def scores(c, first_step):
        t0 = base + c * QBLOCK
        s = jnp.dot(k_ref[pl.ds(t0, nk), :], qa_ref[c], preferred_element_type=F32) + strip_ref[...]
        if first_step and c * QBLOCK < window:
            row = lax.broadcasted_iota(jnp.int32, (nk, HQ), 0)
            s = jnp.where(row >= window - c * QBLOCK, s, -MASK_BIG)
        s_ref[c] = s
        m = jnp.max(s, axis=0, keepdims=True)
        return jnp.maximum(m, sink_ref[0]) if use_sink else m

    def attend(c, m):
        t0 = base + c * QBLOCK
        e = jnp.exp2(s_ref[c] - m)
        o_t = jnp.dot(vt_ref[0, :, pl.ds(t0, nk)], e.astype(BF16), preferred_element_type=F32)
        den = o_t[HEAD_DIM:HEAD_DIM + 1]
        den = den + jnp.exp2(sink_ref[0] - m) if use_sink else jnp.maximum(den, 1e-30)
        o_t = o_t[0:HEAD_DIM] * (1.0 / den)
        for r in range(REP):
            o_ref[0, r * HEAD_DIM:(r + 1) * HEAD_DIM, c * QBLOCK:(c + 1) * QBLOCK] = (
                o_t[:, r * QBLOCK:(r + 1) * QBLOCK].astype(o_ref.dtype))

    def chunks(first_step):
        m = scores(0, first_step)
        for c in range(cb):
            m_next = scores(c + 1, first_step) if c + 1 < cb else None
            attend(c, m)
            m = m_next

    @pl.when(pl.program_id(1) == 0)
    def _():
        chunks(True)

    @pl.when(pl.program_id(1) > 0)
    def _():
        chunks(False)


def _banded(thr, tab, q, kpad, vtpad, sink, s, window, use_sink, name, cb=8):
    assert QBLOCK * cb >= window
    nk = window + QBLOCK
    smem = pl.BlockSpec(memory_space=pltpu.SMEM)
    return pl.pallas_call(
        functools.partial(_band_kernel, window=window, use_sink=use_sink, cb=cb),
        grid=(GROUPS, s // (QBLOCK * cb)),
        in_specs=[smem, smem,
                  pl.BlockSpec((1, REP * HEAD_DIM, QBLOCK * cb), lambda g, c: (g, 0, c)),
                  pl.BlockSpec(kpad.shape, lambda g, c: (0, 0)),
                  pl.BlockSpec((1, vtpad.shape[1], s + window), lambda g, c: (g, 0, 0)),
                  pl.BlockSpec((1, 1, HQ), lambda g, c: (g, 0, 0))],
        out_specs=pl.BlockSpec((1, REP * HEAD_DIM, QBLOCK * cb), lambda g, c: (g, 0, c)),
        out_shape=jax.ShapeDtypeStruct((GROUPS, REP * HEAD_DIM, s), BF16),
        scratch_shapes=[pltpu.VMEM((cb, GROUPS * HEAD_DIM, HQ), BF16), pltpu.VMEM((nk, HQ), F32),
                        pltpu.VMEM((cb, nk, HQ), F32)],
        compiler_params=_params("arbitrary", "arbitrary"),
        name=name,
    )(thr, tab, q, kpad, vtpad, sink)


def _merge_kernel(oc_ref, os_ref, ow_ref, ob_ref, gn_ref, gab_ref, x_ref, wa_ref, wb_ref, wo_ref, gp_ref, o_ref):
    d = x_ref.shape[1]
    gn = jax.nn.sigmoid(gn_ref[...])
    parts = []
    for h in range(HEADS):
        rows = slice(h * HEAD_DIM, (h + 1) * HEAD_DIM)
        parts.append(gn[h:h + 1] * oc_ref[rows, :].astype(F32)
                     + gn[HEADS + h:HEADS + h + 1] * os_ref[rows, :].astype(F32)
                     + gn[2 * HEADS + h:2 * HEADS + h + 1] * ow_ref[rows, :].astype(F32))
    oa_t = jnp.concatenate(parts, axis=0).astype(BF16)
    ya = lax.dot_general(oa_t, wa_ref[...], _TN, preferred_element_type=F32)
    yb = lax.dot_general(ob_ref[...], wb_ref[...], _TN, preferred_element_type=F32)
    y = (jax.nn.sigmoid(gab_ref[:, :d].astype(F32)) * ya
         + jax.nn.sigmoid(gab_ref[:, d:].astype(F32)) * yb)
    z = jnp.dot(y.astype(BF16), wo_ref[...], preferred_element_type=F32)
    o_ref[...] = x_ref[...] + _rms(z, gp_ref[...])


def _merge(oc, os_, ow, ob, gn, gab, x, wa, wb, wo, gpost, ts):
    s, d = x.shape
    hd = oc.shape[0]
    tok = lambda rows: pl.BlockSpec((rows, ts), lambda i: (0, i))
    full = lambda shp: pl.BlockSpec(shp, lambda i: (0, 0))
    return pl.pallas_call(
        _merge_kernel,
        grid=(s // ts,),
        in_specs=[tok(hd), tok(hd), tok(hd), tok(hd), tok(gn.shape[0]),
                  pl.BlockSpec((ts, 2 * d), lambda i: (i, 0)), pl.BlockSpec((ts, d), lambda i: (i, 0)),
                  full((hd, d)), full((hd, d)), full((d, d)), full((1, d))],
        out_specs=pl.BlockSpec((ts, d), lambda i: (i, 0)),
        out_shape=jax.ShapeDtypeStruct((s, d), F32),
        compiler_params=_params("arbitrary"),
        name="merge_out_proj",
    )(oc, os_, ow, ob, gn, gab, x, wa, wb, wo, gpost)


def _mlp_kernel(h_ref, gpre_ref, w1_ref, w2_ref, gpost_ref, o_ref, u_ref, acc_ref):
    f = pl.program_id(1)

    @pl.when(f == 0)
    def _():
        u_ref[...] = _rms(h_ref[...], gpre_ref[...]).astype(BF16)
        acc_ref[...] = jnp.zeros(acc_ref.shape, F32)

    a = jnp.maximum(jnp.dot(u_ref[...], w1_ref[...], preferred_element_type=F32), 0.0)
    acc_ref[...] += jnp.dot((a * a).astype(BF16), w2_ref[...], preferred_element_type=F32)

    @pl.when(f == pl.num_programs(1) - 1)
    def _():
        o_ref[...] = h_ref[...] + _rms(acc_ref[...], gpost_ref[...])


def _mlp(h, gpre, w1, w2, gpost, ts, tf):
    s, d = h.shape
    dff = w1.shape[1]
    return pl.pallas_call(
        _mlp_kernel,
        grid=(s // ts, dff // tf),
        in_specs=[pl.BlockSpec((ts, d), lambda i, f: (i, 0)), pl.BlockSpec((1, d), lambda i, f: (0, 0)),
                  pl.BlockSpec((d, tf), lambda i, f: (0, f)), pl.BlockSpec((tf, d), lambda i, f: (f, 0)),
                  pl.BlockSpec((1, d), lambda i, f: (0, 0))],
        out_specs=pl.BlockSpec((ts, d), lambda i, f: (i, 0)),
        out_shape=jax.ShapeDtypeStruct((s, d), F32),
        scratch_shapes=[pltpu.VMEM((ts, d), BF16), pltpu.VMEM((ts, d), F32)],
        compiler_params=_params("arbitrary", "arbitrary"),
        name="mlp_relu2",
    )(h, gpre, w1, w2, gpost)


def kernel(x, norm_mix_pre, norm_mix_post, norm_mlp_pre, norm_mlp_post, w_in,
           cmp_pos_k, cmp_w1_k, cmp_w2_k, cmp_pos_v, cmp_w1_v, cmp_w2_v,
           attn_sinks, rel_bias, w_up_nsa, w_up_swa, w_out, w_mlp_in, w_mlp_out):
    b, s, d = x.shape
    assert b == 1 and s % SEL_TILE == 0 and w_in.shape[0] == 1
    qd = HEADS * HEAD_DIM
    kvd = GROUPS * HEAD_DIM
    sizes = (qd,) + (kvd,) * 6 + (3 * HEADS, qd, kvd, kvd, d, d)
    offs = [0]
    for z in sizes:
        offs.append(offs[-1] + z)
    w = w_in[0]
    col = lambda i: w[:, offs[i]:offs[i + 1]]
    (w_qn, w_kc, w_vc, w_ks, w_vs, w_kw, w_vw, w_gn, w_qs, w_k_s, w_v_s, w_ga, w_gb) = [col(i) for i in range(13)]
    scale = HEAD_DIM ** -0.5 * LOG2E
    w_gn = w_gn.reshape(d, HEADS, 3).transpose(0, 2, 1).reshape(d, 3 * HEADS)
    w_gn = jnp.pad(w_gn, ((0, 0), (0, 8)))
    wn = jnp.concatenate([w_kc, w_vc, w_ks, w_kw, w_k_s], axis=1).astype(BF16)
    wg = jnp.concatenate([w_ga, w_gb], axis=1).astype(BF16)
    wt = jnp.concatenate([w_qn * scale, w_qs * scale, w_vs, w_vw, w_v_s, w_gn], axis=1).T.astype(BF16)
    nqv = 2 * qd + 3 * kvd

    x2 = x[0]
    kv, gab, qv, gn, kb = _project(x2, norm_mix_pre, wn, wg, wt, nqv, ts=512)
    q_nsa = qv[0:qd].reshape(GROUPS, REP * HEAD_DIM, s)
    q_swa = qv[qd:2 * qd].reshape(GROUPS, REP * HEAD_DIM, s)
    vs_t = qv[2 * qd:2 * qd + kvd].reshape(GROUPS, HEAD_DIM, s)
    vw_t = qv[2 * qd + kvd:2 * qd + 2 * kvd].reshape(GROUPS, HEAD_DIM, s)
    vswa_t = qv[2 * qd + 2 * kvd:].reshape(GROUPS, HEAD_DIM, s)

    n16 = s // CMP_STRIDE
    pos8 = jnp.stack([cmp_pos_k[0], cmp_pos_v[0]]).reshape(2, 1, CMP_BLOCK * HEAD_DIM)
    pos8 = jnp.broadcast_to(pos8, (2, 8, CMP_BLOCK * HEAD_DIM)).astype(BF16)
    w1 = jnp.stack([cmp_w1_k[0], cmp_w1_v[0]]).astype(BF16)
    w2 = jnp.stack([cmp_w2_k[0], cmp_w2_v[0]]).astype(BF16)
    cmp_n, cmp_t = _compress(kb, pos8, w1, w2, w2.transpose(0, 2, 1))

    oc, sel = _cmp_select(q_nsa, cmp_n, cmp_t, s)

    bias_nsa = rel_bias[:, :HEADS].reshape(NUM_BUCKETS, GROUPS, REP) * LOG2E
    bias_swa = rel_bias[:, HEADS:].reshape(NUM_BUCKETS, GROUPS, REP) * LOG2E
    thr = jnp.asarray(_bucket_starts())
    tab_nsa = bias_nsa.transpose(1, 2, 0).reshape(-1)
    tab_swa = bias_swa.transpose(1, 2, 0).reshape(-1)

    tk = SEL_TILE
    nb = tk // SEL_BLOCK
    posn = jnp.arange(s)
    onehot = ((posn // SEL_BLOCK) % nb)[:, None] == jnp.arange(nb)[None, :]
    aug = jnp.concatenate([onehot.astype(BF16), jnp.ones((s, 2), BF16),
                           jnp.zeros((s, kvd - nb - 2), BF16)], axis=1)
    kaug = jnp.concatenate([kv[:, 2 * kvd:3 * kvd], aug], axis=1)
    far = bias_nsa[NUM_BUCKETS - 1]
    far_hi = far.astype(BF16)
    far_lo = (far - far_hi.astype(F32)).astype(BF16)
    cst = jnp.stack([far_hi, far_lo], axis=1)
    cst = jnp.broadcast_to(cst[:, :, :, None], (GROUPS, 2, REP, QBLOCK)).reshape(GROUPS, 2, HQ)
    cst = jnp.pad(cst, ((0, 0), (0, 14), (0, 0)))
    ones_row = jnp.concatenate([jnp.ones((GROUPS, 1, s), BF16), jnp.zeros((GROUPS, 15, s), BF16)], axis=1)
    vs_aug = jnp.concatenate([vs_t, ones_row], axis=1)
    o_sel = _selected(thr, tab_nsa, q_nsa, sel, kaug, vs_aug, cst, s, tk)

    pad_rows = lambda a, wdw: jnp.pad(a, ((wdw, 0), (0, 0)))
    pad_lanes = lambda a, wdw: jnp.pad(jnp.concatenate([a, ones_row], axis=1), ((0, 0), (0, 0), (wdw, 0)))
    no_sink = jnp.zeros((GROUPS, 1, HQ), F32)
    o_win = _banded(thr, tab_nsa, q_nsa, pad_rows(kv[:, 3 * kvd:4 * kvd], NSA_WINDOW), pad_lanes(vw_t, NSA_WINDOW),
                    no_sink, s, NSA_WINDOW, False, "nsa_window")
    sink = attn_sinks[0].reshape(GROUPS, 1, REP, 1).astype(F32) * LOG2E
    sink = jnp.broadcast_to(sink, (GROUPS, 1, REP, QBLOCK)).reshape(GROUPS, 1, HQ)
    o_swa = _banded(thr, tab_swa, q_swa, pad_rows(kv[:, 4 * kvd:5 * kvd], SWA_WINDOW), pad_lanes(vswa_t, SWA_WINDOW),
                    sink, s, SWA_WINDOW, True, "swa_sink")

    flat = lambda a: a.reshape(qd, s)
    h1 = _merge(flat(oc), flat(o_sel), flat(o_win), flat(o_swa), gn, gab, x2,
                w_up_nsa[0].astype(BF16), w_up_swa[0].astype(BF16), w_out[0].astype(BF16),
                norm_mix_post, ts=512)
    out = _mlp(h1, norm_mlp_pre, w_mlp_in[0].astype(BF16), w_mlp_out[0].astype(BF16), norm_mlp_post,
               ts=1024, tf=2048)
    return out[None]
```

```python
import functools
import math

import jax
import jax.numpy as jnp
import numpy as np
from jax import lax
from jax.experimental import pallas as pl
from jax.experimental.pallas import tpu as pltpu

F32 = jnp.float32
BF16 = jnp.bfloat16

HEAD_DIM = 64
GROUPS = 2
REP = 4
HEADS = GROUPS * REP
CMP_BLOCK = 32
CMP_STRIDE = 16
SEL_BLOCK = 64
SEL_TOPK = 16
NSA_WINDOW = 512
SWA_WINDOW = 128
QBLOCK = 128
NUM_BUCKETS = 32
MAX_DISTANCE = 1024
RMS_EPS = 1e-6
LOG2E = math.log2(math.e)
MASK_BIG = 2.0 ** 99
HQ = REP * QBLOCK

SEL_TILE = 1024
TILES_PER_TRIP = 8
PROJ_TOKENS = 512
MLP_TOKENS = 1024
MLP_HIDDEN = 2048
BAND_CHUNKS = 8
CMP_CHUNKS = 4
CMP_VARIANT_KEYS = 128
V7X_VMEM_BYTES = 64 * 1024 * 1024
VMEM_LIMIT = V7X_VMEM_BYTES * 7 // 8

_NT = (((1,), (1,)), ((), ()))
_TN = (((0,), (0,)), ((), ()))


def _params(*sem):
    return pltpu.CompilerParams(dimension_semantics=sem, vmem_limit_bytes=VMEM_LIMIT)


def _bucket_starts():
    max_exact = NUM_BUCKETS // 2
    d = np.arange(2 * MAX_DISTANCE)
    large = max_exact + (np.log(np.maximum(d, 1).astype(np.float64) / max_exact)
                         / math.log(MAX_DISTANCE / max_exact) * (NUM_BUCKETS - max_exact)).astype(np.int64)
    bucket = np.where(d < max_exact, d, np.minimum(large, NUM_BUCKETS - 1))
    return (bucket[None, :] < np.arange(NUM_BUCKETS)[:, None]).sum(axis=1).astype(np.int32)


def _rms(x, gain):
    return x * lax.rsqrt(jnp.mean(x * x, axis=-1, keepdims=True) + RMS_EPS) * gain


def _proj_kernel(x_ref, g_ref, wn_ref, wg_ref, wt_ref, kv_ref, gab_ref, qv_ref, gn_ref, kb_ref, y_ref, *, nqv):
    u = _rms(x_ref[...], g_ref[...]).astype(BF16)
    kvf = jnp.dot(u, wn_ref[...], preferred_element_type=F32)
    kv_ref[...] = kvf.astype(BF16)
    gw = GROUPS * HEAD_DIM
    nrow = y_ref.shape[1] // CMP_STRIDE
    for a in range(2):
        y_ref[a] = kvf[:, a * gw:(a + 1) * gw]
        both = [y_ref[a, pl.ds(p, nrow, stride=CMP_STRIDE), :] for p in range(CMP_STRIDE)]
        for gi in range(GROUPS):
            kb_ref[a, gi] = jnp.concatenate([t[:, gi * HEAD_DIM:(gi + 1) * HEAD_DIM] for t in both],
                                            axis=1).astype(BF16)
    gab_ref[...] = jnp.dot(u, wg_ref[...], preferred_element_type=F32).astype(BF16)
    t = lax.dot_general(wt_ref[...], u, _NT, preferred_element_type=F32)
    qv_ref[...] = t[:nqv].astype(BF16)
    gn_ref[...] = t[nqv:]


def _project(x, gain, wn, wg, wt, nqv, ts):
    s, d = x.shape
    nn, ng, ntr = wn.shape[1], wg.shape[1], wt.shape[0]
    full = lambda shp: pl.BlockSpec(shp, lambda i: (0, 0))
    return pl.pallas_call(
        functools.partial(_proj_kernel, nqv=nqv),
        grid=(s // ts,),
        in_specs=[pl.BlockSpec((ts, d), lambda i: (i, 0)), full((1, d)),
                  full((d, nn)), full((d, ng)), full((ntr, d))],
        out_specs=[pl.BlockSpec((ts, nn), lambda i: (i, 0)),
                   pl.BlockSpec((ts, ng), lambda i: (i, 0)),
                   pl.BlockSpec((nqv, ts), lambda i: (0, i)),
                   pl.BlockSpec((ntr - nqv, ts), lambda i: (0, i)),
                   pl.BlockSpec((2, GROUPS, ts // CMP_STRIDE, CMP_STRIDE * HEAD_DIM), lambda i: (0, 0, i, 0))],
        out_shape=[jax.ShapeDtypeStruct((s, nn), BF16), jax.ShapeDtypeStruct((s, ng), BF16),
                   jax.ShapeDtypeStruct((nqv, s), BF16), jax.ShapeDtypeStruct((ntr - nqv, s), F32),
                   jax.ShapeDtypeStruct((2, GROUPS, s // CMP_STRIDE, CMP_STRIDE * HEAD_DIM), BF16)],
        scratch_shapes=[pltpu.VMEM((2, ts, GROUPS * HEAD_DIM), F32)],
        compiler_params=_params("arbitrary"),
        name="in_proj",
    )(x, gain, wn, wg, wt)


def _gelu_tanh(x):
    return x * (0.5 * (1.0 + jnp.tanh(math.sqrt(2.0 / math.pi) * (x + 0.044715 * (x * x * x)))))


def _compress_kernel(kb_ref, pos_ref, w1_ref, w2_ref, w2t_ref, o_ref, ot_ref):
    kb = kb_ref[0, 0]
    half = kb.shape[1]
    n16 = kb.shape[0]
    first = jnp.dot(kb, w1_ref[0, :half, :], preferred_element_type=F32)
    second = jnp.dot(kb, w1_ref[0, half:, :], preferred_element_type=F32)
    posb = jnp.dot(pos_ref[0], w1_ref[0], preferred_element_type=F32)[0:1]
    pre = first + pltpu.roll(second, n16 - 1, 0) + posb
    h = _gelu_tanh(pre).astype(BF16)
    o_ref[0, 0] = jnp.dot(h, w2_ref[0], preferred_element_type=F32).astype(BF16)
    ot_ref[0, 0] = lax.dot_general(w2t_ref[0], h, _NT, preferred_element_type=F32).astype(BF16)


def _compress(kb, pos8, w1, w2, w2t):
    _, g, n16, cin2 = kb.shape
    hid = w1.shape[2]
    return pl.pallas_call(
        _compress_kernel,
        grid=(2, g),
        in_specs=[pl.BlockSpec((1, 1, n16, cin2), lambda a, b: (a, b, 0, 0)),
                  pl.BlockSpec((1, 8, 2 * cin2), lambda a, b: (a, 0, 0)),
                  pl.BlockSpec((1, 2 * cin2, hid), lambda a, b: (a, 0, 0)),
                  pl.BlockSpec((1, hid, HEAD_DIM), lambda a, b: (a, 0, 0)),
                  pl.BlockSpec((1, HEAD_DIM, hid), lambda a, b: (a, 0, 0))],
        out_specs=[pl.BlockSpec((1, 1, n16, HEAD_DIM), lambda a, b: (a, b, 0, 0)),
                   pl.BlockSpec((1, 1, HEAD_DIM, n16), lambda a, b: (a, b, 0, 0))],
        out_shape=[jax.ShapeDtypeStruct((2, g, n16, HEAD_DIM), BF16),
                   jax.ShapeDtypeStruct((2, g, HEAD_DIM, n16), BF16)],
        compiler_params=_params("arbitrary", "arbitrary"),
        name="kv_compress",
    )(kb, pos8, w1, w2, w2t)


def _cmpsel_body(q_ref, kc_ref, vct_ref, oc_ref, sel_ref, pg_ref, s_ref, t_base, *, cb, rows, nomask, nsel, topk):
    mrows = rows - nomask
    ratio = SEL_BLOCK // CMP_STRIDE
    nblk = rows // ratio
    n_io = nomask + lax.broadcasted_iota(jnp.int32, (mrows, HQ), 0)
    lane_t = lax.broadcasted_iota(jnp.int32, (mrows, HQ), 1) & (QBLOCK - 1)
    blk = lax.broadcasted_iota(jnp.int32, (nblk, QBLOCK), 0)

    def raw_scores(c):
        t0 = t_base + c * QBLOCK
        lanes = slice(c * QBLOCK, (c + 1) * QBLOCK)
        q4 = jnp.concatenate([q_ref[0, r * HEAD_DIM:(r + 1) * HEAD_DIM, lanes] for r in range(REP)], axis=1)
        s = jnp.dot(kc_ref[0, 0, 0:rows, :], q4, preferred_element_type=F32)
        edge = jnp.where(n_io * CMP_STRIDE + (CMP_BLOCK - 1) <= t0 + lane_t, s[nomask:], -jnp.inf)
        s = jnp.concatenate([s[:nomask], edge], axis=0) if nomask else edge
        s_ref[c, 0:rows, :] = s
        m = jnp.max(s, axis=0, keepdims=True)
        return jnp.where(m == -jnp.inf, 0.0, m)

    def attend(c, m):
        t0 = t_base + c * QBLOCK
        lanes = slice(c * QBLOCK, (c + 1) * QBLOCK)
        e = jnp.exp2(s_ref[c, 0:rows, :] - m)
        den = jnp.maximum(jnp.sum(e, axis=0, keepdims=True), 1e-30)
        p = e * (1.0 / den)
        o_t = jnp.dot(vct_ref[0, 0, :, 0:rows], p.astype(BF16), preferred_element_type=F32)
        for r in range(REP):
            oc_ref[0, r * HEAD_DIM:(r + 1) * HEAD_DIM, lanes] = (
                o_t[:, r * QBLOCK:(r + 1) * QBLOCK].astype(oc_ref.dtype))

        pg = ((p[:, 0:QBLOCK] + p[:, QBLOCK:2 * QBLOCK]) + p[:, 2 * QBLOCK:3 * QBLOCK]) + p[:, 3 * QBLOCK:]
        pg_ref[c, 0:8, :] = jnp.zeros((8, QBLOCK), F32)
        pg_ref[c, 8:8 + rows, :] = pg
        ps = pg_ref[c, pl.ds(7, nblk, stride=ratio), :]
        for o in range(1, ratio + 1):
            ps = ps + pg_ref[c, pl.ds(7 + o, nblk, stride=ratio), :]
        cur = (t0 + lax.broadcasted_iota(jnp.int32, (nblk, QBLOCK), 1)) // SEL_BLOCK
        valid = blk <= cur
        forced = (blk == 0) | (blk == cur) | (blk == cur - 1)
        return jnp.where(valid & ~forced, ps, -jnp.inf), valid

    scores, valids = [], []
    m = raw_scores(0)
    for c in range(cb):
        m_next = raw_scores(c + 1) if c + 1 < cb else None
        score, valid = attend(c, m)
        scores.append(score)
        valids.append(valid)
        m = m_next

    def pick_one(_, rems):
        out = []
        for rem in rems:
            top = jnp.max(rem, axis=0, keepdims=True)
            first = jnp.min(jnp.where(rem == top, blk, nblk), axis=0, keepdims=True)
            out.append(jnp.where(blk == first, -jnp.inf, rem))
        return tuple(out)

    rems = lax.fori_loop(0, max(topk - 3, 0), pick_one, tuple(scores))
    for c in range(cb):
        lanes = slice(c * QBLOCK, (c + 1) * QBLOCK)
        sel_ref[0, 0:nblk, lanes] = jnp.where((rems[c] == -jnp.inf) & valids[c], 0.0, -MASK_BIG).astype(BF16)
        if nblk < nsel:
            sel_ref[0, nblk:, lanes] = jnp.full((nsel - nblk, QBLOCK), -MASK_BIG, BF16)


def _cmpsel_kernel(q_ref, kc_ref, vct_ref, oc_ref, sel_ref, pg_ref, s_ref, *, n16, nsel, topk, nvar, cb):
    step = pl.program_id(1)
    per = n16 // nvar
    spv = per // (QBLOCK // CMP_STRIDE) // cb
    for k in range(1, nvar + 1):
        @pl.when((step >= (k - 1) * spv) & (step < k * spv))
        def _(k=k):
            _cmpsel_body(q_ref, kc_ref, vct_ref, oc_ref, sel_ref, pg_ref, s_ref, step * (cb * QBLOCK), cb=cb,
                         rows=per * k, nomask=max(per * (k - 1) - 8, 0), nsel=nsel, topk=topk)


def _cmp_select(q, kc_all, vct_all, s, cb=CMP_CHUNKS):
    n16 = s // CMP_STRIDE
    nsel = s // SEL_BLOCK
    topk = min(SEL_TOPK, nsel)
    nvar = max(n16 // CMP_VARIANT_KEYS, 1)
    return pl.pallas_call(
        functools.partial(_cmpsel_kernel, n16=n16, nsel=nsel, topk=topk, nvar=nvar, cb=cb),
        grid=(GROUPS, s // (QBLOCK * cb)),
        in_specs=[pl.BlockSpec((1, REP * HEAD_DIM, QBLOCK * cb), lambda g, c: (g, 0, c)),
                  pl.BlockSpec((1, 1, n16, HEAD_DIM), lambda g, c: (0, g, 0, 0)),
                  pl.BlockSpec((1, 1, HEAD_DIM, n16), lambda g, c: (1, g, 0, 0))],
        out_specs=[pl.BlockSpec((1, REP * HEAD_DIM, QBLOCK * cb), lambda g, c: (g, 0, c)),
                   pl.BlockSpec((1, nsel, QBLOCK * cb), lambda g, c: (g, 0, c))],
        out_shape=[jax.ShapeDtypeStruct((GROUPS, REP * HEAD_DIM, s), BF16),
                   jax.ShapeDtypeStruct((GROUPS, nsel, s), BF16)],
        scratch_shapes=[pltpu.VMEM((cb, n16 + 8, QBLOCK), F32), pltpu.VMEM((cb, n16, HQ), F32)],
        compiler_params=_params("arbitrary", "arbitrary"),
        name="cmp_select",
    )(q, kc_all, vct_all)


def _fill_bias_strip(strip_ref, thr_ref, tab_ref, g, *, dist0, lo, hi, minus_far):
    nrows = strip_ref.shape[0]

    def block(bi, carry):
        i0 = pl.multiple_of(bi * QBLOCK, QBLOCK)
        d = (lax.broadcasted_iota(jnp.int32, (QBLOCK, QBLOCK), 1)
             - lax.broadcasted_iota(jnp.int32, (QBLOCK, QBLOCK), 0) + (dist0 - i0))
        ok = d >= lo if hi is None else (d >= lo) & (d < hi)
        for r in range(REP):
            base = (g * REP + r) * NUM_BUCKETS
            out = jnp.full((QBLOCK, QBLOCK), tab_ref[base], F32)
            for b in range(1, NUM_BUCKETS):
                out = jnp.where(d >= thr_ref[b], tab_ref[base + b], out)
            if minus_far:
                out = out - tab_ref[base + NUM_BUCKETS - 1]
            strip_ref[pl.ds(i0, QBLOCK), r * QBLOCK:(r + 1) * QBLOCK] = jnp.where(ok, out, -MASK_BIG)
        return carry

    lax.fori_loop(0, nrows // QBLOCK, block, 0)


def _sel_kernel(thr_ref, tab_ref, q_ref, sel_ref, kaug_ref, vt_ref, cst_ref, o_ref,
                qa_ref, s_ref, tmax_ref, m_ref, acc_ref, strip_ref, *, tk):
    g = pl.program_id(0)
    st = pl.program_id(1)
    nb = tk // SEL_BLOCK
    cpt = tk // QBLOCK
    kd = GROUPS * HEAD_DIM
    dmax = 2 * tk - QBLOCK

    @pl.when(st == 0)
    def _():
        _fill_bias_strip(strip_ref, thr_ref, tab_ref, g, dist0=dmax, lo=0, hi=None, minus_far=True)

    qa_ref[...] = jnp.zeros(qa_ref.shape, BF16)
    for b in range(2):
        qa_ref[b, kd + nb:kd + nb + 16, :] = cst_ref[0]
    m_ref[...] = jnp.full(m_ref.shape, -jnp.inf, F32)
    acc_ref[...] = jnp.zeros(acc_ref.shape, F32)
    row0 = pl.multiple_of(g * HEAD_DIM, HEAD_DIM)
    nnear = jnp.minimum(st + 1, 2)
    nfar = st + 1 - nnear

    def run(count, first_tile, near):

        def advance(c, i):
            wrap = i + 1 == count
            return jnp.where(wrap, c + 1, c), jnp.where(wrap, 0, i + 1)

        def qk(c, i, buf):
            cc = jnp.minimum(c, cpt - 1)
            j = first_tile + i
            lane0 = pl.multiple_of(cc * QBLOCK, QBLOCK)
            for r in range(REP):
                qa_ref[buf, pl.ds(row0, HEAD_DIM), r * QBLOCK:(r + 1) * QBLOCK] = (
                    q_ref[0, r * HEAD_DIM:(r + 1) * HEAD_DIM, pl.ds(lane0, QBLOCK)])
            sb = sel_ref[0, pl.ds(pl.multiple_of(j * nb, nb), nb), pl.ds(lane0, QBLOCK)]
            for r in range(REP):
                qa_ref[buf, kd:kd + nb, r * QBLOCK:(r + 1) * QBLOCK] = sb
            k0 = pl.multiple_of(j * tk, tk)
            s = jnp.dot(kaug_ref[pl.ds(k0, tk), :], qa_ref[buf], preferred_element_type=F32)
            if near:
                delta = (st - j) * tk + cc * QBLOCK
                s = s + strip_ref[pl.ds(pl.multiple_of(dmax - delta, QBLOCK), tk), :]
            s_ref[buf, :, 0:HQ] = s
            tmax_ref[buf] = jnp.max(s, axis=0, keepdims=True)

        def softmax_pv(c, i, buf):
            k0 = pl.multiple_of((first_tile + i) * tk, tk)
            m_prev = m_ref[c]
            m_new = jnp.maximum(m_prev, tmax_ref[buf])
            alpha = jnp.exp2(m_prev - m_new)
            p = jnp.exp2(s_ref[buf, :, 0:HQ] - m_new)
            acc = alpha * acc_ref[c] + jnp.dot(vt_ref[0, :, pl.ds(k0, tk)], p.astype(BF16),
                                               preferred_element_type=F32)
            acc_ref[c] = acc
            m_ref[c] = m_new
            if near:
                lane0 = pl.multiple_of(c * QBLOCK, QBLOCK)
                out = acc[0:HEAD_DIM] * (1.0 / jnp.maximum(acc[HEAD_DIM:HEAD_DIM + 1], 1e-30))
                for r in range(REP):
                    o_ref[0, r * HEAD_DIM:(r + 1) * HEAD_DIM, pl.ds(lane0, QBLOCK)] = (
                        out[:, r * QBLOCK:(r + 1) * QBLOCK].astype(o_ref.dtype))

        zero = jnp.int32(0)
        qk(zero, zero, 0)

        def trip(_, carry):
            cur = carry
            for k in range(TILES_PER_TRIP):
                nxt = advance(*cur)
                qk(*nxt, (k + 1) % 2)
                softmax_pv(*cur, k % 2)
                cur = nxt
            return cur

        lax.fori_loop(0, (cpt * count) // TILES_PER_TRIP, trip, (zero, zero))

    @pl.when(nfar > 0)
    def _():
        run(nfar, 0, False)

    run(nnear, nfar, True)


def _selected(thr, tab, q, sel, kaug, vt, cst, s, tk):
    nsel = s // SEL_BLOCK
    sl = 3 * tk - QBLOCK
    cpt = tk // QBLOCK
    smem = pl.BlockSpec(memory_space=pltpu.SMEM)
    return pl.pallas_call(
        functools.partial(_sel_kernel, tk=tk),
        grid=(GROUPS, s // tk),
        in_specs=[smem, smem,
                  pl.BlockSpec((1, REP * HEAD_DIM, tk), lambda g, c: (g, 0, c)),
                  pl.BlockSpec((1, nsel, tk), lambda g, c: (g, 0, c)),
                  pl.BlockSpec(kaug.shape, lambda g, c: (0, 0)),
                  pl.BlockSpec((1, vt.shape[1], s), lambda g, c: (g, 0, 0)),
                  pl.BlockSpec((1, 16, HQ), lambda g, c: (g, 0, 0))],
        out_specs=pl.BlockSpec((1, REP * HEAD_DIM, tk), lambda g, c: (g, 0, c)),
        out_shape=jax.ShapeDtypeStruct((GROUPS, REP * HEAD_DIM, s), BF16),
        scratch_shapes=[pltpu.VMEM((2, kaug.shape[1], HQ), BF16), pltpu.VMEM((2, tk, HQ + QBLOCK), F32),
                        pltpu.VMEM((2, 1, HQ), F32),
                        pltpu.VMEM((cpt, 1, HQ), F32), pltpu.VMEM((cpt, vt.shape[1], HQ), F32),
                        pltpu.VMEM((sl, HQ), F32)],
        compiler_params=_params("arbitrary", "arbitrary"),
        name="selected_attn",
    )(thr, tab, q, sel, kaug, vt, cst)


def _band_kernel(thr_ref, tab_ref, q_ref, k_ref, vt_ref, sink_ref, o_ref, qa_ref, strip_ref, s_ref, *,
                 window, use_sink, cb):
    g = pl.program_id(0)

    @pl.when(pl.program_id(1) == 0)
    def _():
        _fill_bias_strip(strip_ref, thr_ref, tab_ref, g, dist0=window, lo=0, hi=window, minus_far=False)

    base = pl.multiple_of(pl.program_id(1) * (QBLOCK * cb), QBLOCK * cb)
    nk = window + QBLOCK
    qa_ref[...] = jnp.zeros(qa_ref.shape, BF16)
    row0 = pl.multiple_of(g * HEAD_DIM, HEAD_DIM)
    for c in range(cb):
        for r in range(REP):
            qa_ref[c, pl.ds(row0, HEAD_DIM), r * QBLOCK:(r + 1) * QBLOCK] = (
                q_ref[0, r * HEAD_DIM:(r + 1) * HEAD_DIM, c * QBLOCK:(c + 1) * QBLOCK])

    def scores(c, first_step):
        t0 = base + c * QBLOCK
        s = jnp.dot(k_ref[pl.ds(t0, nk), :], qa_ref[c], preferred_element_type=F32) + strip_ref[...]
        if first_step and c * QBLOCK < window:
            row = lax.broadcasted_iota(jnp.int32, (nk, HQ), 0)
            s = jnp.where(row >= window - c * QBLOCK, s, -MASK_BIG)
        s_ref[c] = s
        m = jnp.max(s, axis=0, keepdims=True)
        return jnp.maximum(m, sink_ref[0]) if use_sink else m

    def attend(c, m):
        t0 = base + c * QBLOCK
        e = jnp.exp2(s_ref[c] - m)
        o_t = jnp.dot(vt_ref[0, :, pl.ds(t0, nk)], e.astype(BF16), preferred_element_type=F32)
        den = o_t[HEAD_DIM:HEAD_DIM + 1]
        den = den + jnp.exp2(sink_ref[0] - m) if use_sink else jnp.maximum(den, 1e-30)
        o_t = o_t[0:HEAD_DIM] * (1.0 / den)
        for r in range(REP):
            o_ref[0, r * HEAD_DIM:(r + 1) * HEAD_DIM, c * QBLOCK:(c + 1) * QBLOCK] = (
                o_t[:, r * QBLOCK:(r + 1) * QBLOCK].astype(o_ref.dtype))

    def chunks(first_step):
        m = scores(0, first_step)
        for c in range(cb):
            m_next = scores(c + 1, first_step) if c + 1 < cb else None
            attend(c, m)
            m = m_next

    @pl.when(pl.program_id(1) == 0)
    def _():
        chunks(True)

    @pl.when(pl.program_id(1) > 0)
    def _():
        chunks(False)


def _banded(thr, tab, q, kpad, vtpad, sink, s, window, use_sink, name, cb=BAND_CHUNKS):
    assert QBLOCK * cb >= window
    nk = window + QBLOCK
    smem = pl.BlockSpec(memory_space=pltpu.SMEM)
    return pl.pallas_call(
        functools.partial(_band_kernel, window=window, use_sink=use_sink, cb=cb),
        grid=(GROUPS, s // (QBLOCK * cb)),
        in_specs=[smem, smem,
                  pl.BlockSpec((1, REP * HEAD_DIM, QBLOCK * cb), lambda g, c: (g, 0, c)),
                  pl.BlockSpec(kpad.shape, lambda g, c: (0, 0)),
                  pl.BlockSpec((1, vtpad.shape[1], s + window), lambda g, c: (g, 0, 0)),
                  pl.BlockSpec((1, 1, HQ), lambda g, c: (g, 0, 0))],
        out_specs=pl.BlockSpec((1, REP * HEAD_DIM, QBLOCK * cb), lambda g, c: (g, 0, c)),
        out_shape=jax.ShapeDtypeStruct((GROUPS, REP * HEAD_DIM, s), BF16),
        scratch_shapes=[pltpu.VMEM((cb, GROUPS * HEAD_DIM, HQ), BF16), pltpu.VMEM((nk, HQ), F32),
                        pltpu.VMEM((cb, nk, HQ), F32)],
        compiler_params=_params("arbitrary", "arbitrary"),
        name=name,
    )(thr, tab, q, kpad, vtpad, sink)


def _merge_kernel(oc_ref, os_ref, ow_ref, ob_ref, gn_ref, gab_ref, x_ref, wa_ref, wb_ref, wo_ref, gp_ref, o_ref):
    d = x_ref.shape[1]
    gn = jax.nn.sigmoid(gn_ref[...])
    parts = []
    for h in range(HEADS):
        rows = slice(h * HEAD_DIM, (h + 1) * HEAD_DIM)
        parts.append(gn[h:h + 1] * oc_ref[rows, :].astype(F32)
                     + gn[HEADS + h:HEADS + h + 1] * os_ref[rows, :].astype(F32)
                     + gn[2 * HEADS + h:2 * HEADS + h + 1] * ow_ref[rows, :].astype(F32))
    oa_t = jnp.concatenate(parts, axis=0).astype(BF16)
    ya = lax.dot_general(oa_t, wa_ref[...], _TN, preferred_element_type=F32)
    yb = lax.dot_general(ob_ref[...], wb_ref[...], _TN, preferred_element_type=F32)
    y = (jax.nn.sigmoid(gab_ref[:, :d].astype(F32)) * ya
         + jax.nn.sigmoid(gab_ref[:, d:].astype(F32)) * yb)
    z = jnp.dot(y.astype(BF16), wo_ref[...], preferred_element_type=F32)
    o_ref[...] = x_ref[...] + _rms(z, gp_ref[...])


def _merge(oc, os_, ow, ob, gn, gab, x, wa, wb, wo, gpost, ts):
    s, d = x.shape
    hd = oc.shape[0]
    tok = lambda rows: pl.BlockSpec((rows, ts), lambda i: (0, i))
    full = lambda shp: pl.BlockSpec(shp, lambda i: (0, 0))
    return pl.pallas_call(
        _merge_kernel,
        grid=(s // ts,),
        in_specs=[tok(hd), tok(hd), tok(hd), tok(hd), tok(gn.shape[0]),
                  pl.BlockSpec((ts, 2 * d), lambda i: (i, 0)), pl.BlockSpec((ts, d), lambda i: (i, 0)),
                  full((hd, d)), full((hd, d)), full((d, d)), full((1, d))],
        out_specs=pl.BlockSpec((ts, d), lambda i: (i, 0)),
        out_shape=jax.ShapeDtypeStruct((s, d), F32),
        compiler_params=_params("arbitrary"),
        name="merge_out_proj",
    )(oc, os_, ow, ob, gn, gab, x, wa, wb, wo, gpost)


def _mlp_kernel(h_ref, gpre_ref, w1_ref, w2_ref, gpost_ref, o_ref, u_ref, acc_ref):
    f = pl.program_id(1)

    @pl.when(f == 0)
    def _():
        u_ref[...] = _rms(h_ref[...], gpre_ref[...]).astype(BF16)
        acc_ref[...] = jnp.zeros(acc_ref.shape, F32)

    a = jnp.maximum(jnp.dot(u_ref[...], w1_ref[...], preferred_element_type=F32), 0.0)
    acc_ref[...] += jnp.dot((a * a).astype(BF16), w2_ref[...], preferred_element_type=F32)

    @pl.when(f == pl.num_programs(1) - 1)
    def _():
        o_ref[...] = h_ref[...] + _rms(acc_ref[...], gpost_ref[...])


def _mlp(h, gpre, w1, w2, gpost, ts, tf):
    s, d = h.shape
    dff = w1.shape[1]
    return pl.pallas_call(
        _mlp_kernel,
        grid=(s // ts, dff // tf),
        in_specs=[pl.BlockSpec((ts, d), lambda i, f: (i, 0)), pl.BlockSpec((1, d), lambda i, f: (0, 0)),
                  pl.BlockSpec((d, tf), lambda i, f: (0, f)), pl.BlockSpec((tf, d), lambda i, f: (f, 0)),
                  pl.BlockSpec((1, d), lambda i, f: (0, 0))],
        out_specs=pl.BlockSpec((ts, d), lambda i, f: (i, 0)),
        out_shape=jax.ShapeDtypeStruct((s, d), F32),
        scratch_shapes=[pltpu.VMEM((ts, d), BF16), pltpu.VMEM((ts, d), F32)],
        compiler_params=_params("arbitrary", "arbitrary"),
        name="mlp_relu2",
    )(h, gpre, w1, w2, gpost)


def kernel(x, norm_mix_pre, norm_mix_post, norm_mlp_pre, norm_mlp_post, w_in,
           cmp_pos_k, cmp_w1_k, cmp_w2_k, cmp_pos_v, cmp_w1_v, cmp_w2_v,
           attn_sinks, rel_bias, w_up_nsa, w_up_swa, w_out, w_mlp_in, w_mlp_out):
    b, s, d = x.shape
    assert b == 1 and s % SEL_TILE == 0 and w_in.shape[0] == 1
    qd = HEADS * HEAD_DIM
    kvd = GROUPS * HEAD_DIM
    sizes = (qd,) + (kvd,) * 6 + (3 * HEADS, qd, kvd, kvd, d, d)
    offs = [0]
    for z in sizes:
        offs.append(offs[-1] + z)
    w = w_in[0]
    col = lambda i: w[:, offs[i]:offs[i + 1]]
    (w_qn, w_kc, w_vc, w_ks, w_vs, w_kw, w_vw, w_gn, w_qs, w_k_s, w_v_s, w_ga, w_gb) = [col(i) for i in range(13)]
    scale = HEAD_DIM ** -0.5 * LOG2E
    w_gn = w_gn.reshape(d, HEADS, 3).transpose(0, 2, 1).reshape(d, 3 * HEADS)
    w_gn = jnp.pad(w_gn, ((0, 0), (0, 8)))
    wn = jnp.concatenate([w_kc, w_vc, w_ks, w_kw, w_k_s], axis=1).astype(BF16)
    wg = jnp.concatenate([w_ga, w_gb], axis=1).astype(BF16)
    wt = jnp.concatenate([w_qn * scale, w_qs * scale, w_vs, w_vw, w_v_s, w_gn], axis=1).T.astype(BF16)
    nqv = 2 * qd + 3 * kvd

    x2 = x[0]
    kv, gab, qv, gn, kb = _project(x2, norm_mix_pre, wn, wg, wt, nqv, ts=PROJ_TOKENS)
    q_nsa = qv[0:qd].reshape(GROUPS, REP * HEAD_DIM, s)
    q_swa = qv[qd:2 * qd].reshape(GROUPS, REP * HEAD_DIM, s)
    vs_t = qv[2 * qd:2 * qd + kvd].reshape(GROUPS, HEAD_DIM, s)
    vw_t = qv[2 * qd + kvd:2 * qd + 2 * kvd].reshape(GROUPS, HEAD_DIM, s)
    vswa_t = qv[2 * qd + 2 * kvd:].reshape(GROUPS, HEAD_DIM, s)

    n16 = s // CMP_STRIDE
    pos8 = jnp.stack([cmp_pos_k[0], cmp_pos_v[0]]).reshape(2, 1, CMP_BLOCK * HEAD_DIM)
    pos8 = jnp.broadcast_to(pos8, (2, 8, CMP_BLOCK * HEAD_DIM)).astype(BF16)
    w1 = jnp.stack([cmp_w1_k[0], cmp_w1_v[0]]).astype(BF16)
    w2 = jnp.stack([cmp_w2_k[0], cmp_w2_v[0]]).astype(BF16)
    cmp_n, cmp_t = _compress(kb, pos8, w1, w2, w2.transpose(0, 2, 1))

    oc, sel = _cmp_select(q_nsa, cmp_n, cmp_t, s)

    bias_nsa = rel_bias[:, :HEADS].reshape(NUM_BUCKETS, GROUPS, REP) * LOG2E
    bias_swa = rel_bias[:, HEADS:].reshape(NUM_BUCKETS, GROUPS, REP) * LOG2E
    thr = jnp.asarray(_bucket_starts())
    tab_nsa = bias_nsa.transpose(1, 2, 0).reshape(-1)
    tab_swa = bias_swa.transpose(1, 2, 0).reshape(-1)

    tk = SEL_TILE
    nb = tk // SEL_BLOCK
    posn = jnp.arange(s)
    onehot = ((posn // SEL_BLOCK) % nb)[:, None] == jnp.arange(nb)[None, :]
    aug = jnp.concatenate([onehot.astype(BF16), jnp.ones((s, 2), BF16),
                           jnp.zeros((s, kvd - nb - 2), BF16)], axis=1)
    kaug = jnp.concatenate([kv[:, 2 * kvd:3 * kvd], aug], axis=1)
    far = bias_nsa[NUM_BUCKETS - 1]
    far_hi = far.astype(BF16)
    far_lo = (far - far_hi.astype(F32)).astype(BF16)
    cst = jnp.stack([far_hi, far_lo], axis=1)
    cst = jnp.broadcast_to(cst[:, :, :, None], (GROUPS, 2, REP, QBLOCK)).reshape(GROUPS, 2, HQ)
    cst = jnp.pad(cst, ((0, 0), (0, 14), (0, 0)))
    ones_row = jnp.concatenate([jnp.ones((GROUPS, 1, s), BF16), jnp.zeros((GROUPS, 15, s), BF16)], axis=1)
    vs_aug = jnp.concatenate([vs_t, ones_row], axis=1)
    o_sel = _selected(thr, tab_nsa, q_nsa, sel, kaug, vs_aug, cst, s, tk)

    pad_rows = lambda a, wdw: jnp.pad(a, ((wdw, 0), (0, 0)))
    pad_lanes = lambda a, wdw: jnp.pad(jnp.concatenate([a, ones_row], axis=1), ((0, 0), (0, 0), (wdw, 0)))
    no_sink = jnp.zeros((GROUPS, 1, HQ), F32)
    o_win = _banded(thr, tab_nsa, q_nsa, pad_rows(kv[:, 3 * kvd:4 * kvd], NSA_WINDOW), pad_lanes(vw_t, NSA_WINDOW),
                    no_sink, s, NSA_WINDOW, False, "nsa_window")
    sink = attn_sinks[0].reshape(GROUPS, 1, REP, 1).astype(F32) * LOG2E
    sink = jnp.broadcast_to(sink, (GROUPS, 1, REP, QBLOCK)).reshape(GROUPS, 1, HQ)
    o_swa = _banded(thr, tab_swa, q_swa, pad_rows(kv[:, 4 * kvd:5 * kvd], SWA_WINDOW), pad_lanes(vswa_t, SWA_WINDOW),
                    sink, s, SWA_WINDOW, True, "swa_sink")

    flat = lambda a: a.reshape(qd, s)
    h1 = _merge(flat(oc), flat(o_sel), flat(o_win), flat(o_swa), gn, gab, x2,
                w_up_nsa[0].astype(BF16), w_up_swa[0].astype(BF16), w_out[0].astype(BF16),
                norm_mix_post, ts=PROJ_TOKENS)
    out = _mlp(h1, norm_mlp_pre, w_mlp_in[0].astype(BF16), w_mlp_out[0].astype(BF16), norm_mlp_post,
               ts=MLP_TOKENS, tf=MLP_HIDDEN)
    return out[None]
```

```python
import functools
import math

import jax
import jax.numpy as jnp
import numpy as np
from jax import lax
from jax.experimental import pallas as pl
from jax.experimental.pallas import tpu as pltpu

F32 = jnp.float32
BF16 = jnp.bfloat16

HEAD_DIM = 64
GROUPS = 2
REP = 4
HEADS = GROUPS * REP
CMP_BLOCK = 32
CMP_STRIDE = 16
SEL_BLOCK = 64
SEL_TOPK = 16
NSA_WINDOW = 512
SWA_WINDOW = 128
QBLOCK = 128
NUM_BUCKETS = 32
MAX_DISTANCE = 1024
RMS_EPS = 1e-6
LOG2E = math.log2(math.e)
MASK_BIG = 2.0 ** 99
HQ = REP * QBLOCK

SEL_TILE = 1024
TILES_PER_TRIP = 8
PROJ_TOKENS = 512
MLP_TOKENS = 1024
MLP_HIDDEN = 2048
BAND_CHUNKS = 8
CMP_CHUNKS = 4
CMP_VARIANT_KEYS = 128
V7X_VMEM_BYTES = 64 * 1024 * 1024
VMEM_LIMIT = V7X_VMEM_BYTES * 7 // 8

_NT = (((1,), (1,)), ((), ()))
_TN = (((0,), (0,)), ((), ()))


def _params(*sem):
    return pltpu.CompilerParams(dimension_semantics=sem, vmem_limit_bytes=VMEM_LIMIT)


def _bucket_starts():
    max_exact = NUM_BUCKETS // 2
    d = np.arange(2 * MAX_DISTANCE)
    large = max_exact + (np.log(np.maximum(d, 1).astype(np.float64) / max_exact)
                         / math.log(MAX_DISTANCE / max_exact) * (NUM_BUCKETS - max_exact)).astype(np.int64)
    bucket = np.where(d < max_exact, d, np.minimum(large, NUM_BUCKETS - 1))
    return (bucket[None, :] < np.arange(NUM_BUCKETS)[:, None]).sum(axis=1).astype(np.int32)


def _rms(x, gain):
    return x * lax.rsqrt(jnp.mean(x * x, axis=-1, keepdims=True) + RMS_EPS) * gain


def _proj_kernel(x_ref, g_ref, wn_ref, wg_ref, wt_ref, kv_ref, gab_ref, qv_ref, gn_ref, kb_ref, y_ref, *, nqv):
    u = _rms(x_ref[...], g_ref[...]).astype(BF16)
    kvf = jnp.dot(u, wn_ref[...], preferred_element_type=F32)
    gw = GROUPS * HEAD_DIM
    kv_ref[...] = kvf[:, 2 * gw:].astype(BF16)
    nrow = y_ref.shape[1] // CMP_STRIDE
    for a in range(2):
        y_ref[a] = kvf[:, a * gw:(a + 1) * gw]
        both = [y_ref[a, pl.ds(p, nrow, stride=CMP_STRIDE), :] for p in range(CMP_STRIDE)]
        for gi in range(GROUPS):
            kb_ref[a, gi] = jnp.concatenate([t[:, gi * HEAD_DIM:(gi + 1) * HEAD_DIM] for t in both],
                                            axis=1).astype(BF16)
    gab_ref[...] = jnp.dot(u, wg_ref[...], preferred_element_type=F32).astype(BF16)
    t = lax.dot_general(wt_ref[...], u, _NT, preferred_element_type=F32)
    qv_ref[...] = t[:nqv].astype(BF16)
    gn_ref[...] = t[nqv:]


def _project(x, gain, wn, wg, wt, nqv, ts):
    s, d = x.shape
    nn, ng, ntr = wn.shape[1] - 2 * GROUPS * HEAD_DIM, wg.shape[1], wt.shape[0]
    full = lambda shp: pl.BlockSpec(shp, lambda i: (0, 0))
    return pl.pallas_call(
        functools.partial(_proj_kernel, nqv=nqv),
        grid=(s // ts,),
        in_specs=[pl.BlockSpec((ts, d), lambda i: (i, 0)), full((1, d)),
                  full(wn.shape), full((d, ng)), full((ntr, d))],
        out_specs=[pl.BlockSpec((ts, nn), lambda i: (i, 0)),
                   pl.BlockSpec((ts, ng), lambda i: (i, 0)),
                   pl.BlockSpec((nqv, ts), lambda i: (0, i)),
                   pl.BlockSpec((ntr - nqv, ts), lambda i: (0, i)),
                   pl.BlockSpec((2, GROUPS, ts // CMP_STRIDE, CMP_STRIDE * HEAD_DIM), lambda i: (0, 0, i, 0))],
        out_shape=[jax.ShapeDtypeStruct((s, nn), BF16), jax.ShapeDtypeStruct((s, ng), BF16),
                   jax.ShapeDtypeStruct((nqv, s), BF16), jax.ShapeDtypeStruct((ntr - nqv, s), F32),
                   jax.ShapeDtypeStruct((2, GROUPS, s // CMP_STRIDE, CMP_STRIDE * HEAD_DIM), BF16)],
        scratch_shapes=[pltpu.VMEM((2, ts, GROUPS * HEAD_DIM), F32)],
        compiler_params=_params("arbitrary"),
        name="in_proj",
    )(x, gain, wn, wg, wt)


def _gelu_tanh(x):
    return x * (0.5 * (1.0 + jnp.tanh(math.sqrt(2.0 / math.pi) * (x + 0.044715 * (x * x * x)))))


def _compress_kernel(kb_ref, pos_ref, w1_ref, w2_ref, w2t_ref, o_ref, ot_ref):
    kb = kb_ref[0, 0]
    half = kb.shape[1]
    n16 = kb.shape[0]
    first = jnp.dot(kb, w1_ref[0, :half, :], preferred_element_type=F32)
    second = jnp.dot(kb, w1_ref[0, half:, :], preferred_element_type=F32)
    posb = jnp.dot(pos_ref[0], w1_ref[0], preferred_element_type=F32)[0:1]
    pre = first + pltpu.roll(second, n16 - 1, 0) + posb
    h = _gelu_tanh(pre).astype(BF16)
    o_ref[0, 0] = jnp.dot(h, w2_ref[0], preferred_element_type=F32).astype(BF16)
    ot_ref[0, 0] = lax.dot_general(w2t_ref[0], h, _NT, preferred_element_type=F32).astype(BF16)


def _compress(kb, pos8, w1, w2, w2t):
    _, g, n16, cin2 = kb.shape
    hid = w1.shape[2]
    return pl.pallas_call(
        _compress_kernel,
        grid=(2, g),
        in_specs=[pl.BlockSpec((1, 1, n16, cin2), lambda a, b: (a, b, 0, 0)),
                  pl.BlockSpec((1, 8, 2 * cin2), lambda a, b: (a, 0, 0)),
                  pl.BlockSpec((1, 2 * cin2, hid), lambda a, b: (a, 0, 0)),
                  pl.BlockSpec((1, hid, HEAD_DIM), lambda a, b: (a, 0, 0)),
                  pl.BlockSpec((1, HEAD_DIM, hid), lambda a, b: (a, 0, 0))],
        out_specs=[pl.BlockSpec((1, 1, n16, HEAD_DIM), lambda a, b: (a, b, 0, 0)),
                   pl.BlockSpec((1, 1, HEAD_DIM, n16), lambda a, b: (a, b, 0, 0))],
        out_shape=[jax.ShapeDtypeStruct((2, g, n16, HEAD_DIM), BF16),
                   jax.ShapeDtypeStruct((2, g, HEAD_DIM, n16), BF16)],
        compiler_params=_params("arbitrary", "arbitrary"),
        name="kv_compress",
    )(kb, pos8, w1, w2, w2t)


def _cmpsel_body(q_ref, kc_ref, vct_ref, oc_ref, sel_ref, pg_ref, s_ref, t_base, *, cb, rows, nomask, nsel, topk):
    mrows = rows - nomask
    ratio = SEL_BLOCK // CMP_STRIDE
    nblk = rows // ratio
    n_io = nomask + lax.broadcasted_iota(jnp.int32, (mrows, HQ), 0)
    lane_t = lax.broadcasted_iota(jnp.int32, (mrows, HQ), 1) & (QBLOCK - 1)
    blk = lax.broadcasted_iota(jnp.int32, (nblk, QBLOCK), 0)

    def raw_scores(c):
        t0 = t_base + c * QBLOCK
        lanes = slice(c * QBLOCK, (c + 1) * QBLOCK)
        q4 = jnp.concatenate([q_ref[0, r * HEAD_DIM:(r + 1) * HEAD_DIM, lanes] for r in range(REP)], axis=1)
        s = jnp.dot(kc_ref[0, 0, 0:rows, :], q4, preferred_element_type=F32)
        edge = jnp.where(n_io * CMP_STRIDE + (CMP_BLOCK - 1) <= t0 + lane_t, s[nomask:], -jnp.inf)
        s = jnp.concatenate([s[:nomask], edge], axis=0) if nomask else edge
        s_ref[c, 0:rows, :] = s
        m = jnp.max(s, axis=0, keepdims=True)
        return jnp.where(m == -jnp.inf, 0.0, m)

    def attend(c, m):
        t0 = t_base + c * QBLOCK
        lanes = slice(c * QBLOCK, (c + 1) * QBLOCK)
        e = jnp.exp2(s_ref[c, 0:rows, :] - m)
        den = jnp.maximum(jnp.sum(e, axis=0, keepdims=True), 1e-30)
        p = e * (1.0 / den)
        o_t = jnp.dot(vct_ref[0, 0, :, 0:rows], p.astype(BF16), preferred_element_type=F32)
        for r in range(REP):
            oc_ref[0, r * HEAD_DIM:(r + 1) * HEAD_DIM, lanes] = (
                o_t[:, r * QBLOCK:(r + 1) * QBLOCK].astype(oc_ref.dtype))

        pg = ((p[:, 0:QBLOCK] + p[:, QBLOCK:2 * QBLOCK]) + p[:, 2 * QBLOCK:3 * QBLOCK]) + p[:, 3 * QBLOCK:]
        pg_ref[c, 0:8, :] = jnp.zeros((8, QBLOCK), F32)
        pg_ref[c, 8:8 + rows, :] = pg
        ps = pg_ref[c, pl.ds(7, nblk, stride=ratio), :]
        for o in range(1, ratio + 1):
            ps = ps + pg_ref[c, pl.ds(7 + o, nblk, stride=ratio), :]
        cur = (t0 + lax.broadcasted_iota(jnp.int32, (nblk, QBLOCK), 1)) // SEL_BLOCK
        valid = blk <= cur
        forced = (blk == 0) | (blk == cur) | (blk == cur - 1)
        return jnp.where(valid & ~forced, ps, -jnp.inf), valid

    scores, valids = [], []
    m = raw_scores(0)
    for c in range(cb):
        m_next = raw_scores(c + 1) if c + 1 < cb else None
        score, valid = attend(c, m)
        scores.append(score)
        valids.append(valid)
        m = m_next

    def pick_one(_, rems):
        out = []
        for rem in rems:
            top = jnp.max(rem, axis=0, keepdims=True)
            first = jnp.min(jnp.where(rem == top, blk, nblk), axis=0, keepdims=True)
            out.append(jnp.where(blk == first, -jnp.inf, rem))
        return tuple(out)

    rems = lax.fori_loop(0, max(topk - 3, 0), pick_one, tuple(scores))
    for c in range(cb):
        lanes = slice(c * QBLOCK, (c + 1) * QBLOCK)
        sel_ref[0, 0:nblk, lanes] = jnp.where((rems[c] == -jnp.inf) & valids[c], 0.0, -MASK_BIG).astype(BF16)
        if nblk < nsel:
            sel_ref[0, nblk:, lanes] = jnp.full((nsel - nblk, QBLOCK), -MASK_BIG, BF16)


def _cmpsel_kernel(q_ref, kc_ref, vct_ref, oc_ref, sel_ref, pg_ref, s_ref, *, n16, nsel, topk, nvar, cb):
    step = pl.program_id(1)
    per = n16 // nvar
    spv = per // (QBLOCK // CMP_STRIDE) // cb
    for k in range(1, nvar + 1):
        @pl.when((step >= (k - 1) * spv) & (step < k * spv))
        def _(k=k):
            _cmpsel_body(q_ref, kc_ref, vct_ref, oc_ref, sel_ref, pg_ref, s_ref, step * (cb * QBLOCK), cb=cb,
                         rows=per * k, nomask=max(per * (k - 1) - 8, 0), nsel=nsel, topk=topk)


def _cmp_select(q, kc_all, vct_all, s, cb=CMP_CHUNKS):
    n16 = s // CMP_STRIDE
    nsel = s // SEL_BLOCK
    topk = min(SEL_TOPK, nsel)
    nvar = max(n16 // CMP_VARIANT_KEYS, 1)
    return pl.pallas_call(
        functools.partial(_cmpsel_kernel, n16=n16, nsel=nsel, topk=topk, nvar=nvar, cb=cb),
        grid=(GROUPS, s // (QBLOCK * cb)),
        in_specs=[pl.BlockSpec((1, REP * HEAD_DIM, QBLOCK * cb), lambda g, c: (g, 0, c)),
                  pl.BlockSpec((1, 1, n16, HEAD_DIM), lambda g, c: (0, g, 0, 0)),
                  pl.BlockSpec((1, 1, HEAD_DIM, n16), lambda g, c: (1, g, 0, 0))],
        out_specs=[pl.BlockSpec((1, REP * HEAD_DIM, QBLOCK * cb), lambda g, c: (g, 0, c)),
                   pl.BlockSpec((1, nsel, QBLOCK * cb), lambda g, c: (g, 0, c))],
        out_shape=[jax.ShapeDtypeStruct((GROUPS, REP * HEAD_DIM, s), BF16),
                   jax.ShapeDtypeStruct((GROUPS, nsel, s), BF16)],
        scratch_shapes=[pltpu.VMEM((cb, n16 + 8, QBLOCK), F32), pltpu.VMEM((cb, n16, HQ), F32)],
        compiler_params=_params("arbitrary", "arbitrary"),
        name="cmp_select",
    )(q, kc_all, vct_all)


def _fill_bias_strip(strip_ref, thr_ref, tab_ref, g, *, dist0, lo, hi, minus_far):
    nrows = strip_ref.shape[0]

    def block(bi, carry):
        i0 = pl.multiple_of(bi * QBLOCK, QBLOCK)
        d = (lax.broadcasted_iota(jnp.int32, (QBLOCK, QBLOCK), 1)
             - lax.broadcasted_iota(jnp.int32, (QBLOCK, QBLOCK), 0) + (dist0 - i0))
        ok = d >= lo if hi is None else (d >= lo) & (d < hi)
        for r in range(REP):
            base = (g * REP + r) * NUM_BUCKETS
            out = jnp.full((QBLOCK, QBLOCK), tab_ref[base], F32)
            for b in range(1, NUM_BUCKETS):
                out = jnp.where(d >= thr_ref[b], tab_ref[base + b], out)
            if minus_far:
                out = out - tab_ref[base + NUM_BUCKETS - 1]
            strip_ref[pl.ds(i0, QBLOCK), r * QBLOCK:(r + 1) * QBLOCK] = jnp.where(ok, out, -MASK_BIG)
        return carry

    lax.fori_loop(0, nrows // QBLOCK, block, 0)


def _sel_kernel(thr_ref, tab_ref, q_ref, sel_ref, kaug_ref, vt_ref, cst_ref, o_ref,
                qa_ref, s_ref, tmax_ref, m_ref, acc_ref, strip_ref, *, tk):
    g = pl.program_id(0)
    st = pl.program_id(1)
    nb = tk // SEL_BLOCK
    cpt = tk // QBLOCK
    kd = GROUPS * HEAD_DIM
    dmax = 2 * tk - QBLOCK

    @pl.when(st == 0)
    def _():
        _fill_bias_strip(strip_ref, thr_ref, tab_ref, g, dist0=dmax, lo=0, hi=None, minus_far=True)

    qa_ref[...] = jnp.zeros(qa_ref.shape, BF16)
    for b in range(2):
        qa_ref[b, kd + nb:kd + nb + 16, :] = cst_ref[0]
    m_ref[...] = jnp.full(m_ref.shape, -jnp.inf, F32)
    acc_ref[...] = jnp.zeros(acc_ref.shape, F32)
    row0 = pl.multiple_of(g * HEAD_DIM, HEAD_DIM)

    def far_tiles(count):

        def advance(c, j):
            wrap = j + 1 == count
            return jnp.where(wrap, c + 1, c), jnp.where(wrap, 0, j + 1)

        def qk(c, j, buf):
            cc = jnp.minimum(c, cpt - 1)
            lane0 = pl.multiple_of(cc * QBLOCK, QBLOCK)
            for r in range(REP):
                qa_ref[buf, pl.ds(row0, HEAD_DIM), r * QBLOCK:(r + 1) * QBLOCK] = (
                    q_ref[0, r * HEAD_DIM:(r + 1) * HEAD_DIM, pl.ds(lane0, QBLOCK)])
            sb = sel_ref[0, pl.ds(pl.multiple_of(j * nb, nb), nb), pl.ds(lane0, QBLOCK)]
            for r in range(REP):
                qa_ref[buf, kd:kd + nb, r * QBLOCK:(r + 1) * QBLOCK] = sb
            k0 = pl.multiple_of(j * tk, tk)
            s = jnp.dot(kaug_ref[pl.ds(k0, tk), :], qa_ref[buf], preferred_element_type=F32)
            s_ref[buf] = s
            tmax_ref[buf] = jnp.max(s, axis=0, keepdims=True)

        def softmax_pv(c, j, buf):
            k0 = pl.multiple_of(j * tk, tk)
            m_prev = m_ref[c]
            m_new = jnp.maximum(m_prev, tmax_ref[buf])
            alpha = jnp.exp2(m_prev - m_new)
            p = jnp.exp2(s_ref[buf] - m_new)
            acc_ref[c] = alpha * acc_ref[c] + jnp.dot(vt_ref[0, :, pl.ds(k0, tk)], p.astype(BF16),
                                                      preferred_element_type=F32)
            m_ref[c] = m_new

        zero = jnp.int32(0)
        qk(zero, zero, 0)

        def trip(_, carry):
            cur = carry
            for k in range(TILES_PER_TRIP):
                nxt = advance(*cur)
                qk(*nxt, (k + 1) % 2)
                softmax_pv(*cur, k % 2)
                cur = nxt
            return cur

        lax.fori_loop(0, (cpt * count) // TILES_PER_TRIP, trip, (zero, zero))

    def near_tiles(with_previous):
        tiles = []
        for c in range(cpt):
            if with_previous:
                tiles.append((c, st - 1, tk + c * QBLOCK, tk, False))
            tiles.append((c, st, c * QBLOCK, min(tk, -(-(c + 1) * QBLOCK // 256) * 256), True))

        def qk(c, j, delta, rows, diagonal, buf):
            lanes = slice(c * QBLOCK, (c + 1) * QBLOCK)
            for r in range(REP):
                qa_ref[buf, pl.ds(row0, HEAD_DIM), r * QBLOCK:(r + 1) * QBLOCK] = (
                    q_ref[0, r * HEAD_DIM:(r + 1) * HEAD_DIM, lanes])
            sb = sel_ref[0, pl.ds(pl.multiple_of(j * nb, nb), nb), lanes]
            for r in range(REP):
                qa_ref[buf, kd:kd + nb, r * QBLOCK:(r + 1) * QBLOCK] = sb
            k0 = pl.multiple_of(j * tk, tk)
            s = (jnp.dot(kaug_ref[pl.ds(k0, rows), :], qa_ref[buf], preferred_element_type=F32)
                 + strip_ref[dmax - delta:dmax - delta + rows, :])
            s_ref[buf, 0:rows, :] = s
            tmax_ref[buf] = jnp.max(s, axis=0, keepdims=True)

        def softmax_pv(c, j, rows, diagonal, buf):
            k0 = pl.multiple_of(j * tk, tk)
            m_prev = m_ref[c]
            m_new = jnp.maximum(m_prev, tmax_ref[buf])
            alpha = jnp.exp2(m_prev - m_new)
            p = jnp.exp2(s_ref[buf, 0:rows, :] - m_new)
            acc = alpha * acc_ref[c] + jnp.dot(vt_ref[0, :, pl.ds(k0, rows)], p.astype(BF16),
                                               preferred_element_type=F32)
            acc_ref[c] = acc
            m_ref[c] = m_new
            if diagonal:
                out = acc[0:HEAD_DIM] * (1.0 / jnp.maximum(acc[HEAD_DIM:HEAD_DIM + 1], 1e-30))
                for r in range(REP):
                    o_ref[0, r * HEAD_DIM:(r + 1) * HEAD_DIM, c * QBLOCK:(c + 1) * QBLOCK] = (
                        out[:, r * QBLOCK:(r + 1) * QBLOCK].astype(o_ref.dtype))

        qk(*tiles[0], 0)
        for n, (c, j, delta, rows, diagonal) in enumerate(tiles):
            if n + 1 < len(tiles):
                qk(*tiles[n + 1], (n + 1) % 2)
            softmax_pv(c, j, rows, diagonal, n % 2)

    @pl.when(st > 1)
    def _():
        far_tiles(st - 1)

    @pl.when(st == 0)
    def _():
        near_tiles(False)

    @pl.when(st > 0)
    def _():
        near_tiles(True)


def _selected(thr, tab, q, sel, kaug, vt, cst, s, tk):
    nsel = s // SEL_BLOCK
    sl = 3 * tk - QBLOCK
    cpt = tk // QBLOCK
    smem = pl.BlockSpec(memory_space=pltpu.SMEM)
    return pl.pallas_call(
        functools.partial(_sel_kernel, tk=tk),
        grid=(GROUPS, s // tk),
        in_specs=[smem, smem,
                  pl.BlockSpec((1, REP * HEAD_DIM, tk), lambda g, c: (g, 0, c)),
                  pl.BlockSpec((1, nsel, tk), lambda g, c: (g, 0, c)),
                  pl.BlockSpec(kaug.shape, lambda g, c: (0, 0)),
                  pl.BlockSpec((1, vt.shape[1], s), lambda g, c: (g, 0, 0)),
                  pl.BlockSpec((1, 16, HQ), lambda g, c: (g, 0, 0))],
        out_specs=pl.BlockSpec((1, REP * HEAD_DIM, tk), lambda g, c: (g, 0, c)),
        out_shape=jax.ShapeDtypeStruct((GROUPS, REP * HEAD_DIM, s), BF16),
        scratch_shapes=[pltpu.VMEM((2, kaug.shape[1], HQ), BF16), pltpu.VMEM((2, tk, HQ), F32),
                        pltpu.VMEM((2, 1, HQ), F32),
                        pltpu.VMEM((cpt, 1, HQ), F32), pltpu.VMEM((cpt, vt.shape[1], HQ), F32),
                        pltpu.VMEM((sl, HQ), F32)],
        compiler_params=_params("arbitrary", "arbitrary"),
        name="selected_attn",
    )(thr, tab, q, sel, kaug, vt, cst)


def _band_kernel(thr_ref, tab_ref, q_ref, k_ref, vt_ref, sink_ref, o_ref, qa_ref, strip_ref, s_ref, *,
                 window, use_sink, cb):
    g = pl.program_id(0)

    @pl.when(pl.program_id(1) == 0)
    def _():
        _fill_bias_strip(strip_ref, thr_ref, tab_ref, g, dist0=window, lo=0, hi=window, minus_far=False)

    base = pl.multiple_of(pl.program_id(1) * (QBLOCK * cb), QBLOCK * cb)
    nk = window + QBLOCK
    qa_ref[...] = jnp.zeros(qa_ref.shape, BF16)
    row0 = pl.multiple_of(g * HEAD_DIM, HEAD_DIM)
    for c in range(cb):
        for r in range(REP):
            qa_ref[c, pl.ds(row0, HEAD_DIM), r * QBLOCK:(r + 1) * QBLOCK] = (
                q_ref[0, r * HEAD_DIM:(r + 1) * HEAD_DIM, c * QBLOCK:(c + 1) * QBLOCK])

    def scores(c, first_step):
        t0 = base + c * QBLOCK
        s = jnp.dot(k_ref[pl.ds(t0, nk), :], qa_ref[c], preferred_element_type=F32) + strip_ref[...]
        if first_step and c * QBLOCK < window:
            row = lax.broadcasted_iota(jnp.int32, (nk, HQ), 0)
            s = jnp.where(row >= window - c * QBLOCK, s, -MASK_BIG)
        s_ref[c] = s
        m = jnp.max(s, axis=0, keepdims=True)
        return jnp.maximum(m, sink_ref[0]) if use_sink else m

    def attend(c, m):
        t0 = base + c * QBLOCK
        e = jnp.exp2(s_ref[c] - m)
        o_t = jnp.dot(vt_ref[0, :, pl.ds(t0, nk)], e.astype(BF16), preferred_element_type=F32)
        den = o_t[HEAD_DIM:HEAD_DIM + 1]
        den = den + jnp.exp2(sink_ref[0] - m) if use_sink else jnp.maximum(den, 1e-30)
        o_t = o_t[0:HEAD_DIM] * (1.0 / den)
        for r in range(REP):
            o_ref[0, r * HEAD_DIM:(r + 1) * HEAD_DIM, c * QBLOCK:(c + 1) * QBLOCK] = (
                o_t[:, r * QBLOCK:(r + 1) * QBLOCK].astype(o_ref.dtype))

    def chunks(first_step):
        m = scores(0, first_step)
        for c in range(cb):
            m_next = scores(c + 1, first_step) if c + 1 < cb else None
            attend(c, m)
            m = m_next

    @pl.when(pl.program_id(1) == 0)
    def _():
        chunks(True)

    @pl.when(pl.program_id(1) > 0)
    def _():
        chunks(False)


def _banded(thr, tab, q, kpad, vtpad, sink, s, window, use_sink, name, cb=BAND_CHUNKS):
    assert QBLOCK * cb >= window
    nk = window + QBLOCK
    smem = pl.BlockSpec(memory_space=pltpu.SMEM)
    return pl.pallas_call(
        functools.partial(_band_kernel, window=window, use_sink=use_sink, cb=cb),
        grid=(GROUPS, s // (QBLOCK * cb)),
        in_specs=[smem, smem,
                  pl.BlockSpec((1, REP * HEAD_DIM, QBLOCK * cb), lambda g, c: (g, 0, c)),
                  pl.BlockSpec(kpad.shape, lambda g, c: (0, 0)),
                  pl.BlockSpec((1, vtpad.shape[1], s + window), lambda g, c: (g, 0, 0)),
                  pl.BlockSpec((1, 1, HQ), lambda g, c: (g, 0, 0))],
        out_specs=pl.BlockSpec((1, REP * HEAD_DIM, QBLOCK * cb), lambda g, c: (g, 0, c)),
        out_shape=jax.ShapeDtypeStruct((GROUPS, REP * HEAD_DIM, s), BF16),
        scratch_shapes=[pltpu.VMEM((cb, GROUPS * HEAD_DIM, HQ), BF16), pltpu.VMEM((nk, HQ), F32),
                        pltpu.VMEM((cb, nk, HQ), F32)],
        compiler_params=_params("arbitrary", "arbitrary"),
        name=name,
    )(thr, tab, q, kpad, vtpad, sink)


def _merge_kernel(oc_ref, os_ref, ow_ref, ob_ref, gn_ref, gab_ref, x_ref, wa_ref, wb_ref, wo_ref, gp_ref, o_ref):
    d = x_ref.shape[1]
    gn = jax.nn.sigmoid(gn_ref[...])
    parts = []
    for h in range(HEADS):
        rows = slice(h * HEAD_DIM, (h + 1) * HEAD_DIM)
        parts.append(gn[h:h + 1] * oc_ref[rows, :].astype(F32)
                     + gn[HEADS + h:HEADS + h + 1] * os_ref[rows, :].astype(F32)
                     + gn[2 * HEADS + h:2 * HEADS + h + 1] * ow_ref[rows, :].astype(F32))
    oa_t = jnp.concatenate(parts, axis=0).astype(BF16)
    ya = lax.dot_general(oa_t, wa_ref[...], _TN, preferred_element_type=F32)
    yb = lax.dot_general(ob_ref[...], wb_ref[...], _TN, preferred_element_type=F32)
    y = (jax.nn.sigmoid(gab_ref[:, :d].astype(F32)) * ya
         + jax.nn.sigmoid(gab_ref[:, d:].astype(F32)) * yb)
    z = jnp.dot(y.astype(BF16), wo_ref[...], preferred_element_type=F32)
    o_ref[...] = x_ref[...] + _rms(z, gp_ref[...])


def _merge(oc, os_, ow, ob, gn, gab, x, wa, wb, wo, gpost, ts):
    s, d = x.shape
    hd = oc.shape[0]
    tok = lambda rows: pl.BlockSpec((rows, ts), lambda i: (0, i))
    full = lambda shp: pl.BlockSpec(shp, lambda i: (0, 0))
    return pl.pallas_call(
        _merge_kernel,
        grid=(s // ts,),
        in_specs=[tok(hd), tok(hd), tok(hd), tok(hd), tok(gn.shape[0]),
                  pl.BlockSpec((ts, 2 * d), lambda i: (i, 0)), pl.BlockSpec((ts, d), lambda i: (i, 0)),
                  full((hd, d)), full((hd, d)), full((d, d)), full((1, d))],
        out_specs=pl.BlockSpec((ts, d), lambda i: (i, 0)),
        out_shape=jax.ShapeDtypeStruct((s, d), F32),
        compiler_params=_params("arbitrary"),
        name="merge_out_proj",
    )(oc, os_, ow, ob, gn, gab, x, wa, wb, wo, gpost)


def _mlp_kernel(h_ref, gpre_ref, w1_ref, w2_ref, gpost_ref, o_ref, u_ref, acc_ref):
    f = pl.program_id(1)

    @pl.when(f == 0)
    def _():
        u_ref[...] = _rms(h_ref[...], gpre_ref[...]).astype(BF16)
        acc_ref[...] = jnp.zeros(acc_ref.shape, F32)

    a = jnp.maximum(jnp.dot(u_ref[...], w1_ref[...], preferred_element_type=F32), 0.0)
    acc_ref[...] += jnp.dot((a * a).astype(BF16), w2_ref[...], preferred_element_type=F32)

    @pl.when(f == pl.num_programs(1) - 1)
    def _():
        o_ref[...] = h_ref[...] + _rms(acc_ref[...], gpost_ref[...])


def _mlp(h, gpre, w1, w2, gpost, ts, tf):
    s, d = h.shape
    dff = w1.shape[1]
    return pl.pallas_call(
        _mlp_kernel,
        grid=(s // ts, dff // tf),
        in_specs=[pl.BlockSpec((ts, d), lambda i, f: (i, 0)), pl.BlockSpec((1, d), lambda i, f: (0, 0)),
                  pl.BlockSpec((d, tf), lambda i, f: (0, f)), pl.BlockSpec((tf, d), lambda i, f: (f, 0)),
                  pl.BlockSpec((1, d), lambda i, f: (0, 0))],
        out_specs=pl.BlockSpec((ts, d), lambda i, f: (i, 0)),
        out_shape=jax.ShapeDtypeStruct((s, d), F32),
        scratch_shapes=[pltpu.VMEM((ts, d), BF16), pltpu.VMEM((ts, d), F32)],
        compiler_params=_params("arbitrary", "arbitrary"),
        name="mlp_relu2",
    )(h, gpre, w1, w2, gpost)


def kernel(x, norm_mix_pre, norm_mix_post, norm_mlp_pre, norm_mlp_post, w_in,
           cmp_pos_k, cmp_w1_k, cmp_w2_k, cmp_pos_v, cmp_w1_v, cmp_w2_v,
           attn_sinks, rel_bias, w_up_nsa, w_up_swa, w_out, w_mlp_in, w_mlp_out):
    b, s, d = x.shape
    assert b == 1 and s % SEL_TILE == 0 and w_in.shape[0] == 1
    assert SEL_TILE >= MAX_DISTANCE
    qd = HEADS * HEAD_DIM
    kvd = GROUPS * HEAD_DIM
    sizes = (qd,) + (kvd,) * 6 + (3 * HEADS, qd, kvd, kvd, d, d)
    offs = [0]
    for z in sizes:
        offs.append(offs[-1] + z)
    w = w_in[0]
    col = lambda i: w[:, offs[i]:offs[i + 1]]
    (w_qn, w_kc, w_vc, w_ks, w_vs, w_kw, w_vw, w_gn, w_qs, w_k_s, w_v_s, w_ga, w_gb) = [col(i) for i in range(13)]
    scale = HEAD_DIM ** -0.5 * LOG2E
    w_gn = w_gn.reshape(d, HEADS, 3).transpose(0, 2, 1).reshape(d, 3 * HEADS)
    w_gn = jnp.pad(w_gn, ((0, 0), (0, 8)))
    wn = jnp.concatenate([w_kc, w_vc, w_ks, w_kw, w_k_s], axis=1).astype(BF16)
    wg = jnp.concatenate([w_ga, w_gb], axis=1).astype(BF16)
    wt = jnp.concatenate([w_qn * scale, w_qs * scale, w_vs, w_vw, w_v_s, w_gn], axis=1).T.astype(BF16)
    nqv = 2 * qd + 3 * kvd

    x2 = x[0]
    kv, gab, qv, gn, kb = _project(x2, norm_mix_pre, wn, wg, wt, nqv, ts=PROJ_TOKENS)
    q_nsa = qv[0:qd].reshape(GROUPS, REP * HEAD_DIM, s)
    q_swa = qv[qd:2 * qd].reshape(GROUPS, REP * HEAD_DIM, s)
    vs_t = qv[2 * qd:2 * qd + kvd].reshape(GROUPS, HEAD_DIM, s)
    vw_t = qv[2 * qd + kvd:2 * qd + 2 * kvd].reshape(GROUPS, HEAD_DIM, s)
    vswa_t = qv[2 * qd + 2 * kvd:].reshape(GROUPS, HEAD_DIM, s)

    n16 = s // CMP_STRIDE
    pos8 = jnp.stack([cmp_pos_k[0], cmp_pos_v[0]]).reshape(2, 1, CMP_BLOCK * HEAD_DIM)
    pos8 = jnp.broadcast_to(pos8, (2, 8, CMP_BLOCK * HEAD_DIM)).astype(BF16)
    w1 = jnp.stack([cmp_w1_k[0], cmp_w1_v[0]]).astype(BF16)
    w2 = jnp.stack([cmp_w2_k[0], cmp_w2_v[0]]).astype(BF16)
    cmp_n, cmp_t = _compress(kb, pos8, w1, w2, w2.transpose(0, 2, 1))

    oc, sel = _cmp_select(q_nsa, cmp_n, cmp_t, s)

    bias_nsa = rel_bias[:, :HEADS].reshape(NUM_BUCKETS, GROUPS, REP) * LOG2E
    bias_swa = rel_bias[:, HEADS:].reshape(NUM_BUCKETS, GROUPS, REP) * LOG2E
    thr = jnp.asarray(_bucket_starts())
    tab_nsa = bias_nsa.transpose(1, 2, 0).reshape(-1)
    tab_swa = bias_swa.transpose(1, 2, 0).reshape(-1)

    tk = SEL_TILE
    nb = tk // SEL_BLOCK
    posn = jnp.arange(s)
    onehot = ((posn // SEL_BLOCK) % nb)[:, None] == jnp.arange(nb)[None, :]
    aug = jnp.concatenate([onehot.astype(BF16), jnp.ones((s, 2), BF16),
                           jnp.zeros((s, kvd - nb - 2), BF16)], axis=1)
    kaug = jnp.concatenate([kv[:, 0:kvd], aug], axis=1)
    far = bias_nsa[NUM_BUCKETS - 1]
    far_hi = far.astype(BF16)
    far_lo = (far - far_hi.astype(F32)).astype(BF16)
    cst = jnp.stack([far_hi, far_lo], axis=1)
    cst = jnp.broadcast_to(cst[:, :, :, None], (GROUPS, 2, REP, QBLOCK)).reshape(GROUPS, 2, HQ)
    cst = jnp.pad(cst, ((0, 0), (0, 14), (0, 0)))
    ones_row = jnp.concatenate([jnp.ones((GROUPS, 1, s), BF16), jnp.zeros((GROUPS, 15, s), BF16)], axis=1)
    vs_aug = jnp.concatenate([vs_t, ones_row], axis=1)
    o_sel = _selected(thr, tab_nsa, q_nsa, sel, kaug, vs_aug, cst, s, tk)

    pad_rows = lambda a, wdw: jnp.pad(a, ((wdw, 0), (0, 0)))
    pad_lanes = lambda a, wdw: jnp.pad(jnp.concatenate([a, ones_row], axis=1), ((0, 0), (0, 0), (wdw, 0)))
    no_sink = jnp.zeros((GROUPS, 1, HQ), F32)
    o_win = _banded(thr, tab_nsa, q_nsa, pad_rows(kv[:, kvd:2 * kvd], NSA_WINDOW), pad_lanes(vw_t, NSA_WINDOW),
                    no_sink, s, NSA_WINDOW, False, "nsa_window")
    sink = attn_sinks[0].reshape(GROUPS, 1, REP, 1).astype(F32) * LOG2E
    sink = jnp.broadcast_to(sink, (GROUPS, 1, REP, QBLOCK)).reshape(GROUPS, 1, HQ)
    o_swa = _banded(thr, tab_swa, q_swa, pad_rows(kv[:, 2 * kvd:3 * kvd], SWA_WINDOW), pad_lanes(vswa_t, SWA_WINDOW),
                    sink, s, SWA_WINDOW, True, "swa_sink")

    flat = lambda a: a.reshape(qd, s)
    h1 = _merge(flat(oc), flat(o_sel), flat(o_win), flat(o_swa), gn, gab, x2,
                w_up_nsa[0].astype(BF16), w_up_swa[0].astype(BF16), w_out[0].astype(BF16),
                norm_mix_post, ts=PROJ_TOKENS)
    out = _mlp(h1, norm_mlp_pre, w_mlp_in[0].astype(BF16), w_mlp_out[0].astype(BF16), norm_mlp_post,
               ts=MLP_TOKENS, tf=MLP_HIDDEN)
    return out[None]
```

```python
import functools
import math

import jax
import jax.numpy as jnp
import numpy as np
from jax import lax
from jax.experimental import pallas as pl
from jax.experimental.pallas import tpu as pltpu

F32 = jnp.float32
BF16 = jnp.bfloat16

HEAD_DIM = 64
GROUPS = 2
REP = 4
HEADS = GROUPS * REP
CMP_BLOCK = 32
CMP_STRIDE = 16
SEL_BLOCK = 64
SEL_TOPK = 16
NSA_WINDOW = 512
SWA_WINDOW = 128
QBLOCK = 128
NUM_BUCKETS = 32
MAX_DISTANCE = 1024
RMS_EPS = 1e-6
LOG2E = math.log2(math.e)
MASK_BIG = 2.0 ** 99
HQ = REP * QBLOCK

SEL_TILE = 1024
TILES_PER_TRIP = 8
PROJ_TOKENS = 512
MLP_TOKENS = 1024
MLP_HIDDEN = 2048
BAND_CHUNKS = 16
CMP_CHUNKS = 4
CMP_VARIANT_KEYS = 128
V7X_VMEM_BYTES = 64 * 1024 * 1024
VMEM_LIMIT = V7X_VMEM_BYTES * 7 // 8

_NT = (((1,), (1,)), ((), ()))
_TN = (((0,), (0,)), ((), ()))


def _params(*sem):
    return pltpu.CompilerParams(dimension_semantics=sem, vmem_limit_bytes=VMEM_LIMIT)


def _bucket_starts():
    max_exact = NUM_BUCKETS // 2
    d = np.arange(2 * MAX_DISTANCE)
    large = max_exact + (np.log(np.maximum(d, 1).astype(np.float64) / max_exact)
                         / math.log(MAX_DISTANCE / max_exact) * (NUM_BUCKETS - max_exact)).astype(np.int64)
    bucket = np.where(d < max_exact, d, np.minimum(large, NUM_BUCKETS - 1))
    return (bucket[None, :] < np.arange(NUM_BUCKETS)[:, None]).sum(axis=1).astype(np.int32)


def _rms(x, gain):
    return x * lax.rsqrt(jnp.mean(x * x, axis=-1, keepdims=True) + RMS_EPS) * gain


def _proj_kernel(x_ref, g_ref, wn_ref, wg_ref, wt_ref, kv_ref, gab_ref, qv_ref, gn_ref, kb_ref, y_ref, *, nqv):
    u = _rms(x_ref[...], g_ref[...]).astype(BF16)
    kvf = jnp.dot(u, wn_ref[...], preferred_element_type=F32)
    gw = GROUPS * HEAD_DIM
    kv_ref[...] = kvf[:, 2 * gw:].astype(BF16)
    nrow = y_ref.shape[1] // CMP_STRIDE
    for a in range(2):
        y_ref[a] = kvf[:, a * gw:(a + 1) * gw]
        both = [y_ref[a, pl.ds(p, nrow, stride=CMP_STRIDE), :] for p in range(CMP_STRIDE)]
        for gi in range(GROUPS):
            kb_ref[a, gi] = jnp.concatenate([t[:, gi * HEAD_DIM:(gi + 1) * HEAD_DIM] for t in both],
                                            axis=1).astype(BF16)
    gab_ref[...] = jnp.dot(u, wg_ref[...], preferred_element_type=F32).astype(BF16)
    t = lax.dot_general(wt_ref[...], u, _NT, preferred_element_type=F32)
    qv_ref[...] = t[:nqv].astype(BF16)
    gn_ref[...] = t[nqv:]


def _project(x, gain, wn, wg, wt, nqv, ts):
    s, d = x.shape
    nn, ng, ntr = wn.shape[1] - 2 * GROUPS * HEAD_DIM, wg.shape[1], wt.shape[0]
    full = lambda shp: pl.BlockSpec(shp, lambda i: (0, 0))
    return pl.pallas_call(
        functools.partial(_proj_kernel, nqv=nqv),
        grid=(s // ts,),
        in_specs=[pl.BlockSpec((ts, d), lambda i: (i, 0)), full((1, d)),
                  full(wn.shape), full((d, ng)), full((ntr, d))],
        out_specs=[pl.BlockSpec((ts, nn), lambda i: (i, 0)),
                   pl.BlockSpec((ts, ng), lambda i: (i, 0)),
                   pl.BlockSpec((nqv, ts), lambda i: (0, i)),
                   pl.BlockSpec((ntr - nqv, ts), lambda i: (0, i)),
                   pl.BlockSpec((2, GROUPS, ts // CMP_STRIDE, CMP_STRIDE * HEAD_DIM), lambda i: (0, 0, i, 0))],
        out_shape=[jax.ShapeDtypeStruct((s, nn), BF16), jax.ShapeDtypeStruct((s, ng), BF16),
                   jax.ShapeDtypeStruct((nqv, s), BF16), jax.ShapeDtypeStruct((ntr - nqv, s), F32),
                   jax.ShapeDtypeStruct((2, GROUPS, s // CMP_STRIDE, CMP_STRIDE * HEAD_DIM), BF16)],
        scratch_shapes=[pltpu.VMEM((2, ts, GROUPS * HEAD_DIM), F32)],
        compiler_params=_params("arbitrary"),
        name="in_proj",
    )(x, gain, wn, wg, wt)


def _gelu_tanh(x):
    return x * (0.5 * (1.0 + jnp.tanh(math.sqrt(2.0 / math.pi) * (x + 0.044715 * (x * x * x)))))


def _compress_kernel(kb_ref, pos_ref, w1_ref, w2_ref, w2t_ref, o_ref, ot_ref):
    kb = kb_ref[0, 0]
    half = kb.shape[1]
    n16 = kb.shape[0]
    first = jnp.dot(kb, w1_ref[0, :half, :], preferred_element_type=F32)
    second = jnp.dot(kb, w1_ref[0, half:, :], preferred_element_type=F32)
    posb = jnp.dot(pos_ref[0], w1_ref[0], preferred_element_type=F32)[0:1]
    pre = first + pltpu.roll(second, n16 - 1, 0) + posb
    h = _gelu_tanh(pre).astype(BF16)
    o_ref[0, 0] = jnp.dot(h, w2_ref[0], preferred_element_type=F32).astype(BF16)
    ot_ref[0, 0] = lax.dot_general(w2t_ref[0], h, _NT, preferred_element_type=F32).astype(BF16)


def _compress(kb, pos8, w1, w2, w2t):
    _, g, n16, cin2 = kb.shape
    hid = w1.shape[2]
    return pl.pallas_call(
        _compress_kernel,
        grid=(2, g),
        in_specs=[pl.BlockSpec((1, 1, n16, cin2), lambda a, b: (a, b, 0, 0)),
                  pl.BlockSpec((1, 8, 2 * cin2), lambda a, b: (a, 0, 0)),
                  pl.BlockSpec((1, 2 * cin2, hid), lambda a, b: (a, 0, 0)),
                  pl.BlockSpec((1, hid, HEAD_DIM), lambda a, b: (a, 0, 0)),
                  pl.BlockSpec((1, HEAD_DIM, hid), lambda a, b: (a, 0, 0))],
        out_specs=[pl.BlockSpec((1, 1, n16, HEAD_DIM), lambda a, b: (a, b, 0, 0)),
                   pl.BlockSpec((1, 1, HEAD_DIM, n16), lambda a, b: (a, b, 0, 0))],
        out_shape=[jax.ShapeDtypeStruct((2, g, n16, HEAD_DIM), BF16),
                   jax.ShapeDtypeStruct((2, g, HEAD_DIM, n16), BF16)],
        compiler_params=_params("arbitrary", "arbitrary"),
        name="kv_compress",
    )(kb, pos8, w1, w2, w2t)


def _cmpsel_body(q_ref, kc_ref, vct_ref, oc_ref, sel_ref, pg_ref, s_ref, t_base, *, cb, rows, nomask, nsel, topk):
    mrows = rows - nomask
    ratio = SEL_BLOCK // CMP_STRIDE
    nblk = rows // ratio
    n_io = nomask + lax.broadcasted_iota(jnp.int32, (mrows, HQ), 0)
    lane_t = lax.broadcasted_iota(jnp.int32, (mrows, HQ), 1) & (QBLOCK - 1)
    blk = lax.broadcasted_iota(jnp.int32, (nblk, QBLOCK), 0)

    def raw_scores(c):
        t0 = t_base + c * QBLOCK
        lanes = slice(c * QBLOCK, (c + 1) * QBLOCK)
        q4 = jnp.concatenate([q_ref[0, r * HEAD_DIM:(r + 1) * HEAD_DIM, lanes] for r in range(REP)], axis=1)
        s = jnp.dot(kc_ref[0, 0, 0:rows, :], q4, preferred_element_type=F32)
        edge = jnp.where(n_io * CMP_STRIDE + (CMP_BLOCK - 1) <= t0 + lane_t, s[nomask:], -jnp.inf)
        s = jnp.concatenate([s[:nomask], edge], axis=0) if nomask else edge
        s_ref[c, 0:rows, :] = s
        m = jnp.max(s, axis=0, keepdims=True)
        return jnp.where(m == -jnp.inf, 0.0, m)

    def attend(c, m):
        t0 = t_base + c * QBLOCK
        lanes = slice(c * QBLOCK, (c + 1) * QBLOCK)
        e = jnp.exp2(s_ref[c, 0:rows, :] - m)
        den = jnp.maximum(jnp.sum(e, axis=0, keepdims=True), 1e-30)
        p = e * (1.0 / den)
        o_t = jnp.dot(vct_ref[0, 0, :, 0:rows], p.astype(BF16), preferred_element_type=F32)
        for r in range(REP):
            oc_ref[0, r * HEAD_DIM:(r + 1) * HEAD_DIM, lanes] = (
                o_t[:, r * QBLOCK:(r + 1) * QBLOCK].astype(oc_ref.dtype))

        pg = ((p[:, 0:QBLOCK] + p[:, QBLOCK:2 * QBLOCK]) + p[:, 2 * QBLOCK:3 * QBLOCK]) + p[:, 3 * QBLOCK:]
        pg_ref[c, 0:8, :] = jnp.zeros((8, QBLOCK), F32)
        pg_ref[c, 8:8 + rows, :] = pg
        ps = pg_ref[c, pl.ds(7, nblk, stride=ratio), :]
        for o in range(1, ratio + 1):
            ps = ps + pg_ref[c, pl.ds(7 + o, nblk, stride=ratio), :]
        cur = (t0 + lax.broadcasted_iota(jnp.int32, (nblk, QBLOCK), 1)) // SEL_BLOCK
        valid = blk <= cur
        forced = (blk == 0) | (blk == cur) | (blk == cur - 1)
        return jnp.where(valid & ~forced, ps, -jnp.inf), valid

    scores, valids = [], []
    m = raw_scores(0)
    for c in range(cb):
        m_next = raw_scores(c + 1) if c + 1 < cb else None
        score, valid = attend(c, m)
        scores.append(score)
        valids.append(valid)
        m = m_next

    def pick_one(_, rems):
        out = []
        for rem in rems:
            top = jnp.max(rem, axis=0, keepdims=True)
            first = jnp.min(jnp.where(rem == top, blk, nblk), axis=0, keepdims=True)
            out.append(jnp.where(blk == first, -jnp.inf, rem))
        return tuple(out)

    rems = lax.fori_loop(0, max(topk - 3, 0), pick_one, tuple(scores))
    for c in range(cb):
        lanes = slice(c * QBLOCK, (c + 1) * QBLOCK)
        sel_ref[0, 0:nblk, lanes] = jnp.where((rems[c] == -jnp.inf) & valids[c], 0.0, -MASK_BIG).astype(BF16)
        if nblk < nsel:
            sel_ref[0, nblk:, lanes] = jnp.full((nsel - nblk, QBLOCK), -MASK_BIG, BF16)


def _cmpsel_kernel(q_ref, kc_ref, vct_ref, oc_ref, sel_ref, pg_ref, s_ref, *, n16, nsel, topk, nvar, cb):
    step = pl.program_id(1)
    per = n16 // nvar
    spv = per // (QBLOCK // CMP_STRIDE) // cb
    for k in range(1, nvar + 1):
        @pl.when((step >= (k - 1) * spv) & (step < k * spv))
        def _(k=k):
            _cmpsel_body(q_ref, kc_ref, vct_ref, oc_ref, sel_ref, pg_ref, s_ref, step * (cb * QBLOCK), cb=cb,
                         rows=per * k, nomask=max(per * (k - 1) - 8, 0), nsel=nsel, topk=topk)


def _cmp_select(q, kc_all, vct_all, s, cb=CMP_CHUNKS):
    n16 = s // CMP_STRIDE
    nsel = s // SEL_BLOCK
    topk = min(SEL_TOPK, nsel)
    nvar = max(n16 // CMP_VARIANT_KEYS, 1)
    return pl.pallas_call(
        functools.partial(_cmpsel_kernel, n16=n16, nsel=nsel, topk=topk, nvar=nvar, cb=cb),
        grid=(GROUPS, s // (QBLOCK * cb)),
        in_specs=[pl.BlockSpec((1, REP * HEAD_DIM, QBLOCK * cb), lambda g, c: (g, 0, c)),
                  pl.BlockSpec((1, 1, n16, HEAD_DIM), lambda g, c: (0, g, 0, 0)),
                  pl.BlockSpec((1, 1, HEAD_DIM, n16), lambda g, c: (1, g, 0, 0))],
        out_specs=[pl.BlockSpec((1, REP * HEAD_DIM, QBLOCK * cb), lambda g, c: (g, 0, c)),
                   pl.BlockSpec((1, nsel, QBLOCK * cb), lambda g, c: (g, 0, c))],
        out_shape=[jax.ShapeDtypeStruct((GROUPS, REP * HEAD_DIM, s), BF16),
                   jax.ShapeDtypeStruct((GROUPS, nsel, s), BF16)],
        scratch_shapes=[pltpu.VMEM((cb, n16 + 8, QBLOCK), F32), pltpu.VMEM((cb, n16, HQ), F32)],
        compiler_params=_params("arbitrary", "arbitrary"),
        name="cmp_select",
    )(q, kc_all, vct_all)


def _fill_bias_strip(strip_ref, thr_ref, tab_ref, g, *, dist0, lo, hi, minus_far):
    nrows = strip_ref.shape[0]

    def block(bi, carry):
        i0 = pl.multiple_of(bi * QBLOCK, QBLOCK)
        d = (lax.broadcasted_iota(jnp.int32, (QBLOCK, QBLOCK), 1)
             - lax.broadcasted_iota(jnp.int32, (QBLOCK, QBLOCK), 0) + (dist0 - i0))
        ok = d >= lo if hi is None else (d >= lo) & (d < hi)
        for r in range(REP):
            base = (g * REP + r) * NUM_BUCKETS
            out = jnp.full((QBLOCK, QBLOCK), tab_ref[base], F32)
            for b in range(1, NUM_BUCKETS):
                out = jnp.where(d >= thr_ref[b], tab_ref[base + b], out)
            if minus_far:
                out = out - tab_ref[base + NUM_BUCKETS - 1]
            strip_ref[pl.ds(i0, QBLOCK), r * QBLOCK:(r + 1) * QBLOCK] = jnp.where(ok, out, -MASK_BIG)
        return carry

    lax.fori_loop(0, nrows // QBLOCK, block, 0)


def _sel_kernel(thr_ref, tab_ref, q_ref, sel_ref, kaug_ref, vt_ref, cst_ref, o_ref,
                qa_ref, s_ref, tmax_ref, m_ref, acc_ref, strip_ref, *, tk):
    g = pl.program_id(0)
    st = pl.program_id(1)
    nb = tk // SEL_BLOCK
    cpt = tk // QBLOCK
    kd = GROUPS * HEAD_DIM
    dmax = 2 * tk - QBLOCK

    @pl.when(st == 0)
    def _():
        _fill_bias_strip(strip_ref, thr_ref, tab_ref, g, dist0=dmax, lo=0, hi=None, minus_far=True)

    qa_ref[...] = jnp.zeros(qa_ref.shape, BF16)
    for b in range(2):
        qa_ref[b, kd + nb:kd + nb + 16, :] = cst_ref[0]
    m_ref[...] = jnp.full(m_ref.shape, -jnp.inf, F32)
    acc_ref[...] = jnp.zeros(acc_ref.shape, F32)
    row0 = pl.multiple_of(g * HEAD_DIM, HEAD_DIM)

    def far_tiles(count):

        def advance(c, j):
            wrap = j + 1 == count
            return jnp.where(wrap, c + 1, c), jnp.where(wrap, 0, j + 1)

        def qk(c, j, buf):
            cc = jnp.minimum(c, cpt - 1)
            lane0 = pl.multiple_of(cc * QBLOCK, QBLOCK)
            for r in range(REP):
                qa_ref[buf, pl.ds(row0, HEAD_DIM), r * QBLOCK:(r + 1) * QBLOCK] = (
                    q_ref[0, r * HEAD_DIM:(r + 1) * HEAD_DIM, pl.ds(lane0, QBLOCK)])
            sb = sel_ref[0, pl.ds(pl.multiple_of(j * nb, nb), nb), pl.ds(lane0, QBLOCK)]
            for r in range(REP):
                qa_ref[buf, kd:kd + nb, r * QBLOCK:(r + 1) * QBLOCK] = sb
            k0 = pl.multiple_of(j * tk, tk)
            s = jnp.dot(kaug_ref[pl.ds(k0, tk), :], qa_ref[buf], preferred_element_type=F32)
            s_ref[buf] = s
            tmax_ref[buf] = jnp.max(s, axis=0, keepdims=True)

        def softmax_pv(c, j, buf):
            k0 = pl.multiple_of(j * tk, tk)
            m_prev = m_ref[c]
            m_new = jnp.maximum(m_prev, tmax_ref[buf])
            alpha = jnp.exp2(m_prev - m_new)
            p = jnp.exp2(s_ref[buf] - m_new)
            acc_ref[c] = alpha * acc_ref[c] + jnp.dot(vt_ref[0, :, pl.ds(k0, tk)], p.astype(BF16),
                                                      preferred_element_type=F32)
            m_ref[c] = m_new

        zero = jnp.int32(0)
        qk(zero, zero, 0)

        def trip(_, carry):
            cur = carry
            for k in range(TILES_PER_TRIP):
                nxt = advance(*cur)
                qk(*nxt, (k + 1) % 2)
                softmax_pv(*cur, k % 2)
                cur = nxt
            return cur

        lax.fori_loop(0, (cpt * count) // TILES_PER_TRIP, trip, (zero, zero))

    def near_tiles(with_previous):
        tiles = []
        for c in range(cpt):
            if with_previous:
                tiles.append((c, st - 1, tk + c * QBLOCK, tk, False))
            tiles.append((c, st, c * QBLOCK, (c + 1) * QBLOCK, True))

        def qk(c, j, delta, rows, diagonal, buf):
            lanes = slice(c * QBLOCK, (c + 1) * QBLOCK)
            for r in range(REP):
                qa_ref[buf, pl.ds(row0, HEAD_DIM), r * QBLOCK:(r + 1) * QBLOCK] = (
                    q_ref[0, r * HEAD_DIM:(r + 1) * HEAD_DIM, lanes])
            sb = sel_ref[0, pl.ds(pl.multiple_of(j * nb, nb), nb), lanes]
            for r in range(REP):
                qa_ref[buf, kd:kd + nb, r * QBLOCK:(r + 1) * QBLOCK] = sb
            k0 = pl.multiple_of(j * tk, tk)
            s = (jnp.dot(kaug_ref[pl.ds(k0, rows), :], qa_ref[buf], preferred_element_type=F32)
                 + strip_ref[dmax - delta:dmax - delta + rows, :])
            s_ref[buf, 0:rows, :] = s
            tmax_ref[buf] = jnp.max(s, axis=0, keepdims=True)

        def softmax_pv(c, j, rows, diagonal, buf):
            k0 = pl.multiple_of(j * tk, tk)
            m_prev = m_ref[c]
            m_new = jnp.maximum(m_prev, tmax_ref[buf])
            alpha = jnp.exp2(m_prev - m_new)
            p = jnp.exp2(s_ref[buf, 0:rows, :] - m_new)
            acc = alpha * acc_ref[c] + jnp.dot(vt_ref[0, :, pl.ds(k0, rows)], p.astype(BF16),
                                               preferred_element_type=F32)
            acc_ref[c] = acc
            m_ref[c] = m_new
            if diagonal:
                out = acc[0:HEAD_DIM] * (1.0 / jnp.maximum(acc[HEAD_DIM:HEAD_DIM + 1], 1e-30))
                for r in range(REP):
                    o_ref[0, r * HEAD_DIM:(r + 1) * HEAD_DIM, c * QBLOCK:(c + 1) * QBLOCK] = (
                        out[:, r * QBLOCK:(r + 1) * QBLOCK].astype(o_ref.dtype))

        qk(*tiles[0], 0)
        for n, (c, j, delta, rows, diagonal) in enumerate(tiles):
            if n + 1 < len(tiles):
                qk(*tiles[n + 1], (n + 1) % 2)
            softmax_pv(c, j, rows, diagonal, n % 2)

    @pl.when(st > 1)
    def _():
        far_tiles(st - 1)

    @pl.when(st == 0)
    def _():
        near_tiles(False)

    @pl.when(st > 0)
    def _():
        near_tiles(True)


def _selected(thr, tab, q, sel, kaug, vt, cst, s, tk):
    nsel = s // SEL_BLOCK
    sl = 3 * tk - QBLOCK
    cpt = tk // QBLOCK
    smem = pl.BlockSpec(memory_space=pltpu.SMEM)
    return pl.pallas_call(
        functools.partial(_sel_kernel, tk=tk),
        grid=(GROUPS, s // tk),
        in_specs=[smem, smem,
                  pl.BlockSpec((1, REP * HEAD_DIM, tk), lambda g, c: (g, 0, c)),
                  pl.BlockSpec((1, nsel, tk), lambda g, c: (g, 0, c)),
                  pl.BlockSpec(kaug.shape, lambda g, c: (0, 0)),
                  pl.BlockSpec((1, vt.shape[1], s), lambda g, c: (g, 0, 0)),
                  pl.BlockSpec((1, 16, HQ), lambda g, c: (g, 0, 0))],
        out_specs=pl.BlockSpec((1, REP * HEAD_DIM, tk), lambda g, c: (g, 0, c)),
        out_shape=jax.ShapeDtypeStruct((GROUPS, REP * HEAD_DIM, s), BF16),
        scratch_shapes=[pltpu.VMEM((2, kaug.shape[1], HQ), BF16), pltpu.VMEM((2, tk, HQ), F32),
                        pltpu.VMEM((2, 1, HQ), F32),
                        pltpu.VMEM((cpt, 1, HQ), F32), pltpu.VMEM((cpt, vt.shape[1], HQ), F32),
                        pltpu.VMEM((sl, HQ), F32)],
        compiler_params=_params("arbitrary", "arbitrary"),
        name="selected_attn",
    )(thr, tab, q, sel, kaug, vt, cst)


def _band_kernel(thr_ref, tab_ref, q_ref, k_ref, vt_ref, sink_ref, o_ref, qa_ref, strip_ref, s_ref, *,
                 window, use_sink, cb):
    g = pl.program_id(0)

    @pl.when(pl.program_id(1) == 0)
    def _():
        _fill_bias_strip(strip_ref, thr_ref, tab_ref, g, dist0=window, lo=0, hi=window, minus_far=False)

    base = pl.multiple_of(pl.program_id(1) * (QBLOCK * cb), QBLOCK * cb)
    nk = window + QBLOCK
    qa_ref[...] = jnp.zeros(qa_ref.shape, BF16)
    row0 = pl.multiple_of(g * HEAD_DIM, HEAD_DIM)
    for c in range(cb):
        for r in range(REP):
            qa_ref[c, pl.ds(row0, HEAD_DIM), r * QBLOCK:(r + 1) * QBLOCK] = (
                q_ref[0, r * HEAD_DIM:(r + 1) * HEAD_DIM, c * QBLOCK:(c + 1) * QBLOCK])

    def scores(c, first_step):
        t0 = base + c * QBLOCK
        s = jnp.dot(k_ref[pl.ds(t0, nk), :], qa_ref[c], preferred_element_type=F32) + strip_ref[...]
        if first_step and c * QBLOCK < window:
            row = lax.broadcasted_iota(jnp.int32, (nk, HQ), 0)
            s = jnp.where(row >= window - c * QBLOCK, s, -MASK_BIG)
        s_ref[c] = s
        m = jnp.max(s, axis=0, keepdims=True)
        return jnp.maximum(m, sink_ref[0]) if use_sink else m

    def attend(c, m):
        t0 = base + c * QBLOCK
        e = jnp.exp2(s_ref[c] - m)
        o_t = jnp.dot(vt_ref[0, :, pl.ds(t0, nk)], e.astype(BF16), preferred_element_type=F32)
        den = o_t[HEAD_DIM:HEAD_DIM + 1]
        den = den + jnp.exp2(sink_ref[0] - m) if use_sink else jnp.maximum(den, 1e-30)
        o_t = o_t[0:HEAD_DIM] * (1.0 / den)
        for r in range(REP):
            o_ref[0, r * HEAD_DIM:(r + 1) * HEAD_DIM, c * QBLOCK:(c + 1) * QBLOCK] = (
                o_t[:, r * QBLOCK:(r + 1) * QBLOCK].astype(o_ref.dtype))

    def chunks(first_step):
        m = scores(0, first_step)
        for c in range(cb):
            m_next = scores(c + 1, first_step) if c + 1 < cb else None
            attend(c, m)
            m = m_next

    @pl.when(pl.program_id(1) == 0)
    def _():
        chunks(True)

    @pl.when(pl.program_id(1) > 0)
    def _():
        chunks(False)


def _banded(thr, tab, q, kpad, vtpad, sink, s, window, use_sink, name, cb=BAND_CHUNKS):
    assert QBLOCK * cb >= window
    nk = window + QBLOCK
    smem = pl.BlockSpec(memory_space=pltpu.SMEM)
    return pl.pallas_call(
        functools.partial(_band_kernel, window=window, use_sink=use_sink, cb=cb),
        grid=(GROUPS, s // (QBLOCK * cb)),
        in_specs=[smem, smem,
                  pl.BlockSpec((1, REP * HEAD_DIM, QBLOCK * cb), lambda g, c: (g, 0, c)),
                  pl.BlockSpec(kpad.shape, lambda g, c: (0, 0)),
                  pl.BlockSpec((1, vtpad.shape[1], s + window), lambda g, c: (g, 0, 0)),
                  pl.BlockSpec((1, 1, HQ), lambda g, c: (g, 0, 0))],
        out_specs=pl.BlockSpec((1, REP * HEAD_DIM, QBLOCK * cb), lambda g, c: (g, 0, c)),
        out_shape=jax.ShapeDtypeStruct((GROUPS, REP * HEAD_DIM, s), BF16),
        scratch_shapes=[pltpu.VMEM((cb, GROUPS * HEAD_DIM, HQ), BF16), pltpu.VMEM((nk, HQ), F32),
                        pltpu.VMEM((cb, nk, HQ), F32)],
        compiler_params=_params("arbitrary", "arbitrary"),
        name=name,
    )(thr, tab, q, kpad, vtpad, sink)


def _merge_kernel(oc_ref, os_ref, ow_ref, ob_ref, gn_ref, gab_ref, x_ref, wa_ref, wb_ref, wo_ref, gp_ref, o_ref):
    d = x_ref.shape[1]
    gn = jax.nn.sigmoid(gn_ref[...])
    parts = []
    for h in range(HEADS):
        rows = slice(h * HEAD_DIM, (h + 1) * HEAD_DIM)
        parts.append(gn[h:h + 1] * oc_ref[rows, :].astype(F32)
                     + gn[HEADS + h:HEADS + h + 1] * os_ref[rows, :].astype(F32)
                     + gn[2 * HEADS + h:2 * HEADS + h + 1] * ow_ref[rows, :].astype(F32))
    oa_t = jnp.concatenate(parts, axis=0).astype(BF16)
    ya = lax.dot_general(oa_t, wa_ref[...], _TN, preferred_element_type=F32)
    yb = lax.dot_general(ob_ref[...], wb_ref[...], _TN, preferred_element_type=F32)
    y = (jax.nn.sigmoid(gab_ref[:, :d].astype(F32)) * ya
         + jax.nn.sigmoid(gab_ref[:, d:].astype(F32)) * yb)
    z = jnp.dot(y.astype(BF16), wo_ref[...], preferred_element_type=F32)
    o_ref[...] = x_ref[...] + _rms(z, gp_ref[...])


def _merge(oc, os_, ow, ob, gn, gab, x, wa, wb, wo, gpost, ts):
    s, d = x.shape
    hd = oc.shape[0]
    tok = lambda rows: pl.BlockSpec((rows, ts), lambda i: (0, i))
    full = lambda shp: pl.BlockSpec(shp, lambda i: (0, 0))
    return pl.pallas_call(
        _merge_kernel,
        grid=(s // ts,),
        in_specs=[tok(hd), tok(hd), tok(hd), tok(hd), tok(gn.shape[0]),
                  pl.BlockSpec((ts, 2 * d), lambda i: (i, 0)), pl.BlockSpec((ts, d), lambda i: (i, 0)),
                  full((hd, d)), full((hd, d)), full((d, d)), full((1, d))],
        out_specs=pl.BlockSpec((ts, d), lambda i: (i, 0)),
        out_shape=jax.ShapeDtypeStruct((s, d), F32),
        compiler_params=_params("arbitrary"),
        name="merge_out_proj",
    )(oc, os_, ow, ob, gn, gab, x, wa, wb, wo, gpost)


def _mlp_kernel(h_ref, gpre_ref, w1_ref, w2_ref, gpost_ref, o_ref, u_ref, acc_ref):
    f = pl.program_id(1)

    @pl.when(f == 0)
    def _():
        u_ref[...] = _rms(h_ref[...], gpre_ref[...]).astype(BF16)
        acc_ref[...] = jnp.zeros(acc_ref.shape, F32)

    a = jnp.maximum(jnp.dot(u_ref[...], w1_ref[...], preferred_element_type=F32), 0.0)
    acc_ref[...] += jnp.dot((a * a).astype(BF16), w2_ref[...], preferred_element_type=F32)

    @pl.when(f == pl.num_programs(1) - 1)
    def _():
        o_ref[...] = h_ref[...] + _rms(acc_ref[...], gpost_ref[...])


def _mlp(h, gpre, w1, w2, gpost, ts, tf):
    s, d = h.shape
    dff = w1.shape[1]
    return pl.pallas_call(
        _mlp_kernel,
        grid=(s // ts, dff // tf),
        in_specs=[pl.BlockSpec((ts, d), lambda i, f: (i, 0)), pl.BlockSpec((1, d), lambda i, f: (0, 0)),
                  pl.BlockSpec((d, tf), lambda i, f: (0, f)), pl.BlockSpec((tf, d), lambda i, f: (f, 0)),
                  pl.BlockSpec((1, d), lambda i, f: (0, 0))],
        out_specs=pl.BlockSpec((ts, d), lambda i, f: (i, 0)),
        out_shape=jax.ShapeDtypeStruct((s, d), F32),
        scratch_shapes=[pltpu.VMEM((ts, d), BF16), pltpu.VMEM((ts, d), F32)],
        compiler_params=_params("arbitrary", "arbitrary"),
        name="mlp_relu2",
    )(h, gpre, w1, w2, gpost)


def kernel(x, norm_mix_pre, norm_mix_post, norm_mlp_pre, norm_mlp_post, w_in,
           cmp_pos_k, cmp_w1_k, cmp_w2_k, cmp_pos_v, cmp_w1_v, cmp_w2_v,
           attn_sinks, rel_bias, w_up_nsa, w_up_swa, w_out, w_mlp_in, w_mlp_out):
    b, s, d = x.shape
    assert b == 1 and s % SEL_TILE == 0 and w_in.shape[0] == 1
    assert SEL_TILE >= MAX_DISTANCE
    qd = HEADS * HEAD_DIM
    kvd = GROUPS * HEAD_DIM
    sizes = (qd,) + (kvd,) * 6 + (3 * HEADS, qd, kvd, kvd, d, d)
    offs = [0]
    for z in sizes:
        offs.append(offs[-1] + z)
    w = w_in[0]
    col = lambda i: w[:, offs[i]:offs[i + 1]]
    (w_qn, w_kc, w_vc, w_ks, w_vs, w_kw, w_vw, w_gn, w_qs, w_k_s, w_v_s, w_ga, w_gb) = [col(i) for i in range(13)]
    scale = HEAD_DIM ** -0.5 * LOG2E
    w_gn = w_gn.reshape(d, HEADS, 3).transpose(0, 2, 1).reshape(d, 3 * HEADS)
    w_gn = jnp.pad(w_gn, ((0, 0), (0, 8)))
    wn = jnp.concatenate([w_kc, w_vc, w_ks, w_kw, w_k_s], axis=1).astype(BF16)
    wg = jnp.concatenate([w_ga, w_gb], axis=1).astype(BF16)
    wt = jnp.concatenate([w_qn * scale, w_qs * scale, w_vs, w_vw, w_v_s, w_gn], axis=1).T.astype(BF16)
    nqv = 2 * qd + 3 * kvd

    x2 = x[0]
    kv, gab, qv, gn, kb = _project(x2, norm_mix_pre, wn, wg, wt, nqv, ts=PROJ_TOKENS)
    q_nsa = qv[0:qd].reshape(GROUPS, REP * HEAD_DIM, s)
    q_swa = qv[qd:2 * qd].reshape(GROUPS, REP * HEAD_DIM, s)
    vs_t = qv[2 * qd:2 * qd + kvd].reshape(GROUPS, HEAD_DIM, s)
    vw_t = qv[2 * qd + kvd:2 * qd + 2 * kvd].reshape(GROUPS, HEAD_DIM, s)
    vswa_t = qv[2 * qd + 2 * kvd:].reshape(GROUPS, HEAD_DIM, s)

    n16 = s // CMP_STRIDE
    pos8 = jnp.stack([cmp_pos_k[0], cmp_pos_v[0]]).reshape(2, 1, CMP_BLOCK * HEAD_DIM)
    pos8 = jnp.broadcast_to(pos8, (2, 8, CMP_BLOCK * HEAD_DIM)).astype(BF16)
    w1 = jnp.stack([cmp_w1_k[0], cmp_w1_v[0]]).astype(BF16)
    w2 = jnp.stack([cmp_w2_k[0], cmp_w2_v[0]]).astype(BF16)
    cmp_n, cmp_t = _compress(kb, pos8, w1, w2, w2.transpose(0, 2, 1))

    oc, sel = _cmp_select(q_nsa, cmp_n, cmp_t, s)

    bias_nsa = rel_bias[:, :HEADS].reshape(NUM_BUCKETS, GROUPS, REP) * LOG2E
    bias_swa = rel_bias[:, HEADS:].reshape(NUM_BUCKETS, GROUPS, REP) * LOG2E
    thr = jnp.asarray(_bucket_starts())
    tab_nsa = bias_nsa.transpose(1, 2, 0).reshape(-1)
    tab_swa = bias_swa.transpose(1, 2, 0).reshape(-1)

    tk = SEL_TILE
    nb = tk // SEL_BLOCK
    posn = jnp.arange(s)
    onehot = ((posn // SEL_BLOCK) % nb)[:, None] == jnp.arange(nb)[None, :]
    aug = jnp.concatenate([onehot.astype(BF16), jnp.ones((s, 2), BF16),
                           jnp.zeros((s, kvd - nb - 2), BF16)], axis=1)
    kaug = jnp.concatenate([kv[:, 0:kvd], aug], axis=1)
    far = bias_nsa[NUM_BUCKETS - 1]
    far_hi = far.astype(BF16)
    far_lo = (far - far_hi.astype(F32)).astype(BF16)
    cst = jnp.stack([far_hi, far_lo], axis=1)
    cst = jnp.broadcast_to(cst[:, :, :, None], (GROUPS, 2, REP, QBLOCK)).reshape(GROUPS, 2, HQ)
    cst = jnp.pad(cst, ((0, 0), (0, 14), (0, 0)))
    ones_row = jnp.concatenate([jnp.ones((GROUPS, 1, s), BF16), jnp.zeros((GROUPS, 15, s), BF16)], axis=1)
    vs_aug = jnp.concatenate([vs_t, ones_row], axis=1)
    o_sel = _selected(thr, tab_nsa, q_nsa, sel, kaug, vs_aug, cst, s, tk)

    pad_rows = lambda a, wdw: jnp.pad(a, ((wdw, 0), (0, 0)))
    pad_lanes = lambda a, wdw: jnp.pad(jnp.concatenate([a, ones_row], axis=1), ((0, 0), (0, 0), (wdw, 0)))
    no_sink = jnp.zeros((GROUPS, 1, HQ), F32)
    o_win = _banded(thr, tab_nsa, q_nsa, pad_rows(kv[:, kvd:2 * kvd], NSA_WINDOW), pad_lanes(vw_t, NSA_WINDOW),
                    no_sink, s, NSA_WINDOW, False, "nsa_window")
    sink = attn_sinks[0].reshape(GROUPS, 1, REP, 1).astype(F32) * LOG2E
    sink = jnp.broadcast_to(sink, (GROUPS, 1, REP, QBLOCK)).reshape(GROUPS, 1, HQ)
    o_swa = _banded(thr, tab_swa, q_swa, pad_rows(kv[:, 2 * kvd:3 * kvd], SWA_WINDOW), pad_lanes(vswa_t, SWA_WINDOW),
                    sink, s, SWA_WINDOW, True, "swa_sink")

    flat = lambda a: a.reshape(qd, s)
    h1 = _merge(flat(oc), flat(o_sel), flat(o_win), flat(o_swa), gn, gab, x2,
                w_up_nsa[0].astype(BF16), w_up_swa[0].astype(BF16), w_out[0].astype(BF16),
                norm_mix_post, ts=PROJ_TOKENS)
    out = _mlp(h1, norm_mlp_pre, w_mlp_in[0].astype(BF16), w_mlp_out[0].astype(BF16), norm_mlp_post,
               ts=MLP_TOKENS, tf=MLP_HIDDEN)
    return out[None]
```

```python
import functools
import math

import jax
import jax.numpy as jnp
import numpy as np
from jax import lax
from jax.experimental import pallas as pl
from jax.experimental.pallas import tpu as pltpu

F32 = jnp.float32
BF16 = jnp.bfloat16

HEAD_DIM = 64
GROUPS = 2
REP = 4
HEADS = GROUPS * REP
CMP_BLOCK = 32
CMP_STRIDE = 16
SEL_BLOCK = 64
SEL_TOPK = 16
NSA_WINDOW = 512
SWA_WINDOW = 128
QBLOCK = 128
NUM_BUCKETS = 32
MAX_DISTANCE = 1024
RMS_EPS = 1e-6
LOG2E = math.log2(math.e)
MASK_BIG = 2.0 ** 99
HQ = REP * QBLOCK

SEL_TILE = 1024
TILES_PER_TRIP = 8
PROJ_TOKENS = 512
MLP_TOKENS = 1024
MLP_HIDDEN = 2048
BAND_CHUNKS = 16
CMP_CHUNKS = 4
CMP_VARIANT_KEYS = 128
V7X_VMEM_BYTES = 64 * 1024 * 1024
VMEM_LIMIT = V7X_VMEM_BYTES * 7 // 8

_NT = (((1,), (1,)), ((), ()))
_TN = (((0,), (0,)), ((), ()))


def _params(*sem):
    return pltpu.CompilerParams(dimension_semantics=sem, vmem_limit_bytes=VMEM_LIMIT)


def _bucket_starts():
    max_exact = NUM_BUCKETS // 2
    d = np.arange(2 * MAX_DISTANCE)
    large = max_exact + (np.log(np.maximum(d, 1).astype(np.float64) / max_exact)
                         / math.log(MAX_DISTANCE / max_exact) * (NUM_BUCKETS - max_exact)).astype(np.int64)
    bucket = np.where(d < max_exact, d, np.minimum(large, NUM_BUCKETS - 1))
    return (bucket[None, :] < np.arange(NUM_BUCKETS)[:, None]).sum(axis=1).astype(np.int32)


def _rms(x, gain):
    return x * lax.rsqrt(jnp.mean(x * x, axis=-1, keepdims=True) + RMS_EPS) * gain


def _proj_kernel(x_ref, g_ref, wn_ref, wg_ref, wt_ref, kv_ref, gab_ref, qv_ref, gn_ref, kb_ref, y_ref, *, nqv):
    u = _rms(x_ref[...], g_ref[...]).astype(BF16)
    kvf = jnp.dot(u, wn_ref[...], preferred_element_type=F32)
    gw = GROUPS * HEAD_DIM
    kv_ref[...] = kvf[:, 2 * gw:].astype(BF16)
    nrow = y_ref.shape[1] // CMP_STRIDE
    for a in range(2):
        y_ref[a] = kvf[:, a * gw:(a + 1) * gw]
        both = [y_ref[a, pl.ds(p, nrow, stride=CMP_STRIDE), :] for p in range(CMP_STRIDE)]
        for gi in range(GROUPS):
            kb_ref[a, gi] = jnp.concatenate([t[:, gi * HEAD_DIM:(gi + 1) * HEAD_DIM] for t in both],
                                            axis=1).astype(BF16)
    gab_ref[...] = jnp.dot(u, wg_ref[...], preferred_element_type=F32).astype(BF16)
    t = lax.dot_general(wt_ref[...], u, _NT, preferred_element_type=F32)
    qv_ref[...] = t[:nqv].astype(BF16)
    gn_ref[...] = t[nqv:]


def _project(x, gain, wn, wg, wt, nqv, ts):
    s, d = x.shape
    nn, ng, ntr = wn.shape[1] - 2 * GROUPS * HEAD_DIM, wg.shape[1], wt.shape[0]
    full = lambda shp: pl.BlockSpec(shp, lambda i: (0, 0))
    return pl.pallas_call(
        functools.partial(_proj_kernel, nqv=nqv),
        grid=(s // ts,),
        in_specs=[pl.BlockSpec((ts, d), lambda i: (i, 0)), full((1, d)),
                  full(wn.shape), full((d, ng)), full((ntr, d))],
        out_specs=[pl.BlockSpec((ts, nn), lambda i: (i, 0)),
                   pl.BlockSpec((ts, ng), lambda i: (i, 0)),
                   pl.BlockSpec((nqv, ts), lambda i: (0, i)),
                   pl.BlockSpec((ntr - nqv, ts), lambda i: (0, i)),
                   pl.BlockSpec((2, GROUPS, ts // CMP_STRIDE, CMP_STRIDE * HEAD_DIM), lambda i: (0, 0, i, 0))],
        out_shape=[jax.ShapeDtypeStruct((s, nn), BF16), jax.ShapeDtypeStruct((s, ng), BF16),
                   jax.ShapeDtypeStruct((nqv, s), BF16), jax.ShapeDtypeStruct((ntr - nqv, s), F32),
                   jax.ShapeDtypeStruct((2, GROUPS, s // CMP_STRIDE, CMP_STRIDE * HEAD_DIM), BF16)],
        scratch_shapes=[pltpu.VMEM((2, ts, GROUPS * HEAD_DIM), F32)],
        compiler_params=_params("arbitrary"),
        name="in_proj",
    )(x, gain, wn, wg, wt)


def _gelu_tanh(x):
    return x * (0.5 * (1.0 + jnp.tanh(math.sqrt(2.0 / math.pi) * (x + 0.044715 * (x * x * x)))))


def _compress_kernel(kb_ref, pos_ref, w1_ref, w2_ref, w2t_ref, o_ref, ot_ref):
    kb = kb_ref[0, 0]
    half = kb.shape[1]
    n16 = kb.shape[0]
    first = jnp.dot(kb, w1_ref[0, :half, :], preferred_element_type=F32)
    second = jnp.dot(kb, w1_ref[0, half:, :], preferred_element_type=F32)
    posb = jnp.dot(pos_ref[0], w1_ref[0], preferred_element_type=F32)[0:1]
    pre = first + pltpu.roll(second, n16 - 1, 0) + posb
    h = _gelu_tanh(pre).astype(BF16)
    o_ref[0, 0] = jnp.dot(h, w2_ref[0], preferred_element_type=F32).astype(BF16)
    ot_ref[0, 0] = lax.dot_general(w2t_ref[0], h, _NT, preferred_element_type=F32).astype(BF16)


def _compress(kb, pos8, w1, w2, w2t):
    _, g, n16, cin2 = kb.shape
    hid = w1.shape[2]
    return pl.pallas_call(
        _compress_kernel,
        grid=(2, g),
        in_specs=[pl.BlockSpec((1, 1, n16, cin2), lambda a, b: (a, b, 0, 0)),
                  pl.BlockSpec((1, 8, 2 * cin2), lambda a, b: (a, 0, 0)),
                  pl.BlockSpec((1, 2 * cin2, hid), lambda a, b: (a, 0, 0)),
                  pl.BlockSpec((1, hid, HEAD_DIM), lambda a, b: (a, 0, 0)),
                  pl.BlockSpec((1, HEAD_DIM, hid), lambda a, b: (a, 0, 0))],
        out_specs=[pl.BlockSpec((1, 1, n16, HEAD_DIM), lambda a, b: (a, b, 0, 0)),
                   pl.BlockSpec((1, 1, HEAD_DIM, n16), lambda a, b: (a, b, 0, 0))],
        out_shape=[jax.ShapeDtypeStruct((2, g, n16, HEAD_DIM), BF16),
                   jax.ShapeDtypeStruct((2, g, HEAD_DIM, n16), BF16)],
        compiler_params=_params("arbitrary", "arbitrary"),
        name="kv_compress",
    )(kb, pos8, w1, w2, w2t)


def _cmpsel_body(q_ref, kc_ref, vct_ref, oc_ref, sel_ref, pg_ref, s_ref, t_base, *, cb, rows, nomask, nsel, topk):
    mrows = rows - nomask
    ratio = SEL_BLOCK // CMP_STRIDE
    nblk = rows // ratio
    n_io = nomask + lax.broadcasted_iota(jnp.int32, (mrows, HQ), 0)
    lane_t = lax.broadcasted_iota(jnp.int32, (mrows, HQ), 1) & (QBLOCK - 1)
    blk = lax.broadcasted_iota(jnp.int32, (nblk, QBLOCK), 0)

    def raw_scores(c):
        t0 = t_base + c * QBLOCK
        lanes = slice(c * QBLOCK, (c + 1) * QBLOCK)
        q4 = jnp.concatenate([q_ref[0, r * HEAD_DIM:(r + 1) * HEAD_DIM, lanes] for r in range(REP)], axis=1)
        s = jnp.dot(kc_ref[0, 0, 0:rows, :], q4, preferred_element_type=F32)
        edge = jnp.where(n_io * CMP_STRIDE + (CMP_BLOCK - 1) <= t0 + lane_t, s[nomask:], -jnp.inf)
        s = jnp.concatenate([s[:nomask], edge], axis=0) if nomask else edge
        s_ref[c, 0:rows, :] = s
        m = jnp.max(s, axis=0, keepdims=True)
        return jnp.where(m == -jnp.inf, 0.0, m)

    def attend(c, m):
        t0 = t_base + c * QBLOCK
        lanes = slice(c * QBLOCK, (c + 1) * QBLOCK)
        e = jnp.exp2(s_ref[c, 0:rows, :] - m)
        den = jnp.maximum(jnp.sum(e, axis=0, keepdims=True), 1e-30)
        p = e * (1.0 / den)
        o_t = jnp.dot(vct_ref[0, 0, :, 0:rows], p.astype(BF16), preferred_element_type=F32)
        for r in range(REP):
            oc_ref[0, r * HEAD_DIM:(r + 1) * HEAD_DIM, lanes] = (
                o_t[:, r * QBLOCK:(r + 1) * QBLOCK].astype(oc_ref.dtype))

        pg = ((p[:, 0:QBLOCK] + p[:, QBLOCK:2 * QBLOCK]) + p[:, 2 * QBLOCK:3 * QBLOCK]) + p[:, 3 * QBLOCK:]
        pg_ref[c, 0:8, :] = jnp.zeros((8, QBLOCK), F32)
        pg_ref[c, 8:8 + rows, :] = pg
        ps = pg_ref[c, pl.ds(7, nblk, stride=ratio), :]
        for o in range(1, ratio + 1):
            ps = ps + pg_ref[c, pl.ds(7 + o, nblk, stride=ratio), :]
        cur = (t0 + lax.broadcasted_iota(jnp.int32, (nblk, QBLOCK), 1)) // SEL_BLOCK
        valid = blk <= cur
        forced = (blk == 0) | (blk == cur) | (blk == cur - 1)
        return jnp.where(valid & ~forced, ps, -jnp.inf), valid

    scores, valids = [], []
    m = raw_scores(0)
    for c in range(cb):
        m_next = raw_scores(c + 1) if c + 1 < cb else None
        score, valid = attend(c, m)
        scores.append(score)
        valids.append(valid)
        m = m_next

    def pick_one(_, rems):
        out = []
        for rem in rems:
            top = jnp.max(rem, axis=0, keepdims=True)
            first = jnp.min(jnp.where(rem == top, blk, nblk), axis=0, keepdims=True)
            out.append(jnp.where(blk == first, -jnp.inf, rem))
        return tuple(out)

    rems = lax.fori_loop(0, max(topk - 3, 0), pick_one, tuple(scores))
    for c in range(cb):
        lanes = slice(c * QBLOCK, (c + 1) * QBLOCK)
        sel_ref[0, 0:nblk, lanes] = jnp.where((rems[c] == -jnp.inf) & valids[c], 0.0, -MASK_BIG).astype(BF16)
        if nblk < nsel:
            sel_ref[0, nblk:, lanes] = jnp.full((nsel - nblk, QBLOCK), -MASK_BIG, BF16)


def _cmpsel_kernel(q_ref, kc_ref, vct_ref, oc_ref, sel_ref, pg_ref, s_ref, *, n16, nsel, topk, nvar, cb):
    step = pl.program_id(1)
    per = n16 // nvar
    spv = per // (QBLOCK // CMP_STRIDE) // cb
    for k in range(1, nvar + 1):
        @pl.when((step >= (k - 1) * spv) & (step < k * spv))
        def _(k=k):
            _cmpsel_body(q_ref, kc_ref, vct_ref, oc_ref, sel_ref, pg_ref, s_ref, step * (cb * QBLOCK), cb=cb,
                         rows=per * k, nomask=max(per * (k - 1) - 8, 0), nsel=nsel, topk=topk)


def _cmp_select(q, kc_all, vct_all, s, cb=CMP_CHUNKS):
    n16 = s // CMP_STRIDE
    nsel = s // SEL_BLOCK
    topk = min(SEL_TOPK, nsel)
    nvar = max(n16 // CMP_VARIANT_KEYS, 1)
    return pl.pallas_call(
        functools.partial(_cmpsel_kernel, n16=n16, nsel=nsel, topk=topk, nvar=nvar, cb=cb),
        grid=(GROUPS, s // (QBLOCK * cb)),
        in_specs=[pl.BlockSpec((1, REP * HEAD_DIM, QBLOCK * cb), lambda g, c: (g, 0, c)),
                  pl.BlockSpec((1, 1, n16, HEAD_DIM), lambda g, c: (0, g, 0, 0)),
                  pl.BlockSpec((1, 1, HEAD_DIM, n16), lambda g, c: (1, g, 0, 0))],
        out_specs=[pl.BlockSpec((1, REP * HEAD_DIM, QBLOCK * cb), lambda g, c: (g, 0, c)),
                   pl.BlockSpec((1, nsel, QBLOCK * cb), lambda g, c: (g, 0, c))],
        out_shape=[jax.ShapeDtypeStruct((GROUPS, REP * HEAD_DIM, s), BF16),
                   jax.ShapeDtypeStruct((GROUPS, nsel, s), BF16)],
        scratch_shapes=[pltpu.VMEM((cb, n16 + 8, QBLOCK), F32), pltpu.VMEM((cb, n16, HQ), F32)],
        compiler_params=_params("arbitrary", "arbitrary"),
        name="cmp_select",
    )(q, kc_all, vct_all)


def _fill_bias_strip(strip_ref, thr_ref, tab_ref, g, *, dist0, lo, hi, minus_far):
    nrows = strip_ref.shape[0]

    def block(bi, carry):
        i0 = pl.multiple_of(bi * QBLOCK, QBLOCK)
        d = (lax.broadcasted_iota(jnp.int32, (QBLOCK, QBLOCK), 1)
             - lax.broadcasted_iota(jnp.int32, (QBLOCK, QBLOCK), 0) + (dist0 - i0))
        ok = d >= lo if hi is None else (d >= lo) & (d < hi)
        for r in range(REP):
            base = (g * REP + r) * NUM_BUCKETS
            out = jnp.full((QBLOCK, QBLOCK), tab_ref[base], F32)
            for b in range(1, NUM_BUCKETS):
                out = jnp.where(d >= thr_ref[b], tab_ref[base + b], out)
            if minus_far:
                out = out - tab_ref[base + NUM_BUCKETS - 1]
            strip_ref[pl.ds(i0, QBLOCK), r * QBLOCK:(r + 1) * QBLOCK] = jnp.where(ok, out, -MASK_BIG)
        return carry

    lax.fori_loop(0, nrows // QBLOCK, block, 0)


def _sel_kernel(thr_ref, tab_ref, q_ref, sel_ref, kaug_ref, vt_ref, cst_ref, o_ref,
                qa_ref, s_ref, tmax_ref, m_ref, acc_ref, strip_ref, *, tk):
    g = pl.program_id(0)
    st = pl.program_id(1)
    nb = tk // SEL_BLOCK
    cpt = tk // QBLOCK
    kd = GROUPS * HEAD_DIM
    dmax = 2 * tk - QBLOCK

    @pl.when(st == 0)
    def _():
        _fill_bias_strip(strip_ref, thr_ref, tab_ref, g, dist0=dmax, lo=0, hi=None, minus_far=True)

    qa_ref[...] = jnp.zeros(qa_ref.shape, BF16)
    for b in range(2):
        qa_ref[b, kd + nb:kd + nb + 16, :] = cst_ref[0]
    m_ref[...] = jnp.full(m_ref.shape, -jnp.inf, F32)
    acc_ref[...] = jnp.zeros(acc_ref.shape, F32)
    row0 = pl.multiple_of(g * HEAD_DIM, HEAD_DIM)

    def far_tiles(count):

        def advance(c, j):
            wrap = j + 1 == count
            return jnp.where(wrap, c + 1, c), jnp.where(wrap, 0, j + 1)

        def qk(c, j, buf):
            cc = jnp.minimum(c, cpt - 1)
            lane0 = pl.multiple_of(cc * QBLOCK, QBLOCK)
            for r in range(REP):
                qa_ref[buf, pl.ds(row0, HEAD_DIM), r * QBLOCK:(r + 1) * QBLOCK] = (
                    q_ref[0, r * HEAD_DIM:(r + 1) * HEAD_DIM, pl.ds(lane0, QBLOCK)])
            sb = sel_ref[0, pl.ds(pl.multiple_of(j * nb, nb), nb), pl.ds(lane0, QBLOCK)]
            for r in range(REP):
                qa_ref[buf, kd:kd + nb, r * QBLOCK:(r + 1) * QBLOCK] = sb
            k0 = pl.multiple_of(j * tk, tk)
            s = jnp.dot(kaug_ref[pl.ds(k0, tk), :], qa_ref[buf], preferred_element_type=F32)
            s_ref[buf] = s
            tmax_ref[buf] = jnp.max(s, axis=0, keepdims=True)

        def softmax_pv(c, j, buf):
            k0 = pl.multiple_of(j * tk, tk)
            m_prev = m_ref[c]
            m_new = jnp.maximum(m_prev, tmax_ref[buf])
            alpha = jnp.exp2(m_prev - m_new)
            p = jnp.exp2(s_ref[buf] - m_new)
            acc_ref[c] = alpha * acc_ref[c] + jnp.dot(vt_ref[0, :, pl.ds(k0, tk)], p.astype(BF16),
                                                      preferred_element_type=F32)
            m_ref[c] = m_new

        zero = jnp.int32(0)
        qk(zero, zero, 0)

        def trip(_, carry):
            cur = carry
            for k in range(TILES_PER_TRIP):
                nxt = advance(*cur)
                qk(*nxt, (k + 1) % 2)
                softmax_pv(*cur, k % 2)
                cur = nxt
            return cur

        lax.fori_loop(0, (cpt * count) // TILES_PER_TRIP, trip, (zero, zero))

    def near_tiles(with_previous):
        tiles = []
        for c in range(cpt):
            if with_previous:
                tiles.append((c, st - 1, tk + c * QBLOCK, tk, False))
            tiles.append((c, st, c * QBLOCK, (c + 1) * QBLOCK, True))

        def qk(c, j, delta, rows, diagonal, buf):
            lanes = slice(c * QBLOCK, (c + 1) * QBLOCK)
            for r in range(REP):
                qa_ref[buf, pl.ds(row0, HEAD_DIM), r * QBLOCK:(r + 1) * QBLOCK] = (
                    q_ref[0, r * HEAD_DIM:(r + 1) * HEAD_DIM, lanes])
            sb = sel_ref[0, pl.ds(pl.multiple_of(j * nb, nb), nb), lanes]
            for r in range(REP):
                qa_ref[buf, kd:kd + nb, r * QBLOCK:(r + 1) * QBLOCK] = sb
            k0 = pl.multiple_of(j * tk, tk)
            s = (jnp.dot(kaug_ref[pl.ds(k0, rows), :], qa_ref[buf], preferred_element_type=F32)
                 + strip_ref[dmax - delta:dmax - delta + rows, :])
            s_ref[buf, 0:rows, :] = s
            tmax_ref[buf] = jnp.max(s, axis=0, keepdims=True)

        def softmax_pv(c, j, rows, diagonal, buf):
            k0 = pl.multiple_of(j * tk, tk)
            m_prev = m_ref[c]
            m_new = jnp.maximum(m_prev, tmax_ref[buf])
            alpha = jnp.exp2(m_prev - m_new)
            p = jnp.exp2(s_ref[buf, 0:rows, :] - m_new)
            acc = alpha * acc_ref[c] + jnp.dot(vt_ref[0, :, pl.ds(k0, rows)], p.astype(BF16),
                                               preferred_element_type=F32)
            acc_ref[c] = acc
            m_ref[c] = m_new
            if diagonal:
                out = acc[0:HEAD_DIM] * (1.0 / jnp.maximum(acc[HEAD_DIM:HEAD_DIM + 1], 1e-30))
                for r in range(REP):
                    o_ref[0, r * HEAD_DIM:(r + 1) * HEAD_DIM, c * QBLOCK:(c + 1) * QBLOCK] = (
                        out[:, r * QBLOCK:(r + 1) * QBLOCK].astype(o_ref.dtype))

        qk(*tiles[0], 0)
        for n, (c, j, delta, rows, diagonal) in enumerate(tiles):
            if n + 1 < len(tiles):
                qk(*tiles[n + 1], (n + 1) % 2)
            softmax_pv(c, j, rows, diagonal, n % 2)

    @pl.when(st > 1)
    def _():
        far_tiles(st - 1)

    @pl.when(st == 0)
    def _():
        near_tiles(False)

    @pl.when(st > 0)
    def _():
        near_tiles(True)


def _selected(thr, tab, q, sel, kaug, vt, cst, s, tk):
    nsel = s // SEL_BLOCK
    sl = 3 * tk - QBLOCK
    cpt = tk // QBLOCK
    smem = pl.BlockSpec(memory_space=pltpu.SMEM)
    return pl.pallas_call(
        functools.partial(_sel_kernel, tk=tk),
        grid=(GROUPS, s // tk),
        in_specs=[smem, smem,
                  pl.BlockSpec((1, REP * HEAD_DIM, tk), lambda g, c: (g, 0, c)),
                  pl.BlockSpec((1, nsel, tk), lambda g, c: (g, 0, c)),
                  pl.BlockSpec(kaug.shape, lambda g, c: (0, 0)),
                  pl.BlockSpec((1, vt.shape[1], s), lambda g, c: (g, 0, 0)),
                  pl.BlockSpec((1, 16, HQ), lambda g, c: (g, 0, 0))],
        out_specs=pl.BlockSpec((1, REP * HEAD_DIM, tk), lambda g, c: (g, 0, c)),
        out_shape=jax.ShapeDtypeStruct((GROUPS, REP * HEAD_DIM, s), BF16),
        scratch_shapes=[pltpu.VMEM((2, kaug.shape[1], HQ), BF16), pltpu.VMEM((2, tk, HQ), F32),
                        pltpu.VMEM((2, 1, HQ), F32),
                        pltpu.VMEM((cpt, 1, HQ), F32), pltpu.VMEM((cpt, vt.shape[1], HQ), F32),
                        pltpu.VMEM((sl, HQ), F32)],
        compiler_params=_params("arbitrary", "arbitrary"),
        name="selected_attn",
    )(thr, tab, q, sel, kaug, vt, cst)


def _band_kernel(thr_ref, tab_ref, q_ref, k_ref, vt_ref, sink_ref, o_ref, qa_ref, strip_ref, s_ref, *,
                 window, use_sink, cb):
    g = pl.program_id(0)

    @pl.when(pl.program_id(1) == 0)
    def _():
        _fill_bias_strip(strip_ref, thr_ref, tab_ref, g, dist0=window, lo=0, hi=window, minus_far=False)

    base = pl.multiple_of(pl.program_id(1) * (QBLOCK * cb), QBLOCK * cb)
    nk = window + QBLOCK
    qa_ref[...] = jnp.zeros(qa_ref.shape, BF16)
    row0 = pl.multiple_of(g * HEAD_DIM, HEAD_DIM)
    for c in range(cb):
        for r in range(REP):
            qa_ref[c, pl.ds(row0, HEAD_DIM), r * QBLOCK:(r + 1) * QBLOCK] = (
                q_ref[0, r * HEAD_DIM:(r + 1) * HEAD_DIM, c * QBLOCK:(c + 1) * QBLOCK])

    def scores(c, first_step):
        t0 = base + c * QBLOCK
        s = jnp.dot(k_ref[pl.ds(t0, nk), :], qa_ref[c], preferred_element_type=F32) + strip_ref[...]
        if first_step and c * QBLOCK < window:
            row = lax.broadcasted_iota(jnp.int32, (nk, HQ), 0)
            s = jnp.where(row >= window - c * QBLOCK, s, -MASK_BIG)
        s_ref[c] = s
        m = jnp.max(s, axis=0, keepdims=True)
        return jnp.maximum(m, sink_ref[0]) if use_sink else m

    def attend(c, m):
        t0 = base + c * QBLOCK
        e = jnp.exp2(s_ref[c] - m)
        o_t = jnp.dot(vt_ref[0, :, pl.ds(t0, nk)], e.astype(BF16), preferred_element_type=F32)
        den = o_t[HEAD_DIM:HEAD_DIM + 1]
        den = den + jnp.exp2(sink_ref[0] - m) if use_sink else jnp.maximum(den, 1e-30)
        o_t = o_t[0:HEAD_DIM] * (1.0 / den)
        for r in range(REP):
            o_ref[0, r * HEAD_DIM:(r + 1) * HEAD_DIM, c * QBLOCK:(c + 1) * QBLOCK] = (
                o_t[:, r * QBLOCK:(r + 1) * QBLOCK].astype(o_ref.dtype))

    def chunks(first_step):
        m = scores(0, first_step)
        for c in range(cb):
            m_next = scores(c + 1, first_step) if c + 1 < cb else None
            attend(c, m)
            m = m_next

    @pl.when(pl.program_id(1) == 0)
    def _():
        chunks(True)

    @pl.when(pl.program_id(1) > 0)
    def _():
        chunks(False)


def _banded(thr, tab, q, kpad, vtpad, sink, s, window, use_sink, name, cb=BAND_CHUNKS):
    assert QBLOCK * cb >= window
    nk = window + QBLOCK
    smem = pl.BlockSpec(memory_space=pltpu.SMEM)
    return pl.pallas_call(
        functools.partial(_band_kernel, window=window, use_sink=use_sink, cb=cb),
        grid=(GROUPS, s // (QBLOCK * cb)),
        in_specs=[smem, smem,
                  pl.BlockSpec((1, REP * HEAD_DIM, QBLOCK * cb), lambda g, c: (g, 0, c)),
                  pl.BlockSpec(kpad.shape, lambda g, c: (0, 0)),
                  pl.BlockSpec((1, vtpad.shape[1], s + window), lambda g, c: (g, 0, 0)),
                  pl.BlockSpec((1, 1, HQ), lambda g, c: (g, 0, 0))],
        out_specs=pl.BlockSpec((1, REP * HEAD_DIM, QBLOCK * cb), lambda g, c: (g, 0, c)),
        out_shape=jax.ShapeDtypeStruct((GROUPS, REP * HEAD_DIM, s), BF16),
        scratch_shapes=[pltpu.VMEM((cb, GROUPS * HEAD_DIM, HQ), BF16), pltpu.VMEM((nk, HQ), F32),
                        pltpu.VMEM((cb, nk, HQ), F32)],
        compiler_params=_params("arbitrary", "arbitrary"),
        name=name,
    )(thr, tab, q, kpad, vtpad, sink)


def _merge_kernel(oc_ref, os_ref, ow_ref, ob_ref, gn_ref, gab_ref, x_ref, wa_ref, wb_ref, wo_ref, gp_ref, o_ref):
    d = x_ref.shape[1]
    gn = jax.nn.sigmoid(gn_ref[...])
    parts = []
    for h in range(HEADS):
        rows = slice(h * HEAD_DIM, (h + 1) * HEAD_DIM)
        parts.append(gn[h:h + 1] * oc_ref[rows, :].astype(F32)
                     + gn[HEADS + h:HEADS + h + 1] * os_ref[rows, :].astype(F32)
                     + gn[2 * HEADS + h:2 * HEADS + h + 1] * ow_ref[rows, :].astype(F32))
    oa_t = jnp.concatenate(parts, axis=0).astype(BF16)
    ya = lax.dot_general(oa_t, wa_ref[...], _TN, preferred_element_type=F32)
    yb = lax.dot_general(ob_ref[...], wb_ref[...], _TN, preferred_element_type=F32)
    y = (jax.nn.sigmoid(gab_ref[:, :d].astype(F32)) * ya
         + jax.nn.sigmoid(gab_ref[:, d:].astype(F32)) * yb)
    z = jnp.dot(y.astype(BF16), wo_ref[...], preferred_element_type=F32)
    o_ref[...] = x_ref[...] + _rms(z, gp_ref[...])


def _merge(oc, os_, ow, ob, gn, gab, x, wa, wb, wo, gpost, ts):
    s, d = x.shape
    hd = oc.shape[0]
    tok = lambda rows: pl.BlockSpec((rows, ts), lambda i: (0, i))
    full = lambda shp: pl.BlockSpec(shp, lambda i: (0, 0))
    return pl.pallas_call(
        _merge_kernel,
        grid=(s // ts,),
        in_specs=[tok(hd), tok(hd), tok(hd), tok(hd), tok(gn.shape[0]),
                  pl.BlockSpec((ts, 2 * d), lambda i: (i, 0)), pl.BlockSpec((ts, d), lambda i: (i, 0)),
                  full((hd, d)), full((hd, d)), full((d, d)), full((1, d))],
        out_specs=pl.BlockSpec((ts, d), lambda i: (i, 0)),
        out_shape=jax.ShapeDtypeStruct((s, d), F32),
        compiler_params=_params("arbitrary"),
        name="merge_out_proj",
    )(oc, os_, ow, ob, gn, gab, x, wa, wb, wo, gpost)


def _mlp_kernel(h_ref, gpre_ref, w1_ref, w2_ref, gpost_ref, o_ref, u_ref, acc_ref):
    f = pl.program_id(1)

    @pl.when(f == 0)
    def _():
        u_ref[...] = _rms(h_ref[...], gpre_ref[...]).astype(BF16)
        acc_ref[...] = jnp.zeros(acc_ref.shape, F32)

    a = jnp.maximum(jnp.dot(u_ref[...], w1_ref[...], preferred_element_type=F32), 0.0)
    acc_ref[...] += jnp.dot((a * a).astype(BF16), w2_ref[...], preferred_element_type=F32)

    @pl.when(f == pl.num_programs(1) - 1)
    def _():
        o_ref[...] = h_ref[...] + _rms(acc_ref[...], gpost_ref[...])


def _mlp(h, gpre, w1, w2, gpost, ts, tf):
    s, d = h.shape
    dff = w1.shape[1]
    return pl.pallas_call(
        _mlp_kernel,
        grid=(s // ts, dff // tf),
        in_specs=[pl.BlockSpec((ts, d), lambda i, f: (i, 0)), pl.BlockSpec((1, d), lambda i, f: (0, 0)),
                  pl.BlockSpec((d, tf), lambda i, f: (0, f)), pl.BlockSpec((tf, d), lambda i, f: (f, 0)),
                  pl.BlockSpec((1, d), lambda i, f: (0, 0))],
        out_specs=pl.BlockSpec((ts, d), lambda i, f: (i, 0)),
        out_shape=jax.ShapeDtypeStruct((s, d), F32),
        scratch_shapes=[pltpu.VMEM((ts, d), BF16), pltpu.VMEM((ts, d), F32)],
        compiler_params=_params("arbitrary", "arbitrary"),
        name="mlp_relu2",
    )(h, gpre, w1, w2, gpost)


def kernel(x, norm_mix_pre, norm_mix_post, norm_mlp_pre, norm_mlp_post, w_in,
           cmp_pos_k, cmp_w1_k, cmp_w2_k, cmp_pos_v, cmp_w1_v, cmp_w2_v,
           attn_sinks, rel_bias, w_up_nsa, w_up_swa, w_out, w_mlp_in, w_mlp_out):
    b, s, d = x.shape
    assert b == 1 and s % SEL_TILE == 0 and w_in.shape[0] == 1
    assert SEL_TILE >= MAX_DISTANCE
    qd = HEADS * HEAD_DIM
    kvd = GROUPS * HEAD_DIM
    sizes = (qd,) + (kvd,) * 6 + (3 * HEADS, qd, kvd, kvd, d, d)
    offs = [0]
    for z in sizes:
        offs.append(offs[-1] + z)
    w = w_in[0]
    col = lambda i: w[:, offs[i]:offs[i + 1]]
    (w_qn, w_kc, w_vc, w_ks, w_vs, w_kw, w_vw, w_gn, w_qs, w_k_s, w_v_s, w_ga, w_gb) = [col(i) for i in range(13)]
    scale = HEAD_DIM ** -0.5 * LOG2E
    w_gn = w_gn.reshape(d, HEADS, 3).transpose(0, 2, 1).reshape(d, 3 * HEADS)
    w_gn = jnp.pad(w_gn, ((0, 0), (0, 8)))
    wn = jnp.concatenate([w_kc, w_vc, w_ks, w_kw, w_k_s], axis=1).astype(BF16)
    wg = jnp.concatenate([w_ga, w_gb], axis=1).astype(BF16)
    wt = jnp.concatenate([w_qn * scale, w_qs * scale, w_vs, w_vw, w_v_s, w_gn], axis=1).astype(BF16).T
    nqv = 2 * qd + 3 * kvd

    x2 = x[0]
    kv, gab, qv, gn, kb = _project(x2, norm_mix_pre, wn, wg, wt, nqv, ts=PROJ_TOKENS)
    q_nsa = qv[0:qd].reshape(GROUPS, REP * HEAD_DIM, s)
    q_swa = qv[qd:2 * qd].reshape(GROUPS, REP * HEAD_DIM, s)
    vs_t = qv[2 * qd:2 * qd + kvd].reshape(GROUPS, HEAD_DIM, s)
    vw_t = qv[2 * qd + kvd:2 * qd + 2 * kvd].reshape(GROUPS, HEAD_DIM, s)
    vswa_t = qv[2 * qd + 2 * kvd:].reshape(GROUPS, HEAD_DIM, s)

    n16 = s // CMP_STRIDE
    pos8 = jnp.stack([cmp_pos_k[0], cmp_pos_v[0]]).reshape(2, 1, CMP_BLOCK * HEAD_DIM)
    pos8 = jnp.broadcast_to(pos8, (2, 8, CMP_BLOCK * HEAD_DIM)).astype(BF16)
    w1 = jnp.stack([cmp_w1_k[0], cmp_w1_v[0]]).astype(BF16)
    w2 = jnp.stack([cmp_w2_k[0], cmp_w2_v[0]]).astype(BF16)
    cmp_n, cmp_t = _compress(kb, pos8, w1, w2, w2.transpose(0, 2, 1))

    oc, sel = _cmp_select(q_nsa, cmp_n, cmp_t, s)

    bias_nsa = rel_bias[:, :HEADS].reshape(NUM_BUCKETS, GROUPS, REP) * LOG2E
    bias_swa = rel_bias[:, HEADS:].reshape(NUM_BUCKETS, GROUPS, REP) * LOG2E
    thr = jnp.asarray(_bucket_starts())
    tab_nsa = bias_nsa.transpose(1, 2, 0).reshape(-1)
    tab_swa = bias_swa.transpose(1, 2, 0).reshape(-1)

    tk = SEL_TILE
    nb = tk // SEL_BLOCK
    posn = jnp.arange(s)
    onehot = ((posn // SEL_BLOCK) % nb)[:, None] == jnp.arange(nb)[None, :]
    aug = jnp.concatenate([onehot.astype(BF16), jnp.ones((s, 2), BF16),
                           jnp.zeros((s, kvd - nb - 2), BF16)], axis=1)
    kaug = jnp.concatenate([kv[:, 0:kvd], aug], axis=1)
    far = bias_nsa[NUM_BUCKETS - 1]
    far_hi = far.astype(BF16)
    far_lo = (far - far_hi.astype(F32)).astype(BF16)
    cst = jnp.stack([far_hi, far_lo], axis=1)
    cst = jnp.broadcast_to(cst[:, :, :, None], (GROUPS, 2, REP, QBLOCK)).reshape(GROUPS, 2, HQ)
    cst = jnp.pad(cst, ((0, 0), (0, 14), (0, 0)))
    ones_row = jnp.concatenate([jnp.ones((GROUPS, 1, s), BF16), jnp.zeros((GROUPS, 15, s), BF16)], axis=1)
    vs_aug = jnp.concatenate([vs_t, ones_row], axis=1)
    o_sel = _selected(thr, tab_nsa, q_nsa, sel, kaug, vs_aug, cst, s, tk)

    pad_rows = lambda a, wdw: jnp.pad(a, ((wdw, 0), (0, 0)))
    pad_lanes = lambda a, wdw: jnp.pad(jnp.concatenate([a, ones_row], axis=1), ((0, 0), (0, 0), (wdw, 0)))
    no_sink = jnp.zeros((GROUPS, 1, HQ), F32)
    o_win = _banded(thr, tab_nsa, q_nsa, pad_rows(kv[:, kvd:2 * kvd], NSA_WINDOW), pad_lanes(vw_t, NSA_WINDOW),
                    no_sink, s, NSA_WINDOW, False, "nsa_window")
    sink = attn_sinks[0].reshape(GROUPS, 1, REP, 1).astype(F32) * LOG2E
    sink = jnp.broadcast_to(sink, (GROUPS, 1, REP, QBLOCK)).reshape(GROUPS, 1, HQ)
    o_swa = _banded(thr, tab_swa, q_swa, pad_rows(kv[:, 2 * kvd:3 * kvd], SWA_WINDOW), pad_lanes(vswa_t, SWA_WINDOW),
                    sink, s, SWA_WINDOW, True, "swa_sink")

    flat = lambda a: a.reshape(qd, s)
    h1 = _merge(flat(oc), flat(o_sel), flat(o_win), flat(o_swa), gn, gab, x2,
                w_up_nsa[0].astype(BF16), w_up_swa[0].astype(BF16), w_out[0].astype(BF16),
                norm_mix_post, ts=PROJ_TOKENS)
    out = _mlp(h1, norm_mlp_pre, w_mlp_in[0].astype(BF16), w_mlp_out[0].astype(BF16), norm_mlp_post,
               ts=MLP_TOKENS, tf=MLP_HIDDEN)
    return out[None]
```

```python
import functools
import math

import jax
import jax.numpy as jnp
import numpy as np
from jax import lax
from jax.experimental import pallas as pl
from jax.experimental.pallas import tpu as pltpu

F32 = jnp.float32
BF16 = jnp.bfloat16

HEAD_DIM = 64
GROUPS = 2
REP = 4
HEADS = GROUPS * REP
CMP_BLOCK = 32
CMP_STRIDE = 16
SEL_BLOCK = 64
SEL_TOPK = 16
NSA_WINDOW = 512
SWA_WINDOW = 128
QBLOCK = 128
NUM_BUCKETS = 32
MAX_DISTANCE = 1024
RMS_EPS = 1e-6
LOG2E = math.log2(math.e)
MASK_BIG = 2.0 ** 99
HQ = REP * QBLOCK

SEL_TILE = 1024
TILES_PER_TRIP = 8
PROJ_TOKENS = 512
MLP_TOKENS = 1024
MLP_HIDDEN = 2048
BAND_CHUNKS = 16
CMP_CHUNKS = 4
CMP_VARIANT_KEYS = 128
V7X_VMEM_BYTES = 64 * 1024 * 1024
VMEM_LIMIT = V7X_VMEM_BYTES * 7 // 8

_NT = (((1,), (1,)), ((), ()))
_TN = (((0,), (0,)), ((), ()))


def _params(*sem):
    return pltpu.CompilerParams(dimension_semantics=sem, vmem_limit_bytes=VMEM_LIMIT)


def _bucket_starts():
    max_exact = NUM_BUCKETS // 2
    d = np.arange(2 * MAX_DISTANCE)
    large = max_exact + (np.log(np.maximum(d, 1).astype(np.float64) / max_exact)
                         / math.log(MAX_DISTANCE / max_exact) * (NUM_BUCKETS - max_exact)).astype(np.int64)
    bucket = np.where(d < max_exact, d, np.minimum(large, NUM_BUCKETS - 1))
    return (bucket[None, :] < np.arange(NUM_BUCKETS)[:, None]).sum(axis=1).astype(np.int32)


def _rms(x, gain):
    return x * lax.rsqrt(jnp.mean(x * x, axis=-1, keepdims=True) + RMS_EPS) * gain


def _proj_kernel(x_ref, g_ref, wn_ref, wg_ref, wt_ref, kv_ref, gab_ref, qv_ref, gn_ref, kb_ref, y_ref, *, nqv):
    u = _rms(x_ref[...], g_ref[...]).astype(BF16)
    kvf = jnp.dot(u, wn_ref[...], preferred_element_type=F32)
    gw = GROUPS * HEAD_DIM
    kv_ref[...] = kvf[:, 2 * gw:].astype(BF16)
    nrow = y_ref.shape[1] // CMP_STRIDE
    for a in range(2):
        y_ref[a] = kvf[:, a * gw:(a + 1) * gw]
        both = [y_ref[a, pl.ds(p, nrow, stride=CMP_STRIDE), :] for p in range(CMP_STRIDE)]
        for gi in range(GROUPS):
            kb_ref[a, gi] = jnp.concatenate([t[:, gi * HEAD_DIM:(gi + 1) * HEAD_DIM] for t in both],
                                            axis=1).astype(BF16)
    gab_ref[...] = jnp.dot(u, wg_ref[...], preferred_element_type=F32).astype(BF16)
    t = lax.dot_general(wt_ref[...], u, _NT, preferred_element_type=F32)
    qv_ref[...] = t[:nqv].astype(BF16)
    gn_ref[...] = t[nqv:]


def _project(x, gain, wn, wg, wt, nqv, ts):
    s, d = x.shape
    nn, ng, ntr = wn.shape[1] - 2 * GROUPS * HEAD_DIM, wg.shape[1], wt.shape[0]
    full = lambda shp: pl.BlockSpec(shp, lambda i: (0, 0))
    return pl.pallas_call(
        functools.partial(_proj_kernel, nqv=nqv),
        grid=(s // ts,),
        in_specs=[pl.BlockSpec((ts, d), lambda i: (i, 0)), full((1, d)),
                  full(wn.shape), full((d, ng)), full((ntr, d))],
        out_specs=[pl.BlockSpec((ts, nn), lambda i: (i, 0)),
                   pl.BlockSpec((ts, ng), lambda i: (i, 0)),
                   pl.BlockSpec((nqv, ts), lambda i: (0, i)),
                   pl.BlockSpec((ntr - nqv, ts), lambda i: (0, i)),
                   pl.BlockSpec((2, GROUPS, ts // CMP_STRIDE, CMP_STRIDE * HEAD_DIM), lambda i: (0, 0, i, 0))],
        out_shape=[jax.ShapeDtypeStruct((s, nn), BF16), jax.ShapeDtypeStruct((s, ng), BF16),
                   jax.ShapeDtypeStruct((nqv, s), BF16), jax.ShapeDtypeStruct((ntr - nqv, s), F32),
                   jax.ShapeDtypeStruct((2, GROUPS, s // CMP_STRIDE, CMP_STRIDE * HEAD_DIM), BF16)],
        scratch_shapes=[pltpu.VMEM((2, ts, GROUPS * HEAD_DIM), F32)],
        compiler_params=_params("arbitrary"),
        name="in_proj",
    )(x, gain, wn, wg, wt)


def _gelu_tanh(x):
    return x * (0.5 * (1.0 + jnp.tanh(math.sqrt(2.0 / math.pi) * (x + 0.044715 * (x * x * x)))))


def _compress_kernel(kb_ref, pos_ref, w1_ref, w2_ref, w2t_ref, o_ref, ot_ref):
    kb = kb_ref[0, 0]
    half = kb.shape[1]
    n16 = kb.shape[0]
    first = jnp.dot(kb, w1_ref[0, :half, :], preferred_element_type=F32)
    second = jnp.dot(kb, w1_ref[0, half:, :], preferred_element_type=F32)
    posb = jnp.dot(pos_ref[0], w1_ref[0], preferred_element_type=F32)[0:1]
    pre = first + pltpu.roll(second, n16 - 1, 0) + posb
    h = _gelu_tanh(pre).astype(BF16)
    o_ref[0, 0] = jnp.dot(h, w2_ref[0], preferred_element_type=F32).astype(BF16)
    ot_ref[0, 0] = lax.dot_general(w2t_ref[0], h, _NT, preferred_element_type=F32).astype(BF16)


def _compress(kb, pos8, w1, w2, w2t):
    _, g, n16, cin2 = kb.shape
    hid = w1.shape[2]
    return pl.pallas_call(
        _compress_kernel,
        grid=(2, g),
        in_specs=[pl.BlockSpec((1, 1, n16, cin2), lambda a, b: (a, b, 0, 0)),
                  pl.BlockSpec((1, 8, 2 * cin2), lambda a, b: (a, 0, 0)),
                  pl.BlockSpec((1, 2 * cin2, hid), lambda a, b: (a, 0, 0)),
                  pl.BlockSpec((1, hid, HEAD_DIM), lambda a, b: (a, 0, 0)),
                  pl.BlockSpec((1, HEAD_DIM, hid), lambda a, b: (a, 0, 0))],
        out_specs=[pl.BlockSpec((1, 1, n16, HEAD_DIM), lambda a, b: (a, b, 0, 0)),
                   pl.BlockSpec((1, 1, HEAD_DIM, n16), lambda a, b: (a, b, 0, 0))],
        out_shape=[jax.ShapeDtypeStruct((2, g, n16, HEAD_DIM), BF16),
                   jax.ShapeDtypeStruct((2, g, HEAD_DIM, n16), BF16)],
        compiler_params=_params("arbitrary", "arbitrary"),
        name="kv_compress",
    )(kb, pos8, w1, w2, w2t)


def _cmpsel_body(q_ref, kc_ref, vct_ref, oc_ref, sel_ref, pg_ref, s_ref, t_base, *, cb, rows, nomask, nsel, topk):
    mrows = rows - nomask
    ratio = SEL_BLOCK // CMP_STRIDE
    nblk = rows // ratio
    n_io = nomask + lax.broadcasted_iota(jnp.int32, (mrows, HQ), 0)
    lane_t = lax.broadcasted_iota(jnp.int32, (mrows, HQ), 1) & (QBLOCK - 1)
    blk = lax.broadcasted_iota(jnp.int32, (nblk, QBLOCK), 0)

    def raw_scores(c):
        t0 = t_base + c * QBLOCK
        lanes = slice(c * QBLOCK, (c + 1) * QBLOCK)
        q4 = jnp.concatenate([q_ref[0, r * HEAD_DIM:(r + 1) * HEAD_DIM, lanes] for r in range(REP)], axis=1)
        s = jnp.dot(kc_ref[0, 0, 0:rows, :], q4, preferred_element_type=F32)
        edge = jnp.where(n_io * CMP_STRIDE + (CMP_BLOCK - 1) <= t0 + lane_t, s[nomask:], -jnp.inf)
        s = jnp.concatenate([s[:nomask], edge], axis=0) if nomask else edge
        s_ref[c, 0:rows, :] = s
        m = jnp.max(s, axis=0, keepdims=True)
        return jnp.where(m == -jnp.inf, 0.0, m)

    def attend(c, m):
        t0 = t_base + c * QBLOCK
        lanes = slice(c * QBLOCK, (c + 1) * QBLOCK)
        e = jnp.exp2(s_ref[c, 0:rows, :] - m)
        den = jnp.maximum(jnp.sum(e, axis=0, keepdims=True), 1e-30)
        p = e * (1.0 / den)
        o_t = jnp.dot(vct_ref[0, 0, :, 0:rows], p.astype(BF16), preferred_element_type=F32)
        for r in range(REP):
            oc_ref[0, r * HEAD_DIM:(r + 1) * HEAD_DIM, lanes] = (
                o_t[:, r * QBLOCK:(r + 1) * QBLOCK].astype(oc_ref.dtype))

        pg = ((p[:, 0:QBLOCK] + p[:, QBLOCK:2 * QBLOCK]) + p[:, 2 * QBLOCK:3 * QBLOCK]) + p[:, 3 * QBLOCK:]
        pg_ref[c, 0:8, :] = jnp.zeros((8, QBLOCK), F32)
        pg_ref[c, 8:8 + rows, :] = pg
        ps = pg_ref[c, pl.ds(7, nblk, stride=ratio), :]
        for o in range(1, ratio + 1):
            ps = ps + pg_ref[c, pl.ds(7 + o, nblk, stride=ratio), :]
        cur = (t0 + lax.broadcasted_iota(jnp.int32, (nblk, QBLOCK), 1)) // SEL_BLOCK
        valid = blk <= cur
        forced = (blk == 0) | (blk == cur) | (blk == cur - 1)
        return jnp.where(valid & ~forced, ps, -jnp.inf), valid

    scores, valids = [], []
    m = raw_scores(0)
    for c in range(cb):
        m_next = raw_scores(c + 1) if c + 1 < cb else None
        score, valid = attend(c, m)
        scores.append(score)
        valids.append(valid)
        m = m_next

    def pick_one(_, rems):
        out = []
        for rem in rems:
            top = jnp.max(rem, axis=0, keepdims=True)
            first = jnp.min(jnp.where(rem == top, blk, nblk), axis=0, keepdims=True)
            out.append(jnp.where(blk == first, -jnp.inf, rem))
        return tuple(out)

    rems = lax.fori_loop(0, max(topk - 3, 0), pick_one, tuple(scores))
    for c in range(cb):
        lanes = slice(c * QBLOCK, (c + 1) * QBLOCK)
        sel_ref[0, 0:nblk, lanes] = jnp.where((rems[c] == -jnp.inf) & valids[c], 0.0, -MASK_BIG).astype(BF16)
        if nblk < nsel:
            sel_ref[0, nblk:, lanes] = jnp.full((nsel - nblk, QBLOCK), -MASK_BIG, BF16)


def _cmpsel_kernel(q_ref, kc_ref, vct_ref, oc_ref, sel_ref, pg_ref, s_ref, *, n16, nsel, topk, nvar, cb):
    step = pl.program_id(1)
    per = n16 // nvar
    spv = per // (QBLOCK // CMP_STRIDE) // cb
    for k in range(1, nvar + 1):
        @pl.when((step >= (k - 1) * spv) & (step < k * spv))
        def _(k=k):
            _cmpsel_body(q_ref, kc_ref, vct_ref, oc_ref, sel_ref, pg_ref, s_ref, step * (cb * QBLOCK), cb=cb,
                         rows=per * k, nomask=max(per * (k - 1) - 8, 0), nsel=nsel, topk=topk)


def _cmp_select(q, kc_all, vct_all, s, cb=CMP_CHUNKS):
    n16 = s // CMP_STRIDE
    nsel = s // SEL_BLOCK
    topk = min(SEL_TOPK, nsel)
    nvar = max(n16 // CMP_VARIANT_KEYS, 1)
    return pl.pallas_call(
        functools.partial(_cmpsel_kernel, n16=n16, nsel=nsel, topk=topk, nvar=nvar, cb=cb),
        grid=(GROUPS, s // (QBLOCK * cb)),
        in_specs=[pl.BlockSpec((1, REP * HEAD_DIM, QBLOCK * cb), lambda g, c: (g, 0, c)),
                  pl.BlockSpec((1, 1, n16, HEAD_DIM), lambda g, c: (0, g, 0, 0)),
                  pl.BlockSpec((1, 1, HEAD_DIM, n16), lambda g, c: (1, g, 0, 0))],
        out_specs=[pl.BlockSpec((1, REP * HEAD_DIM, QBLOCK * cb), lambda g, c: (g, 0, c)),
                   pl.BlockSpec((1, nsel, QBLOCK * cb), lambda g, c: (g, 0, c))],
        out_shape=[jax.ShapeDtypeStruct((GROUPS, REP * HEAD_DIM, s), BF16),
                   jax.ShapeDtypeStruct((GROUPS, nsel, s), BF16)],
        scratch_shapes=[pltpu.VMEM((cb, n16 + 8, QBLOCK), F32), pltpu.VMEM((cb, n16, HQ), F32)],
        compiler_params=_params("arbitrary", "arbitrary"),
        name="cmp_select",
    )(q, kc_all, vct_all)


def _fill_bias_strip(strip_ref, thr_ref, tab_ref, g, *, dist0, lo, hi, minus_far):
    nrows = strip_ref.shape[0]

    def block(bi, carry):
        i0 = pl.multiple_of(bi * QBLOCK, QBLOCK)
        d = (lax.broadcasted_iota(jnp.int32, (QBLOCK, QBLOCK), 1)
             - lax.broadcasted_iota(jnp.int32, (QBLOCK, QBLOCK), 0) + (dist0 - i0))
        ok = d >= lo if hi is None else (d >= lo) & (d < hi)
        for r in range(REP):
            base = (g * REP + r) * NUM_BUCKETS
            out = jnp.full((QBLOCK, QBLOCK), tab_ref[base], F32)
            for b in range(1, NUM_BUCKETS):
                out = jnp.where(d >= thr_ref[b], tab_ref[base + b], out)
            if minus_far:
                out = out - tab_ref[base + NUM_BUCKETS - 1]
            strip_ref[pl.ds(i0, QBLOCK), r * QBLOCK:(r + 1) * QBLOCK] = jnp.where(ok, out, -MASK_BIG)
        return carry

    lax.fori_loop(0, nrows // QBLOCK, block, 0)


def _sel_kernel(thr_ref, tab_ref, q_ref, sel_ref, kaug_ref, vt_ref, cst_ref, o_ref,
                qa_ref, s_ref, tmax_ref, m_ref, acc_ref, strip_ref, *, tk):
    g = pl.program_id(0)
    st = pl.program_id(1)
    nb = tk // SEL_BLOCK
    cpt = tk // QBLOCK
    kd = GROUPS * HEAD_DIM
    dmax = 2 * tk - QBLOCK

    @pl.when(st == 0)
    def _():
        _fill_bias_strip(strip_ref, thr_ref, tab_ref, g, dist0=dmax, lo=0, hi=None, minus_far=True)

    qa_ref[...] = jnp.zeros(qa_ref.shape, BF16)
    for b in range(2):
        qa_ref[b, kd + nb:kd + nb + 16, :] = cst_ref[0]
    m_ref[...] = jnp.full(m_ref.shape, -jnp.inf, F32)
    acc_ref[...] = jnp.zeros(acc_ref.shape, F32)
    row0 = pl.multiple_of(g * HEAD_DIM, HEAD_DIM)

    def far_tiles(count):

        def advance(c, j):
            wrap = j + 1 == count
            return jnp.where(wrap, c + 1, c), jnp.where(wrap, 0, j + 1)

        def qk(c, j, buf):
            cc = jnp.minimum(c, cpt - 1)
            lane0 = pl.multiple_of(cc * QBLOCK, QBLOCK)
            for r in range(REP):
                qa_ref[buf, pl.ds(row0, HEAD_DIM), r * QBLOCK:(r + 1) * QBLOCK] = (
                    q_ref[0, r * HEAD_DIM:(r + 1) * HEAD_DIM, pl.ds(lane0, QBLOCK)])
            sb = sel_ref[0, pl.ds(pl.multiple_of(j * nb, nb), nb), pl.ds(lane0, QBLOCK)]
            for r in range(REP):
                qa_ref[buf, kd:kd + nb, r * QBLOCK:(r + 1) * QBLOCK] = sb
            k0 = pl.multiple_of(j * tk, tk)
            s = jnp.dot(kaug_ref[pl.ds(k0, tk), :], qa_ref[buf], preferred_element_type=F32)
            s_ref[buf] = s
            tmax_ref[buf] = jnp.max(s, axis=0, keepdims=True)

        def softmax_pv(c, j, buf):
            k0 = pl.multiple_of(j * tk, tk)
            m_prev = m_ref[c]
            m_new = jnp.maximum(m_prev, tmax_ref[buf])
            alpha = jnp.exp2(m_prev - m_new)
            p = jnp.exp2(s_ref[buf] - m_new)
            acc_ref[c] = alpha * acc_ref[c] + jnp.dot(vt_ref[0, :, pl.ds(k0, tk)], p.astype(BF16),
                                                      preferred_element_type=F32)
            m_ref[c] = m_new

        zero = jnp.int32(0)
        qk(zero, zero, 0)

        def trip(ntiles):
            def body(_, carry):
                cur = carry
                for k in range(ntiles):
                    nxt = advance(*cur)
                    qk(*nxt, (k + 1) % 2)
                    softmax_pv(*cur, k % 2)
                    cur = nxt
                return cur
            return body

        halves = (cpt // TILES_PER_TRIP) * count
        cur = lax.fori_loop(0, halves // 2, trip(2 * TILES_PER_TRIP), (zero, zero))

        @pl.when(halves % 2 == 1)
        def _():
            trip(TILES_PER_TRIP)(0, cur)

    def near_tiles(with_previous):
        tiles = []
        for c in range(cpt):
            if with_previous:
                tiles.append((c, st - 1, tk + c * QBLOCK, tk, False))
            tiles.append((c, st, c * QBLOCK, (c + 1) * QBLOCK, True))

        def qk(c, j, delta, rows, diagonal, buf):
            lanes = slice(c * QBLOCK, (c + 1) * QBLOCK)
            for r in range(REP):
                qa_ref[buf, pl.ds(row0, HEAD_DIM), r * QBLOCK:(r + 1) * QBLOCK] = (
                    q_ref[0, r * HEAD_DIM:(r + 1) * HEAD_DIM, lanes])
            sb = sel_ref[0, pl.ds(pl.multiple_of(j * nb, nb), nb), lanes]
            for r in range(REP):
                qa_ref[buf, kd:kd + nb, r * QBLOCK:(r + 1) * QBLOCK] = sb
            k0 = pl.multiple_of(j * tk, tk)
            s = (jnp.dot(kaug_ref[pl.ds(k0, rows), :], qa_ref[buf], preferred_element_type=F32)
                 + strip_ref[dmax - delta:dmax - delta + rows, :])
            s_ref[buf, 0:rows, :] = s
            tmax_ref[buf] = jnp.max(s, axis=0, keepdims=True)

        def softmax_pv(c, j, rows, diagonal, buf):
            k0 = pl.multiple_of(j * tk, tk)
            m_prev = m_ref[c]
            m_new = jnp.maximum(m_prev, tmax_ref[buf])
            alpha = jnp.exp2(m_prev - m_new)
            p = jnp.exp2(s_ref[buf, 0:rows, :] - m_new)
            acc = alpha * acc_ref[c] + jnp.dot(vt_ref[0, :, pl.ds(k0, rows)], p.astype(BF16),
                                               preferred_element_type=F32)
            acc_ref[c] = acc
            m_ref[c] = m_new
            if diagonal:
                out = acc[0:HEAD_DIM] * (1.0 / jnp.maximum(acc[HEAD_DIM:HEAD_DIM + 1], 1e-30))
                for r in range(REP):
                    o_ref[0, r * HEAD_DIM:(r + 1) * HEAD_DIM, c * QBLOCK:(c + 1) * QBLOCK] = (
                        out[:, r * QBLOCK:(r + 1) * QBLOCK].astype(o_ref.dtype))

        qk(*tiles[0], 0)
        for n, (c, j, delta, rows, diagonal) in enumerate(tiles):
            if n + 1 < len(tiles):
                qk(*tiles[n + 1], (n + 1) % 2)
            softmax_pv(c, j, rows, diagonal, n % 2)

    @pl.when(st > 1)
    def _():
        far_tiles(st - 1)

    @pl.when(st == 0)
    def _():
        near_tiles(False)

    @pl.when(st > 0)
    def _():
        near_tiles(True)


def _selected(thr, tab, q, sel, kaug, vt, cst, s, tk):
    nsel = s // SEL_BLOCK
    sl = 3 * tk - QBLOCK
    cpt = tk // QBLOCK
    smem = pl.BlockSpec(memory_space=pltpu.SMEM)
    return pl.pallas_call(
        functools.partial(_sel_kernel, tk=tk),
        grid=(GROUPS, s // tk),
        in_specs=[smem, smem,
                  pl.BlockSpec((1, REP * HEAD_DIM, tk), lambda g, c: (g, 0, c)),
                  pl.BlockSpec((1, nsel, tk), lambda g, c: (g, 0, c)),
                  pl.BlockSpec(kaug.shape, lambda g, c: (0, 0)),
                  pl.BlockSpec((1, vt.shape[1], s), lambda g, c: (g, 0, 0)),
                  pl.BlockSpec((1, 16, HQ), lambda g, c: (g, 0, 0))],
        out_specs=pl.BlockSpec((1, REP * HEAD_DIM, tk), lambda g, c: (g, 0, c)),
        out_shape=jax.ShapeDtypeStruct((GROUPS, REP * HEAD_DIM, s), BF16),
        scratch_shapes=[pltpu.VMEM((2, kaug.shape[1], HQ), BF16), pltpu.VMEM((2, tk, HQ), F32),
                        pltpu.VMEM((2, 1, HQ), F32),
                        pltpu.VMEM((cpt, 1, HQ), F32), pltpu.VMEM((cpt, vt.shape[1], HQ), F32),
                        pltpu.VMEM((sl, HQ), F32)],
        compiler_params=_params("arbitrary", "arbitrary"),
        name="selected_attn",
    )(thr, tab, q, sel, kaug, vt, cst)


def _band_kernel(thr_ref, tab_ref, q_ref, k_ref, vt_ref, sink_ref, o_ref, qa_ref, strip_ref, s_ref, *,
                 window, use_sink, cb):
    g = pl.program_id(0)

    @pl.when(pl.program_id(1) == 0)
    def _():
        _fill_bias_strip(strip_ref, thr_ref, tab_ref, g, dist0=window, lo=0, hi=window, minus_far=False)

    base = pl.multiple_of(pl.program_id(1) * (QBLOCK * cb), QBLOCK * cb)
    nk = window + QBLOCK
    qa_ref[...] = jnp.zeros(qa_ref.shape, BF16)
    row0 = pl.multiple_of(g * HEAD_DIM, HEAD_DIM)
    for c in range(cb):
        for r in range(REP):
            qa_ref[c, pl.ds(row0, HEAD_DIM), r * QBLOCK:(r + 1) * QBLOCK] = (
                q_ref[0, r * HEAD_DIM:(r + 1) * HEAD_DIM, c * QBLOCK:(c + 1) * QBLOCK])

    def scores(c, first_step):
        t0 = base + c * QBLOCK
        s = jnp.dot(k_ref[pl.ds(t0, nk), :], qa_ref[c], preferred_element_type=F32) + strip_ref[...]
        if first_step and c * QBLOCK < window:
            row = lax.broadcasted_iota(jnp.int32, (nk, HQ), 0)
            s = jnp.where(row >= window - c * QBLOCK, s, -MASK_BIG)
        s_ref[c] = s
        m = jnp.max(s, axis=0, keepdims=True)
        return jnp.maximum(m, sink_ref[0]) if use_sink else m

    def attend(c, m):
        t0 = base + c * QBLOCK
        e = jnp.exp2(s_ref[c] - m)
        o_t = jnp.dot(vt_ref[0, :, pl.ds(t0, nk)], e.astype(BF16), preferred_element_type=F32)
        den = o_t[HEAD_DIM:HEAD_DIM + 1]
        den = den + jnp.exp2(sink_ref[0] - m) if use_sink else jnp.maximum(den, 1e-30)
        o_t = o_t[0:HEAD_DIM] * (1.0 / den)
        for r in range(REP):
            o_ref[0, r * HEAD_DIM:(r + 1) * HEAD_DIM, c * QBLOCK:(c + 1) * QBLOCK] = (
                o_t[:, r * QBLOCK:(r + 1) * QBLOCK].astype(o_ref.dtype))

    def chunks(first_step):
        m = scores(0, first_step)
        for c in range(cb):
            m_next = scores(c + 1, first_step) if c + 1 < cb else None
            attend(c, m)
            m = m_next

    @pl.when(pl.program_id(1) == 0)
    def _():
        chunks(True)

    @pl.when(pl.program_id(1) > 0)
    def _():
        chunks(False)


def _banded(thr, tab, q, kpad, vtpad, sink, s, window, use_sink, name, cb=BAND_CHUNKS):
    assert QBLOCK * cb >= window
    nk = window + QBLOCK
    smem = pl.BlockSpec(memory_space=pltpu.SMEM)
    return pl.pallas_call(
        functools.partial(_band_kernel, window=window, use_sink=use_sink, cb=cb),
        grid=(GROUPS, s // (QBLOCK * cb)),
        in_specs=[smem, smem,
                  pl.BlockSpec((1, REP * HEAD_DIM, QBLOCK * cb), lambda g, c: (g, 0, c)),
                  pl.BlockSpec(kpad.shape, lambda g, c: (0, 0)),
                  pl.BlockSpec((1, vtpad.shape[1], s + window), lambda g, c: (g, 0, 0)),
                  pl.BlockSpec((1, 1, HQ), lambda g, c: (g, 0, 0))],
        out_specs=pl.BlockSpec((1, REP * HEAD_DIM, QBLOCK * cb), lambda g, c: (g, 0, c)),
        out_shape=jax.ShapeDtypeStruct((GROUPS, REP * HEAD_DIM, s), BF16),
        scratch_shapes=[pltpu.VMEM((cb, GROUPS * HEAD_DIM, HQ), BF16), pltpu.VMEM((nk, HQ), F32),
                        pltpu.VMEM((cb, nk, HQ), F32)],
        compiler_params=_params("arbitrary", "arbitrary"),
        name=name,
    )(thr, tab, q, kpad, vtpad, sink)


def _merge_kernel(oc_ref, os_ref, ow_ref, ob_ref, gn_ref, gab_ref, x_ref, wa_ref, wb_ref, wo_ref, gp_ref, o_ref):
    d = x_ref.shape[1]
    gn = jax.nn.sigmoid(gn_ref[...])
    parts = []
    for h in range(HEADS):
        rows = slice(h * HEAD_DIM, (h + 1) * HEAD_DIM)
        parts.append(gn[h:h + 1] * oc_ref[rows, :].astype(F32)
                     + gn[HEADS + h:HEADS + h + 1] * os_ref[rows, :].astype(F32)
                     + gn[2 * HEADS + h:2 * HEADS + h + 1] * ow_ref[rows, :].astype(F32))
    oa_t = jnp.concatenate(parts, axis=0).astype(BF16)
    ya = lax.dot_general(oa_t, wa_ref[...], _TN, preferred_element_type=F32)
    yb = lax.dot_general(ob_ref[...], wb_ref[...], _TN, preferred_element_type=F32)
    y = (jax.nn.sigmoid(gab_ref[:, :d].astype(F32)) * ya
         + jax.nn.sigmoid(gab_ref[:, d:].astype(F32)) * yb)
    z = jnp.dot(y.astype(BF16), wo_ref[...], preferred_element_type=F32)
    o_ref[...] = x_ref[...] + _rms(z, gp_ref[...])


def _merge(oc, os_, ow, ob, gn, gab, x, wa, wb, wo, gpost, ts):
    s, d = x.shape
    hd = oc.shape[0]
    tok = lambda rows: pl.BlockSpec((rows, ts), lambda i: (0, i))
    full = lambda shp: pl.BlockSpec(shp, lambda i: (0, 0))
    return pl.pallas_call(
        _merge_kernel,
        grid=(s // ts,),
        in_specs=[tok(hd), tok(hd), tok(hd), tok(hd), tok(gn.shape[0]),
                  pl.BlockSpec((ts, 2 * d), lambda i: (i, 0)), pl.BlockSpec((ts, d), lambda i: (i, 0)),
                  full((hd, d)), full((hd, d)), full((d, d)), full((1, d))],
        out_specs=pl.BlockSpec((ts, d), lambda i: (i, 0)),
        out_shape=jax.ShapeDtypeStruct((s, d), F32),
        compiler_params=_params("arbitrary"),
        name="merge_out_proj",
    )(oc, os_, ow, ob, gn, gab, x, wa, wb, wo, gpost)


def _mlp_kernel(h_ref, gpre_ref, w1_ref, w2_ref, gpost_ref, o_ref, u_ref, acc_ref):
    f = pl.program_id(1)

    @pl.when(f == 0)
    def _():
        u_ref[...] = _rms(h_ref[...], gpre_ref[...]).astype(BF16)
        acc_ref[...] = jnp.zeros(acc_ref.shape, F32)

    a = jnp.maximum(jnp.dot(u_ref[...], w1_ref[...], preferred_element_type=F32), 0.0)
    acc_ref[...] += jnp.dot((a * a).astype(BF16), w2_ref[...], preferred_element_type=F32)

    @pl.when(f == pl.num_programs(1) - 1)
    def _():
        o_ref[...] = h_ref[...] + _rms(acc_ref[...], gpost_ref[...])


def _mlp(h, gpre, w1, w2, gpost, ts, tf):
    s, d = h.shape
    dff = w1.shape[1]
    return pl.pallas_call(
        _mlp_kernel,
        grid=(s // ts, dff // tf),
        in_specs=[pl.BlockSpec((ts, d), lambda i, f: (i, 0)), pl.BlockSpec((1, d), lambda i, f: (0, 0)),
                  pl.BlockSpec((d, tf), lambda i, f: (0, f)), pl.BlockSpec((tf, d), lambda i, f: (f, 0)),
                  pl.BlockSpec((1, d), lambda i, f: (0, 0))],
        out_specs=pl.BlockSpec((ts, d), lambda i, f: (i, 0)),
        out_shape=jax.ShapeDtypeStruct((s, d), F32),
        scratch_shapes=[pltpu.VMEM((ts, d), BF16), pltpu.VMEM((ts, d), F32)],
        compiler_params=_params("arbitrary", "arbitrary"),
        name="mlp_relu2",
    )(h, gpre, w1, w2, gpost)


def kernel(x, norm_mix_pre, norm_mix_post, norm_mlp_pre, norm_mlp_post, w_in,
           cmp_pos_k, cmp_w1_k, cmp_w2_k, cmp_pos_v, cmp_w1_v, cmp_w2_v,
           attn_sinks, rel_bias, w_up_nsa, w_up_swa, w_out, w_mlp_in, w_mlp_out):
    b, s, d = x.shape
    assert b == 1 and s % SEL_TILE == 0 and w_in.shape[0] == 1
    assert SEL_TILE >= MAX_DISTANCE
    qd = HEADS * HEAD_DIM
    kvd = GROUPS * HEAD_DIM
    sizes = (qd,) + (kvd,) * 6 + (3 * HEADS, qd, kvd, kvd, d, d)
    offs = [0]
    for z in sizes:
        offs.append(offs[-1] + z)
    w = w_in[0]
    col = lambda i: w[:, offs[i]:offs[i + 1]]
    (w_qn, w_kc, w_vc, w_ks, w_vs, w_kw, w_vw, w_gn, w_qs, w_k_s, w_v_s, w_ga, w_gb) = [col(i) for i in range(13)]
    scale = HEAD_DIM ** -0.5 * LOG2E
    w_gn = w_gn.reshape(d, HEADS, 3).transpose(0, 2, 1).reshape(d, 3 * HEADS)
    w_gn = jnp.pad(w_gn, ((0, 0), (0, 8)))
    wn = jnp.concatenate([w_kc, w_vc, w_ks, w_kw, w_k_s], axis=1).astype(BF16)
    wg = jnp.concatenate([w_ga, w_gb], axis=1).astype(BF16)
    wt = jnp.concatenate([w_qn * scale, w_qs * scale, w_vs, w_vw, w_v_s, w_gn], axis=1).T.astype(BF16)
    nqv = 2 * qd + 3 * kvd

    x2 = x[0]
    kv, gab, qv, gn, kb = _project(x2, norm_mix_pre, wn, wg, wt, nqv, ts=PROJ_TOKENS)
    q_nsa = qv[0:qd].reshape(GROUPS, REP * HEAD_DIM, s)
    q_swa = qv[qd:2 * qd].reshape(GROUPS, REP * HEAD_DIM, s)
    vs_t = qv[2 * qd:2 * qd + kvd].reshape(GROUPS, HEAD_DIM, s)
    vw_t = qv[2 * qd + kvd:2 * qd + 2 * kvd].reshape(GROUPS, HEAD_DIM, s)
    vswa_t = qv[2 * qd + 2 * kvd:].reshape(GROUPS, HEAD_DIM, s)

    n16 = s // CMP_STRIDE
    pos8 = jnp.stack([cmp_pos_k[0], cmp_pos_v[0]]).reshape(2, 1, CMP_BLOCK * HEAD_DIM)
    pos8 = jnp.broadcast_to(pos8, (2, 8, CMP_BLOCK * HEAD_DIM)).astype(BF16)
    w1 = jnp.stack([cmp_w1_k[0], cmp_w1_v[0]]).astype(BF16)
    w2 = jnp.stack([cmp_w2_k[0], cmp_w2_v[0]]).astype(BF16)
    cmp_n, cmp_t = _compress(kb, pos8, w1, w2, w2.transpose(0, 2, 1))

    oc, sel = _cmp_select(q_nsa, cmp_n, cmp_t, s)

    bias_nsa = rel_bias[:, :HEADS].reshape(NUM_BUCKETS, GROUPS, REP) * LOG2E
    bias_swa = rel_bias[:, HEADS:].reshape(NUM_BUCKETS, GROUPS, REP) * LOG2E
    thr = jnp.asarray(_bucket_starts())
    tab_nsa = bias_nsa.transpose(1, 2, 0).reshape(-1)
    tab_swa = bias_swa.transpose(1, 2, 0).reshape(-1)

    tk = SEL_TILE
    nb = tk // SEL_BLOCK
    posn = jnp.arange(s)
    onehot = ((posn // SEL_BLOCK) % nb)[:, None] == jnp.arange(nb)[None, :]
    aug = jnp.concatenate([onehot.astype(BF16), jnp.ones((s, 2), BF16),
                           jnp.zeros((s, kvd - nb - 2), BF16)], axis=1)
    kaug = jnp.concatenate([kv[:, 0:kvd], aug], axis=1)
    far = bias_nsa[NUM_BUCKETS - 1]
    far_hi = far.astype(BF16)
    far_lo = (far - far_hi.astype(F32)).astype(BF16)
    cst = jnp.stack([far_hi, far_lo], axis=1)
    cst = jnp.broadcast_to(cst[:, :, :, None], (GROUPS, 2, REP, QBLOCK)).reshape(GROUPS, 2, HQ)
    cst = jnp.pad(cst, ((0, 0), (0, 14), (0, 0)))
    ones_row = jnp.concatenate([jnp.ones((GROUPS, 1, s), BF16), jnp.zeros((GROUPS, 15, s), BF16)], axis=1)
    vs_aug = jnp.concatenate([vs_t, ones_row], axis=1)
    o_sel = _selected(thr, tab_nsa, q_nsa, sel, kaug, vs_aug, cst, s, tk)

    pad_rows = lambda a, wdw: jnp.pad(a, ((wdw, 0), (0, 0)))
    pad_lanes = lambda a, wdw: jnp.pad(jnp.concatenate([a, ones_row], axis=1), ((0, 0), (0, 0), (wdw, 0)))
    no_sink = jnp.zeros((GROUPS, 1, HQ), F32)
    o_win = _banded(thr, tab_nsa, q_nsa, pad_rows(kv[:, kvd:2 * kvd], NSA_WINDOW), pad_lanes(vw_t, NSA_WINDOW),
                    no_sink, s, NSA_WINDOW, False, "nsa_window")
    sink = attn_sinks[0].reshape(GROUPS, 1, REP, 1).astype(F32) * LOG2E
    sink = jnp.broadcast_to(sink, (GROUPS, 1, REP, QBLOCK)).reshape(GROUPS, 1, HQ)
    o_swa = _banded(thr, tab_swa, q_swa, pad_rows(kv[:, 2 * kvd:3 * kvd], SWA_WINDOW), pad_lanes(vswa_t, SWA_WINDOW),
                    sink, s, SWA_WINDOW, True, "swa_sink")

    flat = lambda a: a.reshape(qd, s)
    h1 = _merge(flat(oc), flat(o_sel), flat(o_win), flat(o_swa), gn, gab, x2,
                w_up_nsa[0].astype(BF16), w_up_swa[0].astype(BF16), w_out[0].astype(BF16),
                norm_mix_post, ts=PROJ_TOKENS)
    out = _mlp(h1, norm_mlp_pre, w_mlp_in[0].astype(BF16), w_mlp_out[0].astype(BF16), norm_mlp_post,
               ts=MLP_TOKENS, tf=MLP_HIDDEN)
    return out[None]
```

```python
import functools
import math

import jax
import jax.numpy as jnp
import numpy as np
from jax import lax
from jax.experimental import pallas as pl
from jax.experimental.pallas import tpu as pltpu

F32 = jnp.float32
BF16 = jnp.bfloat16

HEAD_DIM = 64
GROUPS = 2
REP = 4
HEADS = GROUPS * REP
CMP_BLOCK = 32
CMP_STRIDE = 16
SEL_BLOCK = 64
SEL_TOPK = 16
NSA_WINDOW = 512
SWA_WINDOW = 128
QBLOCK = 128
NUM_BUCKETS = 32
MAX_DISTANCE = 1024
RMS_EPS = 1e-6
LOG2E = math.log2(math.e)
MASK_BIG = 2.0 ** 99
HQ = REP * QBLOCK

SEL_TILE = 1024
TILES_PER_TRIP = 8
PROJ_TOKENS = 512
MLP_TOKENS = 512
MLP_HIDDEN = 4096
BAND_CHUNKS = 16
CMP_CHUNKS = 4
CMP_VARIANT_KEYS = 128
V7X_VMEM_BYTES = 64 * 1024 * 1024
VMEM_LIMIT = V7X_VMEM_BYTES * 7 // 8

_NT = (((1,), (1,)), ((), ()))
_TN = (((0,), (0,)), ((), ()))


def _params(*sem):
    return pltpu.CompilerParams(dimension_semantics=sem, vmem_limit_bytes=VMEM_LIMIT)


def _bucket_starts():
    max_exact = NUM_BUCKETS // 2
    d = np.arange(2 * MAX_DISTANCE)
    large = max_exact + (np.log(np.maximum(d, 1).astype(np.float64) / max_exact)
                         / math.log(MAX_DISTANCE / max_exact) * (NUM_BUCKETS - max_exact)).astype(np.int64)
    bucket = np.where(d < max_exact, d, np.minimum(large, NUM_BUCKETS - 1))
    return (bucket[None, :] < np.arange(NUM_BUCKETS)[:, None]).sum(axis=1).astype(np.int32)


def _rms(x, gain):
    return x * lax.rsqrt(jnp.mean(x * x, axis=-1, keepdims=True) + RMS_EPS) * gain


def _proj_kernel(x_ref, g_ref, wn_ref, wg_ref, wt_ref, kv_ref, gab_ref, qv_ref, gn_ref, kb_ref, y_ref, *, nqv):
    u = _rms(x_ref[...], g_ref[...]).astype(BF16)
    kvf = jnp.dot(u, wn_ref[...], preferred_element_type=F32)
    gw = GROUPS * HEAD_DIM
    kv_ref[...] = kvf[:, 2 * gw:].astype(BF16)
    nrow = y_ref.shape[1] // CMP_STRIDE
    for a in range(2):
        y_ref[a] = kvf[:, a * gw:(a + 1) * gw]
        both = [y_ref[a, pl.ds(p, nrow, stride=CMP_STRIDE), :] for p in range(CMP_STRIDE)]
        for gi in range(GROUPS):
            kb_ref[a, gi] = jnp.concatenate([t[:, gi * HEAD_DIM:(gi + 1) * HEAD_DIM] for t in both],
                                            axis=1).astype(BF16)
    gab_ref[...] = jnp.dot(u, wg_ref[...], preferred_element_type=F32).astype(BF16)
    t = lax.dot_general(wt_ref[...], u, _NT, preferred_element_type=F32)
    qv_ref[...] = t[:nqv].astype(BF16)
    gn_ref[...] = t[nqv:]


def _project(x, gain, wn, wg, wt, nqv, ts):
    s, d = x.shape
    nn, ng, ntr = wn.shape[1] - 2 * GROUPS * HEAD_DIM, wg.shape[1], wt.shape[0]
    full = lambda shp: pl.BlockSpec(shp, lambda i: (0, 0))
    return pl.pallas_call(
        functools.partial(_proj_kernel, nqv=nqv),
        grid=(s // ts,),
        in_specs=[pl.BlockSpec((ts, d), lambda i: (i, 0)), full((1, d)),
                  full(wn.shape), full((d, ng)), full((ntr, d))],
        out_specs=[pl.BlockSpec((ts, nn), lambda i: (i, 0)),
                   pl.BlockSpec((ts, ng), lambda i: (i, 0)),
                   pl.BlockSpec((nqv, ts), lambda i: (0, i)),
                   pl.BlockSpec((ntr - nqv, ts), lambda i: (0, i)),
                   pl.BlockSpec((2, GROUPS, ts // CMP_STRIDE, CMP_STRIDE * HEAD_DIM), lambda i: (0, 0, i, 0))],
        out_shape=[jax.ShapeDtypeStruct((s, nn), BF16), jax.ShapeDtypeStruct((s, ng), BF16),
                   jax.ShapeDtypeStruct((nqv, s), BF16), jax.ShapeDtypeStruct((ntr - nqv, s), F32),
                   jax.ShapeDtypeStruct((2, GROUPS, s // CMP_STRIDE, CMP_STRIDE * HEAD_DIM), BF16)],
        scratch_shapes=[pltpu.VMEM((2, ts, GROUPS * HEAD_DIM), F32)],
        compiler_params=_params("arbitrary"),
        name="in_proj",
    )(x, gain, wn, wg, wt)


def _gelu_tanh(x):
    return x * (0.5 * (1.0 + jnp.tanh(math.sqrt(2.0 / math.pi) * (x + 0.044715 * (x * x * x)))))


def _compress_kernel(kb_ref, pos_ref, w1_ref, w2_ref, w2t_ref, o_ref, ot_ref):
    kb = kb_ref[0, 0]
    half = kb.shape[1]
    n16 = kb.shape[0]
    first = jnp.dot(kb, w1_ref[0, :half, :], preferred_element_type=F32)
    second = jnp.dot(kb, w1_ref[0, half:, :], preferred_element_type=F32)
    posb = jnp.dot(pos_ref[0], w1_ref[0], preferred_element_type=F32)[0:1]
    pre = first + pltpu.roll(second, n16 - 1, 0) + posb
    h = _gelu_tanh(pre).astype(BF16)
    o_ref[0, 0] = jnp.dot(h, w2_ref[0], preferred_element_type=F32).astype(BF16)
    ot_ref[0, 0] = lax.dot_general(w2t_ref[0], h, _NT, preferred_element_type=F32).astype(BF16)


def _compress(kb, pos8, w1, w2, w2t):
    _, g, n16, cin2 = kb.shape
    hid = w1.shape[2]
    return pl.pallas_call(
        _compress_kernel,
        grid=(2, g),
        in_specs=[pl.BlockSpec((1, 1, n16, cin2), lambda a, b: (a, b, 0, 0)),
                  pl.BlockSpec((1, 8, 2 * cin2), lambda a, b: (a, 0, 0)),
                  pl.BlockSpec((1, 2 * cin2, hid), lambda a, b: (a, 0, 0)),
                  pl.BlockSpec((1, hid, HEAD_DIM), lambda a, b: (a, 0, 0)),
                  pl.BlockSpec((1, HEAD_DIM, hid), lambda a, b: (a, 0, 0))],
        out_specs=[pl.BlockSpec((1, 1, n16, HEAD_DIM), lambda a, b: (a, b, 0, 0)),
                   pl.BlockSpec((1, 1, HEAD_DIM, n16), lambda a, b: (a, b, 0, 0))],
        out_shape=[jax.ShapeDtypeStruct((2, g, n16, HEAD_DIM), BF16),
                   jax.ShapeDtypeStruct((2, g, HEAD_DIM, n16), BF16)],
        compiler_params=_params("arbitrary", "arbitrary"),
        name="kv_compress",
    )(kb, pos8, w1, w2, w2t)


def _cmpsel_body(q_ref, kc_ref, vct_ref, oc_ref, sel_ref, pg_ref, s_ref, t_base, *, cb, rows, nomask, nsel, topk):
    mrows = rows - nomask
    ratio = SEL_BLOCK // CMP_STRIDE
    nblk = rows // ratio
    n_io = nomask + lax.broadcasted_iota(jnp.int32, (mrows, HQ), 0)
    lane_t = lax.broadcasted_iota(jnp.int32, (mrows, HQ), 1) & (QBLOCK - 1)
    blk = lax.broadcasted_iota(jnp.int32, (nblk, QBLOCK), 0)

    def raw_scores(c):
        t0 = t_base + c * QBLOCK
        lanes = slice(c * QBLOCK, (c + 1) * QBLOCK)
        q4 = jnp.concatenate([q_ref[0, r * HEAD_DIM:(r + 1) * HEAD_DIM, lanes] for r in range(REP)], axis=1)
        s = jnp.dot(kc_ref[0, 0, 0:rows, :], q4, preferred_element_type=F32)
        edge = jnp.where(n_io * CMP_STRIDE + (CMP_BLOCK - 1) <= t0 + lane_t, s[nomask:], -jnp.inf)
        s = jnp.concatenate([s[:nomask], edge], axis=0) if nomask else edge
        s_ref[c, 0:rows, :] = s
        m = jnp.max(s, axis=0, keepdims=True)
        return jnp.where(m == -jnp.inf, 0.0, m)

    def attend(c, m):
        t0 = t_base + c * QBLOCK
        lanes = slice(c * QBLOCK, (c + 1) * QBLOCK)
        e = jnp.exp2(s_ref[c, 0:rows, :] - m)
        den = jnp.maximum(jnp.sum(e, axis=0, keepdims=True), 1e-30)
        p = e * (1.0 / den)
        o_t = jnp.dot(vct_ref[0, 0, :, 0:rows], p.astype(BF16), preferred_element_type=F32)
        for r in range(REP):
            oc_ref[0, r * HEAD_DIM:(r + 1) * HEAD_DIM, lanes] = (
                o_t[:, r * QBLOCK:(r + 1) * QBLOCK].astype(oc_ref.dtype))

        pg = ((p[:, 0:QBLOCK] + p[:, QBLOCK:2 * QBLOCK]) + p[:, 2 * QBLOCK:3 * QBLOCK]) + p[:, 3 * QBLOCK:]
        pg_ref[c, 0:8, :] = jnp.zeros((8, QBLOCK), F32)
        pg_ref[c, 8:8 + rows, :] = pg
        ps = pg_ref[c, pl.ds(7, nblk, stride=ratio), :]
        for o in range(1, ratio + 1):
            ps = ps + pg_ref[c, pl.ds(7 + o, nblk, stride=ratio), :]
        cur = (t0 + lax.broadcasted_iota(jnp.int32, (nblk, QBLOCK), 1)) // SEL_BLOCK
        valid = blk <= cur
        forced = (blk == 0) | (blk == cur) | (blk == cur - 1)
        return jnp.where(valid & ~forced, ps, -jnp.inf), valid

    scores, valids = [], []
    m = raw_scores(0)
    for c in range(cb):
        m_next = raw_scores(c + 1) if c + 1 < cb else None
        score, valid = attend(c, m)
        scores.append(score)
        valids.append(valid)
        m = m_next

    def pick_one(_, rems):
        out = []
        for rem in rems:
            top = jnp.max(rem, axis=0, keepdims=True)
            first = jnp.min(jnp.where(rem == top, blk, nblk), axis=0, keepdims=True)
            out.append(jnp.where(blk == first, -jnp.inf, rem))
        return tuple(out)

    rems = lax.fori_loop(0, max(topk - 3, 0), pick_one, tuple(scores))
    for c in range(cb):
        lanes = slice(c * QBLOCK, (c + 1) * QBLOCK)
        sel_ref[0, 0:nblk, lanes] = jnp.where((rems[c] == -jnp.inf) & valids[c], 0.0, -MASK_BIG).astype(BF16)
        if nblk < nsel:
            sel_ref[0, nblk:, lanes] = jnp.full((nsel - nblk, QBLOCK), -MASK_BIG, BF16)


def _cmpsel_kernel(q_ref, kc_ref, vct_ref, oc_ref, sel_ref, pg_ref, s_ref, *, n16, nsel, topk, nvar, cb):
    step = pl.program_id(1)
    per = n16 // nvar
    spv = per // (QBLOCK // CMP_STRIDE) // cb
    for k in range(1, nvar + 1):
        @pl.when((step >= (k - 1) * spv) & (step < k * spv))
        def _(k=k):
            _cmpsel_body(q_ref, kc_ref, vct_ref, oc_ref, sel_ref, pg_ref, s_ref, step * (cb * QBLOCK), cb=cb,
                         rows=per * k, nomask=max(per * (k - 1) - 8, 0), nsel=nsel, topk=topk)


def _cmp_select(q, kc_all, vct_all, s, cb=CMP_CHUNKS):
    n16 = s // CMP_STRIDE
    nsel = s // SEL_BLOCK
    topk = min(SEL_TOPK, nsel)
    nvar = max(n16 // CMP_VARIANT_KEYS, 1)
    return pl.pallas_call(
        functools.partial(_cmpsel_kernel, n16=n16, nsel=nsel, topk=topk, nvar=nvar, cb=cb),
        grid=(GROUPS, s // (QBLOCK * cb)),
        in_specs=[pl.BlockSpec((1, REP * HEAD_DIM, QBLOCK * cb), lambda g, c: (g, 0, c)),
                  pl.BlockSpec((1, 1, n16, HEAD_DIM), lambda g, c: (0, g, 0, 0)),
                  pl.BlockSpec((1, 1, HEAD_DIM, n16), lambda g, c: (1, g, 0, 0))],
        out_specs=[pl.BlockSpec((1, REP * HEAD_DIM, QBLOCK * cb), lambda g, c: (g, 0, c)),
                   pl.BlockSpec((1, nsel, QBLOCK * cb), lambda g, c: (g, 0, c))],
        out_shape=[jax.ShapeDtypeStruct((GROUPS, REP * HEAD_DIM, s), BF16),
                   jax.ShapeDtypeStruct((GROUPS, nsel, s), BF16)],
        scratch_shapes=[pltpu.VMEM((cb, n16 + 8, QBLOCK), F32), pltpu.VMEM((cb, n16, HQ), F32)],
        compiler_params=_params("arbitrary", "arbitrary"),
        name="cmp_select",
    )(q, kc_all, vct_all)


def _fill_bias_strip(strip_ref, thr_ref, tab_ref, g, *, dist0, lo, hi, minus_far):
    nrows = strip_ref.shape[0]

    def block(bi, carry):
        i0 = pl.multiple_of(bi * QBLOCK, QBLOCK)
        d = (lax.broadcasted_iota(jnp.int32, (QBLOCK, QBLOCK), 1)
             - lax.broadcasted_iota(jnp.int32, (QBLOCK, QBLOCK), 0) + (dist0 - i0))
        ok = d >= lo if hi is None else (d >= lo) & (d < hi)
        for r in range(REP):
            base = (g * REP + r) * NUM_BUCKETS
            out = jnp.full((QBLOCK, QBLOCK), tab_ref[base], F32)
            for b in range(1, NUM_BUCKETS):
                out = jnp.where(d >= thr_ref[b], tab_ref[base + b], out)
            if minus_far:
                out = out - tab_ref[base + NUM_BUCKETS - 1]
            strip_ref[pl.ds(i0, QBLOCK), r * QBLOCK:(r + 1) * QBLOCK] = jnp.where(ok, out, -MASK_BIG)
        return carry

    lax.fori_loop(0, nrows // QBLOCK, block, 0)


def _sel_kernel(thr_ref, tab_ref, q_ref, sel_ref, kaug_ref, vt_ref, cst_ref, o_ref,
                qa_ref, s_ref, tmax_ref, m_ref, acc_ref, strip_ref, *, tk):
    g = pl.program_id(0)
    st = pl.program_id(1)
    nb = tk // SEL_BLOCK
    cpt = tk // QBLOCK
    kd = GROUPS * HEAD_DIM
    dmax = 2 * tk - QBLOCK

    @pl.when(st == 0)
    def _():
        _fill_bias_strip(strip_ref, thr_ref, tab_ref, g, dist0=dmax, lo=0, hi=None, minus_far=True)

    qa_ref[...] = jnp.zeros(qa_ref.shape, BF16)
    for b in range(2):
        qa_ref[b, kd + nb:kd + nb + 16, :] = cst_ref[0]
    m_ref[...] = jnp.full(m_ref.shape, -jnp.inf, F32)
    acc_ref[...] = jnp.zeros(acc_ref.shape, F32)
    row0 = pl.multiple_of(g * HEAD_DIM, HEAD_DIM)

    def far_tiles(count):

        def advance(c, j):
            wrap = j + 1 == count
            return jnp.where(wrap, c + 1, c), jnp.where(wrap, 0, j + 1)

        def qk(c, j, buf):
            cc = jnp.minimum(c, cpt - 1)
            lane0 = pl.multiple_of(cc * QBLOCK, QBLOCK)
            for r in range(REP):
                qa_ref[buf, pl.ds(row0, HEAD_DIM), r * QBLOCK:(r + 1) * QBLOCK] = (
                    q_ref[0, r * HEAD_DIM:(r + 1) * HEAD_DIM, pl.ds(lane0, QBLOCK)])
            sb = sel_ref[0, pl.ds(pl.multiple_of(j * nb, nb), nb), pl.ds(lane0, QBLOCK)]
            for r in range(REP):
                qa_ref[buf, kd:kd + nb, r * QBLOCK:(r + 1) * QBLOCK] = sb
            k0 = pl.multiple_of(j * tk, tk)
            s = jnp.dot(kaug_ref[pl.ds(k0, tk), :], qa_ref[buf], preferred_element_type=F32)
            s_ref[buf] = s
            tmax_ref[buf] = jnp.max(s, axis=0, keepdims=True)

        def softmax_pv(c, j, buf):
            k0 = pl.multiple_of(j * tk, tk)
            m_prev = m_ref[c]
            m_new = jnp.maximum(m_prev, tmax_ref[buf])
            alpha = jnp.exp2(m_prev - m_new)
            p = jnp.exp2(s_ref[buf] - m_new)
            acc_ref[c] = alpha * acc_ref[c] + jnp.dot(vt_ref[0, :, pl.ds(k0, tk)], p.astype(BF16),
                                                      preferred_element_type=F32)
            m_ref[c] = m_new

        zero = jnp.int32(0)
        qk(zero, zero, 0)

        def trip(_, carry):
            cur = carry
            for k in range(TILES_PER_TRIP):
                nxt = advance(*cur)
                qk(*nxt, (k + 1) % 2)
                softmax_pv(*cur, k % 2)
                cur = nxt
            return cur

        lax.fori_loop(0, (cpt * count) // TILES_PER_TRIP, trip, (zero, zero))

    def near_tiles(with_previous):
        tiles = []
        for c in range(cpt):
            if with_previous:
                tiles.append((c, st - 1, tk + c * QBLOCK, tk, False))
            tiles.append((c, st, c * QBLOCK, (c + 1) * QBLOCK, True))

        def qk(c, j, delta, rows, diagonal, buf):
            lanes = slice(c * QBLOCK, (c + 1) * QBLOCK)
            for r in range(REP):
                qa_ref[buf, pl.ds(row0, HEAD_DIM), r * QBLOCK:(r + 1) * QBLOCK] = (
                    q_ref[0, r * HEAD_DIM:(r + 1) * HEAD_DIM, lanes])
            sb = sel_ref[0, pl.ds(pl.multiple_of(j * nb, nb), nb), lanes]
            for r in range(REP):
                qa_ref[buf, kd:kd + nb, r * QBLOCK:(r + 1) * QBLOCK] = sb
            k0 = pl.multiple_of(j * tk, tk)
            s = (jnp.dot(kaug_ref[pl.ds(k0, rows), :], qa_ref[buf], preferred_element_type=F32)
                 + strip_ref[dmax - delta:dmax - delta + rows, :])
            s_ref[buf, 0:rows, :] = s
            tmax_ref[buf] = jnp.max(s, axis=0, keepdims=True)

        def softmax_pv(c, j, rows, diagonal, buf):
            k0 = pl.multiple_of(j * tk, tk)
            m_prev = m_ref[c]
            m_new = jnp.maximum(m_prev, tmax_ref[buf])
            alpha = jnp.exp2(m_prev - m_new)
            p = jnp.exp2(s_ref[buf, 0:rows, :] - m_new)
            acc = alpha * acc_ref[c] + jnp.dot(vt_ref[0, :, pl.ds(k0, rows)], p.astype(BF16),
                                               preferred_element_type=F32)
            acc_ref[c] = acc
            m_ref[c] = m_new
            if diagonal:
                out = acc[0:HEAD_DIM] * (1.0 / jnp.maximum(acc[HEAD_DIM:HEAD_DIM + 1], 1e-30))
                for r in range(REP):
                    o_ref[0, r * HEAD_DIM:(r + 1) * HEAD_DIM, c * QBLOCK:(c + 1) * QBLOCK] = (
                        out[:, r * QBLOCK:(r + 1) * QBLOCK].astype(o_ref.dtype))

        qk(*tiles[0], 0)
        for n, (c, j, delta, rows, diagonal) in enumerate(tiles):
            if n + 1 < len(tiles):
                qk(*tiles[n + 1], (n + 1) % 2)
            softmax_pv(c, j, rows, diagonal, n % 2)

    @pl.when(st > 1)
    def _():
        far_tiles(st - 1)

    @pl.when(st == 0)
    def _():
        near_tiles(False)

    @pl.when(st > 0)
    def _():
        near_tiles(True)


def _selected(thr, tab, q, sel, kaug, vt, cst, s, tk):
    nsel = s // SEL_BLOCK
    sl = 3 * tk - QBLOCK
    cpt = tk // QBLOCK
    smem = pl.BlockSpec(memory_space=pltpu.SMEM)
    return pl.pallas_call(
        functools.partial(_sel_kernel, tk=tk),
        grid=(GROUPS, s // tk),
        in_specs=[smem, smem,
                  pl.BlockSpec((1, REP * HEAD_DIM, tk), lambda g, c: (g, 0, c)),
                  pl.BlockSpec((1, nsel, tk), lambda g, c: (g, 0, c)),
                  pl.BlockSpec(kaug.shape, lambda g, c: (0, 0)),
                  pl.BlockSpec((1, vt.shape[1], s), lambda g, c: (g, 0, 0)),
                  pl.BlockSpec((1, 16, HQ), lambda g, c: (g, 0, 0))],
        out_specs=pl.BlockSpec((1, REP * HEAD_DIM, tk), lambda g, c: (g, 0, c)),
        out_shape=jax.ShapeDtypeStruct((GROUPS, REP * HEAD_DIM, s), BF16),
        scratch_shapes=[pltpu.VMEM((2, kaug.shape[1], HQ), BF16), pltpu.VMEM((2, tk, HQ), F32),
                        pltpu.VMEM((2, 1, HQ), F32),
                        pltpu.VMEM((cpt, 1, HQ), F32), pltpu.VMEM((cpt, vt.shape[1], HQ), F32),
                        pltpu.VMEM((sl, HQ), F32)],
        compiler_params=_params("arbitrary", "arbitrary"),
        name="selected_attn",
    )(thr, tab, q, sel, kaug, vt, cst)


def _band_kernel(thr_ref, tab_ref, q_ref, k_ref, vt_ref, sink_ref, o_ref, qa_ref, strip_ref, s_ref, *,
                 window, use_sink, cb):
    g = pl.program_id(0)

    @pl.when(pl.program_id(1) == 0)
    def _():
        _fill_bias_strip(strip_ref, thr_ref, tab_ref, g, dist0=window, lo=0, hi=window, minus_far=False)

    base = pl.multiple_of(pl.program_id(1) * (QBLOCK * cb), QBLOCK * cb)
    nk = window + QBLOCK
    qa_ref[...] = jnp.zeros(qa_ref.shape, BF16)
    row0 = pl.multiple_of(g * HEAD_DIM, HEAD_DIM)
    for c in range(cb):
        for r in range(REP):
            qa_ref[c, pl.ds(row0, HEAD_DIM), r * QBLOCK:(r + 1) * QBLOCK] = (
                q_ref[0, r * HEAD_DIM:(r + 1) * HEAD_DIM, c * QBLOCK:(c + 1) * QBLOCK])

    def scores(c, first_step):
        t0 = base + c * QBLOCK
        s = jnp.dot(k_ref[pl.ds(t0, nk), :], qa_ref[c], preferred_element_type=F32) + strip_ref[...]
        if first_step and c * QBLOCK < window:
            row = lax.broadcasted_iota(jnp.int32, (nk, HQ), 0)
            s = jnp.where(row >= window - c * QBLOCK, s, -MASK_BIG)
        s_ref[c] = s
        m = jnp.max(s, axis=0, keepdims=True)
        return jnp.maximum(m, sink_ref[0]) if use_sink else m

    def attend(c, m):
        t0 = base + c * QBLOCK
        e = jnp.exp2(s_ref[c] - m)
        o_t = jnp.dot(vt_ref[0, :, pl.ds(t0, nk)], e.astype(BF16), preferred_element_type=F32)
        den = o_t[HEAD_DIM:HEAD_DIM + 1]
        den = den + jnp.exp2(sink_ref[0] - m) if use_sink else jnp.maximum(den, 1e-30)
        o_t = o_t[0:HEAD_DIM] * (1.0 / den)
        for r in range(REP):
            o_ref[0, r * HEAD_DIM:(r + 1) * HEAD_DIM, c * QBLOCK:(c + 1) * QBLOCK] = (
                o_t[:, r * QBLOCK:(r + 1) * QBLOCK].astype(o_ref.dtype))

    def chunks(first_step):
        m = scores(0, first_step)
        for c in range(cb):
            m_next = scores(c + 1, first_step) if c + 1 < cb else None
            attend(c, m)
            m = m_next

    @pl.when(pl.program_id(1) == 0)
    def _():
        chunks(True)

    @pl.when(pl.program_id(1) > 0)
    def _():
        chunks(False)


def _banded(thr, tab, q, kpad, vtpad, sink, s, window, use_sink, name, cb=BAND_CHUNKS):
    assert QBLOCK * cb >= window
    nk = window + QBLOCK
    smem = pl.BlockSpec(memory_space=pltpu.SMEM)
    return pl.pallas_call(
        functools.partial(_band_kernel, window=window, use_sink=use_sink, cb=cb),
        grid=(GROUPS, s // (QBLOCK * cb)),
        in_specs=[smem, smem,
                  pl.BlockSpec((1, REP * HEAD_DIM, QBLOCK * cb), lambda g, c: (g, 0, c)),
                  pl.BlockSpec(kpad.shape, lambda g, c: (0, 0)),
                  pl.BlockSpec((1, vtpad.shape[1], s + window), lambda g, c: (g, 0, 0)),
                  pl.BlockSpec((1, 1, HQ), lambda g, c: (g, 0, 0))],
        out_specs=pl.BlockSpec((1, REP * HEAD_DIM, QBLOCK * cb), lambda g, c: (g, 0, c)),
        out_shape=jax.ShapeDtypeStruct((GROUPS, REP * HEAD_DIM, s), BF16),
        scratch_shapes=[pltpu.VMEM((cb, GROUPS * HEAD_DIM, HQ), BF16), pltpu.VMEM((nk, HQ), F32),
                        pltpu.VMEM((cb, nk, HQ), F32)],
        compiler_params=_params("arbitrary", "arbitrary"),
        name=name,
    )(thr, tab, q, kpad, vtpad, sink)


def _merge_kernel(oc_ref, os_ref, ow_ref, ob_ref, gn_ref, gab_ref, x_ref, wa_ref, wb_ref, wo_ref, gp_ref, o_ref):
    d = x_ref.shape[1]
    gn = jax.nn.sigmoid(gn_ref[...])
    parts = []
    for h in range(HEADS):
        rows = slice(h * HEAD_DIM, (h + 1) * HEAD_DIM)
        parts.append(gn[h:h + 1] * oc_ref[rows, :].astype(F32)
                     + gn[HEADS + h:HEADS + h + 1] * os_ref[rows, :].astype(F32)
                     + gn[2 * HEADS + h:2 * HEADS + h + 1] * ow_ref[rows, :].astype(F32))
    oa_t = jnp.concatenate(parts, axis=0).astype(BF16)
    ya = lax.dot_general(oa_t, wa_ref[...], _TN, preferred_element_type=F32)
    yb = lax.dot_general(ob_ref[...], wb_ref[...], _TN, preferred_element_type=F32)
    y = (jax.nn.sigmoid(gab_ref[:, :d].astype(F32)) * ya
         + jax.nn.sigmoid(gab_ref[:, d:].astype(F32)) * yb)
    z = jnp.dot(y.astype(BF16), wo_ref[...], preferred_element_type=F32)
    o_ref[...] = x_ref[...] + _rms(z, gp_ref[...])


def _merge(oc, os_, ow, ob, gn, gab, x, wa, wb, wo, gpost, ts):
    s, d = x.shape
    hd = oc.shape[0]
    tok = lambda rows: pl.BlockSpec((rows, ts), lambda i: (0, i))
    full = lambda shp: pl.BlockSpec(shp, lambda i: (0, 0))
    return pl.pallas_call(
        _merge_kernel,
        grid=(s // ts,),
        in_specs=[tok(hd), tok(hd), tok(hd), tok(hd), tok(gn.shape[0]),
                  pl.BlockSpec((ts, 2 * d), lambda i: (i, 0)), pl.BlockSpec((ts, d), lambda i: (i, 0)),
                  full((hd, d)), full((hd, d)), full((d, d)), full((1, d))],
        out_specs=pl.BlockSpec((ts, d), lambda i: (i, 0)),
        out_shape=jax.ShapeDtypeStruct((s, d), F32),
        compiler_params=_params("arbitrary"),
        name="merge_out_proj",
    )(oc, os_, ow, ob, gn, gab, x, wa, wb, wo, gpost)


def _mlp_kernel(h_ref, gpre_ref, w1_ref, w2_ref, gpost_ref, o_ref, u_ref, acc_ref):
    f = pl.program_id(1)

    @pl.when(f == 0)
    def _():
        u_ref[...] = _rms(h_ref[...], gpre_ref[...]).astype(BF16)
        acc_ref[...] = jnp.zeros(acc_ref.shape, F32)

    a = jnp.maximum(jnp.dot(u_ref[...], w1_ref[...], preferred_element_type=F32), 0.0)
    acc_ref[...] += jnp.dot((a * a).astype(BF16), w2_ref[...], preferred_element_type=F32)

    @pl.when(f == pl.num_programs(1) - 1)
    def _():
        o_ref[...] = h_ref[...] + _rms(acc_ref[...], gpost_ref[...])


def _mlp(h, gpre, w1, w2, gpost, ts, tf):
    s, d = h.shape
    dff = w1.shape[1]
    return pl.pallas_call(
        _mlp_kernel,
        grid=(s // ts, dff // tf),
        in_specs=[pl.BlockSpec((ts, d), lambda i, f: (i, 0)), pl.BlockSpec((1, d), lambda i, f: (0, 0)),
                  pl.BlockSpec((d, tf), lambda i, f: (0, f)), pl.BlockSpec((tf, d), lambda i, f: (f, 0)),
                  pl.BlockSpec((1, d), lambda i, f: (0, 0))],
        out_specs=pl.BlockSpec((ts, d), lambda i, f: (i, 0)),
        out_shape=jax.ShapeDtypeStruct((s, d), F32),
        scratch_shapes=[pltpu.VMEM((ts, d), BF16), pltpu.VMEM((ts, d), F32)],
        compiler_params=_params("arbitrary", "arbitrary"),
        name="mlp_relu2",
    )(h, gpre, w1, w2, gpost)


def kernel(x, norm_mix_pre, norm_mix_post, norm_mlp_pre, norm_mlp_post, w_in,
           cmp_pos_k, cmp_w1_k, cmp_w2_k, cmp_pos_v, cmp_w1_v, cmp_w2_v,
           attn_sinks, rel_bias, w_up_nsa, w_up_swa, w_out, w_mlp_in, w_mlp_out):
    b, s, d = x.shape
    assert b == 1 and s % SEL_TILE == 0 and w_in.shape[0] == 1
    assert SEL_TILE >= MAX_DISTANCE
    qd = HEADS * HEAD_DIM
    kvd = GROUPS * HEAD_DIM
    sizes = (qd,) + (kvd,) * 6 + (3 * HEADS, qd, kvd, kvd, d, d)
    offs = [0]
    for z in sizes:
        offs.append(offs[-1] + z)
    w = w_in[0]
    col = lambda i: w[:, offs[i]:offs[i + 1]]
    (w_qn, w_kc, w_vc, w_ks, w_vs, w_kw, w_vw, w_gn, w_qs, w_k_s, w_v_s, w_ga, w_gb) = [col(i) for i in range(13)]
    scale = HEAD_DIM ** -0.5 * LOG2E
    w_gn = w_gn.reshape(d, HEADS, 3).transpose(0, 2, 1).reshape(d, 3 * HEADS)
    w_gn = jnp.pad(w_gn, ((0, 0), (0, 8)))
    wn = jnp.concatenate([w_kc, w_vc, w_ks, w_kw, w_k_s], axis=1).astype(BF16)
    wg = jnp.concatenate([w_ga, w_gb], axis=1).astype(BF16)
    wt = jnp.concatenate([w_qn * scale, w_qs * scale, w_vs, w_vw, w_v_s, w_gn], axis=1).T.astype(BF16)
    nqv = 2 * qd + 3 * kvd

    x2 = x[0]
    kv, gab, qv, gn, kb = _project(x2, norm_mix_pre, wn, wg, wt, nqv, ts=PROJ_TOKENS)
    q_nsa = qv[0:qd].reshape(GROUPS, REP * HEAD_DIM, s)
    q_swa = qv[qd:2 * qd].reshape(GROUPS, REP * HEAD_DIM, s)
    vs_t = qv[2 * qd:2 * qd + kvd].reshape(GROUPS, HEAD_DIM, s)
    vw_t = qv[2 * qd + kvd:2 * qd + 2 * kvd].reshape(GROUPS, HEAD_DIM, s)
    vswa_t = qv[2 * qd + 2 * kvd:].reshape(GROUPS, HEAD_DIM, s)

    n16 = s // CMP_STRIDE
    pos8 = jnp.stack([cmp_pos_k[0], cmp_pos_v[0]]).reshape(2, 1, CMP_BLOCK * HEAD_DIM)
    pos8 = jnp.broadcast_to(pos8, (2, 8, CMP_BLOCK * HEAD_DIM)).astype(BF16)
    w1 = jnp.stack([cmp_w1_k[0], cmp_w1_v[0]]).astype(BF16)
    w2 = jnp.stack([cmp_w2_k[0], cmp_w2_v[0]]).astype(BF16)
    cmp_n, cmp_t = _compress(kb, pos8, w1, w2, w2.transpose(0, 2, 1))

    oc, sel = _cmp_select(q_nsa, cmp_n, cmp_t, s)

    bias_nsa = rel_bias[:, :HEADS].reshape(NUM_BUCKETS, GROUPS, REP) * LOG2E
    bias_swa = rel_bias[:, HEADS:].reshape(NUM_BUCKETS, GROUPS, REP) * LOG2E
    thr = jnp.asarray(_bucket_starts())
    tab_nsa = bias_nsa.transpose(1, 2, 0).reshape(-1)
    tab_swa = bias_swa.transpose(1, 2, 0).reshape(-1)

    tk = SEL_TILE
    nb = tk // SEL_BLOCK
    posn = jnp.arange(s)
    onehot = ((posn // SEL_BLOCK) % nb)[:, None] == jnp.arange(nb)[None, :]
    aug = jnp.concatenate([onehot.astype(BF16), jnp.ones((s, 2), BF16),
                           jnp.zeros((s, kvd - nb - 2), BF16)], axis=1)
    kaug = jnp.concatenate([kv[:, 0:kvd], aug], axis=1)
    far = bias_nsa[NUM_BUCKETS - 1]
    far_hi = far.astype(BF16)
    far_lo = (far - far_hi.astype(F32)).astype(BF16)
    cst = jnp.stack([far_hi, far_lo], axis=1)
    cst = jnp.broadcast_to(cst[:, :, :, None], (GROUPS, 2, REP, QBLOCK)).reshape(GROUPS, 2, HQ)
    cst = jnp.pad(cst, ((0, 0), (0, 14), (0, 0)))
    ones_row = jnp.concatenate([jnp.ones((GROUPS, 1, s), BF16), jnp.zeros((GROUPS, 15, s), BF16)], axis=1)
    vs_aug = jnp.concatenate([vs_t, ones_row], axis=1)
    o_sel = _selected(thr, tab_nsa, q_nsa, sel, kaug, vs_aug, cst, s, tk)

    pad_rows = lambda a, wdw: jnp.pad(a, ((wdw, 0), (0, 0)))
    pad_lanes = lambda a, wdw: jnp.pad(jnp.concatenate([a, ones_row], axis=1), ((0, 0), (0, 0), (wdw, 0)))
    no_sink = jnp.zeros((GROUPS, 1, HQ), F32)
    o_win = _banded(thr, tab_nsa, q_nsa, pad_rows(kv[:, kvd:2 * kvd], NSA_WINDOW), pad_lanes(vw_t, NSA_WINDOW),
                    no_sink, s, NSA_WINDOW, False, "nsa_window")
    sink = attn_sinks[0].reshape(GROUPS, 1, REP, 1).astype(F32) * LOG2E
    sink = jnp.broadcast_to(sink, (GROUPS, 1, REP, QBLOCK)).reshape(GROUPS, 1, HQ)
    o_swa = _banded(thr, tab_swa, q_swa, pad_rows(kv[:, 2 * kvd:3 * kvd], SWA_WINDOW), pad_lanes(vswa_t, SWA_WINDOW),
                    sink, s, SWA_WINDOW, True, "swa_sink")

    flat = lambda a: a.reshape(qd, s)
    h1 = _merge(flat(oc), flat(o_sel), flat(o_win), flat(o_swa), gn, gab, x2,
                w_up_nsa[0].astype(BF16), w_up_swa[0].astype(BF16), w_out[0].astype(BF16),
                norm_mix_post, ts=PROJ_TOKENS)
    out = _mlp(h1, norm_mlp_pre, w_mlp_in[0].astype(BF16), w_mlp_out[0].astype(BF16), norm_mlp_post,
               ts=MLP_TOKENS, tf=MLP_HIDDEN)
    return out[None]
```

```python
import functools
import math

import jax
import jax.numpy as jnp
import numpy as np
from jax import lax
from jax.experimental import pallas as pl
from jax.experimental.pallas import tpu as pltpu

F32 = jnp.float32
BF16 = jnp.bfloat16

HEAD_DIM = 64
GROUPS = 2
REP = 4
HEADS = GROUPS * REP
CMP_BLOCK = 32
CMP_STRIDE = 16
SEL_BLOCK = 64
SEL_TOPK = 16
NSA_WINDOW = 512
SWA_WINDOW = 128
QBLOCK = 128
NUM_BUCKETS = 32
MAX_DISTANCE = 1024
RMS_EPS = 1e-6
LOG2E = math.log2(math.e)
MASK_BIG = 2.0 ** 99
HQ = REP * QBLOCK

SEL_TILE = 1024
TILES_PER_TRIP = 8
PROJ_TOKENS = 512
MLP_TOKENS = 512
BAND_CHUNKS = 16
CMP_CHUNKS = 4
CMP_VARIANT_KEYS = 128
V7X_VMEM_BYTES = 64 * 1024 * 1024
VMEM_LIMIT = V7X_VMEM_BYTES * 7 // 8

_NT = (((1,), (1,)), ((), ()))
_TN = (((0,), (0,)), ((), ()))


def _params(*sem):
    return pltpu.CompilerParams(dimension_semantics=sem, vmem_limit_bytes=VMEM_LIMIT)


def _bucket_starts():
    max_exact = NUM_BUCKETS // 2
    d = np.arange(2 * MAX_DISTANCE)
    large = max_exact + (np.log(np.maximum(d, 1).astype(np.float64) / max_exact)
                         / math.log(MAX_DISTANCE / max_exact) * (NUM_BUCKETS - max_exact)).astype(np.int64)
    bucket = np.where(d < max_exact, d, np.minimum(large, NUM_BUCKETS - 1))
    return (bucket[None, :] < np.arange(NUM_BUCKETS)[:, None]).sum(axis=1).astype(np.int32)


def _rms(x, gain):
    return x * lax.rsqrt(jnp.mean(x * x, axis=-1, keepdims=True) + RMS_EPS) * gain


def _proj_kernel(x_ref, g_ref, wn_ref, wg_ref, wt_ref, kv_ref, gab_ref, qv_ref, gn_ref, kb_ref, y_ref, *, nqv):
    u = _rms(x_ref[...], g_ref[...]).astype(BF16)
    kvf = jnp.dot(u, wn_ref[...], preferred_element_type=F32)
    gw = GROUPS * HEAD_DIM
    kv_ref[...] = kvf[:, 2 * gw:].astype(BF16)
    nrow = y_ref.shape[1] // CMP_STRIDE
    for a in range(2):
        y_ref[a] = kvf[:, a * gw:(a + 1) * gw]
        both = [y_ref[a, pl.ds(p, nrow, stride=CMP_STRIDE), :] for p in range(CMP_STRIDE)]
        for gi in range(GROUPS):
            kb_ref[a, gi] = jnp.concatenate([t[:, gi * HEAD_DIM:(gi + 1) * HEAD_DIM] for t in both],
                                            axis=1).astype(BF16)
    gab_ref[...] = jnp.dot(u, wg_ref[...], preferred_element_type=F32).astype(BF16)
    t = lax.dot_general(wt_ref[...], u, _NT, preferred_element_type=F32)
    qv_ref[...] = t[:nqv].astype(BF16)
    gn_ref[...] = t[nqv:]


def _project(x, gain, wn, wg, wt, nqv, ts):
    s, d = x.shape
    nn, ng, ntr = wn.shape[1] - 2 * GROUPS * HEAD_DIM, wg.shape[1], wt.shape[0]
    full = lambda shp: pl.BlockSpec(shp, lambda i: (0, 0))
    return pl.pallas_call(
        functools.partial(_proj_kernel, nqv=nqv),
        grid=(s // ts,),
        in_specs=[pl.BlockSpec((ts, d), lambda i: (i, 0)), full((1, d)),
                  full(wn.shape), full((d, ng)), full((ntr, d))],
        out_specs=[pl.BlockSpec((ts, nn), lambda i: (i, 0)),
                   pl.BlockSpec((ts, ng), lambda i: (i, 0)),
                   pl.BlockSpec((nqv, ts), lambda i: (0, i)),
                   pl.BlockSpec((ntr - nqv, ts), lambda i: (0, i)),
                   pl.BlockSpec((2, GROUPS, ts // CMP_STRIDE, CMP_STRIDE * HEAD_DIM), lambda i: (0, 0, i, 0))],
        out_shape=[jax.ShapeDtypeStruct((s, nn), BF16), jax.ShapeDtypeStruct((s, ng), BF16),
                   jax.ShapeDtypeStruct((nqv, s), BF16), jax.ShapeDtypeStruct((ntr - nqv, s), F32),
                   jax.ShapeDtypeStruct((2, GROUPS, s // CMP_STRIDE, CMP_STRIDE * HEAD_DIM), BF16)],
        scratch_shapes=[pltpu.VMEM((2, ts, GROUPS * HEAD_DIM), F32)],
        compiler_params=_params("arbitrary"),
        name="in_proj",
    )(x, gain, wn, wg, wt)


def _gelu_tanh(x):
    return x * (0.5 * (1.0 + jnp.tanh(math.sqrt(2.0 / math.pi) * (x + 0.044715 * (x * x * x)))))


def _compress_kernel(kb_ref, pos_ref, w1_ref, w2_ref, w2t_ref, o_ref, ot_ref):
    kb = kb_ref[0, 0]
    half = kb.shape[1]
    n16 = kb.shape[0]
    first = jnp.dot(kb, w1_ref[0, :half, :], preferred_element_type=F32)
    second = jnp.dot(kb, w1_ref[0, half:, :], preferred_element_type=F32)
    posb = jnp.dot(pos_ref[0], w1_ref[0], preferred_element_type=F32)[0:1]
    pre = first + pltpu.roll(second, n16 - 1, 0) + posb
    h = _gelu_tanh(pre).astype(BF16)
    o_ref[0, 0] = jnp.dot(h, w2_ref[0], preferred_element_type=F32).astype(BF16)
    ot_ref[0, 0] = lax.dot_general(w2t_ref[0], h, _NT, preferred_element_type=F32).astype(BF16)


def _compress(kb, pos8, w1, w2, w2t):
    _, g, n16, cin2 = kb.shape
    hid = w1.shape[2]
    return pl.pallas_call(
        _compress_kernel,
        grid=(2, g),
        in_specs=[pl.BlockSpec((1, 1, n16, cin2), lambda a, b: (a, b, 0, 0)),
                  pl.BlockSpec((1, 8, 2 * cin2), lambda a, b: (a, 0, 0)),
                  pl.BlockSpec((1, 2 * cin2, hid), lambda a, b: (a, 0, 0)),
                  pl.BlockSpec((1, hid, HEAD_DIM), lambda a, b: (a, 0, 0)),
                  pl.BlockSpec((1, HEAD_DIM, hid), lambda a, b: (a, 0, 0))],
        out_specs=[pl.BlockSpec((1, 1, n16, HEAD_DIM), lambda a, b: (a, b, 0, 0)),
                   pl.BlockSpec((1, 1, HEAD_DIM, n16), lambda a, b: (a, b, 0, 0))],
        out_shape=[jax.ShapeDtypeStruct((2, g, n16, HEAD_DIM), BF16),
                   jax.ShapeDtypeStruct((2, g, HEAD_DIM, n16), BF16)],
        compiler_params=_params("arbitrary", "arbitrary"),
        name="kv_compress",
    )(kb, pos8, w1, w2, w2t)


def _cmpsel_body(q_ref, kc_ref, vct_ref, oc_ref, sel_ref, pg_ref, s_ref, t_base, *, cb, rows, nomask, nsel, topk):
    mrows = rows - nomask
    ratio = SEL_BLOCK // CMP_STRIDE
    nblk = rows // ratio
    n_io = nomask + lax.broadcasted_iota(jnp.int32, (mrows, HQ), 0)
    lane_t = lax.broadcasted_iota(jnp.int32, (mrows, HQ), 1) & (QBLOCK - 1)
    blk = lax.broadcasted_iota(jnp.int32, (nblk, QBLOCK), 0)

    def raw_scores(c):
        t0 = t_base + c * QBLOCK
        lanes = slice(c * QBLOCK, (c + 1) * QBLOCK)
        q4 = jnp.concatenate([q_ref[0, r * HEAD_DIM:(r + 1) * HEAD_DIM, lanes] for r in range(REP)], axis=1)
        s = jnp.dot(kc_ref[0, 0, 0:rows, :], q4, preferred_element_type=F32)
        edge = jnp.where(n_io * CMP_STRIDE + (CMP_BLOCK - 1) <= t0 + lane_t, s[nomask:], -jnp.inf)
        s = jnp.concatenate([s[:nomask], edge], axis=0) if nomask else edge
        s_ref[c, 0:rows, :] = s
        m = jnp.max(s, axis=0, keepdims=True)
        return jnp.where(m == -jnp.inf, 0.0, m)

    def attend(c, m):
        t0 = t_base + c * QBLOCK
        lanes = slice(c * QBLOCK, (c + 1) * QBLOCK)
        e = jnp.exp2(s_ref[c, 0:rows, :] - m)
        den = jnp.maximum(jnp.sum(e, axis=0, keepdims=True), 1e-30)
        p = e * (1.0 / den)
        o_t = jnp.dot(vct_ref[0, 0, :, 0:rows], p.astype(BF16), preferred_element_type=F32)
        for r in range(REP):
            oc_ref[0, r * HEAD_DIM:(r + 1) * HEAD_DIM, lanes] = (
                o_t[:, r * QBLOCK:(r + 1) * QBLOCK].astype(oc_ref.dtype))

        pg = ((p[:, 0:QBLOCK] + p[:, QBLOCK:2 * QBLOCK]) + p[:, 2 * QBLOCK:3 * QBLOCK]) + p[:, 3 * QBLOCK:]
        pg_ref[c, 0:8, :] = jnp.zeros((8, QBLOCK), F32)
        pg_ref[c, 8:8 + rows, :] = pg
        ps = pg_ref[c, pl.ds(7, nblk, stride=ratio), :]
        for o in range(1, ratio + 1):
            ps = ps + pg_ref[c, pl.ds(7 + o, nblk, stride=ratio), :]
        cur = (t0 + lax.broadcasted_iota(jnp.int32, (nblk, QBLOCK), 1)) // SEL_BLOCK
        valid = blk <= cur
        forced = (blk == 0) | (blk == cur) | (blk == cur - 1)
        return jnp.where(valid & ~forced, ps, -jnp.inf), valid

    scores, valids = [], []
    m = raw_scores(0)
    for c in range(cb):
        m_next = raw_scores(c + 1) if c + 1 < cb else None
        score, valid = attend(c, m)
        scores.append(score)
        valids.append(valid)
        m = m_next

    def pick_one(_, rems):
        out = []
        for rem in rems:
            top = jnp.max(rem, axis=0, keepdims=True)
            first = jnp.min(jnp.where(rem == top, blk, nblk), axis=0, keepdims=True)
            out.append(jnp.where(blk == first, -jnp.inf, rem))
        return tuple(out)

    rems = lax.fori_loop(0, max(topk - 3, 0), pick_one, tuple(scores))
    for c in range(cb):
        lanes = slice(c * QBLOCK, (c + 1) * QBLOCK)
        sel_ref[0, 0:nblk, lanes] = jnp.where((rems[c] == -jnp.inf) & valids[c], 0.0, -MASK_BIG).astype(BF16)
        if nblk < nsel:
            sel_ref[0, nblk:, lanes] = jnp.full((nsel - nblk, QBLOCK), -MASK_BIG, BF16)


def _cmpsel_kernel(q_ref, kc_ref, vct_ref, oc_ref, sel_ref, pg_ref, s_ref, *, n16, nsel, topk, nvar, cb):
    step = pl.program_id(1)
    per = n16 // nvar
    spv = per // (QBLOCK // CMP_STRIDE) // cb
    for k in range(1, nvar + 1):
        @pl.when((step >= (k - 1) * spv) & (step < k * spv))
        def _(k=k):
            _cmpsel_body(q_ref, kc_ref, vct_ref, oc_ref, sel_ref, pg_ref, s_ref, step * (cb * QBLOCK), cb=cb,
                         rows=per * k, nomask=max(per * (k - 1) - 8, 0), nsel=nsel, topk=topk)


def _cmp_select(q, kc_all, vct_all, s, cb=CMP_CHUNKS):
    n16 = s // CMP_STRIDE
    nsel = s // SEL_BLOCK
    topk = min(SEL_TOPK, nsel)
    nvar = max(n16 // CMP_VARIANT_KEYS, 1)
    return pl.pallas_call(
        functools.partial(_cmpsel_kernel, n16=n16, nsel=nsel, topk=topk, nvar=nvar, cb=cb),
        grid=(GROUPS, s // (QBLOCK * cb)),
        in_specs=[pl.BlockSpec((1, REP * HEAD_DIM, QBLOCK * cb), lambda g, c: (g, 0, c)),
                  pl.BlockSpec((1, 1, n16, HEAD_DIM), lambda g, c: (0, g, 0, 0)),
                  pl.BlockSpec((1, 1, HEAD_DIM, n16), lambda g, c: (1, g, 0, 0))],
        out_specs=[pl.BlockSpec((1, REP * HEAD_DIM, QBLOCK * cb), lambda g, c: (g, 0, c)),
                   pl.BlockSpec((1, nsel, QBLOCK * cb), lambda g, c: (g, 0, c))],
        out_shape=[jax.ShapeDtypeStruct((GROUPS, REP * HEAD_DIM, s), BF16),
                   jax.ShapeDtypeStruct((GROUPS, nsel, s), BF16)],
        scratch_shapes=[pltpu.VMEM((cb, n16 + 8, QBLOCK), F32), pltpu.VMEM((cb, n16, HQ), F32)],
        compiler_params=_params("arbitrary", "arbitrary"),
        name="cmp_select",
    )(q, kc_all, vct_all)


def _fill_bias_strip(strip_ref, thr_ref, tab_ref, g, *, dist0, lo, hi, minus_far):
    nrows = strip_ref.shape[0]

    def block(bi, carry):
        i0 = pl.multiple_of(bi * QBLOCK, QBLOCK)
        d = (lax.broadcasted_iota(jnp.int32, (QBLOCK, QBLOCK), 1)
             - lax.broadcasted_iota(jnp.int32, (QBLOCK, QBLOCK), 0) + (dist0 - i0))
        ok = d >= lo if hi is None else (d >= lo) & (d < hi)
        for r in range(REP):
            base = (g * REP + r) * NUM_BUCKETS
            out = jnp.full((QBLOCK, QBLOCK), tab_ref[base], F32)
            for b in range(1, NUM_BUCKETS):
                out = jnp.where(d >= thr_ref[b], tab_ref[base + b], out)
            if minus_far:
                out = out - tab_ref[base + NUM_BUCKETS - 1]
            strip_ref[pl.ds(i0, QBLOCK), r * QBLOCK:(r + 1) * QBLOCK] = jnp.where(ok, out, -MASK_BIG)
        return carry

    lax.fori_loop(0, nrows // QBLOCK, block, 0)


def _sel_kernel(thr_ref, tab_ref, q_ref, sel_ref, kaug_ref, vt_ref, cst_ref, o_ref,
                qa_ref, s_ref, tmax_ref, m_ref, acc_ref, strip_ref, *, tk):
    g = pl.program_id(0)
    st = pl.program_id(1)
    nb = tk // SEL_BLOCK
    cpt = tk // QBLOCK
    kd = GROUPS * HEAD_DIM
    dmax = 2 * tk - QBLOCK

    @pl.when(st == 0)
    def _():
        _fill_bias_strip(strip_ref, thr_ref, tab_ref, g, dist0=dmax, lo=0, hi=None, minus_far=True)

    qa_ref[...] = jnp.zeros(qa_ref.shape, BF16)
    for b in range(2):
        qa_ref[b, kd + nb:kd + nb + 16, :] = cst_ref[0]
    m_ref[...] = jnp.full(m_ref.shape, -jnp.inf, F32)
    acc_ref[...] = jnp.zeros(acc_ref.shape, F32)
    row0 = pl.multiple_of(g * HEAD_DIM, HEAD_DIM)

    def far_tiles(count):

        def advance(c, j):
            wrap = j + 1 == count
            return jnp.where(wrap, c + 1, c), jnp.where(wrap, 0, j + 1)

        def qk(c, j, buf):
            cc = jnp.minimum(c, cpt - 1)
            lane0 = pl.multiple_of(cc * QBLOCK, QBLOCK)
            for r in range(REP):
                qa_ref[buf, pl.ds(row0, HEAD_DIM), r * QBLOCK:(r + 1) * QBLOCK] = (
                    q_ref[0, r * HEAD_DIM:(r + 1) * HEAD_DIM, pl.ds(lane0, QBLOCK)])
            sb = sel_ref[0, pl.ds(pl.multiple_of(j * nb, nb), nb), pl.ds(lane0, QBLOCK)]
            for r in range(REP):
                qa_ref[buf, kd:kd + nb, r * QBLOCK:(r + 1) * QBLOCK] = sb
            k0 = pl.multiple_of(j * tk, tk)
            s = jnp.dot(kaug_ref[pl.ds(k0, tk), :], qa_ref[buf], preferred_element_type=F32)
            s_ref[buf] = s
            tmax_ref[buf] = jnp.max(s, axis=0, keepdims=True)

        def softmax_pv(c, j, buf):
            k0 = pl.multiple_of(j * tk, tk)
            m_prev = m_ref[c]
            m_new = jnp.maximum(m_prev, tmax_ref[buf])
            alpha = jnp.exp2(m_prev - m_new)
            p = jnp.exp2(s_ref[buf] - m_new)
            acc_ref[c] = alpha * acc_ref[c] + jnp.dot(vt_ref[0, :, pl.ds(k0, tk)], p.astype(BF16),
                                                      preferred_element_type=F32)
            m_ref[c] = m_new

        zero = jnp.int32(0)
        qk(zero, zero, 0)

        def trip(_, carry):
            cur = carry
            for k in range(TILES_PER_TRIP):
                nxt = advance(*cur)
                qk(*nxt, (k + 1) % 2)
                softmax_pv(*cur, k % 2)
                cur = nxt
            return cur

        lax.fori_loop(0, (cpt * count) // TILES_PER_TRIP, trip, (zero, zero))

    def near_tiles(with_previous):
        tiles = []
        for c in range(cpt):
            if with_previous:
                tiles.append((c, st - 1, tk + c * QBLOCK, tk, False))
            tiles.append((c, st, c * QBLOCK, (c + 1) * QBLOCK, True))

        def qk(c, j, delta, rows, diagonal, buf):
            lanes = slice(c * QBLOCK, (c + 1) * QBLOCK)
            for r in range(REP):
                qa_ref[buf, pl.ds(row0, HEAD_DIM), r * QBLOCK:(r + 1) * QBLOCK] = (
                    q_ref[0, r * HEAD_DIM:(r + 1) * HEAD_DIM, lanes])
            sb = sel_ref[0, pl.ds(pl.multiple_of(j * nb, nb), nb), lanes]
            for r in range(REP):
                qa_ref[buf, kd:kd + nb, r * QBLOCK:(r + 1) * QBLOCK] = sb
            k0 = pl.multiple_of(j * tk, tk)
            s = (jnp.dot(kaug_ref[pl.ds(k0, rows), :], qa_ref[buf], preferred_element_type=F32)
                 + strip_ref[dmax - delta:dmax - delta + rows, :])
            s_ref[buf, 0:rows, :] = s
            tmax_ref[buf] = jnp.max(s, axis=0, keepdims=True)

        def softmax_pv(c, j, rows, diagonal, buf):
            k0 = pl.multiple_of(j * tk, tk)
            m_prev = m_ref[c]
            m_new = jnp.maximum(m_prev, tmax_ref[buf])
            alpha = jnp.exp2(m_prev - m_new)
            p = jnp.exp2(s_ref[buf, 0:rows, :] - m_new)
            acc = alpha * acc_ref[c] + jnp.dot(vt_ref[0, :, pl.ds(k0, rows)], p.astype(BF16),
                                               preferred_element_type=F32)
            acc_ref[c] = acc
            m_ref[c] = m_new
            if diagonal:
                out = acc[0:HEAD_DIM] * (1.0 / jnp.maximum(acc[HEAD_DIM:HEAD_DIM + 1], 1e-30))
                for r in range(REP):
                    o_ref[0, r * HEAD_DIM:(r + 1) * HEAD_DIM, c * QBLOCK:(c + 1) * QBLOCK] = (
                        out[:, r * QBLOCK:(r + 1) * QBLOCK].astype(o_ref.dtype))

        qk(*tiles[0], 0)
        for n, (c, j, delta, rows, diagonal) in enumerate(tiles):
            if n + 1 < len(tiles):
                qk(*tiles[n + 1], (n + 1) % 2)
            softmax_pv(c, j, rows, diagonal, n % 2)

    @pl.when(st > 1)
    def _():
        far_tiles(st - 1)

    @pl.when(st == 0)
    def _():
        near_tiles(False)

    @pl.when(st > 0)
    def _():
        near_tiles(True)


def _selected(thr, tab, q, sel, kaug, vt, cst, s, tk):
    nsel = s // SEL_BLOCK
    sl = 3 * tk - QBLOCK
    cpt = tk // QBLOCK
    smem = pl.BlockSpec(memory_space=pltpu.SMEM)
    return pl.pallas_call(
        functools.partial(_sel_kernel, tk=tk),
        grid=(GROUPS, s // tk),
        in_specs=[smem, smem,
                  pl.BlockSpec((1, REP * HEAD_DIM, tk), lambda g, c: (g, 0, c)),
                  pl.BlockSpec((1, nsel, tk), lambda g, c: (g, 0, c)),
                  pl.BlockSpec(kaug.shape, lambda g, c: (0, 0)),
                  pl.BlockSpec((1, vt.shape[1], s), lambda g, c: (g, 0, 0)),
                  pl.BlockSpec((1, 16, HQ), lambda g, c: (g, 0, 0))],
        out_specs=pl.BlockSpec((1, REP * HEAD_DIM, tk), lambda g, c: (g, 0, c)),
        out_shape=jax.ShapeDtypeStruct((GROUPS, REP * HEAD_DIM, s), BF16),
        scratch_shapes=[pltpu.VMEM((2, kaug.shape[1], HQ), BF16), pltpu.VMEM((2, tk, HQ), F32),
                        pltpu.VMEM((2, 1, HQ), F32),
                        pltpu.VMEM((cpt, 1, HQ), F32), pltpu.VMEM((cpt, vt.shape[1], HQ), F32),
                        pltpu.VMEM((sl, HQ), F32)],
        compiler_params=_params("arbitrary", "arbitrary"),
        name="selected_attn",
    )(thr, tab, q, sel, kaug, vt, cst)


def _band_kernel(thr_ref, tab_ref, q_ref, k_ref, vt_ref, sink_ref, o_ref, qa_ref, strip_ref, s_ref, *,
                 window, use_sink, cb):
    g = pl.program_id(0)

    @pl.when(pl.program_id(1) == 0)
    def _():
        _fill_bias_strip(strip_ref, thr_ref, tab_ref, g, dist0=window, lo=0, hi=window, minus_far=False)

    base = pl.multiple_of(pl.program_id(1) * (QBLOCK * cb), QBLOCK * cb)
    nk = window + QBLOCK
    qa_ref[...] = jnp.zeros(qa_ref.shape, BF16)
    row0 = pl.multiple_of(g * HEAD_DIM, HEAD_DIM)
    for c in range(cb):
        for r in range(REP):
            qa_ref[c, pl.ds(row0, HEAD_DIM), r * QBLOCK:(r + 1) * QBLOCK] = (
                q_ref[0, r * HEAD_DIM:(r + 1) * HEAD_DIM, c * QBLOCK:(c + 1) * QBLOCK])

    def scores(c, first_step):
        t0 = base + c * QBLOCK
        s = jnp.dot(k_ref[pl.ds(t0, nk), :], qa_ref[c], preferred_element_type=F32) + strip_ref[...]
        if first_step and c * QBLOCK < window:
            row = lax.broadcasted_iota(jnp.int32, (nk, HQ), 0)
            s = jnp.where(row >= window - c * QBLOCK, s, -MASK_BIG)
        s_ref[c] = s
        m = jnp.max(s, axis=0, keepdims=True)
        return jnp.maximum(m, sink_ref[0]) if use_sink else m

    def attend(c, m):
        t0 = base + c * QBLOCK
        e = jnp.exp2(s_ref[c] - m)
        o_t = jnp.dot(vt_ref[0, :, pl.ds(t0, nk)], e.astype(BF16), preferred_element_type=F32)
        den = o_t[HEAD_DIM:HEAD_DIM + 1]
        den = den + jnp.exp2(sink_ref[0] - m) if use_sink else jnp.maximum(den, 1e-30)
        o_t = o_t[0:HEAD_DIM] * (1.0 / den)
        for r in range(REP):
            o_ref[0, r * HEAD_DIM:(r + 1) * HEAD_DIM, c * QBLOCK:(c + 1) * QBLOCK] = (
                o_t[:, r * QBLOCK:(r + 1) * QBLOCK].astype(o_ref.dtype))

    def chunks(first_step):
        m = scores(0, first_step)
        for c in range(cb):
            m_next = scores(c + 1, first_step) if c + 1 < cb else None
            attend(c, m)
            m = m_next

    @pl.when(pl.program_id(1) == 0)
    def _():
        chunks(True)

    @pl.when(pl.program_id(1) > 0)
    def _():
        chunks(False)


def _banded(thr, tab, q, kpad, vtpad, sink, s, window, use_sink, name, cb=BAND_CHUNKS):
    assert QBLOCK * cb >= window
    nk = window + QBLOCK
    smem = pl.BlockSpec(memory_space=pltpu.SMEM)
    return pl.pallas_call(
        functools.partial(_band_kernel, window=window, use_sink=use_sink, cb=cb),
        grid=(GROUPS, s // (QBLOCK * cb)),
        in_specs=[smem, smem,
                  pl.BlockSpec((1, REP * HEAD_DIM, QBLOCK * cb), lambda g, c: (g, 0, c)),
                  pl.BlockSpec(kpad.shape, lambda g, c: (0, 0)),
                  pl.BlockSpec((1, vtpad.shape[1], s + window), lambda g, c: (g, 0, 0)),
                  pl.BlockSpec((1, 1, HQ), lambda g, c: (g, 0, 0))],
        out_specs=pl.BlockSpec((1, REP * HEAD_DIM, QBLOCK * cb), lambda g, c: (g, 0, c)),
        out_shape=jax.ShapeDtypeStruct((GROUPS, REP * HEAD_DIM, s), BF16),
        scratch_shapes=[pltpu.VMEM((cb, GROUPS * HEAD_DIM, HQ), BF16), pltpu.VMEM((nk, HQ), F32),
                        pltpu.VMEM((cb, nk, HQ), F32)],
        compiler_params=_params("arbitrary", "arbitrary"),
        name=name,
    )(thr, tab, q, kpad, vtpad, sink)


def _merge_kernel(oc_ref, os_ref, ow_ref, ob_ref, gn_ref, gab_ref, x_ref, wa_ref, wb_ref, wo_ref, gp_ref, o_ref):
    d = x_ref.shape[1]
    gn = jax.nn.sigmoid(gn_ref[...])
    parts = []
    for h in range(HEADS):
        rows = slice(h * HEAD_DIM, (h + 1) * HEAD_DIM)
        parts.append(gn[h:h + 1] * oc_ref[rows, :].astype(F32)
                     + gn[HEADS + h:HEADS + h + 1] * os_ref[rows, :].astype(F32)
                     + gn[2 * HEADS + h:2 * HEADS + h + 1] * ow_ref[rows, :].astype(F32))
    oa_t = jnp.concatenate(parts, axis=0).astype(BF16)
    ya = lax.dot_general(oa_t, wa_ref[...], _TN, preferred_element_type=F32)
    yb = lax.dot_general(ob_ref[...], wb_ref[...], _TN, preferred_element_type=F32)
    y = (jax.nn.sigmoid(gab_ref[:, :d].astype(F32)) * ya
         + jax.nn.sigmoid(gab_ref[:, d:].astype(F32)) * yb)
    z = jnp.dot(y.astype(BF16), wo_ref[...], preferred_element_type=F32)
    o_ref[...] = x_ref[...] + _rms(z, gp_ref[...])


def _merge(oc, os_, ow, ob, gn, gab, x, wa, wb, wo, gpost, ts):
    s, d = x.shape
    hd = oc.shape[0]
    tok = lambda rows: pl.BlockSpec((rows, ts), lambda i: (0, i))
    full = lambda shp: pl.BlockSpec(shp, lambda i: (0, 0))
    return pl.pallas_call(
        _merge_kernel,
        grid=(s // ts,),
        in_specs=[tok(hd), tok(hd), tok(hd), tok(hd), tok(gn.shape[0]),
                  pl.BlockSpec((ts, 2 * d), lambda i: (i, 0)), pl.BlockSpec((ts, d), lambda i: (i, 0)),
                  full((hd, d)), full((hd, d)), full((d, d)), full((1, d))],
        out_specs=pl.BlockSpec((ts, d), lambda i: (i, 0)),
        out_shape=jax.ShapeDtypeStruct((s, d), F32),
        compiler_params=_params("arbitrary"),
        name="merge_out_proj",
    )(oc, os_, ow, ob, gn, gab, x, wa, wb, wo, gpost)


def _mlp_kernel(h_ref, gpre_ref, w1_ref, w2_ref, gpost_ref, o_ref):
    u = _rms(h_ref[...], gpre_ref[...]).astype(BF16)
    a = jnp.maximum(jnp.dot(u, w1_ref[...], preferred_element_type=F32), 0.0)
    f = jnp.dot((a * a).astype(BF16), w2_ref[...], preferred_element_type=F32)
    o_ref[...] = h_ref[...] + _rms(f, gpost_ref[...])


def _mlp(h, gpre, w1, w2, gpost, ts):
    s, d = h.shape
    dff = w1.shape[1]
    full = lambda shp: pl.BlockSpec(shp, lambda i: (0, 0))
    return pl.pallas_call(
        _mlp_kernel,
        grid=(s // ts,),
        in_specs=[pl.BlockSpec((ts, d), lambda i: (i, 0)), full((1, d)), full((d, dff)), full((dff, d)),
                  full((1, d))],
        out_specs=pl.BlockSpec((ts, d), lambda i: (i, 0)),
        out_shape=jax.ShapeDtypeStruct((s, d), F32),
        compiler_params=_params("arbitrary"),
        name="mlp_relu2",
    )(h, gpre, w1, w2, gpost)


def kernel(x, norm_mix_pre, norm_mix_post, norm_mlp_pre, norm_mlp_post, w_in,
           cmp_pos_k, cmp_w1_k, cmp_w2_k, cmp_pos_v, cmp_w1_v, cmp_w2_v,
           attn_sinks, rel_bias, w_up_nsa, w_up_swa, w_out, w_mlp_in, w_mlp_out):
    b, s, d = x.shape
    assert b == 1 and s % SEL_TILE == 0 and w_in.shape[0] == 1
    assert SEL_TILE >= MAX_DISTANCE
    qd = HEADS * HEAD_DIM
    kvd = GROUPS * HEAD_DIM
    sizes = (qd,) + (kvd,) * 6 + (3 * HEADS, qd, kvd, kvd, d, d)
    offs = [0]
    for z in sizes:
        offs.append(offs[-1] + z)
    w = w_in[0]
    col = lambda i: w[:, offs[i]:offs[i + 1]]
    (w_qn, w_kc, w_vc, w_ks, w_vs, w_kw, w_vw, w_gn, w_qs, w_k_s, w_v_s, w_ga, w_gb) = [col(i) for i in range(13)]
    scale = HEAD_DIM ** -0.5 * LOG2E
    w_gn = w_gn.reshape(d, HEADS, 3).transpose(0, 2, 1).reshape(d, 3 * HEADS)
    w_gn = jnp.pad(w_gn, ((0, 0), (0, 8)))
    wn = jnp.concatenate([w_kc, w_vc, w_ks, w_kw, w_k_s], axis=1).astype(BF16)
    wg = jnp.concatenate([w_ga, w_gb], axis=1).astype(BF16)
    wt = jnp.concatenate([w_qn * scale, w_qs * scale, w_vs, w_vw, w_v_s, w_gn], axis=1).T.astype(BF16)
    nqv = 2 * qd + 3 * kvd

    x2 = x[0]
    kv, gab, qv, gn, kb = _project(x2, norm_mix_pre, wn, wg, wt, nqv, ts=PROJ_TOKENS)
    q_nsa = qv[0:qd].reshape(GROUPS, REP * HEAD_DIM, s)
    q_swa = qv[qd:2 * qd].reshape(GROUPS, REP * HEAD_DIM, s)
    vs_t = qv[2 * qd:2 * qd + kvd].reshape(GROUPS, HEAD_DIM, s)
    vw_t = qv[2 * qd + kvd:2 * qd + 2 * kvd].reshape(GROUPS, HEAD_DIM, s)
    vswa_t = qv[2 * qd + 2 * kvd:].reshape(GROUPS, HEAD_DIM, s)

    n16 = s // CMP_STRIDE
    pos8 = jnp.stack([cmp_pos_k[0], cmp_pos_v[0]]).reshape(2, 1, CMP_BLOCK * HEAD_DIM)
    pos8 = jnp.broadcast_to(pos8, (2, 8, CMP_BLOCK * HEAD_DIM)).astype(BF16)
    w1 = jnp.stack([cmp_w1_k[0], cmp_w1_v[0]]).astype(BF16)
    w2 = jnp.stack([cmp_w2_k[0], cmp_w2_v[0]]).astype(BF16)
    cmp_n, cmp_t = _compress(kb, pos8, w1, w2, w2.transpose(0, 2, 1))

    oc, sel = _cmp_select(q_nsa, cmp_n, cmp_t, s)

    bias_nsa = rel_bias[:, :HEADS].reshape(NUM_BUCKETS, GROUPS, REP) * LOG2E
    bias_swa = rel_bias[:, HEADS:].reshape(NUM_BUCKETS, GROUPS, REP) * LOG2E
    thr = jnp.asarray(_bucket_starts())
    tab_nsa = bias_nsa.transpose(1, 2, 0).reshape(-1)
    tab_swa = bias_swa.transpose(1, 2, 0).reshape(-1)

    tk = SEL_TILE
    nb = tk // SEL_BLOCK
    posn = jnp.arange(s)
    onehot = ((posn // SEL_BLOCK) % nb)[:, None] == jnp.arange(nb)[None, :]
    aug = jnp.concatenate([onehot.astype(BF16), jnp.ones((s, 2), BF16),
                           jnp.zeros((s, kvd - nb - 2), BF16)], axis=1)
    kaug = jnp.concatenate([kv[:, 0:kvd], aug], axis=1)
    far = bias_nsa[NUM_BUCKETS - 1]
    far_hi = far.astype(BF16)
    far_lo = (far - far_hi.astype(F32)).astype(BF16)
    cst = jnp.stack([far_hi, far_lo], axis=1)
    cst = jnp.broadcast_to(cst[:, :, :, None], (GROUPS, 2, REP, QBLOCK)).reshape(GROUPS, 2, HQ)
    cst = jnp.pad(cst, ((0, 0), (0, 14), (0, 0)))
    ones_row = jnp.concatenate([jnp.ones((GROUPS, 1, s), BF16), jnp.zeros((GROUPS, 15, s), BF16)], axis=1)
    vs_aug = jnp.concatenate([vs_t, ones_row], axis=1)
    o_sel = _selected(thr, tab_nsa, q_nsa, sel, kaug, vs_aug, cst, s, tk)

    pad_rows = lambda a, wdw: jnp.pad(a, ((wdw, 0), (0, 0)))
    pad_lanes = lambda a, wdw: jnp.pad(jnp.concatenate([a, ones_row], axis=1), ((0, 0), (0, 0), (wdw, 0)))
    no_sink = jnp.zeros((GROUPS, 1, HQ), F32)
    o_win = _banded(thr, tab_nsa, q_nsa, pad_rows(kv[:, kvd:2 * kvd], NSA_WINDOW), pad_lanes(vw_t, NSA_WINDOW),
                    no_sink, s, NSA_WINDOW, False, "nsa_window")
    sink = attn_sinks[0].reshape(GROUPS, 1, REP, 1).astype(F32) * LOG2E
    sink = jnp.broadcast_to(sink, (GROUPS, 1, REP, QBLOCK)).reshape(GROUPS, 1, HQ)
    o_swa = _banded(thr, tab_swa, q_swa, pad_rows(kv[:, 2 * kvd:3 * kvd], SWA_WINDOW), pad_lanes(vswa_t, SWA_WINDOW),
                    sink, s, SWA_WINDOW, True, "swa_sink")

    flat = lambda a: a.reshape(qd, s)
    h1 = _merge(flat(oc), flat(o_sel), flat(o_win), flat(o_swa), gn, gab, x2,
                w_up_nsa[0].astype(BF16), w_up_swa[0].astype(BF16), w_out[0].astype(BF16),
                norm_mix_post, ts=PROJ_TOKENS)
    out = _mlp(h1, norm_mlp_pre, w_mlp_in[0].astype(BF16), w_mlp_out[0].astype(BF16), norm_mlp_post,
               ts=MLP_TOKENS)
    return out[None]
```

```python
import functools
import math

import jax
import jax.numpy as jnp
import numpy as np
from jax import lax
from jax.experimental import pallas as pl
from jax.experimental.pallas import tpu as pltpu

F32 = jnp.float32
BF16 = jnp.bfloat16

HEAD_DIM = 64
GROUPS = 2
REP = 4
HEADS = GROUPS * REP
CMP_BLOCK = 32
CMP_STRIDE = 16
SEL_BLOCK = 64
SEL_TOPK = 16
NSA_WINDOW = 512
SWA_WINDOW = 128
QBLOCK = 128
NUM_BUCKETS = 32
MAX_DISTANCE = 1024
RMS_EPS = 1e-6
LOG2E = math.log2(math.e)
MASK_BIG = 2.0 ** 99
HQ = REP * QBLOCK

SEL_TILE = 1024
TILES_PER_TRIP = 8
PROJ_TOKENS = 1024
MLP_TOKENS = 512
BAND_CHUNKS = 16
CMP_CHUNKS = 4
CMP_VARIANT_KEYS = 128
V7X_VMEM_BYTES = 64 * 1024 * 1024
VMEM_LIMIT = V7X_VMEM_BYTES * 7 // 8

_NT = (((1,), (1,)), ((), ()))
_TN = (((0,), (0,)), ((), ()))


def _params(*sem):
    return pltpu.CompilerParams(dimension_semantics=sem, vmem_limit_bytes=VMEM_LIMIT)


def _bucket_starts():
    max_exact = NUM_BUCKETS // 2
    d = np.arange(2 * MAX_DISTANCE)
    large = max_exact + (np.log(np.maximum(d, 1).astype(np.float64) / max_exact)
                         / math.log(MAX_DISTANCE / max_exact) * (NUM_BUCKETS - max_exact)).astype(np.int64)
    bucket = np.where(d < max_exact, d, np.minimum(large, NUM_BUCKETS - 1))
    return (bucket[None, :] < np.arange(NUM_BUCKETS)[:, None]).sum(axis=1).astype(np.int32)


def _rms(x, gain):
    return x * lax.rsqrt(jnp.mean(x * x, axis=-1, keepdims=True) + RMS_EPS) * gain


def _proj_kernel(x_ref, g_ref, wn_ref, wg_ref, wt_ref, kv_ref, gab_ref, qv_ref, gn_ref, kb_ref, y_ref, *, nqv):
    u = _rms(x_ref[...], g_ref[...]).astype(BF16)
    kvf = jnp.dot(u, wn_ref[...], preferred_element_type=F32)
    gw = GROUPS * HEAD_DIM
    kv_ref[...] = kvf[:, 2 * gw:].astype(BF16)
    nrow = y_ref.shape[1] // CMP_STRIDE
    for a in range(2):
        y_ref[a] = kvf[:, a * gw:(a + 1) * gw]
        both = [y_ref[a, pl.ds(p, nrow, stride=CMP_STRIDE), :] for p in range(CMP_STRIDE)]
        for gi in range(GROUPS):
            kb_ref[a, gi] = jnp.concatenate([t[:, gi * HEAD_DIM:(gi + 1) * HEAD_DIM] for t in both],
                                            axis=1).astype(BF16)
    gab_ref[...] = jnp.dot(u, wg_ref[...], preferred_element_type=F32).astype(BF16)
    t = lax.dot_general(wt_ref[...], u, _NT, preferred_element_type=F32)
    qv_ref[...] = t[:nqv].astype(BF16)
    gn_ref[...] = t[nqv:]


def _project(x, gain, wn, wg, wt, nqv, ts):
    s, d = x.shape
    nn, ng, ntr = wn.shape[1] - 2 * GROUPS * HEAD_DIM, wg.shape[1], wt.shape[0]
    full = lambda shp: pl.BlockSpec(shp, lambda i: (0, 0))
    return pl.pallas_call(
        functools.partial(_proj_kernel, nqv=nqv),
        grid=(s // ts,),
        in_specs=[pl.BlockSpec((ts, d), lambda i: (i, 0)), full((1, d)),
                  full(wn.shape), full((d, ng)), full((ntr, d))],
        out_specs=[pl.BlockSpec((ts, nn), lambda i: (i, 0)),
                   pl.BlockSpec((ts, ng), lambda i: (i, 0)),
                   pl.BlockSpec((nqv, ts), lambda i: (0, i)),
                   pl.BlockSpec((ntr - nqv, ts), lambda i: (0, i)),
                   pl.BlockSpec((2, GROUPS, ts // CMP_STRIDE, CMP_STRIDE * HEAD_DIM), lambda i: (0, 0, i, 0))],
        out_shape=[jax.ShapeDtypeStruct((s, nn), BF16), jax.ShapeDtypeStruct((s, ng), BF16),
                   jax.ShapeDtypeStruct((nqv, s), BF16), jax.ShapeDtypeStruct((ntr - nqv, s), F32),
                   jax.ShapeDtypeStruct((2, GROUPS, s // CMP_STRIDE, CMP_STRIDE * HEAD_DIM), BF16)],
        scratch_shapes=[pltpu.VMEM((2, ts, GROUPS * HEAD_DIM), F32)],
        compiler_params=_params("arbitrary"),
        name="in_proj",
    )(x, gain, wn, wg, wt)


def _gelu_tanh(x):
    return x * (0.5 * (1.0 + jnp.tanh(math.sqrt(2.0 / math.pi) * (x + 0.044715 * (x * x * x)))))


def _compress_kernel(kb_ref, pos_ref, w1_ref, w2_ref, w2t_ref, o_ref, ot_ref):
    kb = kb_ref[0, 0]
    half = kb.shape[1]
    n16 = kb.shape[0]
    first = jnp.dot(kb, w1_ref[0, :half, :], preferred_element_type=F32)
    second = jnp.dot(kb, w1_ref[0, half:, :], preferred_element_type=F32)
    posb = jnp.dot(pos_ref[0], w1_ref[0], preferred_element_type=F32)[0:1]
    pre = first + pltpu.roll(second, n16 - 1, 0) + posb
    h = _gelu_tanh(pre).astype(BF16)
    o_ref[0, 0] = jnp.dot(h, w2_ref[0], preferred_element_type=F32).astype(BF16)
    ot_ref[0, 0] = lax.dot_general(w2t_ref[0], h, _NT, preferred_element_type=F32).astype(BF16)


def _compress(kb, pos8, w1, w2, w2t):
    _, g, n16, cin2 = kb.shape
    hid = w1.shape[2]
    return pl.pallas_call(
        _compress_kernel,
        grid=(2, g),
        in_specs=[pl.BlockSpec((1, 1, n16, cin2), lambda a, b: (a, b, 0, 0)),
                  pl.BlockSpec((1, 8, 2 * cin2), lambda a, b: (a, 0, 0)),
                  pl.BlockSpec((1, 2 * cin2, hid), lambda a, b: (a, 0, 0)),
                  pl.BlockSpec((1, hid, HEAD_DIM), lambda a, b: (a, 0, 0)),
                  pl.BlockSpec((1, HEAD_DIM, hid), lambda a, b: (a, 0, 0))],
        out_specs=[pl.BlockSpec((1, 1, n16, HEAD_DIM), lambda a, b: (a, b, 0, 0)),
                   pl.BlockSpec((1, 1, HEAD_DIM, n16), lambda a, b: (a, b, 0, 0))],
        out_shape=[jax.ShapeDtypeStruct((2, g, n16, HEAD_DIM), BF16),
                   jax.ShapeDtypeStruct((2, g, HEAD_DIM, n16), BF16)],
        compiler_params=_params("arbitrary", "arbitrary"),
        name="kv_compress",
    )(kb, pos8, w1, w2, w2t)


def _cmpsel_body(q_ref, kc_ref, vct_ref, oc_ref, sel_ref, pg_ref, s_ref, t_base, *, cb, rows, nomask, nsel, topk):
    mrows = rows - nomask
    ratio = SEL_BLOCK // CMP_STRIDE
    nblk = rows // ratio
    n_io = nomask + lax.broadcasted_iota(jnp.int32, (mrows, HQ), 0)
    lane_t = lax.broadcasted_iota(jnp.int32, (mrows, HQ), 1) & (QBLOCK - 1)
    blk = lax.broadcasted_iota(jnp.int32, (nblk, QBLOCK), 0)

    def raw_scores(c):
        t0 = t_base + c * QBLOCK
        lanes = slice(c * QBLOCK, (c + 1) * QBLOCK)
        q4 = jnp.concatenate([q_ref[0, r * HEAD_DIM:(r + 1) * HEAD_DIM, lanes] for r in range(REP)], axis=1)
        s = jnp.dot(kc_ref[0, 0, 0:rows, :], q4, preferred_element_type=F32)
        edge = jnp.where(n_io * CMP_STRIDE + (CMP_BLOCK - 1) <= t0 + lane_t, s[nomask:], -jnp.inf)
        s = jnp.concatenate([s[:nomask], edge], axis=0) if nomask else edge
        s_ref[c, 0:rows, :] = s
        m = jnp.max(s, axis=0, keepdims=True)
        return jnp.where(m == -jnp.inf, 0.0, m)

    def attend(c, m):
        t0 = t_base + c * QBLOCK
        lanes = slice(c * QBLOCK, (c + 1) * QBLOCK)
        e = jnp.exp2(s_ref[c, 0:rows, :] - m)
        den = jnp.maximum(jnp.sum(e, axis=0, keepdims=True), 1e-30)
        p = e * (1.0 / den)
        o_t = jnp.dot(vct_ref[0, 0, :, 0:rows], p.astype(BF16), preferred_element_type=F32)
        for r in range(REP):
            oc_ref[0, r * HEAD_DIM:(r + 1) * HEAD_DIM, lanes] = (
                o_t[:, r * QBLOCK:(r + 1) * QBLOCK].astype(oc_ref.dtype))

        pg = ((p[:, 0:QBLOCK] + p[:, QBLOCK:2 * QBLOCK]) + p[:, 2 * QBLOCK:3 * QBLOCK]) + p[:, 3 * QBLOCK:]
        pg_ref[c, 0:8, :] = jnp.zeros((8, QBLOCK), F32)
        pg_ref[c, 8:8 + rows, :] = pg
        ps = pg_ref[c, pl.ds(7, nblk, stride=ratio), :]
        for o in range(1, ratio + 1):
            ps = ps + pg_ref[c, pl.ds(7 + o, nblk, stride=ratio), :]
        cur = (t0 + lax.broadcasted_iota(jnp.int32, (nblk, QBLOCK), 1)) // SEL_BLOCK
        valid = blk <= cur
        forced = (blk == 0) | (blk == cur) | (blk == cur - 1)
        return jnp.where(valid & ~forced, ps, -jnp.inf), valid

    scores, valids = [], []
    m = raw_scores(0)
    for c in range(cb):
        m_next = raw_scores(c + 1) if c + 1 < cb else None
        score, valid = attend(c, m)
        scores.append(score)
        valids.append(valid)
        m = m_next

    def pick_one(_, rems):
        out = []
        for rem in rems:
            top = jnp.max(rem, axis=0, keepdims=True)
            first = jnp.min(jnp.where(rem == top, blk, nblk), axis=0, keepdims=True)
            out.append(jnp.where(blk == first, -jnp.inf, rem))
        return tuple(out)

    rems = lax.fori_loop(0, max(topk - 3, 0), pick_one, tuple(scores))
    for c in range(cb):
        lanes = slice(c * QBLOCK, (c + 1) * QBLOCK)
        sel_ref[0, 0:nblk, lanes] = jnp.where((rems[c] == -jnp.inf) & valids[c], 0.0, -MASK_BIG).astype(BF16)
        if nblk < nsel:
            sel_ref[0, nblk:, lanes] = jnp.full((nsel - nblk, QBLOCK), -MASK_BIG, BF16)


def _cmpsel_kernel(q_ref, kc_ref, vct_ref, oc_ref, sel_ref, pg_ref, s_ref, *, n16, nsel, topk, nvar, cb):
    step = pl.program_id(1)
    per = n16 // nvar
    spv = per // (QBLOCK // CMP_STRIDE) // cb
    for k in range(1, nvar + 1):
        @pl.when((step >= (k - 1) * spv) & (step < k * spv))
        def _(k=k):
            _cmpsel_body(q_ref, kc_ref, vct_ref, oc_ref, sel_ref, pg_ref, s_ref, step * (cb * QBLOCK), cb=cb,
                         rows=per * k, nomask=max(per * (k - 1) - 8, 0), nsel=nsel, topk=topk)


def _cmp_select(q, kc_all, vct_all, s, cb=CMP_CHUNKS):
    n16 = s // CMP_STRIDE
    nsel = s // SEL_BLOCK
    topk = min(SEL_TOPK, nsel)
    nvar = max(n16 // CMP_VARIANT_KEYS, 1)
    return pl.pallas_call(
        functools.partial(_cmpsel_kernel, n16=n16, nsel=nsel, topk=topk, nvar=nvar, cb=cb),
        grid=(GROUPS, s // (QBLOCK * cb)),
        in_specs=[pl.BlockSpec((1, REP * HEAD_DIM, QBLOCK * cb), lambda g, c: (g, 0, c)),
                  pl.BlockSpec((1, 1, n16, HEAD_DIM), lambda g, c: (0, g, 0, 0)),
                  pl.BlockSpec((1, 1, HEAD_DIM, n16), lambda g, c: (1, g, 0, 0))],
        out_specs=[pl.BlockSpec((1, REP * HEAD_DIM, QBLOCK * cb), lambda g, c: (g, 0, c)),
                   pl.BlockSpec((1, nsel, QBLOCK * cb), lambda g, c: (g, 0, c))],
        out_shape=[jax.ShapeDtypeStruct((GROUPS, REP * HEAD_DIM, s), BF16),
                   jax.ShapeDtypeStruct((GROUPS, nsel, s), BF16)],
        scratch_shapes=[pltpu.VMEM((cb, n16 + 8, QBLOCK), F32), pltpu.VMEM((cb, n16, HQ), F32)],
        compiler_params=_params("arbitrary", "arbitrary"),
        name="cmp_select",
    )(q, kc_all, vct_all)


def _fill_bias_strip(strip_ref, thr_ref, tab_ref, g, *, dist0, lo, hi, minus_far):
    nrows = strip_ref.shape[0]

    def block(bi, carry):
        i0 = pl.multiple_of(bi * QBLOCK, QBLOCK)
        d = (lax.broadcasted_iota(jnp.int32, (QBLOCK, QBLOCK), 1)
             - lax.broadcasted_iota(jnp.int32, (QBLOCK, QBLOCK), 0) + (dist0 - i0))
        ok = d >= lo if hi is None else (d >= lo) & (d < hi)
        for r in range(REP):
            base = (g * REP + r) * NUM_BUCKETS
            out = jnp.full((QBLOCK, QBLOCK), tab_ref[base], F32)
            for b in range(1, NUM_BUCKETS):
                out = jnp.where(d >= thr_ref[b], tab_ref[base + b], out)
            if minus_far:
                out = out - tab_ref[base + NUM_BUCKETS - 1]
            strip_ref[pl.ds(i0, QBLOCK), r * QBLOCK:(r + 1) * QBLOCK] = jnp.where(ok, out, -MASK_BIG)
        return carry

    lax.fori_loop(0, nrows // QBLOCK, block, 0)


def _sel_kernel(thr_ref, tab_ref, q_ref, sel_ref, kaug_ref, vt_ref, cst_ref, o_ref,
                qa_ref, s_ref, tmax_ref, m_ref, acc_ref, strip_ref, *, tk):
    g = pl.program_id(0)
    st = pl.program_id(1)
    nb = tk // SEL_BLOCK
    cpt = tk // QBLOCK
    kd = GROUPS * HEAD_DIM
    dmax = 2 * tk - QBLOCK

    @pl.when(st == 0)
    def _():
        _fill_bias_strip(strip_ref, thr_ref, tab_ref, g, dist0=dmax, lo=0, hi=None, minus_far=True)

    qa_ref[...] = jnp.zeros(qa_ref.shape, BF16)
    for b in range(2):
        qa_ref[b, kd + nb:kd + nb + 16, :] = cst_ref[0]
    m_ref[...] = jnp.full(m_ref.shape, -jnp.inf, F32)
    acc_ref[...] = jnp.zeros(acc_ref.shape, F32)
    row0 = pl.multiple_of(g * HEAD_DIM, HEAD_DIM)

    def far_tiles(count):

        def advance(c, j):
            wrap = j + 1 == count
            return jnp.where(wrap, c + 1, c), jnp.where(wrap, 0, j + 1)

        def qk(c, j, buf):
            cc = jnp.minimum(c, cpt - 1)
            lane0 = pl.multiple_of(cc * QBLOCK, QBLOCK)
            for r in range(REP):
                qa_ref[buf, pl.ds(row0, HEAD_DIM), r * QBLOCK:(r + 1) * QBLOCK] = (
                    q_ref[0, r * HEAD_DIM:(r + 1) * HEAD_DIM, pl.ds(lane0, QBLOCK)])
            sb = sel_ref[0, pl.ds(pl.multiple_of(j * nb, nb), nb), pl.ds(lane0, QBLOCK)]
            for r in range(REP):
                qa_ref[buf, kd:kd + nb, r * QBLOCK:(r + 1) * QBLOCK] = sb
            k0 = pl.multiple_of(j * tk, tk)
            s = jnp.dot(kaug_ref[pl.ds(k0, tk), :], qa_ref[buf], preferred_element_type=F32)
            s_ref[buf] = s
            tmax_ref[buf] = jnp.max(s, axis=0, keepdims=True)

        def softmax_pv(c, j, buf):
            k0 = pl.multiple_of(j * tk, tk)
            m_prev = m_ref[c]
            m_new = jnp.maximum(m_prev, tmax_ref[buf])
            alpha = jnp.exp2(m_prev - m_new)
            p = jnp.exp2(s_ref[buf] - m_new)
            acc_ref[c] = alpha * acc_ref[c] + jnp.dot(vt_ref[0, :, pl.ds(k0, tk)], p.astype(BF16),
                                                      preferred_element_type=F32)
            m_ref[c] = m_new

        zero = jnp.int32(0)
        qk(zero, zero, 0)

        def trip(_, carry):
            cur = carry
            for k in range(TILES_PER_TRIP):
                nxt = advance(*cur)
                qk(*nxt, (k + 1) % 2)
                softmax_pv(*cur, k % 2)
                cur = nxt
            return cur

        lax.fori_loop(0, (cpt * count) // TILES_PER_TRIP, trip, (zero, zero))

    def near_tiles(with_previous):
        tiles = []
        for c in range(cpt):
            if with_previous:
                tiles.append((c, st - 1, tk + c * QBLOCK, tk, False))
            tiles.append((c, st, c * QBLOCK, (c + 1) * QBLOCK, True))

        def qk(c, j, delta, rows, diagonal, buf):
            lanes = slice(c * QBLOCK, (c + 1) * QBLOCK)
            for r in range(REP):
                qa_ref[buf, pl.ds(row0, HEAD_DIM), r * QBLOCK:(r + 1) * QBLOCK] = (
                    q_ref[0, r * HEAD_DIM:(r + 1) * HEAD_DIM, lanes])
            sb = sel_ref[0, pl.ds(pl.multiple_of(j * nb, nb), nb), lanes]
            for r in range(REP):
                qa_ref[buf, kd:kd + nb, r * QBLOCK:(r + 1) * QBLOCK] = sb
            k0 = pl.multiple_of(j * tk, tk)
            s = (jnp.dot(kaug_ref[pl.ds(k0, rows), :], qa_ref[buf], preferred_element_type=F32)
                 + strip_ref[dmax - delta:dmax - delta + rows, :])
            s_ref[buf, 0:rows, :] = s
            tmax_ref[buf] = jnp.max(s, axis=0, keepdims=True)

        def softmax_pv(c, j, rows, diagonal, buf):
            k0 = pl.multiple_of(j * tk, tk)
            m_prev = m_ref[c]
            m_new = jnp.maximum(m_prev, tmax_ref[buf])
            alpha = jnp.exp2(m_prev - m_new)
            p = jnp.exp2(s_ref[buf, 0:rows, :] - m_new)
            acc = alpha * acc_ref[c] + jnp.dot(vt_ref[0, :, pl.ds(k0, rows)], p.astype(BF16),
                                               preferred_element_type=F32)
            acc_ref[c] = acc
            m_ref[c] = m_new
            if diagonal:
                out = acc[0:HEAD_DIM] * (1.0 / jnp.maximum(acc[HEAD_DIM:HEAD_DIM + 1], 1e-30))
                for r in range(REP):
                    o_ref[0, r * HEAD_DIM:(r + 1) * HEAD_DIM, c * QBLOCK:(c + 1) * QBLOCK] = (
                        out[:, r * QBLOCK:(r + 1) * QBLOCK].astype(o_ref.dtype))

        qk(*tiles[0], 0)
        for n, (c, j, delta, rows, diagonal) in enumerate(tiles):
            if n + 1 < len(tiles):
                qk(*tiles[n + 1], (n + 1) % 2)
            softmax_pv(c, j, rows, diagonal, n % 2)

    @pl.when(st > 1)
    def _():
        far_tiles(st - 1)

    @pl.when(st == 0)
    def _():
        near_tiles(False)

    @pl.when(st > 0)
    def _():
        near_tiles(True)


def _selected(thr, tab, q, sel, kaug, vt, cst, s, tk):
    nsel = s // SEL_BLOCK
    sl = 3 * tk - QBLOCK
    cpt = tk // QBLOCK
    smem = pl.BlockSpec(memory_space=pltpu.SMEM)
    return pl.pallas_call(
        functools.partial(_sel_kernel, tk=tk),
        grid=(GROUPS, s // tk),
        in_specs=[smem, smem,
                  pl.BlockSpec((1, REP * HEAD_DIM, tk), lambda g, c: (g, 0, c)),
                  pl.BlockSpec((1, nsel, tk), lambda g, c: (g, 0, c)),
                  pl.BlockSpec(kaug.shape, lambda g, c: (0, 0)),
                  pl.BlockSpec((1, vt.shape[1], s), lambda g, c: (g, 0, 0)),
                  pl.BlockSpec((1, 16, HQ), lambda g, c: (g, 0, 0))],
        out_specs=pl.BlockSpec((1, REP * HEAD_DIM, tk), lambda g, c: (g, 0, c)),
        out_shape=jax.ShapeDtypeStruct((GROUPS, REP * HEAD_DIM, s), BF16),
        scratch_shapes=[pltpu.VMEM((2, kaug.shape[1], HQ), BF16), pltpu.VMEM((2, tk, HQ), F32),
                        pltpu.VMEM((2, 1, HQ), F32),
                        pltpu.VMEM((cpt, 1, HQ), F32), pltpu.VMEM((cpt, vt.shape[1], HQ), F32),
                        pltpu.VMEM((sl, HQ), F32)],
        compiler_params=_params("arbitrary", "arbitrary"),
        name="selected_attn",
    )(thr, tab, q, sel, kaug, vt, cst)


def _band_kernel(thr_ref, tab_ref, q_ref, k_ref, vt_ref, sink_ref, o_ref, qa_ref, strip_ref, s_ref, *,
                 window, use_sink, cb):
    g = pl.program_id(0)

    @pl.when(pl.program_id(1) == 0)
    def _():
        _fill_bias_strip(strip_ref, thr_ref, tab_ref, g, dist0=window, lo=0, hi=window, minus_far=False)

    base = pl.multiple_of(pl.program_id(1) * (QBLOCK * cb), QBLOCK * cb)
    nk = window + QBLOCK
    qa_ref[...] = jnp.zeros(qa_ref.shape, BF16)
    row0 = pl.multiple_of(g * HEAD_DIM, HEAD_DIM)
    for c in range(cb):
        for r in range(REP):
            qa_ref[c, pl.ds(row0, HEAD_DIM), r * QBLOCK:(r + 1) * QBLOCK] = (
                q_ref[0, r * HEAD_DIM:(r + 1) * HEAD_DIM, c * QBLOCK:(c + 1) * QBLOCK])

    def scores(c, first_step):
        t0 = base + c * QBLOCK
        s = jnp.dot(k_ref[pl.ds(t0, nk), :], qa_ref[c], preferred_element_type=F32) + strip_ref[...]
        if first_step and c * QBLOCK < window:
            row = lax.broadcasted_iota(jnp.int32, (nk, HQ), 0)
            s = jnp.where(row >= window - c * QBLOCK, s, -MASK_BIG)
        s_ref[c] = s
        m = jnp.max(s, axis=0, keepdims=True)
        return jnp.maximum(m, sink_ref[0]) if use_sink else m

    def attend(c, m):
        t0 = base + c * QBLOCK
        e = jnp.exp2(s_ref[c] - m)
        o_t = jnp.dot(vt_ref[0, :, pl.ds(t0, nk)], e.astype(BF16), preferred_element_type=F32)
        den = o_t[HEAD_DIM:HEAD_DIM + 1]
        den = den + jnp.exp2(sink_ref[0] - m) if use_sink else jnp.maximum(den, 1e-30)
        o_t = o_t[0:HEAD_DIM] * (1.0 / den)
        for r in range(REP):
            o_ref[0, r * HEAD_DIM:(r + 1) * HEAD_DIM, c * QBLOCK:(c + 1) * QBLOCK] = (
                o_t[:, r * QBLOCK:(r + 1) * QBLOCK].astype(o_ref.dtype))

    def chunks(first_step):
        m = scores(0, first_step)
        for c in range(cb):
            m_next = scores(c + 1, first_step) if c + 1 < cb else None
            attend(c, m)
            m = m_next

    @pl.when(pl.program_id(1) == 0)
    def _():
        chunks(True)

    @pl.when(pl.program_id(1) > 0)
    def _():
        chunks(False)


def _banded(thr, tab, q, kpad, vtpad, sink, s, window, use_sink, name, cb=BAND_CHUNKS):
    assert QBLOCK * cb >= window
    nk = window + QBLOCK
    smem = pl.BlockSpec(memory_space=pltpu.SMEM)
    return pl.pallas_call(
        functools.partial(_band_kernel, window=window, use_sink=use_sink, cb=cb),
        grid=(GROUPS, s // (QBLOCK * cb)),
        in_specs=[smem, smem,
                  pl.BlockSpec((1, REP * HEAD_DIM, QBLOCK * cb), lambda g, c: (g, 0, c)),
                  pl.BlockSpec(kpad.shape, lambda g, c: (0, 0)),
                  pl.BlockSpec((1, vtpad.shape[1], s + window), lambda g, c: (g, 0, 0)),
                  pl.BlockSpec((1, 1, HQ), lambda g, c: (g, 0, 0))],
        out_specs=pl.BlockSpec((1, REP * HEAD_DIM, QBLOCK * cb), lambda g, c: (g, 0, c)),
        out_shape=jax.ShapeDtypeStruct((GROUPS, REP * HEAD_DIM, s), BF16),
        scratch_shapes=[pltpu.VMEM((cb, GROUPS * HEAD_DIM, HQ), BF16), pltpu.VMEM((nk, HQ), F32),
                        pltpu.VMEM((cb, nk, HQ), F32)],
        compiler_params=_params("arbitrary", "arbitrary"),
        name=name,
    )(thr, tab, q, kpad, vtpad, sink)


def _merge_kernel(oc_ref, os_ref, ow_ref, ob_ref, gn_ref, gab_ref, x_ref, wa_ref, wb_ref, wo_ref, gp_ref, o_ref):
    d = x_ref.shape[1]
    gn = jax.nn.sigmoid(gn_ref[...])
    parts = []
    for h in range(HEADS):
        rows = slice(h * HEAD_DIM, (h + 1) * HEAD_DIM)
        parts.append(gn[h:h + 1] * oc_ref[rows, :].astype(F32)
                     + gn[HEADS + h:HEADS + h + 1] * os_ref[rows, :].astype(F32)
                     + gn[2 * HEADS + h:2 * HEADS + h + 1] * ow_ref[rows, :].astype(F32))
    oa_t = jnp.concatenate(parts, axis=0).astype(BF16)
    ya = lax.dot_general(oa_t, wa_ref[...], _TN, preferred_element_type=F32)
    yb = lax.dot_general(ob_ref[...], wb_ref[...], _TN, preferred_element_type=F32)
    y = (jax.nn.sigmoid(gab_ref[:, :d].astype(F32)) * ya
         + jax.nn.sigmoid(gab_ref[:, d:].astype(F32)) * yb)
    z = jnp.dot(y.astype(BF16), wo_ref[...], preferred_element_type=F32)
    o_ref[...] = x_ref[...] + _rms(z, gp_ref[...])


def _merge(oc, os_, ow, ob, gn, gab, x, wa, wb, wo, gpost, ts):
    s, d = x.shape
    hd = oc.shape[0]
    tok = lambda rows: pl.BlockSpec((rows, ts), lambda i: (0, i))
    full = lambda shp: pl.BlockSpec(shp, lambda i: (0, 0))
    return pl.pallas_call(
        _merge_kernel,
        grid=(s // ts,),
        in_specs=[tok(hd), tok(hd), tok(hd), tok(hd), tok(gn.shape[0]),
                  pl.BlockSpec((ts, 2 * d), lambda i: (i, 0)), pl.BlockSpec((ts, d), lambda i: (i, 0)),
                  full((hd, d)), full((hd, d)), full((d, d)), full((1, d))],
        out_specs=pl.BlockSpec((ts, d), lambda i: (i, 0)),
        out_shape=jax.ShapeDtypeStruct((s, d), F32),
        compiler_params=_params("arbitrary"),
        name="merge_out_proj",
    )(oc, os_, ow, ob, gn, gab, x, wa, wb, wo, gpost)


def _mlp_kernel(h_ref, gpre_ref, w1_ref, w2_ref, gpost_ref, o_ref):
    u = _rms(h_ref[...], gpre_ref[...]).astype(BF16)
    a = jnp.maximum(jnp.dot(u, w1_ref[...], preferred_element_type=F32), 0.0)
    f = jnp.dot((a * a).astype(BF16), w2_ref[...], preferred_element_type=F32)
    o_ref[...] = h_ref[...] + _rms(f, gpost_ref[...])


def _mlp(h, gpre, w1, w2, gpost, ts):
    s, d = h.shape
    dff = w1.shape[1]
    full = lambda shp: pl.BlockSpec(shp, lambda i: (0, 0))
    return pl.pallas_call(
        _mlp_kernel,
        grid=(s // ts,),
        in_specs=[pl.BlockSpec((ts, d), lambda i: (i, 0)), full((1, d)), full((d, dff)), full((dff, d)),
                  full((1, d))],
        out_specs=pl.BlockSpec((ts, d), lambda i: (i, 0)),
        out_shape=jax.ShapeDtypeStruct((s, d), F32),
        compiler_params=_params("arbitrary"),
        name="mlp_relu2",
    )(h, gpre, w1, w2, gpost)


def kernel(x, norm_mix_pre, norm_mix_post, norm_mlp_pre, norm_mlp_post, w_in,
           cmp_pos_k, cmp_w1_k, cmp_w2_k, cmp_pos_v, cmp_w1_v, cmp_w2_v,
           attn_sinks, rel_bias, w_up_nsa, w_up_swa, w_out, w_mlp_in, w_mlp_out):
    b, s, d = x.shape
    assert b == 1 and s % SEL_TILE == 0 and w_in.shape[0] == 1
    assert SEL_TILE >= MAX_DISTANCE
    qd = HEADS * HEAD_DIM
    kvd = GROUPS * HEAD_DIM
    sizes = (qd,) + (kvd,) * 6 + (3 * HEADS, qd, kvd, kvd, d, d)
    offs = [0]
    for z in sizes:
        offs.append(offs[-1] + z)
    w = w_in[0]
    col = lambda i: w[:, offs[i]:offs[i + 1]]
    (w_qn, w_kc, w_vc, w_ks, w_vs, w_kw, w_vw, w_gn, w_qs, w_k_s, w_v_s, w_ga, w_gb) = [col(i) for i in range(13)]
    scale = HEAD_DIM ** -0.5 * LOG2E
    w_gn = w_gn.reshape(d, HEADS, 3).transpose(0, 2, 1).reshape(d, 3 * HEADS)
    w_gn = jnp.pad(w_gn, ((0, 0), (0, 8)))
    wn = jnp.concatenate([w_kc, w_vc, w_ks, w_kw, w_k_s], axis=1).astype(BF16)
    wg = jnp.concatenate([w_ga, w_gb], axis=1).astype(BF16)
    wt = jnp.concatenate([w_qn * scale, w_qs * scale, w_vs, w_vw, w_v_s, w_gn], axis=1).T.astype(BF16)
    nqv = 2 * qd + 3 * kvd

    x2 = x[0]
    kv, gab, qv, gn, kb = _project(x2, norm_mix_pre, wn, wg, wt, nqv, ts=PROJ_TOKENS)
    q_nsa = qv[0:qd].reshape(GROUPS, REP * HEAD_DIM, s)
    q_swa = qv[qd:2 * qd].reshape(GROUPS, REP * HEAD_DIM, s)
    vs_t = qv[2 * qd:2 * qd + kvd].reshape(GROUPS, HEAD_DIM, s)
    vw_t = qv[2 * qd + kvd:2 * qd + 2 * kvd].reshape(GROUPS, HEAD_DIM, s)
    vswa_t = qv[2 * qd + 2 * kvd:].reshape(GROUPS, HEAD_DIM, s)

    n16 = s // CMP_STRIDE
    pos8 = jnp.stack([cmp_pos_k[0], cmp_pos_v[0]]).reshape(2, 1, CMP_BLOCK * HEAD_DIM)
    pos8 = jnp.broadcast_to(pos8, (2, 8, CMP_BLOCK * HEAD_DIM)).astype(BF16)
    w1 = jnp.stack([cmp_w1_k[0], cmp_w1_v[0]]).astype(BF16)
    w2 = jnp.stack([cmp_w2_k[0], cmp_w2_v[0]]).astype(BF16)
    cmp_n, cmp_t = _compress(kb, pos8, w1, w2, w2.transpose(0, 2, 1))

    oc, sel = _cmp_select(q_nsa, cmp_n, cmp_t, s)

    bias_nsa = rel_bias[:, :HEADS].reshape(NUM_BUCKETS, GROUPS, REP) * LOG2E
    bias_swa = rel_bias[:, HEADS:].reshape(NUM_BUCKETS, GROUPS, REP) * LOG2E
    thr = jnp.asarray(_bucket_starts())
    tab_nsa = bias_nsa.transpose(1, 2, 0).reshape(-1)
    tab_swa = bias_swa.transpose(1, 2, 0).reshape(-1)

    tk = SEL_TILE
    nb = tk // SEL_BLOCK
    posn = jnp.arange(s)
    onehot = ((posn // SEL_BLOCK) % nb)[:, None] == jnp.arange(nb)[None, :]
    aug = jnp.concatenate([onehot.astype(BF16), jnp.ones((s, 2), BF16),
                           jnp.zeros((s, kvd - nb - 2), BF16)], axis=1)
    kaug = jnp.concatenate([kv[:, 0:kvd], aug], axis=1)
    far = bias_nsa[NUM_BUCKETS - 1]
    far_hi = far.astype(BF16)
    far_lo = (far - far_hi.astype(F32)).astype(BF16)
    cst = jnp.stack([far_hi, far_lo], axis=1)
    cst = jnp.broadcast_to(cst[:, :, :, None], (GROUPS, 2, REP, QBLOCK)).reshape(GROUPS, 2, HQ)
    cst = jnp.pad(cst, ((0, 0), (0, 14), (0, 0)))
    ones_row = jnp.concatenate([jnp.ones((GROUPS, 1, s), BF16), jnp.zeros((GROUPS, 15, s), BF16)], axis=1)
    vs_aug = jnp.concatenate([vs_t, ones_row], axis=1)
    o_sel = _selected(thr, tab_nsa, q_nsa, sel, kaug, vs_aug, cst, s, tk)

    pad_rows = lambda a, wdw: jnp.pad(a, ((wdw, 0), (0, 0)))
    pad_lanes = lambda a, wdw: jnp.pad(jnp.concatenate([a, ones_row], axis=1), ((0, 0), (0, 0), (wdw, 0)))
    no_sink = jnp.zeros((GROUPS, 1, HQ), F32)
    o_win = _banded(thr, tab_nsa, q_nsa, pad_rows(kv[:, kvd:2 * kvd], NSA_WINDOW), pad_lanes(vw_t, NSA_WINDOW),
                    no_sink, s, NSA_WINDOW, False, "nsa_window")
    sink = attn_sinks[0].reshape(GROUPS, 1, REP, 1).astype(F32) * LOG2E
    sink = jnp.broadcast_to(sink, (GROUPS, 1, REP, QBLOCK)).reshape(GROUPS, 1, HQ)
    o_swa = _banded(thr, tab_swa, q_swa, pad_rows(kv[:, 2 * kvd:3 * kvd], SWA_WINDOW), pad_lanes(vswa_t, SWA_WINDOW),
                    sink, s, SWA_WINDOW, True, "swa_sink")

    flat = lambda a: a.reshape(qd, s)
    h1 = _merge(flat(oc), flat(o_sel), flat(o_win), flat(o_swa), gn, gab, x2,
                w_up_nsa[0].astype(BF16), w_up_swa[0].astype(BF16), w_out[0].astype(BF16),
                norm_mix_post, ts=PROJ_TOKENS)
    out = _mlp(h1, norm_mlp_pre, w_mlp_in[0].astype(BF16), w_mlp_out[0].astype(BF16), norm_mlp_post,
               ts=MLP_TOKENS)
    return out[None]
```

```python
import functools
import math

import jax
import jax.numpy as jnp
import numpy as np
from jax import lax
from jax.experimental import pallas as pl
from jax.experimental.pallas import tpu as pltpu

F32 = jnp.float32
BF16 = jnp.bfloat16

HEAD_DIM = 64
GROUPS = 2
REP = 4
HEADS = GROUPS * REP
CMP_BLOCK = 32
CMP_STRIDE = 16
SEL_BLOCK = 64
SEL_TOPK = 16
NSA_WINDOW = 512
SWA_WINDOW = 128
QBLOCK = 128
NUM_BUCKETS = 32
MAX_DISTANCE = 1024
RMS_EPS = 1e-6
LOG2E = math.log2(math.e)
MASK_BIG = 2.0 ** 99
HQ = REP * QBLOCK

SEL_TILE = 1024
TILES_PER_TRIP = 8
PROJ_TOKENS = 1024
MLP_TOKENS = 1024
BAND_CHUNKS = 16
CMP_CHUNKS = 4
CMP_VARIANT_KEYS = 128
V7X_VMEM_BYTES = 64 * 1024 * 1024
VMEM_LIMIT = V7X_VMEM_BYTES * 7 // 8

_NT = (((1,), (1,)), ((), ()))
_TN = (((0,), (0,)), ((), ()))


def _params(*sem):
    return pltpu.CompilerParams(dimension_semantics=sem, vmem_limit_bytes=VMEM_LIMIT)


def _bucket_starts():
    max_exact = NUM_BUCKETS // 2
    d = np.arange(2 * MAX_DISTANCE)
    large = max_exact + (np.log(np.maximum(d, 1).astype(np.float64) / max_exact)
                         / math.log(MAX_DISTANCE / max_exact) * (NUM_BUCKETS - max_exact)).astype(np.int64)
    bucket = np.where(d < max_exact, d, np.minimum(large, NUM_BUCKETS - 1))
    return (bucket[None, :] < np.arange(NUM_BUCKETS)[:, None]).sum(axis=1).astype(np.int32)


def _rms(x, gain):
    return x * lax.rsqrt(jnp.mean(x * x, axis=-1, keepdims=True) + RMS_EPS) * gain


def _proj_kernel(x_ref, g_ref, wn_ref, wg_ref, wt_ref, kv_ref, gab_ref, qv_ref, gn_ref, kb_ref, y_ref, *, nqv):
    u = _rms(x_ref[...], g_ref[...]).astype(BF16)
    kvf = jnp.dot(u, wn_ref[...], preferred_element_type=F32)
    gw = GROUPS * HEAD_DIM
    kv_ref[...] = kvf[:, 2 * gw:].astype(BF16)
    nrow = y_ref.shape[1] // CMP_STRIDE
    for a in range(2):
        y_ref[a] = kvf[:, a * gw:(a + 1) * gw]
        both = [y_ref[a, pl.ds(p, nrow, stride=CMP_STRIDE), :] for p in range(CMP_STRIDE)]
        for gi in range(GROUPS):
            kb_ref[a, gi] = jnp.concatenate([t[:, gi * HEAD_DIM:(gi + 1) * HEAD_DIM] for t in both],
                                            axis=1).astype(BF16)
    gab_ref[...] = jnp.dot(u, wg_ref[...], preferred_element_type=F32).astype(BF16)
    t = lax.dot_general(wt_ref[...], u, _NT, preferred_element_type=F32)
    qv_ref[...] = t[:nqv].astype(BF16)
    gn_ref[...] = t[nqv:]


def _project(x, gain, wn, wg, wt, nqv, ts):
    s, d = x.shape
    nn, ng, ntr = wn.shape[1] - 2 * GROUPS * HEAD_DIM, wg.shape[1], wt.shape[0]
    full = lambda shp: pl.BlockSpec(shp, lambda i: (0, 0))
    return pl.pallas_call(
        functools.partial(_proj_kernel, nqv=nqv),
        grid=(s // ts,),
        in_specs=[pl.BlockSpec((ts, d), lambda i: (i, 0)), full((1, d)),
                  full(wn.shape), full((d, ng)), full((ntr, d))],
        out_specs=[pl.BlockSpec((ts, nn), lambda i: (i, 0)),
                   pl.BlockSpec((ts, ng), lambda i: (i, 0)),
                   pl.BlockSpec((nqv, ts), lambda i: (0, i)),
                   pl.BlockSpec((ntr - nqv, ts), lambda i: (0, i)),
                   pl.BlockSpec((2, GROUPS, ts // CMP_STRIDE, CMP_STRIDE * HEAD_DIM), lambda i: (0, 0, i, 0))],
        out_shape=[jax.ShapeDtypeStruct((s, nn), BF16), jax.ShapeDtypeStruct((s, ng), BF16),
                   jax.ShapeDtypeStruct((nqv, s), BF16), jax.ShapeDtypeStruct((ntr - nqv, s), F32),
                   jax.ShapeDtypeStruct((2, GROUPS, s // CMP_STRIDE, CMP_STRIDE * HEAD_DIM), BF16)],
        scratch_shapes=[pltpu.VMEM((2, ts, GROUPS * HEAD_DIM), F32)],
        compiler_params=_params("arbitrary"),
        name="in_proj",
    )(x, gain, wn, wg, wt)


def _gelu_tanh(x):
    return x * (0.5 * (1.0 + jnp.tanh(math.sqrt(2.0 / math.pi) * (x + 0.044715 * (x * x * x)))))


def _compress_kernel(kb_ref, pos_ref, w1_ref, w2_ref, w2t_ref, o_ref, ot_ref):
    kb = kb_ref[0, 0]
    half = kb.shape[1]
    n16 = kb.shape[0]
    first = jnp.dot(kb, w1_ref[0, :half, :], preferred_element_type=F32)
    second = jnp.dot(kb, w1_ref[0, half:, :], preferred_element_type=F32)
    posb = jnp.dot(pos_ref[0], w1_ref[0], preferred_element_type=F32)[0:1]
    pre = first + pltpu.roll(second, n16 - 1, 0) + posb
    h = _gelu_tanh(pre).astype(BF16)
    o_ref[0, 0] = jnp.dot(h, w2_ref[0], preferred_element_type=F32).astype(BF16)
    ot_ref[0, 0] = lax.dot_general(w2t_ref[0], h, _NT, preferred_element_type=F32).astype(BF16)


def _compress(kb, pos8, w1, w2, w2t):
    _, g, n16, cin2 = kb.shape
    hid = w1.shape[2]
    return pl.pallas_call(
        _compress_kernel,
        grid=(2, g),
        in_specs=[pl.BlockSpec((1, 1, n16, cin2), lambda a, b: (a, b, 0, 0)),
                  pl.BlockSpec((1, 8, 2 * cin2), lambda a, b: (a, 0, 0)),
                  pl.BlockSpec((1, 2 * cin2, hid), lambda a, b: (a, 0, 0)),
                  pl.BlockSpec((1, hid, HEAD_DIM), lambda a, b: (a, 0, 0)),
                  pl.BlockSpec((1, HEAD_DIM, hid), lambda a, b: (a, 0, 0))],
        out_specs=[pl.BlockSpec((1, 1, n16, HEAD_DIM), lambda a, b: (a, b, 0, 0)),
                   pl.BlockSpec((1, 1, HEAD_DIM, n16), lambda a, b: (a, b, 0, 0))],
        out_shape=[jax.ShapeDtypeStruct((2, g, n16, HEAD_DIM), BF16),
                   jax.ShapeDtypeStruct((2, g, HEAD_DIM, n16), BF16)],
        compiler_params=_params("arbitrary", "arbitrary"),
        name="kv_compress",
    )(kb, pos8, w1, w2, w2t)


def _cmpsel_body(q_ref, kc_ref, vct_ref, oc_ref, sel_ref, pg_ref, s_ref, t_base, *, cb, rows, nomask, nsel, topk):
    mrows = rows - nomask
    ratio = SEL_BLOCK // CMP_STRIDE
    nblk = rows // ratio
    n_io = nomask + lax.broadcasted_iota(jnp.int32, (mrows, HQ), 0)
    lane_t = lax.broadcasted_iota(jnp.int32, (mrows, HQ), 1) & (QBLOCK - 1)
    blk = lax.broadcasted_iota(jnp.int32, (nblk, QBLOCK), 0)

    def raw_scores(c):
        t0 = t_base + c * QBLOCK
        lanes = slice(c * QBLOCK, (c + 1) * QBLOCK)
        q4 = jnp.concatenate([q_ref[0, r * HEAD_DIM:(r + 1) * HEAD_DIM, lanes] for r in range(REP)], axis=1)
        s = jnp.dot(kc_ref[0, 0, 0:rows, :], q4, preferred_element_type=F32)
        edge = jnp.where(n_io * CMP_STRIDE + (CMP_BLOCK - 1) <= t0 + lane_t, s[nomask:], -jnp.inf)
        s = jnp.concatenate([s[:nomask], edge], axis=0) if nomask else edge
        s_ref[c, 0:rows, :] = s
        m = jnp.max(s, axis=0, keepdims=True)
        return jnp.where(m == -jnp.inf, 0.0, m)

    def attend(c, m):
        t0 = t_base + c * QBLOCK
        lanes = slice(c * QBLOCK, (c + 1) * QBLOCK)
        e = jnp.exp2(s_ref[c, 0:rows, :] - m)
        den = jnp.maximum(jnp.sum(e, axis=0, keepdims=True), 1e-30)
        p = e * (1.0 / den)
        o_t = jnp.dot(vct_ref[0, 0, :, 0:rows], p.astype(BF16), preferred_element_type=F32)
        for r in range(REP):
            oc_ref[0, r * HEAD_DIM:(r + 1) * HEAD_DIM, lanes] = (
                o_t[:, r * QBLOCK:(r + 1) * QBLOCK].astype(oc_ref.dtype))

        pg = ((p[:, 0:QBLOCK] + p[:, QBLOCK:2 * QBLOCK]) + p[:, 2 * QBLOCK:3 * QBLOCK]) + p[:, 3 * QBLOCK:]
        pg_ref[c, 0:8, :] = jnp.zeros((8, QBLOCK), F32)
        pg_ref[c, 8:8 + rows, :] = pg
        ps = pg_ref[c, pl.ds(7, nblk, stride=ratio), :]
        for o in range(1, ratio + 1):
            ps = ps + pg_ref[c, pl.ds(7 + o, nblk, stride=ratio), :]
        cur = (t0 + lax.broadcasted_iota(jnp.int32, (nblk, QBLOCK), 1)) // SEL_BLOCK
        valid = blk <= cur
        forced = (blk == 0) | (blk == cur) | (blk == cur - 1)
        return jnp.where(valid & ~forced, ps, -jnp.inf), valid

    scores, valids = [], []
    m = raw_scores(0)
    for c in range(cb):
        m_next = raw_scores(c + 1) if c + 1 < cb else None
        score, valid = attend(c, m)
        scores.append(score)
        valids.append(valid)
        m = m_next

    def pick_one(_, rems):
        out = []
        for rem in rems:
            top = jnp.max(rem, axis=0, keepdims=True)
            first = jnp.min(jnp.where(rem == top, blk, nblk), axis=0, keepdims=True)
            out.append(jnp.where(blk == first, -jnp.inf, rem))
        return tuple(out)

    rems = lax.fori_loop(0, max(topk - 3, 0), pick_one, tuple(scores))
    for c in range(cb):
        lanes = slice(c * QBLOCK, (c + 1) * QBLOCK)
        sel_ref[0, 0:nblk, lanes] = jnp.where((rems[c] == -jnp.inf) & valids[c], 0.0, -MASK_BIG).astype(BF16)
        if nblk < nsel:
            sel_ref[0, nblk:, lanes] = jnp.full((nsel - nblk, QBLOCK), -MASK_BIG, BF16)


def _cmpsel_kernel(q_ref, kc_ref, vct_ref, oc_ref, sel_ref, pg_ref, s_ref, *, n16, nsel, topk, nvar, cb):
    step = pl.program_id(1)
    per = n16 // nvar
    spv = per // (QBLOCK // CMP_STRIDE) // cb
    for k in range(1, nvar + 1):
        @pl.when((step >= (k - 1) * spv) & (step < k * spv))
        def _(k=k):
            _cmpsel_body(q_ref, kc_ref, vct_ref, oc_ref, sel_ref, pg_ref, s_ref, step * (cb * QBLOCK), cb=cb,
                         rows=per * k, nomask=max(per * (k - 1) - 8, 0), nsel=nsel, topk=topk)


def _cmp_select(q, kc_all, vct_all, s, cb=CMP_CHUNKS):
    n16 = s // CMP_STRIDE
    nsel = s // SEL_BLOCK
    topk = min(SEL_TOPK, nsel)
    nvar = max(n16 // CMP_VARIANT_KEYS, 1)
    return pl.pallas_call(
        functools.partial(_cmpsel_kernel, n16=n16, nsel=nsel, topk=topk, nvar=nvar, cb=cb),
        grid=(GROUPS, s // (QBLOCK * cb)),
        in_specs=[pl.BlockSpec((1, REP * HEAD_DIM, QBLOCK * cb), lambda g, c: (g, 0, c)),
                  pl.BlockSpec((1, 1, n16, HEAD_DIM), lambda g, c: (0, g, 0, 0)),
                  pl.BlockSpec((1, 1, HEAD_DIM, n16), lambda g, c: (1, g, 0, 0))],
        out_specs=[pl.BlockSpec((1, REP * HEAD_DIM, QBLOCK * cb), lambda g, c: (g, 0, c)),
                   pl.BlockSpec((1, nsel, QBLOCK * cb), lambda g, c: (g, 0, c))],
        out_shape=[jax.ShapeDtypeStruct((GROUPS, REP * HEAD_DIM, s), BF16),
                   jax.ShapeDtypeStruct((GROUPS, nsel, s), BF16)],
        scratch_shapes=[pltpu.VMEM((cb, n16 + 8, QBLOCK), F32), pltpu.VMEM((cb, n16, HQ), F32)],
        compiler_params=_params("arbitrary", "arbitrary"),
        name="cmp_select",
    )(q, kc_all, vct_all)


def _fill_bias_strip(strip_ref, thr_ref, tab_ref, g, *, dist0, lo, hi, minus_far):
    nrows = strip_ref.shape[0]

    def block(bi, carry):
        i0 = pl.multiple_of(bi * QBLOCK, QBLOCK)
        d = (lax.broadcasted_iota(jnp.int32, (QBLOCK, QBLOCK), 1)
             - lax.broadcasted_iota(jnp.int32, (QBLOCK, QBLOCK), 0) + (dist0 - i0))
        ok = d >= lo if hi is None else (d >= lo) & (d < hi)
        for r in range(REP):
            base = (g * REP + r) * NUM_BUCKETS
            out = jnp.full((QBLOCK, QBLOCK), tab_ref[base], F32)
            for b in range(1, NUM_BUCKETS):
                out = jnp.where(d >= thr_ref[b], tab_ref[base + b], out)
            if minus_far:
                out = out - tab_ref[base + NUM_BUCKETS - 1]
            strip_ref[pl.ds(i0, QBLOCK), r * QBLOCK:(r + 1) * QBLOCK] = jnp.where(ok, out, -MASK_BIG)
        return carry

    lax.fori_loop(0, nrows // QBLOCK, block, 0)


def _sel_kernel(thr_ref, tab_ref, q_ref, sel_ref, kaug_ref, vt_ref, cst_ref, o_ref,
                qa_ref, s_ref, tmax_ref, m_ref, acc_ref, strip_ref, *, tk):
    g = pl.program_id(0)
    st = pl.program_id(1)
    nb = tk // SEL_BLOCK
    cpt = tk // QBLOCK
    kd = GROUPS * HEAD_DIM
    dmax = 2 * tk - QBLOCK

    @pl.when(st == 0)
    def _():
        _fill_bias_strip(strip_ref, thr_ref, tab_ref, g, dist0=dmax, lo=0, hi=None, minus_far=True)

    qa_ref[...] = jnp.zeros(qa_ref.shape, BF16)
    for b in range(2):
        qa_ref[b, kd + nb:kd + nb + 16, :] = cst_ref[0]
    m_ref[...] = jnp.full(m_ref.shape, -jnp.inf, F32)
    acc_ref[...] = jnp.zeros(acc_ref.shape, F32)
    row0 = pl.multiple_of(g * HEAD_DIM, HEAD_DIM)

    def far_tiles(count):

        def advance(c, j):
            wrap = j + 1 == count
            return jnp.where(wrap, c + 1, c), jnp.where(wrap, 0, j + 1)

        def qk(c, j, buf):
            cc = jnp.minimum(c, cpt - 1)
            lane0 = pl.multiple_of(cc * QBLOCK, QBLOCK)
            for r in range(REP):
                qa_ref[buf, pl.ds(row0, HEAD_DIM), r * QBLOCK:(r + 1) * QBLOCK] = (
                    q_ref[0, r * HEAD_DIM:(r + 1) * HEAD_DIM, pl.ds(lane0, QBLOCK)])
            sb = sel_ref[0, pl.ds(pl.multiple_of(j * nb, nb), nb), pl.ds(lane0, QBLOCK)]
            for r in range(REP):
                qa_ref[buf, kd:kd + nb, r * QBLOCK:(r + 1) * QBLOCK] = sb
            k0 = pl.multiple_of(j * tk, tk)
            s = jnp.dot(kaug_ref[pl.ds(k0, tk), :], qa_ref[buf], preferred_element_type=F32)
            s_ref[buf] = s
            tmax_ref[buf] = jnp.max(s, axis=0, keepdims=True)

        def softmax_pv(c, j, buf):
            k0 = pl.multiple_of(j * tk, tk)
            m_prev = m_ref[c]
            m_new = jnp.maximum(m_prev, tmax_ref[buf])
            alpha = jnp.exp2(m_prev - m_new)
            p = jnp.exp2(s_ref[buf] - m_new)
            acc_ref[c] = alpha * acc_ref[c] + jnp.dot(vt_ref[0, :, pl.ds(k0, tk)], p.astype(BF16),
                                                      preferred_element_type=F32)
            m_ref[c] = m_new

        zero = jnp.int32(0)
        qk(zero, zero, 0)

        def trip(_, carry):
            cur = carry
            for k in range(TILES_PER_TRIP):
                nxt = advance(*cur)
                qk(*nxt, (k + 1) % 2)
                softmax_pv(*cur, k % 2)
                cur = nxt
            return cur

        lax.fori_loop(0, (cpt * count) // TILES_PER_TRIP, trip, (zero, zero))

    def near_tiles(with_previous):
        tiles = []
        for c in range(cpt):
            if with_previous:
                tiles.append((c, st - 1, tk + c * QBLOCK, tk, False))
            tiles.append((c, st, c * QBLOCK, (c + 1) * QBLOCK, True))

        def qk(c, j, delta, rows, diagonal, buf):
            lanes = slice(c * QBLOCK, (c + 1) * QBLOCK)
            for r in range(REP):
                qa_ref[buf, pl.ds(row0, HEAD_DIM), r * QBLOCK:(r + 1) * QBLOCK] = (
                    q_ref[0, r * HEAD_DIM:(r + 1) * HEAD_DIM, lanes])
            sb = sel_ref[0, pl.ds(pl.multiple_of(j * nb, nb), nb), lanes]
            for r in range(REP):
                qa_ref[buf, kd:kd + nb, r * QBLOCK:(r + 1) * QBLOCK] = sb
            k0 = pl.multiple_of(j * tk, tk)
            s = (jnp.dot(kaug_ref[pl.ds(k0, rows), :], qa_ref[buf], preferred_element_type=F32)
                 + strip_ref[dmax - delta:dmax - delta + rows, :])
            s_ref[buf, 0:rows, :] = s
            tmax_ref[buf] = jnp.max(s, axis=0, keepdims=True)

        def softmax_pv(c, j, rows, diagonal, buf):
            k0 = pl.multiple_of(j * tk, tk)
            m_prev = m_ref[c]
            m_new = jnp.maximum(m_prev, tmax_ref[buf])
            alpha = jnp.exp2(m_prev - m_new)
            p = jnp.exp2(s_ref[buf, 0:rows, :] - m_new)
            acc = alpha * acc_ref[c] + jnp.dot(vt_ref[0, :, pl.ds(k0, rows)], p.astype(BF16),
                                               preferred_element_type=F32)
            acc_ref[c] = acc
            m_ref[c] = m_new
            if diagonal:
                out = acc[0:HEAD_DIM] * (1.0 / jnp.maximum(acc[HEAD_DIM:HEAD_DIM + 1], 1e-30))
                for r in range(REP):
                    o_ref[0, r * HEAD_DIM:(r + 1) * HEAD_DIM, c * QBLOCK:(c + 1) * QBLOCK] = (
                        out[:, r * QBLOCK:(r + 1) * QBLOCK].astype(o_ref.dtype))

        qk(*tiles[0], 0)
        for n, (c, j, delta, rows, diagonal) in enumerate(tiles):
            if n + 1 < len(tiles):
                qk(*tiles[n + 1], (n + 1) % 2)
            softmax_pv(c, j, rows, diagonal, n % 2)

    @pl.when(st > 1)
    def _():
        far_tiles(st - 1)

    @pl.when(st == 0)
    def _():
        near_tiles(False)

    @pl.when(st > 0)
    def _():
        near_tiles(True)


def _selected(thr, tab, q, sel, kaug, vt, cst, s, tk):
    nsel = s // SEL_BLOCK
    sl = 3 * tk - QBLOCK
    cpt = tk // QBLOCK
    smem = pl.BlockSpec(memory_space=pltpu.SMEM)
    return pl.pallas_call(
        functools.partial(_sel_kernel, tk=tk),
        grid=(GROUPS, s // tk),
        in_specs=[smem, smem,
                  pl.BlockSpec((1, REP * HEAD_DIM, tk), lambda g, c: (g, 0, c)),
                  pl.BlockSpec((1, nsel, tk), lambda g, c: (g, 0, c)),
                  pl.BlockSpec(kaug.shape, lambda g, c: (0, 0)),
                  pl.BlockSpec((1, vt.shape[1], s), lambda g, c: (g, 0, 0)),
                  pl.BlockSpec((1, 16, HQ), lambda g, c: (g, 0, 0))],
        out_specs=pl.BlockSpec((1, REP * HEAD_DIM, tk), lambda g, c: (g, 0, c)),
        out_shape=jax.ShapeDtypeStruct((GROUPS, REP * HEAD_DIM, s), BF16),
        scratch_shapes=[pltpu.VMEM((2, kaug.shape[1], HQ), BF16), pltpu.VMEM((2, tk, HQ), F32),
                        pltpu.VMEM((2, 1, HQ), F32),
                        pltpu.VMEM((cpt, 1, HQ), F32), pltpu.VMEM((cpt, vt.shape[1], HQ), F32),
                        pltpu.VMEM((sl, HQ), F32)],
        compiler_params=_params("arbitrary", "arbitrary"),
        name="selected_attn",
    )(thr, tab, q, sel, kaug, vt, cst)


def _band_kernel(thr_ref, tab_ref, q_ref, k_ref, vt_ref, sink_ref, o_ref, qa_ref, strip_ref, s_ref, *,
                 window, use_sink, cb):
    g = pl.program_id(0)

    @pl.when(pl.program_id(1) == 0)
    def _():
        _fill_bias_strip(strip_ref, thr_ref, tab_ref, g, dist0=window, lo=0, hi=window, minus_far=False)

    base = pl.multiple_of(pl.program_id(1) * (QBLOCK * cb), QBLOCK * cb)
    nk = window + QBLOCK
    qa_ref[...] = jnp.zeros(qa_ref.shape, BF16)
    row0 = pl.multiple_of(g * HEAD_DIM, HEAD_DIM)
    for c in range(cb):
        for r in range(REP):
            qa_ref[c, pl.ds(row0, HEAD_DIM), r * QBLOCK:(r + 1) * QBLOCK] = (
                q_ref[0, r * HEAD_DIM:(r + 1) * HEAD_DIM, c * QBLOCK:(c + 1) * QBLOCK])

    def scores(c, first_step):
        t0 = base + c * QBLOCK
        s = jnp.dot(k_ref[pl.ds(t0, nk), :], qa_ref[c], preferred_element_type=F32) + strip_ref[...]
        if first_step and c * QBLOCK < window:
            row = lax.broadcasted_iota(jnp.int32, (nk, HQ), 0)
            s = jnp.where(row >= window - c * QBLOCK, s, -MASK_BIG)
        s_ref[c] = s
        m = jnp.max(s, axis=0, keepdims=True)
        return jnp.maximum(m, sink_ref[0]) if use_sink else m

    def attend(c, m):
        t0 = base + c * QBLOCK
        e = jnp.exp2(s_ref[c] - m)
        o_t = jnp.dot(vt_ref[0, :, pl.ds(t0, nk)], e.astype(BF16), preferred_element_type=F32)
        den = o_t[HEAD_DIM:HEAD_DIM + 1]
        den = den + jnp.exp2(sink_ref[0] - m) if use_sink else jnp.maximum(den, 1e-30)
        o_t = o_t[0:HEAD_DIM] * (1.0 / den)
        for r in range(REP):
            o_ref[0, r * HEAD_DIM:(r + 1) * HEAD_DIM, c * QBLOCK:(c + 1) * QBLOCK] = (
                o_t[:, r * QBLOCK:(r + 1) * QBLOCK].astype(o_ref.dtype))

    def chunks(first_step):
        m = scores(0, first_step)
        for c in range(cb):
            m_next = scores(c + 1, first_step) if c + 1 < cb else None
            attend(c, m)
            m = m_next

    @pl.when(pl.program_id(1) == 0)
    def _():
        chunks(True)

    @pl.when(pl.program_id(1) > 0)
    def _():
        chunks(False)


def _banded(thr, tab, q, kpad, vtpad, sink, s, window, use_sink, name, cb=BAND_CHUNKS):
    assert QBLOCK * cb >= window
    nk = window + QBLOCK
    smem = pl.BlockSpec(memory_space=pltpu.SMEM)
    return pl.pallas_call(
        functools.partial(_band_kernel, window=window, use_sink=use_sink, cb=cb),
        grid=(GROUPS, s // (QBLOCK * cb)),
        in_specs=[smem, smem,
                  pl.BlockSpec((1, REP * HEAD_DIM, QBLOCK * cb), lambda g, c: (g, 0, c)),
                  pl.BlockSpec(kpad.shape, lambda g, c: (0, 0)),
                  pl.BlockSpec((1, vtpad.shape[1], s + window), lambda g, c: (g, 0, 0)),
                  pl.BlockSpec((1, 1, HQ), lambda g, c: (g, 0, 0))],
        out_specs=pl.BlockSpec((1, REP * HEAD_DIM, QBLOCK * cb), lambda g, c: (g, 0, c)),
        out_shape=jax.ShapeDtypeStruct((GROUPS, REP * HEAD_DIM, s), BF16),
        scratch_shapes=[pltpu.VMEM((cb, GROUPS * HEAD_DIM, HQ), BF16), pltpu.VMEM((nk, HQ), F32),
                        pltpu.VMEM((cb, nk, HQ), F32)],
        compiler_params=_params("arbitrary", "arbitrary"),
        name=name,
    )(thr, tab, q, kpad, vtpad, sink)


def _merge_kernel(oc_ref, os_ref, ow_ref, ob_ref, gn_ref, gab_ref, x_ref, wa_ref, wb_ref, wo_ref, gp_ref, o_ref):
    d = x_ref.shape[1]
    gn = jax.nn.sigmoid(gn_ref[...])
    parts = []
    for h in range(HEADS):
        rows = slice(h * HEAD_DIM, (h + 1) * HEAD_DIM)
        parts.append(gn[h:h + 1] * oc_ref[rows, :].astype(F32)
                     + gn[HEADS + h:HEADS + h + 1] * os_ref[rows, :].astype(F32)
                     + gn[2 * HEADS + h:2 * HEADS + h + 1] * ow_ref[rows, :].astype(F32))
    oa_t = jnp.concatenate(parts, axis=0).astype(BF16)
    ya = lax.dot_general(oa_t, wa_ref[...], _TN, preferred_element_type=F32)
    yb = lax.dot_general(ob_ref[...], wb_ref[...], _TN, preferred_element_type=F32)
    y = (jax.nn.sigmoid(gab_ref[:, :d].astype(F32)) * ya
         + jax.nn.sigmoid(gab_ref[:, d:].astype(F32)) * yb)
    z = jnp.dot(y.astype(BF16), wo_ref[...], preferred_element_type=F32)
    o_ref[...] = x_ref[...] + _rms(z, gp_ref[...])


def _merge(oc, os_, ow, ob, gn, gab, x, wa, wb, wo, gpost, ts):
    s, d = x.shape
    hd = oc.shape[0]
    tok = lambda rows: pl.BlockSpec((rows, ts), lambda i: (0, i))
    full = lambda shp: pl.BlockSpec(shp, lambda i: (0, 0))
    return pl.pallas_call(
        _merge_kernel,
        grid=(s // ts,),
        in_specs=[tok(hd), tok(hd), tok(hd), tok(hd), tok(gn.shape[0]),
                  pl.BlockSpec((ts, 2 * d), lambda i: (i, 0)), pl.BlockSpec((ts, d), lambda i: (i, 0)),
                  full((hd, d)), full((hd, d)), full((d, d)), full((1, d))],
        out_specs=pl.BlockSpec((ts, d), lambda i: (i, 0)),
        out_shape=jax.ShapeDtypeStruct((s, d), F32),
        compiler_params=_params("arbitrary"),
        name="merge_out_proj",
    )(oc, os_, ow, ob, gn, gab, x, wa, wb, wo, gpost)


def _mlp_kernel(h_ref, gpre_ref, w1_ref, w2_ref, gpost_ref, o_ref):
    u = _rms(h_ref[...], gpre_ref[...]).astype(BF16)
    a = jnp.maximum(jnp.dot(u, w1_ref[...], preferred_element_type=F32), 0.0)
    f = jnp.dot((a * a).astype(BF16), w2_ref[...], preferred_element_type=F32)
    o_ref[...] = h_ref[...] + _rms(f, gpost_ref[...])


def _mlp(h, gpre, w1, w2, gpost, ts):
    s, d = h.shape
    dff = w1.shape[1]
    full = lambda shp: pl.BlockSpec(shp, lambda i: (0, 0))
    return pl.pallas_call(
        _mlp_kernel,
        grid=(s // ts,),
        in_specs=[pl.BlockSpec((ts, d), lambda i: (i, 0)), full((1, d)), full((d, dff)), full((dff, d)),
                  full((1, d))],
        out_specs=pl.BlockSpec((ts, d), lambda i: (i, 0)),
        out_shape=jax.ShapeDtypeStruct((s, d), F32),
        compiler_params=_params("arbitrary"),
        name="mlp_relu2",
    )(h, gpre, w1, w2, gpost)


def kernel(x, norm_mix_pre, norm_mix_post, norm_mlp_pre, norm_mlp_post, w_in,
           cmp_pos_k, cmp_w1_k, cmp_w2_k, cmp_pos_v, cmp_w1_v, cmp_w2_v,
           attn_sinks, rel_bias, w_up_nsa, w_up_swa, w_out, w_mlp_in, w_mlp_out):
    b, s, d = x.shape
    assert b == 1 and s % SEL_TILE == 0 and w_in.shape[0] == 1
    assert SEL_TILE >= MAX_DISTANCE
    qd = HEADS * HEAD_DIM
    kvd = GROUPS * HEAD_DIM
    sizes = (qd,) + (kvd,) * 6 + (3 * HEADS, qd, kvd, kvd, d, d)
    offs = [0]
    for z in sizes:
        offs.append(offs[-1] + z)
    w = w_in[0]
    col = lambda i: w[:, offs[i]:offs[i + 1]]
    (w_qn, w_kc, w_vc, w_ks, w_vs, w_kw, w_vw, w_gn, w_qs, w_k_s, w_v_s, w_ga, w_gb) = [col(i) for i in range(13)]
    scale = HEAD_DIM ** -0.5 * LOG2E
    w_gn = w_gn.reshape(d, HEADS, 3).transpose(0, 2, 1).reshape(d, 3 * HEADS)
    w_gn = jnp.pad(w_gn, ((0, 0), (0, 8)))
    wn = jnp.concatenate([w_kc, w_vc, w_ks, w_kw, w_k_s], axis=1).astype(BF16)
    wg = jnp.concatenate([w_ga, w_gb], axis=1).astype(BF16)
    wt = jnp.concatenate([w_qn * scale, w_qs * scale, w_vs, w_vw, w_v_s, w_gn], axis=1).T.astype(BF16)
    nqv = 2 * qd + 3 * kvd

    x2 = x[0]
    kv, gab, qv, gn, kb = _project(x2, norm_mix_pre, wn, wg, wt, nqv, ts=PROJ_TOKENS)
    q_nsa = qv[0:qd].reshape(GROUPS, REP * HEAD_DIM, s)
    q_swa = qv[qd:2 * qd].reshape(GROUPS, REP * HEAD_DIM, s)
    vs_t = qv[2 * qd:2 * qd + kvd].reshape(GROUPS, HEAD_DIM, s)
    vw_t = qv[2 * qd + kvd:2 * qd + 2 * kvd].reshape(GROUPS, HEAD_DIM, s)
    vswa_t = qv[2 * qd + 2 * kvd:].reshape(GROUPS, HEAD_DIM, s)

    n16 = s // CMP_STRIDE
    pos8 = jnp.stack([cmp_pos_k[0], cmp_pos_v[0]]).reshape(2, 1, CMP_BLOCK * HEAD_DIM)
    pos8 = jnp.broadcast_to(pos8, (2, 8, CMP_BLOCK * HEAD_DIM)).astype(BF16)
    w1 = jnp.stack([cmp_w1_k[0], cmp_w1_v[0]]).astype(BF16)
    w2 = jnp.stack([cmp_w2_k[0], cmp_w2_v[0]]).astype(BF16)
    cmp_n, cmp_t = _compress(kb, pos8, w1, w2, w2.transpose(0, 2, 1))

    oc, sel = _cmp_select(q_nsa, cmp_n, cmp_t, s)

    bias_nsa = rel_bias[:, :HEADS].reshape(NUM_BUCKETS, GROUPS, REP) * LOG2E
    bias_swa = rel_bias[:, HEADS:].reshape(NUM_BUCKETS, GROUPS, REP) * LOG2E
    thr = jnp.asarray(_bucket_starts())
    tab_nsa = bias_nsa.transpose(1, 2, 0).reshape(-1)
    tab_swa = bias_swa.transpose(1, 2, 0).reshape(-1)

    tk = SEL_TILE
    nb = tk // SEL_BLOCK
    posn = jnp.arange(s)
    onehot = ((posn // SEL_BLOCK) % nb)[:, None] == jnp.arange(nb)[None, :]
    aug = jnp.concatenate([onehot.astype(BF16), jnp.ones((s, 2), BF16),
                           jnp.zeros((s, kvd - nb - 2), BF16)], axis=1)
    kaug = jnp.concatenate([kv[:, 0:kvd], aug], axis=1)
    far = bias_nsa[NUM_BUCKETS - 1]
    far_hi = far.astype(BF16)
    far_lo = (far - far_hi.astype(F32)).astype(BF16)
    cst = jnp.stack([far_hi, far_lo], axis=1)
    cst = jnp.broadcast_to(cst[:, :, :, None], (GROUPS, 2, REP, QBLOCK)).reshape(GROUPS, 2, HQ)
    cst = jnp.pad(cst, ((0, 0), (0, 14), (0, 0)))
    ones_row = jnp.concatenate([jnp.ones((GROUPS, 1, s), BF16), jnp.zeros((GROUPS, 15, s), BF16)], axis=1)
    vs_aug = jnp.concatenate([vs_t, ones_row], axis=1)
    o_sel = _selected(thr, tab_nsa, q_nsa, sel, kaug, vs_aug, cst, s, tk)

    pad_rows = lambda a, wdw: jnp.pad(a, ((wdw, 0), (0, 0)))
    pad_lanes = lambda a, wdw: jnp.pad(jnp.concatenate([a, ones_row], axis=1), ((0, 0), (0, 0), (wdw, 0)))
    no_sink = jnp.zeros((GROUPS, 1, HQ), F32)
    o_win = _banded(thr, tab_nsa, q_nsa, pad_rows(kv[:, kvd:2 * kvd], NSA_WINDOW), pad_lanes(vw_t, NSA_WINDOW),
                    no_sink, s, NSA_WINDOW, False, "nsa_window")
    sink = attn_sinks[0].reshape(GROUPS, 1, REP, 1).astype(F32) * LOG2E
    sink = jnp.broadcast_to(sink, (GROUPS, 1, REP, QBLOCK)).reshape(GROUPS, 1, HQ)
    o_swa = _banded(thr, tab_swa, q_swa, pad_rows(kv[:, 2 * kvd:3 * kvd], SWA_WINDOW), pad_lanes(vswa_t, SWA_WINDOW),
                    sink, s, SWA_WINDOW, True, "swa_sink")

    flat = lambda a: a.reshape(qd, s)
    h1 = _merge(flat(oc), flat(o_sel), flat(o_win), flat(o_swa), gn, gab, x2,
                w_up_nsa[0].astype(BF16), w_up_swa[0].astype(BF16), w_out[0].astype(BF16),
                norm_mix_post, ts=PROJ_TOKENS)
    out = _mlp(h1, norm_mlp_pre, w_mlp_in[0].astype(BF16), w_mlp_out[0].astype(BF16), norm_mlp_post,
               ts=MLP_TOKENS)
    return out[None]
```

```python
import functools
import math

import jax
import jax.numpy as jnp
import numpy as np
from jax import lax
from jax.experimental import pallas as pl
from jax.experimental.pallas import tpu as pltpu

F32 = jnp.float32
BF16 = jnp.bfloat16

HEAD_DIM = 64
GROUPS = 2
REP = 4
HEADS = GROUPS * REP
CMP_BLOCK = 32
CMP_STRIDE = 16
SEL_BLOCK = 64
SEL_TOPK = 16
NSA_WINDOW = 512
SWA_WINDOW = 128
QBLOCK = 128
NUM_BUCKETS = 32
MAX_DISTANCE = 1024
RMS_EPS = 1e-6
LOG2E = math.log2(math.e)
MASK_BIG = 2.0 ** 99
HQ = REP * QBLOCK

SEL_TILE = 1024
TILES_PER_TRIP = 8
PROJ_TOKENS = 1024
MLP_TOKENS = 1024
BAND_CHUNKS = 16
CMP_CHUNKS = 4
CMP_VARIANT_KEYS = 128
V7X_VMEM_BYTES = 64 * 1024 * 1024
VMEM_LIMIT = V7X_VMEM_BYTES * 7 // 8

_NT = (((1,), (1,)), ((), ()))
_TN = (((0,), (0,)), ((), ()))


def _params(*sem):
    return pltpu.CompilerParams(dimension_semantics=sem, vmem_limit_bytes=VMEM_LIMIT)


def _bucket_starts():
    max_exact = NUM_BUCKETS // 2
    d = np.arange(2 * MAX_DISTANCE)
    large = max_exact + (np.log(np.maximum(d, 1).astype(np.float64) / max_exact)
                         / math.log(MAX_DISTANCE / max_exact) * (NUM_BUCKETS - max_exact)).astype(np.int64)
    bucket = np.where(d < max_exact, d, np.minimum(large, NUM_BUCKETS - 1))
    return (bucket[None, :] < np.arange(NUM_BUCKETS)[:, None]).sum(axis=1).astype(np.int32)


def _rms(x, gain):
    return x * lax.rsqrt(jnp.mean(x * x, axis=-1, keepdims=True) + RMS_EPS) * gain


def _proj_kernel(x_ref, g_ref, wn_ref, wg_ref, wt_ref, kv_ref, gab_ref, qv_ref, gn_ref, kb_ref, y_ref, *, nqv):
    u = _rms(x_ref[...], g_ref[...]).astype(BF16)
    kvf = jnp.dot(u, wn_ref[...], preferred_element_type=F32)
    gw = GROUPS * HEAD_DIM
    kv_ref[...] = kvf[:, 2 * gw:].astype(BF16)
    nrow = y_ref.shape[1] // CMP_STRIDE
    for a in range(2):
        y_ref[a] = kvf[:, a * gw:(a + 1) * gw]
        both = [y_ref[a, pl.ds(p, nrow, stride=CMP_STRIDE), :] for p in range(CMP_STRIDE)]
        for gi in range(GROUPS):
            kb_ref[a, gi] = jnp.concatenate([t[:, gi * HEAD_DIM:(gi + 1) * HEAD_DIM] for t in both],
                                            axis=1).astype(BF16)
    gab_ref[...] = jnp.dot(u, wg_ref[...], preferred_element_type=F32).astype(BF16)
    t = lax.dot_general(wt_ref[...], u, _NT, preferred_element_type=F32)
    qv_ref[...] = t[:nqv].astype(BF16)
    gn_ref[...] = t[nqv:]


def _project(x, gain, wn, wg, wt, nqv, ts):
    s, d = x.shape
    nn, ng, ntr = wn.shape[1] - 2 * GROUPS * HEAD_DIM, wg.shape[1], wt.shape[0]
    full = lambda shp: pl.BlockSpec(shp, lambda i: (0, 0))
    return pl.pallas_call(
        functools.partial(_proj_kernel, nqv=nqv),
        grid=(s // ts,),
        in_specs=[pl.BlockSpec((ts, d), lambda i: (i, 0)), full((1, d)),
                  full(wn.shape), full((d, ng)), full((ntr, d))],
        out_specs=[pl.BlockSpec((ts, nn), lambda i: (i, 0)),
                   pl.BlockSpec((ts, ng), lambda i: (i, 0)),
                   pl.BlockSpec((nqv, ts), lambda i: (0, i)),
                   pl.BlockSpec((ntr - nqv, ts), lambda i: (0, i)),
                   pl.BlockSpec((2, GROUPS, ts // CMP_STRIDE, CMP_STRIDE * HEAD_DIM), lambda i: (0, 0, i, 0))],
        out_shape=[jax.ShapeDtypeStruct((s, nn), BF16), jax.ShapeDtypeStruct((s, ng), BF16),
                   jax.ShapeDtypeStruct((nqv, s), BF16), jax.ShapeDtypeStruct((ntr - nqv, s), F32),
                   jax.ShapeDtypeStruct((2, GROUPS, s // CMP_STRIDE, CMP_STRIDE * HEAD_DIM), BF16)],
        scratch_shapes=[pltpu.VMEM((2, ts, GROUPS * HEAD_DIM), F32)],
        compiler_params=_params("arbitrary"),
        name="in_proj",
    )(x, gain, wn, wg, wt)


def _gelu_tanh(x):
    return x * (0.5 * (1.0 + jnp.tanh(math.sqrt(2.0 / math.pi) * (x + 0.044715 * (x * x * x)))))


def _compress_kernel(kb_ref, pos_ref, w1_ref, w2_ref, w2t_ref, o_ref, ot_ref):
    kb = kb_ref[0, 0]
    half = kb.shape[1]
    n16 = kb.shape[0]
    first = jnp.dot(kb, w1_ref[0, :half, :], preferred_element_type=F32)
    second = jnp.dot(kb, w1_ref[0, half:, :], preferred_element_type=F32)
    posb = jnp.dot(pos_ref[0], w1_ref[0], preferred_element_type=F32)[0:1]
    pre = first + pltpu.roll(second, n16 - 1, 0) + posb
    h = _gelu_tanh(pre).astype(BF16)
    o_ref[0, 0] = jnp.dot(h, w2_ref[0], preferred_element_type=F32).astype(BF16)
    ot_ref[0, 0] = lax.dot_general(w2t_ref[0], h, _NT, preferred_element_type=F32).astype(BF16)


def _compress(kb, pos8, w1, w2, w2t):
    _, g, n16, cin2 = kb.shape
    hid = w1.shape[2]
    return pl.pallas_call(
        _compress_kernel,
        grid=(2, g),
        in_specs=[pl.BlockSpec((1, 1, n16, cin2), lambda a, b: (a, b, 0, 0)),
                  pl.BlockSpec((1, 8, 2 * cin2), lambda a, b: (a, 0, 0)),
                  pl.BlockSpec((1, 2 * cin2, hid), lambda a, b: (a, 0, 0)),
                  pl.BlockSpec((1, hid, HEAD_DIM), lambda a, b: (a, 0, 0)),
                  pl.BlockSpec((1, HEAD_DIM, hid), lambda a, b: (a, 0, 0))],
        out_specs=[pl.BlockSpec((1, 1, n16, HEAD_DIM), lambda a, b: (a, b, 0, 0)),
                   pl.BlockSpec((1, 1, HEAD_DIM, n16), lambda a, b: (a, b, 0, 0))],
        out_shape=[jax.ShapeDtypeStruct((2, g, n16, HEAD_DIM), BF16),
                   jax.ShapeDtypeStruct((2, g, HEAD_DIM, n16), BF16)],
        compiler_params=_params("arbitrary", "arbitrary"),
        name="kv_compress",
    )(kb, pos8, w1, w2, w2t)


def _cmpsel_body(q_ref, kc_ref, vct_ref, oc_ref, sel_ref, pg_ref, s_ref, t_base, *, cb, rows, nomask, nsel, topk):
    mrows = rows - nomask
    ratio = SEL_BLOCK // CMP_STRIDE
    nblk = rows // ratio
    n_io = nomask + lax.broadcasted_iota(jnp.int32, (mrows, HQ), 0)
    lane_t = lax.broadcasted_iota(jnp.int32, (mrows, HQ), 1) & (QBLOCK - 1)
    blk = lax.broadcasted_iota(jnp.int32, (nblk, QBLOCK), 0)

    def raw_scores(c):
        t0 = t_base + c * QBLOCK
        lanes = slice(c * QBLOCK, (c + 1) * QBLOCK)
        q4 = jnp.concatenate([q_ref[0, r * HEAD_DIM:(r + 1) * HEAD_DIM, lanes] for r in range(REP)], axis=1)
        s = jnp.dot(kc_ref[0, 0, 0:rows, :], q4, preferred_element_type=F32)
        edge = jnp.where(n_io * CMP_STRIDE + (CMP_BLOCK - 1) <= t0 + lane_t, s[nomask:], -jnp.inf)
        s = jnp.concatenate([s[:nomask], edge], axis=0) if nomask else edge
        s_ref[c, 0:rows, :] = s
        m = jnp.max(s, axis=0, keepdims=True)
        return jnp.where(m == -jnp.inf, 0.0, m)

    def attend(c, m):
        t0 = t_base + c * QBLOCK
        lanes = slice(c * QBLOCK, (c + 1) * QBLOCK)
        e = jnp.exp2(s_ref[c, 0:rows, :] - m)
        den = jnp.maximum(jnp.sum(e, axis=0, keepdims=True), 1e-30)
        p = e * (1.0 / den)
        o_t = jnp.dot(vct_ref[0, 0, :, 0:rows], p.astype(BF16), preferred_element_type=F32)
        for r in range(REP):
            oc_ref[0, r * HEAD_DIM:(r + 1) * HEAD_DIM, lanes] = (
                o_t[:, r * QBLOCK:(r + 1) * QBLOCK].astype(oc_ref.dtype))

        pg = ((p[:, 0:QBLOCK] + p[:, QBLOCK:2 * QBLOCK]) + p[:, 2 * QBLOCK:3 * QBLOCK]) + p[:, 3 * QBLOCK:]
        pg_ref[c, 0:8, :] = jnp.zeros((8, QBLOCK), F32)
        pg_ref[c, 8:8 + rows, :] = pg
        ps = pg_ref[c, pl.ds(7, nblk, stride=ratio), :]
        for o in range(1, ratio + 1):
            ps = ps + pg_ref[c, pl.ds(7 + o, nblk, stride=ratio), :]
        cur = (t0 + lax.broadcasted_iota(jnp.int32, (nblk, QBLOCK), 1)) // SEL_BLOCK
        valid = blk <= cur
        forced = (blk == 0) | (blk == cur) | (blk == cur - 1)
        return jnp.where(valid & ~forced, ps, -jnp.inf), valid

    scores, valids = [], []
    m = raw_scores(0)
    for c in range(cb):
        m_next = raw_scores(c + 1) if c + 1 < cb else None
        score, valid = attend(c, m)
        scores.append(score)
        valids.append(valid)
        m = m_next

    def pick_one(_, rems):
        out = []
        for rem in rems:
            top = jnp.max(rem, axis=0, keepdims=True)
            first = jnp.min(jnp.where(rem == top, blk, nblk), axis=0, keepdims=True)
            out.append(jnp.where(blk == first, -jnp.inf, rem))
        return tuple(out)

    rems = lax.fori_loop(0, max(topk - 3, 0), pick_one, tuple(scores))
    for c in range(cb):
        lanes = slice(c * QBLOCK, (c + 1) * QBLOCK)
        sel_ref[0, 0:nblk, lanes] = jnp.where((rems[c] == -jnp.inf) & valids[c], 0.0, -MASK_BIG).astype(BF16)
        if nblk < nsel:
            sel_ref[0, nblk:, lanes] = jnp.full((nsel - nblk, QBLOCK), -MASK_BIG, BF16)


def _cmpsel_kernel(q_ref, kc_ref, vct_ref, oc_ref, sel_ref, pg_ref, s_ref, *, n16, nsel, topk, nvar, cb):
    step = pl.program_id(1)
    per = n16 // nvar
    spv = per // (QBLOCK // CMP_STRIDE) // cb
    for k in range(1, nvar + 1):
        @pl.when((step >= (k - 1) * spv) & (step < k * spv))
        def _(k=k):
            _cmpsel_body(q_ref, kc_ref, vct_ref, oc_ref, sel_ref, pg_ref, s_ref, step * (cb * QBLOCK), cb=cb,
                         rows=per * k, nomask=max(per * (k - 1) - 8, 0), nsel=nsel, topk=topk)


def _cmp_select(q, kc_all, vct_all, s, cb=CMP_CHUNKS):
    n16 = s // CMP_STRIDE
    nsel = s // SEL_BLOCK
    topk = min(SEL_TOPK, nsel)
    nvar = max(n16 // CMP_VARIANT_KEYS, 1)
    return pl.pallas_call(
        functools.partial(_cmpsel_kernel, n16=n16, nsel=nsel, topk=topk, nvar=nvar, cb=cb),
        grid=(GROUPS, s // (QBLOCK * cb)),
        in_specs=[pl.BlockSpec((1, REP * HEAD_DIM, QBLOCK * cb), lambda g, c: (g, 0, c)),
                  pl.BlockSpec((1, 1, n16, HEAD_DIM), lambda g, c: (0, g, 0, 0)),
                  pl.BlockSpec((1, 1, HEAD_DIM, n16), lambda g, c: (1, g, 0, 0))],
        out_specs=[pl.BlockSpec((1, REP * HEAD_DIM, QBLOCK * cb), lambda g, c: (g, 0, c)),
                   pl.BlockSpec((1, nsel, QBLOCK * cb), lambda g, c: (g, 0, c))],
        out_shape=[jax.ShapeDtypeStruct((GROUPS, REP * HEAD_DIM, s), BF16),
                   jax.ShapeDtypeStruct((GROUPS, nsel, s), BF16)],
        scratch_shapes=[pltpu.VMEM((cb, n16 + 8, QBLOCK), F32), pltpu.VMEM((cb, n16, HQ), F32)],
        compiler_params=_params("arbitrary", "arbitrary"),
        name="cmp_select",
    )(q, kc_all, vct_all)


def _fill_bias_strip(strip_ref, thr_ref, tab_ref, g, *, dist0, lo, hi, minus_far):
    nrows = strip_ref.shape[0]

    def block(bi, carry):
        i0 = pl.multiple_of(bi * QBLOCK, QBLOCK)
        d = (lax.broadcasted_iota(jnp.int32, (QBLOCK, QBLOCK), 1)
             - lax.broadcasted_iota(jnp.int32, (QBLOCK, QBLOCK), 0) + (dist0 - i0))
        ok = d >= lo if hi is None else (d >= lo) & (d < hi)
        for r in range(REP):
            base = (g * REP + r) * NUM_BUCKETS
            out = jnp.full((QBLOCK, QBLOCK), tab_ref[base], F32)
            for b in range(1, NUM_BUCKETS):
                out = jnp.where(d >= thr_ref[b], tab_ref[base + b], out)
            if minus_far:
                out = out - tab_ref[base + NUM_BUCKETS - 1]
            strip_ref[pl.ds(i0, QBLOCK), r * QBLOCK:(r + 1) * QBLOCK] = jnp.where(ok, out, -MASK_BIG)
        return carry

    lax.fori_loop(0, nrows // QBLOCK, block, 0)


def _sel_kernel(thr_ref, tab_ref, q_ref, sel_ref, kaug_ref, vt_ref, cst_ref, o_ref,
                qa_ref, s_ref, tmax_ref, m_ref, acc_ref, strip_ref, *, tk):
    g = pl.program_id(0)
    st = pl.program_id(1)
    nb = tk // SEL_BLOCK
    cpt = tk // QBLOCK
    kd = GROUPS * HEAD_DIM
    dmax = 2 * tk - QBLOCK

    @pl.when(st == 0)
    def _():
        _fill_bias_strip(strip_ref, thr_ref, tab_ref, g, dist0=dmax, lo=0, hi=None, minus_far=True)

    qa_ref[...] = jnp.zeros(qa_ref.shape, BF16)
    for b in range(2):
        qa_ref[b, kd + nb:kd + nb + 16, :] = cst_ref[0]
    m_ref[...] = jnp.full(m_ref.shape, -jnp.inf, F32)
    acc_ref[...] = jnp.zeros(acc_ref.shape, F32)
    row0 = pl.multiple_of(g * HEAD_DIM, HEAD_DIM)

    def far_tiles(count):

        def advance(c, j):
            wrap = j + 1 == count
            return jnp.where(wrap, c + 1, c), jnp.where(wrap, 0, j + 1)

        def qk(c, j, buf):
            cc = jnp.minimum(c, cpt - 1)
            lane0 = pl.multiple_of(cc * QBLOCK, QBLOCK)
            for r in range(REP):
                qa_ref[buf, pl.ds(row0, HEAD_DIM), r * QBLOCK:(r + 1) * QBLOCK] = (
                    q_ref[0, r * HEAD_DIM:(r + 1) * HEAD_DIM, pl.ds(lane0, QBLOCK)])
            sb = sel_ref[0, pl.ds(pl.multiple_of(j * nb, nb), nb), pl.ds(lane0, QBLOCK)]
            for r in range(REP):
                qa_ref[buf, kd:kd + nb, r * QBLOCK:(r + 1) * QBLOCK] = sb
            k0 = pl.multiple_of(j * tk, tk)
            s = jnp.dot(kaug_ref[pl.ds(k0, tk), :], qa_ref[buf], preferred_element_type=F32)
            s_ref[buf] = s
            tmax_ref[buf] = jnp.max(s, axis=0, keepdims=True)

        def softmax_pv(c, j, buf):
            k0 = pl.multiple_of(j * tk, tk)
            m_prev = m_ref[c]
            m_new = jnp.maximum(m_prev, tmax_ref[buf])
            alpha = jnp.exp2(m_prev - m_new)
            p = jnp.exp2(s_ref[buf] - m_new)
            acc_ref[c] = alpha * acc_ref[c] + jnp.dot(vt_ref[0, :, pl.ds(k0, tk)], p.astype(BF16),
                                                      preferred_element_type=F32)
            m_ref[c] = m_new

        zero = jnp.int32(0)
        qk(zero, zero, 0)

        def trip(_, carry):
            cur = carry
            for k in range(TILES_PER_TRIP):
                nxt = advance(*cur)
                qk(*nxt, (k + 1) % 2)
                softmax_pv(*cur, k % 2)
                cur = nxt
            return cur

        lax.fori_loop(0, (cpt * count) // TILES_PER_TRIP, trip, (zero, zero))

    def near_tiles(with_previous):
        tiles = []
        for c in range(cpt):
            if with_previous:
                tiles.append((c, st - 1, tk + c * QBLOCK, tk, False))
            tiles.append((c, st, c * QBLOCK, (c + 1) * QBLOCK, True))

        def qk(c, j, delta, rows, diagonal, buf):
            lanes = slice(c * QBLOCK, (c + 1) * QBLOCK)
            for r in range(REP):
                qa_ref[buf, pl.ds(row0, HEAD_DIM), r * QBLOCK:(r + 1) * QBLOCK] = (
                    q_ref[0, r * HEAD_DIM:(r + 1) * HEAD_DIM, lanes])
            sb = sel_ref[0, pl.ds(pl.multiple_of(j * nb, nb), nb), lanes]
            for r in range(REP):
                qa_ref[buf, kd:kd + nb, r * QBLOCK:(r + 1) * QBLOCK] = sb
            k0 = pl.multiple_of(j * tk, tk)
            s = (jnp.dot(kaug_ref[pl.ds(k0, rows), :], qa_ref[buf], preferred_element_type=F32)
                 + strip_ref[dmax - delta:dmax - delta + rows, :])
            s_ref[buf, 0:rows, :] = s
            tmax_ref[buf] = jnp.max(s, axis=0, keepdims=True)

        def softmax_pv(c, j, rows, diagonal, buf):
            k0 = pl.multiple_of(j * tk, tk)
            m_prev = m_ref[c]
            m_new = jnp.maximum(m_prev, tmax_ref[buf])
            alpha = jnp.exp2(m_prev - m_new)
            p = jnp.exp2(s_ref[buf, 0:rows, :] - m_new)
            acc = alpha * acc_ref[c] + jnp.dot(vt_ref[0, :, pl.ds(k0, rows)], p.astype(BF16),
                                               preferred_element_type=F32)
            acc_ref[c] = acc
            m_ref[c] = m_new
            if diagonal:
                out = acc[0:HEAD_DIM] * (1.0 / jnp.maximum(acc[HEAD_DIM:HEAD_DIM + 1], 1e-30))
                for r in range(REP):
                    o_ref[0, r * HEAD_DIM:(r + 1) * HEAD_DIM, c * QBLOCK:(c + 1) * QBLOCK] = (
                        out[:, r * QBLOCK:(r + 1) * QBLOCK].astype(o_ref.dtype))

        qk(*tiles[0], 0)
        for n, (c, j, delta, rows, diagonal) in enumerate(tiles):
            if n + 1 < len(tiles):
                qk(*tiles[n + 1], (n + 1) % 2)
            softmax_pv(c, j, rows, diagonal, n % 2)

    @pl.when(st > 1)
    def _():
        far_tiles(st - 1)

    @pl.when(st == 0)
    def _():
        near_tiles(False)

    @pl.when(st > 0)
    def _():
        near_tiles(True)


def _selected(thr, tab, q, sel, kaug, vt, cst, s, tk):
    nsel = s // SEL_BLOCK
    sl = 3 * tk - QBLOCK
    cpt = tk // QBLOCK
    smem = pl.BlockSpec(memory_space=pltpu.SMEM)
    return pl.pallas_call(
        functools.partial(_sel_kernel, tk=tk),
        grid=(GROUPS, s // tk),
        in_specs=[smem, smem,
                  pl.BlockSpec((1, REP * HEAD_DIM, tk), lambda g, c: (g, 0, c)),
                  pl.BlockSpec((1, nsel, tk), lambda g, c: (g, 0, c)),
                  pl.BlockSpec(kaug.shape, lambda g, c: (0, 0)),
                  pl.BlockSpec((1, vt.shape[1], s), lambda g, c: (g, 0, 0)),
                  pl.BlockSpec((1, 16, HQ), lambda g, c: (g, 0, 0))],
        out_specs=pl.BlockSpec((1, REP * HEAD_DIM, tk), lambda g, c: (g, 0, c)),
        out_shape=jax.ShapeDtypeStruct((GROUPS, REP * HEAD_DIM, s), BF16),
        scratch_shapes=[pltpu.VMEM((2, kaug.shape[1], HQ), BF16), pltpu.VMEM((2, tk, HQ), F32),
                        pltpu.VMEM((2, 1, HQ), F32),
                        pltpu.VMEM((cpt, 1, HQ), F32), pltpu.VMEM((cpt, vt.shape[1], HQ), F32),
                        pltpu.VMEM((sl, HQ), F32)],
        compiler_params=_params("arbitrary", "arbitrary"),
        name="selected_attn",
    )(thr, tab, q, sel, kaug, vt, cst)


def _band_kernel(thr_ref, tab_ref, q_ref, k_ref, vt_ref, sink_ref, o_ref, qa_ref, strip_ref, s_ref, *,
                 window, use_sink, cb):
    g = pl.program_id(0)

    @pl.when(pl.program_id(1) == 0)
    def _():
        _fill_bias_strip(strip_ref, thr_ref, tab_ref, g, dist0=window, lo=0, hi=window, minus_far=False)

    base = pl.multiple_of(pl.program_id(1) * (QBLOCK * cb), QBLOCK * cb)
    nk = window + QBLOCK
    qa_ref[...] = jnp.zeros(qa_ref.shape, BF16)
    row0 = pl.multiple_of(g * HEAD_DIM, HEAD_DIM)
    for c in range(cb):
        for r in range(REP):
            qa_ref[c, pl.ds(row0, HEAD_DIM), r * QBLOCK:(r + 1) * QBLOCK] = (
                q_ref[0, r * HEAD_DIM:(r + 1) * HEAD_DIM, c * QBLOCK:(c + 1) * QBLOCK])

    def scores(c, first_step):
        t0 = base + c * QBLOCK
        s = jnp.dot(k_ref[pl.ds(t0, nk), :], qa_ref[c], preferred_element_type=F32) + strip_ref[...]
        if first_step and c * QBLOCK < window:
            row = lax.broadcasted_iota(jnp.int32, (nk, HQ), 0)
            s = jnp.where(row >= window - c * QBLOCK, s, -MASK_BIG)
        s_ref[c] = s
        m = jnp.max(s, axis=0, keepdims=True)
        return jnp.maximum(m, sink_ref[0]) if use_sink else m

    def attend(c, m):
        t0 = base + c * QBLOCK
        e = jnp.exp2(s_ref[c] - m)
        o_t = jnp.dot(vt_ref[0, :, pl.ds(t0, nk)], e.astype(BF16), preferred_element_type=F32)
        den = o_t[HEAD_DIM:HEAD_DIM + 1]
        den = den + jnp.exp2(sink_ref[0] - m) if use_sink else jnp.maximum(den, 1e-30)
        o_t = o_t[0:HEAD_DIM] * (1.0 / den)
        for r in range(REP):
            o_ref[0, r * HEAD_DIM:(r + 1) * HEAD_DIM, c * QBLOCK:(c + 1) * QBLOCK] = (
                o_t[:, r * QBLOCK:(r + 1) * QBLOCK].astype(o_ref.dtype))

    def chunks(first_step):
        m = scores(0, first_step)
        for c in range(cb):
            m_next = scores(c + 1, first_step) if c + 1 < cb else None
            attend(c, m)
            m = m_next

    @pl.when(pl.program_id(1) == 0)
    def _():
        chunks(True)

    @pl.when(pl.program_id(1) > 0)
    def _():
        chunks(False)


def _banded(thr, tab, q, kpad, vtpad, sink, s, window, use_sink, name, cb=BAND_CHUNKS):
    assert QBLOCK * cb >= window
    nk = window + QBLOCK
    smem = pl.BlockSpec(memory_space=pltpu.SMEM)
    return pl.pallas_call(
        functools.partial(_band_kernel, window=window, use_sink=use_sink, cb=cb),
        grid=(GROUPS, s // (QBLOCK * cb)),
        in_specs=[smem, smem,
                  pl.BlockSpec((1, REP * HEAD_DIM, QBLOCK * cb), lambda g, c: (g, 0, c)),
                  pl.BlockSpec(kpad.shape, lambda g, c: (0, 0)),
                  pl.BlockSpec((1, vtpad.shape[1], s + window), lambda g, c: (g, 0, 0)),
                  pl.BlockSpec((1, 1, HQ), lambda g, c: (g, 0, 0))],
        out_specs=pl.BlockSpec((1, REP * HEAD_DIM, QBLOCK * cb), lambda g, c: (g, 0, c)),
        out_shape=jax.ShapeDtypeStruct((GROUPS, REP * HEAD_DIM, s), BF16),
        scratch_shapes=[pltpu.VMEM((cb, GROUPS * HEAD_DIM, HQ), BF16), pltpu.VMEM((nk, HQ), F32),
                        pltpu.VMEM((cb, nk, HQ), F32)],
        compiler_params=_params("arbitrary", "arbitrary"),
        name=name,
    )(thr, tab, q, kpad, vtpad, sink)


def _band_pair_kernel(thr_ref, tab_a, tab_b, q_a, k_a, vt_a, sink_a, q_b, k_b, vt_b, sink_b, o_a, o_b,
                      qa_a, strip_a, s_a, qa_b, strip_b, s_b, *, cb):
    _band_kernel(thr_ref, tab_a, q_a, k_a, vt_a, sink_a, o_a, qa_a, strip_a, s_a,
                 window=NSA_WINDOW, use_sink=False, cb=cb)
    _band_kernel(thr_ref, tab_b, q_b, k_b, vt_b, sink_b, o_b, qa_b, strip_b, s_b,
                 window=SWA_WINDOW, use_sink=True, cb=cb)


def _banded_pair(thr, tab_a, tab_b, a, b, s, cb):
    assert QBLOCK * cb >= max(NSA_WINDOW, SWA_WINDOW)
    smem = pl.BlockSpec(memory_space=pltpu.SMEM)
    qspec = pl.BlockSpec((1, REP * HEAD_DIM, QBLOCK * cb), lambda g, c: (g, 0, c))

    def specs(ops):
        _, kpad, vtpad, _ = ops
        return [qspec, pl.BlockSpec(kpad.shape, lambda g, c: (0, 0)),
                pl.BlockSpec((1,) + vtpad.shape[1:], lambda g, c: (g, 0, 0)),
                pl.BlockSpec((1, 1, HQ), lambda g, c: (g, 0, 0))]

    def scratch(window):
        nk = window + QBLOCK
        return [pltpu.VMEM((cb, GROUPS * HEAD_DIM, HQ), BF16), pltpu.VMEM((nk, HQ), F32),
                pltpu.VMEM((cb, nk, HQ), F32)]

    out = jax.ShapeDtypeStruct((GROUPS, REP * HEAD_DIM, s), BF16)
    return pl.pallas_call(
        functools.partial(_band_pair_kernel, cb=cb),
        grid=(GROUPS, s // (QBLOCK * cb)),
        in_specs=[smem, smem, smem] + specs(a) + specs(b),
        out_specs=[qspec, qspec],
        out_shape=[out, out],
        scratch_shapes=scratch(NSA_WINDOW) + scratch(SWA_WINDOW),
        compiler_params=_params("arbitrary", "arbitrary"),
        name="banded_pair",
    )(thr, tab_a, tab_b, *a, *b)


def _merge_kernel(oc_ref, os_ref, ow_ref, ob_ref, gn_ref, gab_ref, x_ref, wa_ref, wb_ref, wo_ref, gp_ref, o_ref):
    d = x_ref.shape[1]
    gn = jax.nn.sigmoid(gn_ref[...])
    parts = []
    for h in range(HEADS):
        rows = slice(h * HEAD_DIM, (h + 1) * HEAD_DIM)
        parts.append(gn[h:h + 1] * oc_ref[rows, :].astype(F32)
                     + gn[HEADS + h:HEADS + h + 1] * os_ref[rows, :].astype(F32)
                     + gn[2 * HEADS + h:2 * HEADS + h + 1] * ow_ref[rows, :].astype(F32))
    oa_t = jnp.concatenate(parts, axis=0).astype(BF16)
    ya = lax.dot_general(oa_t, wa_ref[...], _TN, preferred_element_type=F32)
    yb = lax.dot_general(ob_ref[...], wb_ref[...], _TN, preferred_element_type=F32)
    y = (jax.nn.sigmoid(gab_ref[:, :d].astype(F32)) * ya
         + jax.nn.sigmoid(gab_ref[:, d:].astype(F32)) * yb)
    z = jnp.dot(y.astype(BF16), wo_ref[...], preferred_element_type=F32)
    o_ref[...] = x_ref[...] + _rms(z, gp_ref[...])


def _merge(oc, os_, ow, ob, gn, gab, x, wa, wb, wo, gpost, ts):
    s, d = x.shape
    hd = oc.shape[0]
    tok = lambda rows: pl.BlockSpec((rows, ts), lambda i: (0, i))
    full = lambda shp: pl.BlockSpec(shp, lambda i: (0, 0))
    return pl.pallas_call(
        _merge_kernel,
        grid=(s // ts,),
        in_specs=[tok(hd), tok(hd), tok(hd), tok(hd), tok(gn.shape[0]),
                  pl.BlockSpec((ts, 2 * d), lambda i: (i, 0)), pl.BlockSpec((ts, d), lambda i: (i, 0)),
                  full((hd, d)), full((hd, d)), full((d, d)), full((1, d))],
        out_specs=pl.BlockSpec((ts, d), lambda i: (i, 0)),
        out_shape=jax.ShapeDtypeStruct((s, d), F32),
        compiler_params=_params("arbitrary"),
        name="merge_out_proj",
    )(oc, os_, ow, ob, gn, gab, x, wa, wb, wo, gpost)


def _mlp_kernel(h_ref, gpre_ref, w1_ref, w2_ref, gpost_ref, o_ref):
    u = _rms(h_ref[...], gpre_ref[...]).astype(BF16)
    a = jnp.maximum(jnp.dot(u, w1_ref[...], preferred_element_type=F32), 0.0)
    f = jnp.dot((a * a).astype(BF16), w2_ref[...], preferred_element_type=F32)
    o_ref[...] = h_ref[...] + _rms(f, gpost_ref[...])


def _mlp(h, gpre, w1, w2, gpost, ts):
    s, d = h.shape
    dff = w1.shape[1]
    full = lambda shp: pl.BlockSpec(shp, lambda i: (0, 0))
    return pl.pallas_call(
        _mlp_kernel,
        grid=(s // ts,),
        in_specs=[pl.BlockSpec((ts, d), lambda i: (i, 0)), full((1, d)), full((d, dff)), full((dff, d)),
                  full((1, d))],
        out_specs=pl.BlockSpec((ts, d), lambda i: (i, 0)),
        out_shape=jax.ShapeDtypeStruct((s, d), F32),
        compiler_params=_params("arbitrary"),
        name="mlp_relu2",
    )(h, gpre, w1, w2, gpost)


def kernel(x, norm_mix_pre, norm_mix_post, norm_mlp_pre, norm_mlp_post, w_in,
           cmp_pos_k, cmp_w1_k, cmp_w2_k, cmp_pos_v, cmp_w1_v, cmp_w2_v,
           attn_sinks, rel_bias, w_up_nsa, w_up_swa, w_out, w_mlp_in, w_mlp_out):
    b, s, d = x.shape
    assert b == 1 and s % SEL_TILE == 0 and w_in.shape[0] == 1
    assert SEL_TILE >= MAX_DISTANCE
    qd = HEADS * HEAD_DIM
    kvd = GROUPS * HEAD_DIM
    sizes = (qd,) + (kvd,) * 6 + (3 * HEADS, qd, kvd, kvd, d, d)
    offs = [0]
    for z in sizes:
        offs.append(offs[-1] + z)
    w = w_in[0]
    col = lambda i: w[:, offs[i]:offs[i + 1]]
    (w_qn, w_kc, w_vc, w_ks, w_vs, w_kw, w_vw, w_gn, w_qs, w_k_s, w_v_s, w_ga, w_gb) = [col(i) for i in range(13)]
    scale = HEAD_DIM ** -0.5 * LOG2E
    w_gn = w_gn.reshape(d, HEADS, 3).transpose(0, 2, 1).reshape(d, 3 * HEADS)
    w_gn = jnp.pad(w_gn, ((0, 0), (0, 8)))
    wn = jnp.concatenate([w_kc, w_vc, w_ks, w_kw, w_k_s], axis=1).astype(BF16)
    wg = jnp.concatenate([w_ga, w_gb], axis=1).astype(BF16)
    wt = jnp.concatenate([w_qn * scale, w_qs * scale, w_vs, w_vw, w_v_s, w_gn], axis=1).T.astype(BF16)
    nqv = 2 * qd + 3 * kvd

    x2 = x[0]
    kv, gab, qv, gn, kb = _project(x2, norm_mix_pre, wn, wg, wt, nqv, ts=PROJ_TOKENS)
    q_nsa = qv[0:qd].reshape(GROUPS, REP * HEAD_DIM, s)
    q_swa = qv[qd:2 * qd].reshape(GROUPS, REP * HEAD_DIM, s)
    vs_t = qv[2 * qd:2 * qd + kvd].reshape(GROUPS, HEAD_DIM, s)
    vw_t = qv[2 * qd + kvd:2 * qd + 2 * kvd].reshape(GROUPS, HEAD_DIM, s)
    vswa_t = qv[2 * qd + 2 * kvd:].reshape(GROUPS, HEAD_DIM, s)

    n16 = s // CMP_STRIDE
    pos8 = jnp.stack([cmp_pos_k[0], cmp_pos_v[0]]).reshape(2, 1, CMP_BLOCK * HEAD_DIM)
    pos8 = jnp.broadcast_to(pos8, (2, 8, CMP_BLOCK * HEAD_DIM)).astype(BF16)
    w1 = jnp.stack([cmp_w1_k[0], cmp_w1_v[0]]).astype(BF16)
    w2 = jnp.stack([cmp_w2_k[0], cmp_w2_v[0]]).astype(BF16)
    cmp_n, cmp_t = _compress(kb, pos8, w1, w2, w2.transpose(0, 2, 1))

    oc, sel = _cmp_select(q_nsa, cmp_n, cmp_t, s)

    bias_nsa = rel_bias[:, :HEADS].reshape(NUM_BUCKETS, GROUPS, REP) * LOG2E
    bias_swa = rel_bias[:, HEADS:].reshape(NUM_BUCKETS, GROUPS, REP) * LOG2E
    thr = jnp.asarray(_bucket_starts())
    tab_nsa = bias_nsa.transpose(1, 2, 0).reshape(-1)
    tab_swa = bias_swa.transpose(1, 2, 0).reshape(-1)

    tk = SEL_TILE
    nb = tk // SEL_BLOCK
    posn = jnp.arange(s)
    onehot = ((posn // SEL_BLOCK) % nb)[:, None] == jnp.arange(nb)[None, :]
    aug = jnp.concatenate([onehot.astype(BF16), jnp.ones((s, 2), BF16),
                           jnp.zeros((s, kvd - nb - 2), BF16)], axis=1)
    kaug = jnp.concatenate([kv[:, 0:kvd], aug], axis=1)
    far = bias_nsa[NUM_BUCKETS - 1]
    far_hi = far.astype(BF16)
    far_lo = (far - far_hi.astype(F32)).astype(BF16)
    cst = jnp.stack([far_hi, far_lo], axis=1)
    cst = jnp.broadcast_to(cst[:, :, :, None], (GROUPS, 2, REP, QBLOCK)).reshape(GROUPS, 2, HQ)
    cst = jnp.pad(cst, ((0, 0), (0, 14), (0, 0)))
    ones_row = jnp.concatenate([jnp.ones((GROUPS, 1, s), BF16), jnp.zeros((GROUPS, 15, s), BF16)], axis=1)
    vs_aug = jnp.concatenate([vs_t, ones_row], axis=1)
    o_sel = _selected(thr, tab_nsa, q_nsa, sel, kaug, vs_aug, cst, s, tk)

    pad_rows = lambda a, wdw: jnp.pad(a, ((wdw, 0), (0, 0)))
    pad_lanes = lambda a, wdw: jnp.pad(jnp.concatenate([a, ones_row], axis=1), ((0, 0), (0, 0), (wdw, 0)))
    no_sink = jnp.zeros((GROUPS, 1, HQ), F32)
    sink = attn_sinks[0].reshape(GROUPS, 1, REP, 1).astype(F32) * LOG2E
    sink = jnp.broadcast_to(sink, (GROUPS, 1, REP, QBLOCK)).reshape(GROUPS, 1, HQ)
    o_win, o_swa = _banded_pair(
        thr, tab_nsa, tab_swa,
        (q_nsa, pad_rows(kv[:, kvd:2 * kvd], NSA_WINDOW), pad_lanes(vw_t, NSA_WINDOW), no_sink),
        (q_swa, pad_rows(kv[:, 2 * kvd:3 * kvd], SWA_WINDOW), pad_lanes(vswa_t, SWA_WINDOW), sink),
        s, cb=BAND_CHUNKS // 2)

    flat = lambda a: a.reshape(qd, s)
    h1 = _merge(flat(oc), flat(o_sel), flat(o_win), flat(o_swa), gn, gab, x2,
                w_up_nsa[0].astype(BF16), w_up_swa[0].astype(BF16), w_out[0].astype(BF16),
                norm_mix_post, ts=PROJ_TOKENS)
    out = _mlp(h1, norm_mlp_pre, w_mlp_in[0].astype(BF16), w_mlp_out[0].astype(BF16), norm_mlp_post,
               ts=MLP_TOKENS)
    return out[None]
```
